```python
import math
import jax, jax.numpy as jnp
from jax import lax
import numpy as np

D_MODEL = 1024
BATCH = 4
SEQ = 4096
DEPTH = 1
DEC_BATCH = 32
DEC_SEQ = 1
PAST_LEN = 8192
PAGE_SIZE = 128

HEAD_DIM = 64
HEADS_PER_GROUP = 8
WINDOWS = (128, 512, 2048)
DILATIONS = (1, 4, 16)
N_GROUPS = 3
ATTN_WIDTH = N_GROUPS * HEADS_PER_GROUP * HEAD_DIM
ATTN_OUT = HEADS_PER_GROUP * HEAD_DIM
ROT_DIM = HEAD_DIM // 4
ROPE_THETA = 500000.0
MLSTM_HEADS = 8
MLSTM_DQK = 64
MLSTM_DV = 128
MLSTM_QK_WIDTH = MLSTM_HEADS * MLSTM_DQK
MLSTM_V_WIDTH = MLSTM_HEADS * MLSTM_DV
MLSTM_CHUNK = 64
D_FF = -(-8 * D_MODEL // (3 * 256)) * 256
NORM_EPS = 1e-6
SPLITS = (ATTN_WIDTH, ATTN_WIDTH, ATTN_WIDTH, MLSTM_QK_WIDTH, MLSTM_QK_WIDTH,
          MLSTM_V_WIDTH, MLSTM_V_WIDTH, 2 * MLSTM_HEADS, D_MODEL, D_MODEL)
D_IN = sum(SPLITS)

kernel_name = "hybrid_dilated_swa_mlstm_step"


def _rmsnorm(x, g):
    xf = x.astype(jnp.float32)
    y = xf * lax.rsqrt(jnp.mean(xf * xf, axis=-1, keepdims=True) + NORM_EPS)
    return (y * g.astype(jnp.float32)).astype(x.dtype)


def _rope_partial(x, pos):
    half = ROT_DIM // 2
    inv_freq = jnp.exp(-math.log(ROPE_THETA) * jnp.arange(0, ROT_DIM, 2, dtype=jnp.float32) / ROT_DIM)
    ang = pos[:, None] * inv_freq[None, :]
    cos = jnp.cos(ang)[None, :, None, :]
    sin = jnp.sin(ang)[None, :, None, :]
    xf = x.astype(jnp.float32)
    x1, x2, rest = xf[..., :half], xf[..., half:ROT_DIM], xf[..., ROT_DIM:]
    out = jnp.concatenate([x1 * cos - x2 * sin, x2 * cos + x1 * sin, rest], axis=-1)
    return out.astype(x.dtype)


def _project(x, pos, norm1_g, w_in, b_if, q_norm_g, k_norm_g):
    B, T, _ = x.shape
    h = _rmsnorm(x, norm1_g)
    z = jnp.einsum('btd,de->bte', h, w_in)
    offs = np.cumsum(SPLITS)[:-1].tolist()
    qa, ka, va, qm, km, vm, om, gif, ga, gb = jnp.split(z, offs, axis=-1)
    nh = N_GROUPS * HEADS_PER_GROUP
    qa = _rope_partial(_rmsnorm(qa.reshape(B, T, nh, HEAD_DIM), q_norm_g), pos)
    ka = _rope_partial(_rmsnorm(ka.reshape(B, T, nh, HEAD_DIM), k_norm_g), pos)
    va = va.reshape(B, T, nh, HEAD_DIM)
    grp = lambda a: a.reshape(B, T, N_GROUPS, HEADS_PER_GROUP, HEAD_DIM)
    qm = qm.reshape(B, T, MLSTM_HEADS, MLSTM_DQK) * (MLSTM_DQK ** -0.5)
    km = km.reshape(B, T, MLSTM_HEADS, MLSTM_DQK)
    vm = vm.reshape(B, T, MLSTM_HEADS, MLSTM_DV)
    gif = gif.astype(jnp.float32) + b_if.astype(jnp.float32)
    ig = gif[..., :MLSTM_HEADS]
    lf = jax.nn.log_sigmoid(gif[..., MLSTM_HEADS:])
    return grp(qa), grp(ka), grp(va), qm, km, vm, om, ig, lf, ga, gb


def _to_residues(a, d):
    B, S = a.shape[:2]
    rest = a.shape[2:]
    return a.reshape(B, S // d, d, *rest).swapaxes(1, 2).reshape(B * d, S // d, *rest)


def _from_residues(a, d, B):
    N, L = a.shape[:2]
    rest = a.shape[2:]
    return a.reshape(B, d, L, *rest).swapaxes(1, 2).reshape(B, L * d, *rest)


def _banded_causal_attention(q, k, v, span):
    N, L, H, Dh = q.shape
    blk = span
    nb = -(-L // blk)
    pad = nb * blk - L
    padf = lambda a: jnp.pad(a.astype(jnp.float32), ((0, 0), (0, pad), (0, 0), (0, 0))).reshape(N, nb, blk, H, Dh)
    qb, kb, vb = padf(q), padf(k), padf(v)

    def with_prev(a):
        prev = jnp.pad(a[:, :-1], ((0, 0), (1, 0), (0, 0), (0, 0), (0, 0)))
        return jnp.concatenate([prev, a], axis=2)

    kk, vv = with_prev(kb), with_prev(vb)
    s = jnp.einsum('nbqhd,nbkhd->nbhqk', qb, kk) * (Dh ** -0.5)
    qi = jnp.arange(blk)[:, None]
    ki = jnp.arange(2 * blk)[None, :]
    dist = qi + blk - ki
    band = (dist >= 0) & (dist <= span)
    not_before_start = (ki >= blk)[None] | (jnp.arange(nb)[:, None, None] > 0)
    mask = band[None] & not_before_start
    s = jnp.where(mask[None, :, None], s, -jnp.inf)
    lse = jax.nn.logsumexp(s, axis=-1)
    p = jnp.exp(s - lse[..., None])
    o = jnp.einsum('nbhqk,nbkhd->nbqhd', p, vv).reshape(N, nb * blk, H, Dh)[:, :L]
    lse = lse.transpose(0, 1, 3, 2).reshape(N, nb * blk, H)[:, :L]
    return o, lse


def _combine_groups(outs, lses):
    w = jax.nn.softmax(jnp.stack(lses, axis=0), axis=0)
    return jnp.einsum('gbth,gbthd->bthd', w, jnp.stack(outs, axis=0))


def _prompt_attention(qa, ka, va):
    B, S = qa.shape[:2]
    outs, lses, kvs = [], [], []
    for g in range(N_GROUPS):
        d = DILATIONS[g]
        span = WINDOWS[g] // d
        o, lse = _banded_causal_attention(_to_residues(qa[:, :, g], d), _to_residues(ka[:, :, g], d),
                                          _to_residues(va[:, :, g], d), span)
        outs.append(_from_residues(o, d, B))
        lses.append(_from_residues(lse, d, B))
        keep = min(WINDOWS[g], S)
        kvs.append(jnp.stack([ka[:, S - keep:, g], va[:, S - keep:, g]], axis=2))
    return _combine_groups(outs, lses), kvs


def _sample_attention(qa, ka, va, caches):
    T = qa.shape[1]
    outs, lses, kvs = [], [], []
    for g in range(N_GROUPS):
        d = DILATIONS[g]
        span = WINDOWS[g] // d
        cache = caches[g]
        wb = cache.shape[1]
        kvc = jnp.concatenate([cache, jnp.stack([ka[:, :, g], va[:, :, g]], axis=2).astype(cache.dtype)], axis=1)
        idx = wb + jnp.arange(T)[:, None] - jnp.arange(span + 1)[None, :] * d
        valid = idx >= 0
        kvg = jnp.take(kvc, jnp.clip(idx, 0, None), axis=1).astype(jnp.float32)
        s = jnp.einsum('bthd,btjhd->bthj', qa[:, :, g].astype(jnp.float32), kvg[:, :, :, 0]) * (HEAD_DIM ** -0.5)
        s = jnp.where(valid[None, :, None, :], s, -jnp.inf)
        lse = jax.nn.logsumexp(s, axis=-1)
        p = jnp.exp(s - lse[..., None])
        outs.append(jnp.einsum('bthj,btjhd->bthd', p, kvg[:, :, :, 1]))
        lses.append(lse)
        kvs.append(kvc[:, kvc.shape[1] - wb:])
    return _combine_groups(outs, lses), kvs


def _mlstm_chunk(carry, inputs):
    C, n, m = (c.astype(jnp.float32) for c in carry)
    q, k, v, ig, lf = inputs
    q, k, v = q.astype(jnp.float32), k.astype(jnp.float32), v.astype(jnp.float32)
    L = q.shape[1]
    b = jnp.cumsum(lf, axis=1)
    tri = jnp.tril(jnp.ones((L, L), dtype=bool))
    D = b[:, :, None, :] - b[:, None, :, :] + ig[:, None, :, :]
    D = jnp.where(tri[None, :, :, None], D, -jnp.inf)
    inter = b + m[:, None, :]
    m_t = jnp.maximum(inter, jnp.max(D, axis=2))
    w = jnp.exp(D - m_t[:, :, None, :])
    wqk = w * jnp.einsum('bthd,bshd->btsh', q, k)
    dec = jnp.exp(inter - m_t)
    num = dec[..., None] * jnp.einsum('bthd,bhde->bthe', q, C) + jnp.einsum('btsh,bshe->bthe', wqk, v)
    den = dec * jnp.einsum('bthd,bhd->bth', q, n) + jnp.sum(wqk, axis=2)
    h = num / jnp.maximum(jnp.abs(den), jnp.exp(-m_t))[..., None]
    m_new = m_t[:, -1]
    dstate = jnp.exp(b[:, -1] + m - m_new)
    ws = jnp.exp(b[:, -1:, :] - b + ig - m_new[:, None, :])
    C_new = dstate[..., None, None] * C + jnp.einsum('bsh,bshd,bshe->bhde', ws, k, v)
    n_new = dstate[..., None] * n + jnp.einsum('bsh,bshd->bhd', ws, k)
    return (C_new, n_new, m_new), h


def _mlstm_prompt(q, k, v, ig, lf):
    B, S = q.shape[:2]
    nc = S // MLSTM_CHUNK
    ch = lambda a: a.reshape(B, nc, MLSTM_CHUNK, *a.shape[2:]).swapaxes(0, 1)
    init = (jnp.zeros((B, MLSTM_HEADS, MLSTM_DQK, MLSTM_DV), jnp.float32),
            jnp.zeros((B, MLSTM_HEADS, MLSTM_DQK), jnp.float32),
            jnp.zeros((B, MLSTM_HEADS), jnp.float32))
    state, hs = lax.scan(_mlstm_chunk, init, (ch(q), ch(k), ch(v), ch(ig), ch(lf)))
    return state, hs.swapaxes(0, 1).reshape(B, S, MLSTM_HEADS, MLSTM_DV)


def _merge_and_ffn(x, o_att, h_m, om, ga, gb, w_att_out, w_m_out, w_o, norm2_g, w_gate, w_up, w_down):
    B, T, _ = x.shape
    ya = jnp.einsum('bte,ed->btd', o_att.reshape(B, T, ATTN_OUT).astype(x.dtype), w_att_out)
    hm = jax.nn.sigmoid(om) * h_m.reshape(B, T, MLSTM_V_WIDTH).astype(x.dtype)
    yb = jnp.einsum('bte,ed->btd', hm, w_m_out)
    mixed = jax.nn.sigmoid(ga) * ya + jax.nn.sigmoid(gb) * yb
    x = x + jnp.einsum('btd,de->bte', mixed, w_o)
    h = _rmsnorm(x, norm2_g)
    ff = jax.nn.silu(jnp.einsum('btd,df->btf', h, w_gate)) * jnp.einsum('btd,df->btf', h, w_up)
    return x + jnp.einsum('btf,fd->btd', ff, w_down)


def setup_inputs(seed: int = 0) -> dict:
    key = jax.random.key(seed)
    ks = jax.random.split(key, 24)
    f = jnp.float32
    nrm = jax.random.normal
    return {
        "x_prompt": nrm(ks[0], (BATCH, SEQ, D_MODEL), f),
        "x_sample": nrm(ks[1], (DEC_BATCH, DEC_SEQ, D_MODEL), f),
        "cache_kv_w128": nrm(ks[2], (DEC_BATCH, min(WINDOWS[0], PAST_LEN), 2, HEADS_PER_GROUP, HEAD_DIM), f),
        "cache_kv_w512": nrm(ks[3], (DEC_BATCH, min(WINDOWS[1], PAST_LEN), 2, HEADS_PER_GROUP, HEAD_DIM), f),
        "cache_kv_w2048": nrm(ks[4], (DEC_BATCH, min(WINDOWS[2], PAST_LEN), 2, HEADS_PER_GROUP, HEAD_DIM), f),
        "state_mlstm_C": 0.5 * nrm(ks[5], (DEC_BATCH, MLSTM_HEADS, MLSTM_DQK, MLSTM_DV), f),
        "state_mlstm_n": 0.5 * nrm(ks[6], (DEC_BATCH, MLSTM_HEADS, MLSTM_DQK), f),
        "state_mlstm_m": nrm(ks[7], (DEC_BATCH, MLSTM_HEADS), f),
        "norm1_g": 1.0 + 0.1 * nrm(ks[8], (D_MODEL,), f),
        "w_in": nrm(ks[9], (D_MODEL, D_IN), f) * D_MODEL ** -0.5,
        "b_if": jnp.concatenate([0.1 * nrm(ks[10], (MLSTM_HEADS,), f),
                                 3.0 + 0.1 * nrm(ks[11], (MLSTM_HEADS,), f)]),
        "q_norm_g": 1.0 + 0.1 * nrm(ks[12], (HEAD_DIM,), f),
        "k_norm_g": 1.0 + 0.1 * nrm(ks[13], (HEAD_DIM,), f),
        "w_att_out": nrm(ks[14], (ATTN_OUT, D_MODEL), f) * ATTN_OUT ** -0.5,
        "w_m_out": nrm(ks[15], (MLSTM_V_WIDTH, D_MODEL), f) * MLSTM_V_WIDTH ** -0.5,
        "w_o": nrm(ks[16], (D_MODEL, D_MODEL), f) * D_MODEL ** -0.5,
        "norm2_g": 1.0 + 0.1 * nrm(ks[17], (D_MODEL,), f),
        "w_gate": nrm(ks[18], (D_MODEL, D_FF), f) * D_MODEL ** -0.5,
        "w_up": nrm(ks[19], (D_MODEL, D_FF), f) * D_MODEL ** -0.5,
        "w_down": nrm(ks[20], (D_FF, D_MODEL), f) * D_FF ** -0.5,
    }


def reference(x_prompt, x_sample, cache_kv_w128, cache_kv_w512, cache_kv_w2048,
              state_mlstm_C, state_mlstm_n, state_mlstm_m,
              norm1_g, w_in, b_if, q_norm_g, k_norm_g, w_att_out, w_m_out, w_o,
              norm2_g, w_gate, w_up, w_down):
    S = x_prompt.shape[1]
    T = x_sample.shape[1]
    pos_p = jnp.arange(S, dtype=jnp.float32)
    qa, ka, va, qm, km, vm, om, ig, lf, ga, gb = _project(x_prompt, pos_p, norm1_g, w_in, b_if, q_norm_g, k_norm_g)
    o_att_p, kv_p = _prompt_attention(qa, ka, va)
    (C_p, n_p, m_p), h_p = _mlstm_prompt(qm, km, vm, ig, lf)
    y_prompt = _merge_and_ffn(x_prompt, o_att_p, h_p, om, ga, gb, w_att_out, w_m_out, w_o,
                              norm2_g, w_gate, w_up, w_down)
    pos_s = jnp.arange(T, dtype=jnp.float32) + PAST_LEN
    qa_s, ka_s, va_s, qm_s, km_s, vm_s, om_s, ig_s, lf_s, ga_s, gb_s = _project(
        x_sample, pos_s, norm1_g, w_in, b_if, q_norm_g, k_norm_g)
    o_att_s, kv_s = _sample_attention(qa_s, ka_s, va_s, (cache_kv_w128, cache_kv_w512, cache_kv_w2048))
    (C_s, n_s, m_s), h_s = _mlstm_chunk((state_mlstm_C, state_mlstm_n, state_mlstm_m),
                                        (qm_s, km_s, vm_s, ig_s, lf_s))
    y_sample = _merge_and_ffn(x_sample, o_att_s, h_s, om_s, ga_s, gb_s, w_att_out, w_m_out, w_o,
                              norm2_g, w_gate, w_up, w_down)
    return (y_prompt, y_sample, kv_p[0], kv_p[1], kv_p[2], C_p, n_p, m_p,
            kv_s[0], kv_s[1], kv_s[2], C_s, n_s, m_s)
```

```python
import functools
import math

import jax
import jax.numpy as jnp
import numpy as np
from jax import lax
from jax.experimental import pallas as pl
from jax.experimental.pallas import tpu as pltpu

F32 = jnp.float32
BF16 = jnp.bfloat16

HEAD_DIM = 64
HEADS = 8
GROUP_W = HEADS * HEAD_DIM
N_GROUPS = 3
WINDOWS = (128, 512, 2048)
DILATIONS = (1, 4, 16)
SPAN = 128
ROT_DIM = 16
ROPE_THETA = 500000.0
M_HEADS = 8
M_DQK = 64
M_DV = 128
NORM_EPS = 1e-6
PAIRS = GROUP_W // 128
NEG = -1e30

LANES = 128
VMEM_LIMIT = 56 * 1024 * 1024

_ATT_W = N_GROUPS * GROUP_W
_C_QM = 3 * _ATT_W
_C_KM = _C_QM + M_HEADS * M_DQK
_C_VM = _C_KM + M_HEADS * M_DQK
_C_OM = _C_VM + M_HEADS * M_DV
_C_GA = _C_OM + M_HEADS * M_DV
_C_GB = _C_GA + 1024
_C_GIF = _C_GB + 1024
_W_COLS = _C_GIF + LANES


def _const_spec(shape):
    nd = len(shape)
    return pl.BlockSpec(shape, lambda *_: (0,) * nd, pipeline_mode=pl.Buffered(1))


def _log_sigmoid(x):
    return jnp.minimum(x, 0.0) - jnp.log1p(jnp.exp(-jnp.abs(x)))


def _proj_kernel(x_ref, g1_ref, w_ref, bif_ref, qg_ref, kg_ref, ra_ref, rm_ref, rp_ref, gm_ref,
                 q0_ref, q1_ref, q2_ref, k0_ref, k1_ref, k2_ref, v0_ref, v1_ref, v2_ref,
                 qm_ref, km_ref, vm_ref, om_ref, ga_ref, gb_ref, gif_ref,
                 hs_ref, *, tm, dils):
    d_model = x_ref.shape[1]
    x = x_ref[...]
    xn = x * lax.rsqrt(jnp.mean(x * x, axis=-1, keepdims=True) + NORM_EPS) * g1_ref[...]
    h_nat = xn.astype(BF16)
    n_slab = d_model // LANES
    if any(d > 1 for d in dils):
        for c in range(n_slab):
            hs_ref[c] = xn[:, c * LANES:(c + 1) * LANES]

    def permuted_h(d):
        if d == 1:
            return h_nat
        n = tm // d
        rows = [jnp.concatenate([hs_ref[c, pl.ds(r, n, stride=d), :] for c in range(n_slab)], axis=1)
                for r in range(d)]
        return jnp.concatenate(rows, axis=0).astype(BF16)

    def permuted_tab(ref, d):
        if d == 1:
            t = ref[...]
        else:
            n = tm // d
            t = jnp.concatenate([ref[pl.ds(r, n, stride=d), :] for r in range(d)], axis=0)
        return jnp.concatenate([t] * PAIRS, axis=1)

    gmat = gm_ref[...]

    def norm_rope(z, gain, ra, rm, rp):
        zz = (z * z).astype(BF16)
        half = GROUP_W // 2
        ss = jnp.concatenate(
            [jnp.dot(zz[:, :half], gmat, preferred_element_type=F32),
             jnp.dot(zz[:, half:], gmat, preferred_element_type=F32)], axis=1)
        y = z * lax.rsqrt(ss * (1.0 / HEAD_DIM) + NORM_EPS) * gain
        return (y * ra + pltpu.roll(y, GROUP_W - ROT_DIM // 2, 1) * rm
                + pltpu.roll(y, ROT_DIM // 2, 1) * rp)

    def store_pairs(ref, y, d):
        n = tm // d
        for p in range(PAIRS):
            ref[:, p] = y[:, p * LANES:(p + 1) * LANES].reshape(d, n, LANES).astype(ref.dtype)

    q_refs = (q0_ref, q1_ref, q2_ref)
    k_refs = (k0_ref, k1_ref, k2_ref)
    v_refs = (v0_ref, v1_ref, v2_ref)
    for g in range(N_GROUPS):
        d = dils[g]
        hg = permuted_h(d)
        ra, rm, rp = (permuted_tab(r, d) for r in (ra_ref, rm_ref, rp_ref))
        base = g * _ATT_W
        zq = jnp.dot(hg, w_ref[:, base:base + GROUP_W], preferred_element_type=F32)
        store_pairs(q_refs[g], norm_rope(zq, qg_ref[...], ra, rm, rp), d)
        zk = jnp.dot(hg, w_ref[:, base + GROUP_W:base + 2 * GROUP_W], preferred_element_type=F32)
        store_pairs(k_refs[g], norm_rope(zk, kg_ref[...], ra, rm, rp), d)
        zv = jnp.dot(hg, w_ref[:, base + 2 * GROUP_W:base + 3 * GROUP_W], preferred_element_type=F32)
        store_pairs(v_refs[g], zv, d)

    for ref, c0 in ((qm_ref, _C_QM), (km_ref, _C_KM), (vm_ref, _C_VM), (om_ref, _C_OM),
                    (ga_ref, _C_GA), (gb_ref, _C_GB)):
        width = ref.shape[1]
        for cc in range(0, width, GROUP_W):
            z = jnp.dot(h_nat, w_ref[:, c0 + cc:c0 + cc + GROUP_W], preferred_element_type=F32)
            ref[:, cc:cc + GROUP_W] = z.astype(ref.dtype)
    zg = jnp.dot(h_nat, w_ref[:, _C_GIF:_C_GIF + LANES], preferred_element_type=F32)
    gif_ref[...] = zg + bif_ref[...]


def _proj(x2d, seq_len, dils, tm, act_dtype, g1, w_perm, bif, qg, kg, rope_tabs, gmat):
    m_rows, d_model = x2d.shape
    nb = m_rows // seq_len
    tiles_per_seq = seq_len // tm
    grid = (m_rows // tm,)
    row_spec = lambda w: pl.BlockSpec((tm, w), lambda i: (i, 0))
    tab_spec = pl.BlockSpec((tm, LANES), lambda i: (i % tiles_per_seq, 0))

    def grp_spec(d):
        return pl.BlockSpec((None, d, PAIRS, tm // d, LANES),
                            lambda i: (i // tiles_per_seq, 0, 0, i % tiles_per_seq, 0))

    def grp_shape(d):
        return jax.ShapeDtypeStruct((nb, d, PAIRS, seq_len // d, LANES), act_dtype)

    out_shape = ([grp_shape(d) for d in dils] * 3
                 + [jax.ShapeDtypeStruct((m_rows, w), act_dtype)
                    for w in (512, 512, 1024, 1024, 1024, 1024)]
                 + [jax.ShapeDtypeStruct((m_rows, LANES), F32)])
    out_specs = ([grp_spec(d) for d in dils] * 3
                 + [row_spec(w) for w in (512, 512, 1024, 1024, 1024, 1024)]
                 + [row_spec(LANES)])
    in_specs = [row_spec(d_model), _const_spec((1, d_model)), _const_spec(w_perm.shape),
                _const_spec((1, LANES)), _const_spec((1, GROUP_W)), _const_spec((1, GROUP_W)),
                tab_spec, tab_spec, tab_spec, _const_spec(gmat.shape)]
    return pl.pallas_call(
        functools.partial(_proj_kernel, tm=tm, dils=dils),
        grid=grid, in_specs=in_specs, out_specs=out_specs, out_shape=out_shape,
        scratch_shapes=[pltpu.VMEM((d_model // LANES, tm, LANES), F32)],
        compiler_params=pltpu.CompilerParams(dimension_semantics=("arbitrary",),
                                             vmem_limit_bytes=VMEM_LIMIT),
        name="proj",
    )(x2d, g1, w_perm, bif, qg, kg, *rope_tabs, gmat)


ATT_STEP = SPAN * max(DILATIONS)


def _attn_kernel(q0, k0, v0, kh0, vh0, q1, k1, v1, kh1, vh1, q2, k2, v2, kh2, vh2,
                 o_ref, kb0, vb0, kb1, vb1, kb2, vb2, acc_s, m_s, l_s):
    j = pl.program_id(1)
    for kb, vb, kh, vh, kc, vc in ((kb0, vb0, kh0, vh0, k0, v0), (kb1, vb1, kh1, vh1, k1, v1),
                                   (kb2, vb2, kh2, vh2, k2, v2)):
        kb[:, 0:SPAN, :] = kh[...]
        kb[:, SPAN:, :] = kc[...]
        vb[:, 0:SPAN, :] = vh[...]
        vb[:, SPAN:, :] = vc[...]

    qi = lax.broadcasted_iota(jnp.int32, (SPAN, 2 * SPAN), 0)
    ci = lax.broadcasted_iota(jnp.int32, (SPAN, 2 * SPAN), 1)
    band = (ci >= qi) & (ci <= qi + SPAN)
    bias_band = jnp.where(band, 0.0, NEG).astype(F32)
    bias_first = jnp.where(band & (ci >= SPAN), 0.0, NEG).astype(F32)
    lane_q = lax.broadcasted_iota(jnp.int32, (SPAN, LANES), 1)
    lane_kv = lax.broadcasted_iota(jnp.int32, (2 * SPAN, LANES), 1)
    halves_q = (lane_q < HEAD_DIM, lane_q >= HEAD_DIM)
    halves_kv = (lane_kv < HEAD_DIM, lane_kv >= HEAD_DIM)
    scale = HEAD_DIM ** -0.5

    def block(q_ref, kb, vb, r, bi):
        row0 = pl.multiple_of(bi * SPAN, SPAN)
        q2_ = q_ref[r, pl.ds(row0, SPAN), :]
        kk = kb[r, pl.ds(row0, 2 * SPAN), :]
        vv = vb[r, pl.ds(row0, 2 * SPAN), :]
        first = jnp.logical_and(j == 0, bi == 0)
        bias = jnp.where(first, bias_first, bias_band)
        accs, mxs, ls = [], [], []
        for hh in range(2):
            qa = jnp.where(halves_q[hh], q2_, jnp.zeros_like(q2_)) * jnp.asarray(scale, q2_.dtype)
            s = lax.dot_general(qa, kk, (((1,), (1,)), ((), ())), preferred_element_type=F32) + bias
            mx = jnp.max(s, axis=-1, keepdims=True)
            p = jnp.exp(s - mx)
            ls.append(jnp.sum(p, axis=-1, keepdims=True))
            mxs.append(mx)
            vh = jnp.where(halves_kv[hh], vv, jnp.zeros_like(vv))
            accs.append(jnp.dot(p.astype(BF16), vh, preferred_element_type=F32))
        acc = accs[0] + accs[1]
        m_e = jnp.where(halves_q[0], mxs[0], mxs[1])
        l_e = jnp.where(halves_q[0], ls[0], ls[1])
        return acc, m_e, l_e

    def store(g, sl, vals):
        acc, m_e, l_e = vals
        acc_s[g, sl, :] = acc
        m_s[g, sl, :] = m_e
        l_s[g, sl, :] = l_e

    d0, d1, d2 = DILATIONS
    nb0, nb1, nb2 = (ATT_STEP // d // SPAN for d in DILATIONS)

    def body0(bi, c):
        store(0, pl.ds(pl.multiple_of(bi * SPAN, SPAN), SPAN), block(q0, kb0, vb0, 0, bi))
        return c
    lax.fori_loop(0, nb0, body0, 0)

    def body1(idx, c):
        r = idx // nb1
        bi = idx % nb1
        store(1, pl.ds(bi * SPAN * d1 + r, SPAN, stride=d1), block(q1, kb1, vb1, r, bi))
        return c
    lax.fori_loop(0, d1 * nb1, body1, 0)

    def body2(r, c):
        store(2, pl.ds(r, SPAN, stride=d2), block(q2, kb2, vb2, r, 0))
        return c
    lax.fori_loop(0, d2 * nb2, body2, 0)

    def combine(ci_, c):
        sl = pl.ds(pl.multiple_of(ci_ * SPAN, SPAN), SPAN)
        ms = [m_s[g, sl, :] for g in range(N_GROUPS)]
        mm = jnp.maximum(jnp.maximum(ms[0], ms[1]), ms[2])
        es = [jnp.exp(m - mm) for m in ms]
        num = es[0] * acc_s[0, sl, :] + es[1] * acc_s[1, sl, :] + es[2] * acc_s[2, sl, :]
        den = es[0] * l_s[0, sl, :] + es[1] * l_s[1, sl, :] + es[2] * l_s[2, sl, :]
        o_ref[sl, :] = (num / den).astype(o_ref.dtype)
        return c
    lax.fori_loop(0, ATT_STEP // SPAN, combine, 0)


def _attention(qs, ks, vs, seq_len):
    nb = qs[0].shape[0]
    steps = seq_len // ATT_STEP
    in_specs, args, scratch = [], [], []
    for g, d in enumerate(DILATIONS):
        rows = ATT_STEP // d
        cur = pl.BlockSpec((None, d, None, rows, LANES), lambda b, j, p: (b, 0, p, j, 0))
        ratio = rows // SPAN
        halo = pl.BlockSpec((None, d, None, SPAN, LANES),
                            lambda b, j, p, ratio=ratio: (b, 0, p, jnp.maximum(j * ratio - 1, 0), 0))
        in_specs += [cur, cur, cur, halo, halo]
        args += [qs[g], ks[g], vs[g], ks[g], vs[g]]
        scratch += [pltpu.VMEM((d, SPAN + rows, LANES), BF16)] * 2
    scratch += [pltpu.VMEM((N_GROUPS, ATT_STEP, LANES), F32)] * 3
    return pl.pallas_call(
        _attn_kernel,
        grid=(nb, steps, PAIRS),
        in_specs=in_specs,
        out_specs=pl.BlockSpec((None, None, ATT_STEP, LANES), lambda b, j, p: (b, p, j, 0)),
        out_shape=jax.ShapeDtypeStruct((nb, PAIRS, seq_len, LANES), BF16),
        scratch_shapes=scratch,
        compiler_params=pltpu.CompilerParams(
            dimension_semantics=("arbitrary", "arbitrary", "arbitrary"),
            vmem_limit_bytes=VMEM_LIMIT),
        name="attn",
    )(*args)


def _mlstm_kernel(q_ref, k_ref, v_ref, om_ref, gif_ref, gt_ref, hm_ref, st_ref, mo_ref,
                  st_s, m_s, *, lc):
    j = pl.program_id(1)

    @pl.when(j == 0)
    def _():
        st_s[...] = jnp.zeros_like(st_s)
        m_s[...] = jnp.zeros_like(m_s)

    gif = gif_ref[...]
    gt = gt_ref[...]
    lf_c = _log_sigmoid(gif)
    lf_r = _log_sigmoid(gt[M_HEADS:2 * M_HEADS, :])
    row = lax.broadcasted_iota(jnp.int32, (lc, lc), 0)
    col = lax.broadcasted_iota(jnp.int32, (lc, lc), 1)
    causal = col <= row
    tril = causal.astype(F32)
    triu = (row <= col).astype(F32)
    b_c = jnp.dot(tril, lf_c, precision=lax.Precision.HIGHEST, preferred_element_type=F32)
    b_r = jnp.dot(lf_r, triu, precision=lax.Precision.HIGHEST, preferred_element_type=F32)
    lane = lax.broadcasted_iota(jnp.int32, (lc, LANES), 1)
    halves = (lane < M_DQK, lane >= M_DQK)
    sub = lax.broadcasted_iota(jnp.int32, (LANES, 1), 0)
    ones_blk = jnp.ones((lc, LANES), BF16)
    qscale = M_DQK ** -0.5

    for p in range(M_HEADS // 2):
        q2_ = q_ref[:, p * LANES:(p + 1) * LANES]
        k2_ = k_ref[:, p * LANES:(p + 1) * LANES]
        st = st_s[p]
        st_b = st.astype(BF16)
        upds, dstates = [], []
        for hh in range(2):
            h = 2 * p + hh
            qa = jnp.where(halves[hh], q2_, jnp.zeros_like(q2_)) * jnp.asarray(qscale, q2_.dtype)
            ka = jnp.where(halves[hh], k2_, jnp.zeros_like(k2_))
            s = lax.dot_general(qa, k2_, (((1,), (1,)), ((), ())), preferred_element_type=F32)
            bcol = b_c[:, M_HEADS + h:M_HEADS + h + 1]
            igcol = gif[:, h:h + 1]
            brow = b_r[h:h + 1, :]
            igrow = gt[h:h + 1, :]
            dmat = jnp.where(causal, bcol - brow + igrow, NEG)
            m_prev = m_s[h:h + 1, 0:1]
            inter = bcol + m_prev
            m_t = jnp.maximum(inter, jnp.max(dmat, axis=-1, keepdims=True))
            w = jnp.exp(dmat - m_t)
            dec = jnp.exp(inter - m_t)
            qc = jnp.dot(qa, st_b, preferred_element_type=F32)
            v1 = jnp.concatenate([v_ref[:, h * M_DV:(h + 1) * M_DV], ones_blk], axis=1)
            num_den = dec * qc + jnp.dot((w * s).astype(BF16), v1, preferred_element_type=F32)
            num = num_den[:, :M_DV]
            den = num_den[:, M_DV:]
            hval = num / jnp.maximum(jnp.abs(den), jnp.exp(-m_t))
            gate = jax.nn.sigmoid(om_ref[:, h * M_DV:(h + 1) * M_DV].astype(F32))
            hm_ref[:, h * M_DV:(h + 1) * M_DV] = (gate * hval).astype(hm_ref.dtype)
            m_new = m_t[lc - 1:lc, :]
            b_last = bcol[lc - 1:lc, :]
            dstates.append(jnp.exp(b_last + m_prev - m_new))
            ws = jnp.exp(b_last - bcol + igcol - m_new)
            kws = (ka.astype(F32) * ws).astype(BF16)
            upds.append(lax.dot_general(kws, v1, (((0,), (0,)), ((), ())),
                                        preferred_element_type=F32))
            m_s[h:h + 1, :] = jnp.broadcast_to(m_new, (1, LANES))
        drow = jnp.where(sub < M_DQK, dstates[0], dstates[1])
        st_s[p] = drow * st + upds[0] + upds[1]

    @pl.when(j == pl.num_programs(1) - 1)
    def _():
        st_ref[...] = st_s[...]
        mo_ref[...] = m_s[...]


def _mlstm(qm, km, vm, om, gif, gif_t, nb, seq_len, lc):
    nc = seq_len // lc
    row = lambda w: pl.BlockSpec((lc, w), lambda b, j: (b * nc + j, 0))
    return pl.pallas_call(
        functools.partial(_mlstm_kernel, lc=lc),
        grid=(nb, nc),
        in_specs=[row(512), row(512), row(1024), row(1024), row(LANES),
                  pl.BlockSpec((2 * M_HEADS, lc), lambda b, j: (0, b * nc + j))],
        out_specs=[row(1024),
                   pl.BlockSpec((None, M_HEADS // 2, LANES, 2 * LANES), lambda b, j: (b, 0, 0, 0)),
                   pl.BlockSpec((None, M_HEADS, LANES), lambda b, j: (b, 0, 0))],
        out_shape=[jax.ShapeDtypeStruct((nb * seq_len, 1024), BF16),
                   jax.ShapeDtypeStruct((nb, M_HEADS // 2, LANES, 2 * LANES), F32),
                   jax.ShapeDtypeStruct((nb, M_HEADS, LANES), F32)],
        scratch_shapes=[pltpu.VMEM((M_HEADS // 2, LANES, 2 * LANES), F32),
                        pltpu.VMEM((M_HEADS, LANES), F32)],
        compiler_params=pltpu.CompilerParams(dimension_semantics=("arbitrary", "arbitrary"),
                                             vmem_limit_bytes=VMEM_LIMIT),
        name="mlstm",
    )(qm, km, vm, om, gif, gif_t)


FF_CHUNK = 256


def _merge_ffn_kernel(x_ref, o_ref, hm_ref, ga_ref, gb_ref, wa_ref, wm_ref, wo_ref, g2_ref,
                      wg_ref, wu_ref, wd_ref, y_ref):
    o_att = jnp.concatenate([o_ref[p] for p in range(PAIRS)], axis=1).astype(BF16)
    ya = jnp.dot(o_att, wa_ref[...], preferred_element_type=F32)
    yb = jnp.dot(hm_ref[...].astype(BF16), wm_ref[...], preferred_element_type=F32)
    mixed = (jax.nn.sigmoid(ga_ref[...].astype(F32)) * ya
             + jax.nn.sigmoid(gb_ref[...].astype(F32)) * yb)
    x2 = x_ref[...] + jnp.dot(mixed.astype(BF16), wo_ref[...], preferred_element_type=F32)
    h2 = (x2 * lax.rsqrt(jnp.mean(x2 * x2, axis=-1, keepdims=True) + NORM_EPS)
          * g2_ref[...]).astype(BF16)
    acc = x2
    d_ff = wg_ref.shape[1]
    for c in range(0, d_ff, FF_CHUNK):
        gt = jnp.dot(h2, wg_ref[:, c:c + FF_CHUNK], preferred_element_type=F32)
        up = jnp.dot(h2, wu_ref[:, c:c + FF_CHUNK], preferred_element_type=F32)
        ff = (gt * jax.nn.sigmoid(gt) * up).astype(BF16)
        acc = acc + jnp.dot(ff, wd_ref[c:c + FF_CHUNK, :], preferred_element_type=F32)
    y_ref[...] = acc


def _merge_ffn(x2d, o_att, hm, ga, gb, wa, wm, wo, g2, wg, wu, wd, seq_len, tm):
    m_rows, d_model = x2d.shape
    tiles_per_seq = seq_len // tm
    row = lambda w: pl.BlockSpec((tm, w), lambda i: (i, 0))
    o_spec = pl.BlockSpec((None, PAIRS, tm, LANES),
                          lambda i: (i // tiles_per_seq, 0, i % tiles_per_seq, 0))
    return pl.pallas_call(
        _merge_ffn_kernel,
        grid=(m_rows // tm,),
        in_specs=[row(d_model), o_spec, row(1024), row(1024), row(1024),
                  _const_spec(wa.shape), _const_spec(wm.shape), _const_spec(wo.shape),
                  _const_spec((1, d_model)), _const_spec(wg.shape), _const_spec(wu.shape),
                  _const_spec(wd.shape)],
        out_specs=row(d_model),
        out_shape=jax.ShapeDtypeStruct((m_rows, d_model), F32),
        compiler_params=pltpu.CompilerParams(dimension_semantics=("arbitrary",),
                                             vmem_limit_bytes=VMEM_LIMIT),
        name="merge_ffn",
    )(x2d, o_att, hm, ga, gb, wa, wm, wo, g2, wg, wu, wd)


def _sample_attn_kernel(q_ref, k_ref, v_ref, c0_ref, c1_ref, c2_ref, o_ref):
    scale = HEAD_DIM ** -0.5
    parts = []
    for g, c_ref in enumerate((c0_ref, c1_ref, c2_ref)):
        q = q_ref[g] * scale
        kc = c_ref[:, 0]
        vc = c_ref[:, 1]
        s = jnp.sum(kc * q[None], axis=-1, keepdims=True)
        s_new = jnp.sum(k_ref[g] * q, axis=-1, keepdims=True)
        mx = jnp.maximum(jnp.max(s, axis=0), s_new)
        p = jnp.exp(s - mx[None])
        p_new = jnp.exp(s_new - mx)
        l = jnp.sum(p, axis=0) + p_new
        acc = jnp.sum(p * vc, axis=0) + p_new * v_ref[g]
        parts.append((mx, l, acc))
    mm = jnp.maximum(jnp.maximum(parts[0][0], parts[1][0]), parts[2][0])
    es = [jnp.exp(pt[0] - mm) for pt in parts]
    num = es[0] * parts[0][2] + es[1] * parts[1][2] + es[2] * parts[2][2]
    den = es[0] * parts[0][1] + es[1] * parts[1][1] + es[2] * parts[2][1]
    o_ref[...] = num / den


def _sample_attention(q_s, k_s, v_s, caches):
    db = q_s.shape[0]
    new_spec = pl.BlockSpec((None, N_GROUPS, HEADS, HEAD_DIM), lambda b: (b, 0, 0, 0))
    c_specs, c_args = [], []
    for g, d in enumerate(DILATIONS):
        wb = caches[g].shape[1]
        assert wb == SPAN * d
        c_args.append(caches[g].reshape(db, SPAN, d, 2, HEADS, HEAD_DIM))
        c_specs.append(pl.BlockSpec((None, SPAN, None, 2, HEADS, HEAD_DIM),
                                    lambda b: (b, 0, 0, 0, 0, 0)))
    return pl.pallas_call(
        _sample_attn_kernel,
        grid=(db,),
        in_specs=[new_spec, new_spec, new_spec] + c_specs,
        out_specs=pl.BlockSpec((None, HEADS, HEAD_DIM), lambda b: (b, 0, 0)),
        out_shape=jax.ShapeDtypeStruct((db, HEADS, HEAD_DIM), F32),
        compiler_params=pltpu.CompilerParams(dimension_semantics=("arbitrary",),
                                             vmem_limit_bytes=VMEM_LIMIT),
        name="sample_attn",
    )(q_s, k_s, v_s, *c_args)


def _cache_shift_kernel(c0, c1, c2, n0, n1, n2, o0, o1, o2, sems):
    copies = []
    for g, (c, n, o) in enumerate(((c0, n0, o0), (c1, n1, o1), (c2, n2, o2))):
        wb = c.shape[1]
        copies.append(pltpu.make_async_copy(c.at[:, pl.ds(1, wb - 1)], o.at[:, pl.ds(0, wb - 1)],
                                            sems.at[2 * g]))
        copies.append(pltpu.make_async_copy(n, o.at[:, pl.ds(wb - 1, 1)], sems.at[2 * g + 1]))
    for cp in copies:
        cp.start()
    for cp in copies:
        cp.wait()


def _cache_shift(caches, new_rows):
    any_spec = pl.BlockSpec(memory_space=pl.ANY)
    return pl.pallas_call(
        _cache_shift_kernel,
        in_specs=[any_spec] * 6,
        out_specs=[any_spec] * 3,
        out_shape=[jax.ShapeDtypeStruct(c.shape, c.dtype) for c in caches],
        scratch_shapes=[pltpu.SemaphoreType.DMA((6,))],
        name="cache_shift",
    )(*caches, *new_rows)


def _sample_mlstm_kernel(q_ref, k_ref, v_ref, om_ref, gt_ref, mt_ref, n_ref, c_ref, rexp_ref,
                         hm_ref, co_ref, nt_ref, mo_ref, *, db):
    nh, dqk = M_HEADS, M_DQK
    hi = lax.Precision.HIGHEST
    rexp = rexp_ref[...]
    q_t = q_ref[...].T * (dqk ** -0.5)
    k_t = k_ref[...].T
    n_t = n_ref[...].T
    ig = gt_ref[0:nh, :]
    lf = _log_sigmoid(gt_ref[nh:2 * nh, :])
    m_prev = mt_ref[...]
    m_new = jnp.maximum(lf + m_prev, ig)
    w8 = jnp.exp(ig - m_new)
    dec8 = jnp.exp(lf + m_prev - m_new)
    head_sum = lambda a: lax.dot_general(rexp, a, (((0,), (0,)), ((), ())), precision=hi,
                                         preferred_element_type=F32)
    expand = lambda a: jnp.dot(rexp, a, precision=hi, preferred_element_type=F32)
    qk8 = head_sum(q_t * k_t)
    qn8 = head_sum(q_t * n_t)
    wqk8 = w8 * qk8
    den8 = dec8 * qn8 + wqk8
    inv8 = 1.0 / jnp.maximum(jnp.abs(den8), jnp.exp(-m_new))
    dec_x = expand(dec8)
    w_x = expand(w8)
    nt_ref[...] = dec_x * n_t + w_x * k_t
    mo_ref[...] = m_new
    wk_x = w_x * k_t
    for b in range(db):
        cb = c_ref[b].reshape(nh * dqk, M_DV)
        v_b = v_ref[b]
        v_x = jnp.concatenate([jnp.broadcast_to(v_b[h:h + 1, :], (dqk, M_DV)) for h in range(nh)],
                              axis=0)
        co_ref[b] = (dec_x[:, b:b + 1] * cb + wk_x[:, b:b + 1] * v_x).reshape(nh, dqk, M_DV)
        qc = jnp.sum((q_t[:, b:b + 1] * cb).reshape(nh, dqk, M_DV), axis=1)
        num = dec8[:, b:b + 1] * qc + wqk8[:, b:b + 1] * v_b
        hm_ref[b] = jax.nn.sigmoid(om_ref[b]) * (num * inv8[:, b:b + 1])


def _sample_mlstm(q_pad, k_pad, v3, om3, gif_t, m_t, n_pad, c_state, rexp, db):
    vm = pl.BlockSpec(memory_space=pltpu.VMEM)
    return pl.pallas_call(
        functools.partial(_sample_mlstm_kernel, db=db),
        in_specs=[vm] * 9,
        out_specs=[vm] * 4,
        out_shape=[jax.ShapeDtypeStruct((db, M_HEADS, M_DV), F32),
                   jax.ShapeDtypeStruct(c_state.shape, F32),
                   jax.ShapeDtypeStruct((M_HEADS * M_DQK, LANES), F32),
                   jax.ShapeDtypeStruct((M_HEADS, LANES), F32)],
        compiler_params=pltpu.CompilerParams(vmem_limit_bytes=VMEM_LIMIT),
        name="sample_mlstm",
    )(q_pad, k_pad, v3, om3, gif_t, m_t, n_pad, c_state, rexp)


def _rope_tables(pos):
    half = ROT_DIM // 2
    inv_freq = jnp.exp(-math.log(ROPE_THETA) * jnp.arange(0, ROT_DIM, 2, dtype=F32) / ROT_DIM)
    ang = pos[:, None] * inv_freq[None, :]
    cos, sin = jnp.cos(ang), jnp.sin(ang)
    t = pos.shape[0]
    rest = HEAD_DIM - ROT_DIM
    a = jnp.concatenate([cos, cos, jnp.ones((t, rest), F32)], axis=1)
    bm = jnp.concatenate([-sin, jnp.zeros((t, HEAD_DIM - half), F32)], axis=1)
    bp = jnp.concatenate([jnp.zeros((t, half), F32), sin, jnp.zeros((t, rest), F32)], axis=1)
    return tuple(jnp.concatenate([x, x], axis=1) for x in (a, bm, bp))


def _repack_w_in(w_in):
    d_model = w_in.shape[0]
    aw = _ATT_W
    cols = []
    for g in range(N_GROUPS):
        for base in (0, aw, 2 * aw):
            cols.append(w_in[:, base + g * GROUP_W: base + (g + 1) * GROUP_W])
    gif0 = 3 * aw + 2 * M_HEADS * M_DQK + 2 * M_HEADS * M_DV
    cols.append(w_in[:, 3 * aw:gif0])
    cols.append(w_in[:, gif0 + 2 * M_HEADS:])
    cols.append(w_in[:, gif0:gif0 + 2 * M_HEADS])
    cols.append(jnp.zeros((d_model, LANES - 2 * M_HEADS), w_in.dtype))
    w = jnp.concatenate(cols, axis=1).astype(BF16)
    assert w.shape[1] == _W_COLS
    return w


def kernel(x_prompt, x_sample, cache_kv_w128, cache_kv_w512, cache_kv_w2048, state_mlstm_C, state_mlstm_n, state_mlstm_m, norm1_g, w_in, b_if, q_norm_g, k_norm_g, w_att_out, w_m_out, w_o, norm2_g, w_gate, w_up, w_down):
    nb, seq_len, d_model = x_prompt.shape
    db, dec_seq, _ = x_sample.shape
    assert dec_seq == 1 and d_model == 1024 and seq_len % ATT_STEP == 0 and db <= LANES
    caches = (cache_kv_w128, cache_kv_w512, cache_kv_w2048)
    past_len = 8192

    w_perm = _repack_w_in(w_in)
    g1 = norm1_g.reshape(1, d_model)
    g2 = norm2_g.reshape(1, d_model)
    bif = jnp.concatenate([b_if, jnp.zeros((LANES - b_if.shape[0],), F32)]).reshape(1, LANES)
    qg = jnp.tile(q_norm_g, HEADS).reshape(1, GROUP_W)
    kg = jnp.tile(k_norm_g, HEADS).reshape(1, GROUP_W)
    hid = np.arange(GROUP_W // 2) // HEAD_DIM
    gmat = jnp.asarray(hid[:, None] == hid[None, :], dtype=BF16)
    wa, wm, wo = (w.astype(BF16) for w in (w_att_out, w_m_out, w_o))
    wg, wu, wd = (w.astype(BF16) for w in (w_gate, w_up, w_down))

    m_rows = nb * seq_len
    x2d = x_prompt.reshape(m_rows, d_model)
    tabs_p = _rope_tables(jnp.arange(seq_len, dtype=F32))
    outs = _proj(x2d, seq_len, DILATIONS, 256, BF16, g1, w_perm, bif, qg, kg, tabs_p, gmat)
    qs, ks, vs = outs[0:3], outs[3:6], outs[6:9]
    qm, km, vm, om, ga, gb, gif = outs[9:16]

    o_att = _attention(qs, ks, vs, seq_len)
    gif_t = gif[:, :2 * M_HEADS].T
    hm, st_p, m_p = _mlstm(qm, km, vm, om, gif, gif_t, nb, seq_len, 256)
    y_prompt = _merge_ffn(x2d, o_att, hm, ga, gb, wa, wm, wo, g2, wg, wu, wd, seq_len, 256)
    y_prompt = y_prompt.reshape(nb, seq_len, d_model)

    kv_p = []
    for g, d in enumerate(DILATIONS):
        keep = min(WINDOWS[g], seq_len)
        n_m = keep // d
        def tail(a):
            t = a[:, :, :, seq_len // d - n_m:, :].astype(F32)
            return t.transpose(0, 3, 1, 2, 4).reshape(nb, keep, HEADS, HEAD_DIM)
        kv_p.append(jnp.stack([tail(ks[g]), tail(vs[g])], axis=2))
    c_p = st_p[..., :M_DV].reshape(nb, M_HEADS, M_DQK, M_DV)
    n_p = st_p[..., M_DV].reshape(nb, M_HEADS, M_DQK)
    m_pr = m_p[:, :, 0]

    x_s = jnp.zeros((LANES, d_model), F32).at[:db].set(x_sample.reshape(db, d_model))
    tabs_s = _rope_tables(jnp.full((LANES,), float(past_len), F32))
    outs_s = _proj(x_s, LANES, (1, 1, 1), LANES, F32, g1, w_perm, bif, qg, kg, tabs_s, gmat)
    unpair = lambda a: a[0, 0].transpose(1, 0, 2).reshape(LANES, GROUP_W)[:db]
    q_sg = jnp.stack([unpair(a) for a in outs_s[0:3]], axis=1).reshape(db, N_GROUPS, HEADS, HEAD_DIM)
    k_sg = jnp.stack([unpair(a) for a in outs_s[3:6]], axis=1).reshape(db, N_GROUPS, HEADS, HEAD_DIM)
    v_sg = jnp.stack([unpair(a) for a in outs_s[6:9]], axis=1).reshape(db, N_GROUPS, HEADS, HEAD_DIM)
    qm_s, km_s, vm_s, om_s, ga_s, gb_s, gif_s = outs_s[9:16]

    o_att_s = _sample_attention(q_sg, k_sg, v_sg, caches)
    new_rows = [jnp.stack([k_sg[:, g], v_sg[:, g]], axis=1)[:, None] for g in range(N_GROUPS)]
    kv_s = _cache_shift(caches, new_rows)

    rexp = jnp.asarray(np.arange(M_HEADS * M_DQK)[:, None] // M_DQK == np.arange(M_HEADS)[None, :],
                       dtype=F32)
    pad_cols = lambda a: jnp.zeros((a.shape[0], LANES), F32).at[:, :db].set(a)
    m_t = pad_cols(state_mlstm_m.T)
    n_pad = jnp.zeros((LANES, M_HEADS * M_DQK), F32).at[:db].set(state_mlstm_n.reshape(db, -1))
    hm_s3, c_s, n_t, m_so = _sample_mlstm(
        qm_s, km_s, vm_s[:db].reshape(db, M_HEADS, M_DV), om_s[:db].reshape(db, M_HEADS, M_DV),
        gif_s[:, :2 * M_HEADS].T, m_t, n_pad, state_mlstm_C, rexp, db)
    n_s = n_t.T[:db].reshape(db, M_HEADS, M_DQK)
    m_s = m_so[:, :db].T

    o_att_sp = jnp.zeros((LANES, GROUP_W), F32).at[:db].set(o_att_s.reshape(db, GROUP_W))
    o_att_sp = o_att_sp.reshape(LANES, PAIRS, LANES).transpose(1, 0, 2)[None]
    hm_sp = jnp.zeros((LANES, 1024), F32).at[:db].set(hm_s3.reshape(db, 1024))
    y_s = _merge_ffn(x_s, o_att_sp, hm_sp, ga_s, gb_s, wa, wm, wo, g2, wg, wu, wd, LANES, LANES)
    y_sample = y_s[:db].reshape(db, 1, d_model)

    return (y_prompt, y_sample, kv_p[0], kv_p[1], kv_p[2], c_p, n_p, m_pr,
            kv_s[0], kv_s[1], kv_s[2], c_s, n_s, m_s)
```

```python
import functools
import math

import jax
import jax.numpy as jnp
import numpy as np
from jax import lax
from jax.experimental import pallas as pl
from jax.experimental.pallas import tpu as pltpu

F32 = jnp.float32
BF16 = jnp.bfloat16

HEAD_DIM = 64
HEADS = 8
GROUP_W = HEADS * HEAD_DIM
N_GROUPS = 3
WINDOWS = (128, 512, 2048)
DILATIONS = (1, 4, 16)
SPAN = 128
ROT_DIM = 16
ROPE_THETA = 500000.0
M_HEADS = 8
M_DQK = 64
M_DV = 128
NORM_EPS = 1e-6
PAIRS = GROUP_W // 128
NEG = -1e30

LANES = 128
VMEM_LIMIT = 56 * 1024 * 1024

_ATT_W = N_GROUPS * GROUP_W
_C_QM = 3 * _ATT_W
_C_KM = _C_QM + M_HEADS * M_DQK
_C_VM = _C_KM + M_HEADS * M_DQK
_C_OM = _C_VM + M_HEADS * M_DV
_C_GA = _C_OM + M_HEADS * M_DV
_C_GB = _C_GA + 1024
_C_GIF = _C_GB + 1024
_W_COLS = _C_GIF + LANES


def _const_spec(shape):
    nd = len(shape)
    return pl.BlockSpec(shape, lambda *_: (0,) * nd, pipeline_mode=pl.Buffered(1))


def _log_sigmoid(x):
    return jnp.minimum(x, 0.0) - jnp.log1p(jnp.exp(-jnp.abs(x)))


def _proj_kernel(x_ref, g1_ref, w_ref, bif_ref, qg_ref, kg_ref, ra_ref, rm_ref, rp_ref, gm_ref,
                 q0_ref, q1_ref, q2_ref, k0_ref, k1_ref, k2_ref, v0_ref, v1_ref, v2_ref,
                 qm_ref, km_ref, vm_ref, om_ref, ga_ref, gb_ref, gif_ref,
                 hs_ref, *, tm, dils, plain):
    d_model = x_ref.shape[1]
    x = x_ref[...]
    xn = x * lax.rsqrt(jnp.mean(x * x, axis=-1, keepdims=True) + NORM_EPS) * g1_ref[...]
    h_nat = xn.astype(BF16)
    n_slab = d_model // LANES
    if any(d > 1 for d in dils):
        for c in range(n_slab):
            hs_ref[c] = xn[:, c * LANES:(c + 1) * LANES]

    def permuted_h(d):
        if d == 1:
            return h_nat
        n = tm // d
        rows = [jnp.concatenate([hs_ref[c, pl.ds(r, n, stride=d), :] for c in range(n_slab)], axis=1)
                for r in range(d)]
        return jnp.concatenate(rows, axis=0).astype(BF16)

    def permuted_tab(ref, d):
        if d == 1:
            t = ref[...]
        else:
            n = tm // d
            t = jnp.concatenate([ref[pl.ds(r, n, stride=d), :] for r in range(d)], axis=0)
        return jnp.concatenate([t] * PAIRS, axis=1)

    gmat = gm_ref[...]

    def norm_rope(z, gain, ra, rm, rp):
        zz = (z * z).astype(BF16)
        half = GROUP_W // 2
        ss = jnp.concatenate(
            [jnp.dot(zz[:, :half], gmat, preferred_element_type=F32),
             jnp.dot(zz[:, half:], gmat, preferred_element_type=F32)], axis=1)
        y = z * lax.rsqrt(ss * (1.0 / HEAD_DIM) + NORM_EPS) * gain
        return (y * ra + pltpu.roll(y, GROUP_W - ROT_DIM // 2, 1) * rm
                + pltpu.roll(y, ROT_DIM // 2, 1) * rp)

    def store_group(ref, y, d):
        if plain:
            ref[...] = y.astype(ref.dtype)
            return
        n = tm // d
        for p in range(PAIRS):
            ref[:, p] = y[:, p * LANES:(p + 1) * LANES].reshape(d, n, LANES).astype(ref.dtype)

    q_refs = (q0_ref, q1_ref, q2_ref)
    k_refs = (k0_ref, k1_ref, k2_ref)
    v_refs = (v0_ref, v1_ref, v2_ref)
    for g in range(N_GROUPS):
        d = dils[g]
        hg = permuted_h(d)
        ra, rm, rp = (permuted_tab(r, d) for r in (ra_ref, rm_ref, rp_ref))
        base = g * _ATT_W
        zq = jnp.dot(hg, w_ref[:, base:base + GROUP_W], preferred_element_type=F32)
        store_group(q_refs[g], norm_rope(zq, qg_ref[...], ra, rm, rp), d)
        zk = jnp.dot(hg, w_ref[:, base + GROUP_W:base + 2 * GROUP_W], preferred_element_type=F32)
        store_group(k_refs[g], norm_rope(zk, kg_ref[...], ra, rm, rp), d)
        zv = jnp.dot(hg, w_ref[:, base + 2 * GROUP_W:base + 3 * GROUP_W], preferred_element_type=F32)
        store_group(v_refs[g], zv, d)

    for ref, c0, width in ((qm_ref, _C_QM, 512), (km_ref, _C_KM, 512), (vm_ref, _C_VM, 1024),
                           (om_ref, _C_OM, 1024), (ga_ref, _C_GA, 1024), (gb_ref, _C_GB, 1024)):
        for cc in range(0, width, GROUP_W):
            z = jnp.dot(h_nat, w_ref[:, c0 + cc:c0 + cc + GROUP_W], preferred_element_type=F32)
            ref[:, cc:cc + GROUP_W] = z.astype(ref.dtype)
    zg = jnp.dot(h_nat, w_ref[:, _C_GIF:_C_GIF + LANES], preferred_element_type=F32)
    gif_ref[...] = zg + bif_ref[...]


def _proj(x2d, seq_len, dils, tm, plain, g1, w_perm, bif, qg, kg, rope_tabs, gmat):
    m_rows, d_model = x2d.shape
    nb = m_rows // seq_len
    tiles_per_seq = seq_len // tm
    grid = (m_rows // tm,)
    row_spec = lambda w: pl.BlockSpec((tm, w), lambda i: (i, 0))
    tab_spec = pl.BlockSpec((tm, LANES), lambda i: (i % tiles_per_seq, 0))
    sds = jax.ShapeDtypeStruct

    widths = (512, 512, 1024, 1024, 1024, 1024)
    if plain:
        out_shape = ([sds((m_rows, GROUP_W), F32)] * 9 + [sds((m_rows, w), F32) for w in widths]
                     + [sds((m_rows, LANES), F32)])
        out_specs = ([row_spec(GROUP_W)] * 9 + [row_spec(w) for w in widths] + [row_spec(LANES)])
    else:
        def grp_spec(d):
            return pl.BlockSpec((None, d, PAIRS, tm // d, LANES),
                                lambda i: (i // tiles_per_seq, 0, 0, i % tiles_per_seq, 0))
        grp_shape = lambda d: sds((nb, d, PAIRS, seq_len // d, LANES), BF16)
        out_shape = ([grp_shape(d) for d in dils] * 3 + [sds((m_rows, w), BF16) for w in widths]
                     + [sds((m_rows, LANES), F32)])
        out_specs = ([grp_spec(d) for d in dils] * 3 + [row_spec(w) for w in widths]
                     + [row_spec(LANES)])
    in_specs = [row_spec(d_model), _const_spec((1, d_model)), _const_spec(w_perm.shape),
                _const_spec((1, LANES)), _const_spec((1, GROUP_W)), _const_spec((1, GROUP_W)),
                tab_spec, tab_spec, tab_spec, _const_spec(gmat.shape)]
    return pl.pallas_call(
        functools.partial(_proj_kernel, tm=tm, dils=dils, plain=plain),
        grid=grid, in_specs=in_specs, out_specs=out_specs, out_shape=out_shape,
        scratch_shapes=[pltpu.VMEM((d_model // LANES, tm, LANES), F32)],
        compiler_params=pltpu.CompilerParams(dimension_semantics=("arbitrary",),
                                             vmem_limit_bytes=VMEM_LIMIT),
        name="proj",
    )(x2d, g1, w_perm, bif, qg, kg, *rope_tabs, gmat)


ATT_STEP = SPAN * max(DILATIONS)


def _attn_kernel(q0, k0, v0, kh0, vh0, q1, k1, v1, kh1, vh1, q2, k2, v2, kh2, vh2,
                 o_ref, kb0, vb0, kb1, vb1, kb2, vb2, acc_s, m_s, l_s):
    j = pl.program_id(1)
    for kb, vb, kh, vh, kc, vc in ((kb0, vb0, kh0, vh0, k0, v0), (kb1, vb1, kh1, vh1, k1, v1),
                                   (kb2, vb2, kh2, vh2, k2, v2)):
        kb[:, 0:SPAN, :] = kh[...]
        kb[:, SPAN:, :] = kc[...]
        vb[:, 0:SPAN, :] = vh[...]
        vb[:, SPAN:, :] = vc[...]

    qi = lax.broadcasted_iota(jnp.int32, (SPAN, 2 * SPAN), 0)
    ci = lax.broadcasted_iota(jnp.int32, (SPAN, 2 * SPAN), 1)
    band = (ci >= qi) & (ci <= qi + SPAN)
    bias_band = jnp.where(band, 0.0, NEG).astype(F32)
    bias_first = jnp.where(band & (ci >= SPAN), 0.0, NEG).astype(F32)
    lane_q = lax.broadcasted_iota(jnp.int32, (SPAN, LANES), 1)
    lane_kv = lax.broadcasted_iota(jnp.int32, (2 * SPAN, LANES), 1)
    halves_q = (lane_q < HEAD_DIM, lane_q >= HEAD_DIM)
    halves_kv = (lane_kv < HEAD_DIM, lane_kv >= HEAD_DIM)
    scale = HEAD_DIM ** -0.5

    def block(q_ref, kb, vb, r, bi):
        row0 = pl.multiple_of(bi * SPAN, SPAN)
        q2_ = q_ref[r, pl.ds(row0, SPAN), :]
        kk = kb[r, pl.ds(row0, 2 * SPAN), :]
        vv = vb[r, pl.ds(row0, 2 * SPAN), :]
        first = jnp.logical_and(j == 0, bi == 0)
        bias = jnp.where(first, bias_first, bias_band)
        accs, mxs, ls = [], [], []
        for hh in range(2):
            qa = jnp.where(halves_q[hh], q2_, jnp.zeros_like(q2_)) * jnp.asarray(scale, q2_.dtype)
            s = lax.dot_general(qa, kk, (((1,), (1,)), ((), ())), preferred_element_type=F32) + bias
            mx = jnp.max(s, axis=-1, keepdims=True)
            p = jnp.exp(s - mx)
            ls.append(jnp.sum(p, axis=-1, keepdims=True))
            mxs.append(mx)
            vh = jnp.where(halves_kv[hh], vv, jnp.zeros_like(vv))
            accs.append(jnp.dot(p.astype(BF16), vh, preferred_element_type=F32))
        acc = accs[0] + accs[1]
        m_e = jnp.where(halves_q[0], mxs[0], mxs[1])
        l_e = jnp.where(halves_q[0], ls[0], ls[1])
        return acc, m_e, l_e

    def store(g, sl, vals):
        acc, m_e, l_e = vals
        acc_s[g, sl, :] = acc
        m_s[g, sl, :] = m_e
        l_s[g, sl, :] = l_e

    d0, d1, d2 = DILATIONS
    nb0, nb1, nb2 = (ATT_STEP // d // SPAN for d in DILATIONS)

    def body0(bi, c):
        store(0, pl.ds(pl.multiple_of(bi * SPAN, SPAN), SPAN), block(q0, kb0, vb0, 0, bi))
        return c
    lax.fori_loop(0, nb0, body0, 0)

    def body1(idx, c):
        r = idx // nb1
        bi = idx % nb1
        store(1, pl.ds(bi * SPAN * d1 + r, SPAN, stride=d1), block(q1, kb1, vb1, r, bi))
        return c
    lax.fori_loop(0, d1 * nb1, body1, 0)

    def body2(r, c):
        store(2, pl.ds(r, SPAN, stride=d2), block(q2, kb2, vb2, r, 0))
        return c
    lax.fori_loop(0, d2 * nb2, body2, 0)

    def combine(ci_, c):
        sl = pl.ds(pl.multiple_of(ci_ * SPAN, SPAN), SPAN)
        ms = [m_s[g, sl, :] for g in range(N_GROUPS)]
        mm = jnp.maximum(jnp.maximum(ms[0], ms[1]), ms[2])
        es = [jnp.exp(m - mm) for m in ms]
        num = es[0] * acc_s[0, sl, :] + es[1] * acc_s[1, sl, :] + es[2] * acc_s[2, sl, :]
        den = es[0] * l_s[0, sl, :] + es[1] * l_s[1, sl, :] + es[2] * l_s[2, sl, :]
        o_ref[sl, :] = (num / den).astype(o_ref.dtype)
        return c
    lax.fori_loop(0, ATT_STEP // SPAN, combine, 0)


def _attention(qs, ks, vs, seq_len):
    nb = qs[0].shape[0]
    steps = seq_len // ATT_STEP
    in_specs, args, scratch = [], [], []
    for g, d in enumerate(DILATIONS):
        rows = ATT_STEP // d
        cur = pl.BlockSpec((None, d, None, rows, LANES), lambda b, j, p: (b, 0, p, j, 0))
        ratio = rows // SPAN
        halo = pl.BlockSpec((None, d, None, SPAN, LANES),
                            lambda b, j, p, ratio=ratio: (b, 0, p, jnp.maximum(j * ratio - 1, 0), 0))
        in_specs += [cur, cur, cur, halo, halo]
        args += [qs[g], ks[g], vs[g], ks[g], vs[g]]
        scratch += [pltpu.VMEM((d, SPAN + rows, LANES), BF16)] * 2
    scratch += [pltpu.VMEM((N_GROUPS, ATT_STEP, LANES), F32)] * 3
    return pl.pallas_call(
        _attn_kernel,
        grid=(nb, steps, PAIRS),
        in_specs=in_specs,
        out_specs=pl.BlockSpec((None, None, ATT_STEP, LANES), lambda b, j, p: (b, p, j, 0)),
        out_shape=jax.ShapeDtypeStruct((nb, PAIRS, seq_len, LANES), BF16),
        scratch_shapes=scratch,
        compiler_params=pltpu.CompilerParams(
            dimension_semantics=("arbitrary", "arbitrary", "arbitrary"),
            vmem_limit_bytes=VMEM_LIMIT),
        name="attn",
    )(*args)


def _kv_tail_kernel(k0, v0, k1, v1, k2, v2, o0, o1, o2, nat_ref):
    for (k_ref, v_ref, o_ref, d) in ((k0, v0, o0, DILATIONS[0]), (k1, v1, o1, DILATIONS[1]),
                                     (k2, v2, o2, DILATIONS[2])):
        for kvi, ref in enumerate((k_ref, v_ref)):
            for p in range(PAIRS):
                rows = slice(p * LANES, (p + 1) * LANES)
                if d == 1:
                    o_ref[kvi, rows, :] = ref[0, p].astype(F32).T
                    continue
                for r in range(d):
                    nat_ref[pl.ds(r, SPAN, stride=d), :] = ref[r, p].astype(F32)
                for c in range(d):
                    o_ref[kvi, rows, c * SPAN:(c + 1) * SPAN] = nat_ref[c * SPAN:(c + 1) * SPAN, :].T


def _kv_tails(ks, vs, seq_len):
    nb = ks[0].shape[0]
    in_specs, args, out_specs, out_shape = [], [], [], []
    for g, d in enumerate(DILATIONS):
        last = seq_len // d // SPAN - 1
        spec = pl.BlockSpec((None, d, PAIRS, SPAN, LANES), lambda b, last=last: (b, 0, 0, last, 0))
        in_specs += [spec, spec]
        args += [ks[g], vs[g]]
        out_specs.append(pl.BlockSpec((None, 2, GROUP_W, SPAN * d), lambda b: (b, 0, 0, 0)))
        out_shape.append(jax.ShapeDtypeStruct((nb, 2, GROUP_W, SPAN * d), F32))
    return pl.pallas_call(
        _kv_tail_kernel,
        grid=(nb,), in_specs=in_specs, out_specs=out_specs, out_shape=out_shape,
        scratch_shapes=[pltpu.VMEM((SPAN * max(DILATIONS), LANES), F32)],
        compiler_params=pltpu.CompilerParams(dimension_semantics=("arbitrary",),
                                             vmem_limit_bytes=VMEM_LIMIT),
        name="kv_tail",
    )(*args)


def _mlstm_kernel(q_ref, k_ref, v_ref, om_ref, gif_ref, gt_ref, hm_ref, st_ref, mo_ref,
                  st_s, m_s, *, lc):
    j = pl.program_id(1)

    @pl.when(j == 0)
    def _():
        st_s[...] = jnp.zeros_like(st_s)
        m_s[...] = jnp.zeros_like(m_s)

    gif = gif_ref[...]
    gt = gt_ref[...]
    lf_c = _log_sigmoid(gif)
    lf_r = _log_sigmoid(gt[M_HEADS:2 * M_HEADS, :])
    row = lax.broadcasted_iota(jnp.int32, (lc, lc), 0)
    col = lax.broadcasted_iota(jnp.int32, (lc, lc), 1)
    causal = col <= row
    tril = causal.astype(F32)
    triu = (row <= col).astype(F32)
    b_c = jnp.dot(tril, lf_c, precision=lax.Precision.HIGHEST, preferred_element_type=F32)
    b_r = jnp.dot(lf_r, triu, precision=lax.Precision.HIGHEST, preferred_element_type=F32)
    lane = lax.broadcasted_iota(jnp.int32, (lc, LANES), 1)
    halves = (lane < M_DQK, lane >= M_DQK)
    sub = lax.broadcasted_iota(jnp.int32, (LANES, 1), 0)
    ones_blk = jnp.ones((lc, LANES), BF16)
    qscale = M_DQK ** -0.5

    for p in range(M_HEADS // 2):
        q2_ = q_ref[:, p * LANES:(p + 1) * LANES]
        k2_ = k_ref[:, p * LANES:(p + 1) * LANES]
        st = st_s[p]
        st_b = st.astype(BF16)
        upds, dstates = [], []
        for hh in range(2):
            h = 2 * p + hh
            qa = jnp.where(halves[hh], q2_, jnp.zeros_like(q2_)) * jnp.asarray(qscale, q2_.dtype)
            ka = jnp.where(halves[hh], k2_, jnp.zeros_like(k2_))
            s = lax.dot_general(qa, k2_, (((1,), (1,)), ((), ())), preferred_element_type=F32)
            bcol = b_c[:, M_HEADS + h:M_HEADS + h + 1]
            igcol = gif[:, h:h + 1]
            brow = b_r[h:h + 1, :]
            igrow = gt[h:h + 1, :]
            dmat = jnp.where(causal, bcol - brow + igrow, NEG)
            m_prev = m_s[h:h + 1, 0:1]
            inter = bcol + m_prev
            m_t = jnp.maximum(inter, jnp.max(dmat, axis=-1, keepdims=True))
            w = jnp.exp(dmat - m_t)
            dec = jnp.exp(inter - m_t)
            qc = jnp.dot(qa, st_b, preferred_element_type=F32)
            v1 = jnp.concatenate([v_ref[:, h * M_DV:(h + 1) * M_DV], ones_blk], axis=1)
            num_den = dec * qc + jnp.dot((w * s).astype(BF16), v1, preferred_element_type=F32)
            num = num_den[:, :M_DV]
            den = num_den[:, M_DV:]
            hval = num / jnp.maximum(jnp.abs(den), jnp.exp(-m_t))
            gate = jax.nn.sigmoid(om_ref[:, h * M_DV:(h + 1) * M_DV].astype(F32))
            hm_ref[:, h * M_DV:(h + 1) * M_DV] = (gate * hval).astype(hm_ref.dtype)
            m_new = m_t[lc - 1:lc, :]
            b_last = bcol[lc - 1:lc, :]
            dstates.append(jnp.exp(b_last + m_prev - m_new))
            ws = jnp.exp(b_last - bcol + igcol - m_new)
            kws = (ka.astype(F32) * ws).astype(BF16)
            upds.append(lax.dot_general(kws, v1, (((0,), (0,)), ((), ())),
                                        preferred_element_type=F32))
            m_s[h:h + 1, :] = jnp.broadcast_to(m_new, (1, LANES))
        drow = jnp.where(sub < M_DQK, dstates[0], dstates[1])
        st_s[p] = drow * st + upds[0] + upds[1]

    @pl.when(j == pl.num_programs(1) - 1)
    def _():
        st_ref[...] = st_s[...]
        mo_ref[...] = m_s[...]


def _mlstm(qm, km, vm, om, gif, gif_t, nb, seq_len, lc):
    nc = seq_len // lc
    row = lambda w: pl.BlockSpec((lc, w), lambda b, j: (b * nc + j, 0))
    return pl.pallas_call(
        functools.partial(_mlstm_kernel, lc=lc),
        grid=(nb, nc),
        in_specs=[row(512), row(512), row(1024), row(1024), row(LANES),
                  pl.BlockSpec((2 * M_HEADS, lc), lambda b, j: (0, b * nc + j))],
        out_specs=[row(1024),
                   pl.BlockSpec((None, M_HEADS // 2, LANES, 2 * LANES), lambda b, j: (b, 0, 0, 0)),
                   pl.BlockSpec((None, M_HEADS, LANES), lambda b, j: (b, 0, 0))],
        out_shape=[jax.ShapeDtypeStruct((nb * seq_len, 1024), BF16),
                   jax.ShapeDtypeStruct((nb, M_HEADS // 2, LANES, 2 * LANES), F32),
                   jax.ShapeDtypeStruct((nb, M_HEADS, LANES), F32)],
        scratch_shapes=[pltpu.VMEM((M_HEADS // 2, LANES, 2 * LANES), F32),
                        pltpu.VMEM((M_HEADS, LANES), F32)],
        compiler_params=pltpu.CompilerParams(dimension_semantics=("arbitrary", "arbitrary"),
                                             vmem_limit_bytes=VMEM_LIMIT),
        name="mlstm",
    )(qm, km, vm, om, gif, gif_t)


FF_CHUNK = 256


def _merge_ffn_kernel(x_ref, o_ref, hm_ref, ga_ref, gb_ref, wa_ref, wm_ref, wo_ref, g2_ref,
                      wg_ref, wu_ref, wd_ref, y_ref):
    o_att = jnp.concatenate([o_ref[p] for p in range(PAIRS)], axis=1).astype(BF16)
    ya = jnp.dot(o_att, wa_ref[...], preferred_element_type=F32)
    yb = jnp.dot(hm_ref[...].astype(BF16), wm_ref[...], preferred_element_type=F32)
    mixed = (jax.nn.sigmoid(ga_ref[...].astype(F32)) * ya
             + jax.nn.sigmoid(gb_ref[...].astype(F32)) * yb)
    x2 = x_ref[...] + jnp.dot(mixed.astype(BF16), wo_ref[...], preferred_element_type=F32)
    h2 = (x2 * lax.rsqrt(jnp.mean(x2 * x2, axis=-1, keepdims=True) + NORM_EPS)
          * g2_ref[...]).astype(BF16)
    acc = x2
    d_ff = wg_ref.shape[1]
    for c in range(0, d_ff, FF_CHUNK):
        gt = jnp.dot(h2, wg_ref[:, c:c + FF_CHUNK], preferred_element_type=F32)
        up = jnp.dot(h2, wu_ref[:, c:c + FF_CHUNK], preferred_element_type=F32)
        ff = (gt * jax.nn.sigmoid(gt) * up).astype(BF16)
        acc = acc + jnp.dot(ff, wd_ref[c:c + FF_CHUNK, :], preferred_element_type=F32)
    y_ref[...] = acc


def _merge_ffn(x2d, o_att, hm, ga, gb, wa, wm, wo, g2, wg, wu, wd, seq_len, tm):
    m_rows, d_model = x2d.shape
    tiles_per_seq = seq_len // tm
    row = lambda w: pl.BlockSpec((tm, w), lambda i: (i, 0))
    o_spec = pl.BlockSpec((None, PAIRS, tm, LANES),
                          lambda i: (i // tiles_per_seq, 0, i % tiles_per_seq, 0))
    return pl.pallas_call(
        _merge_ffn_kernel,
        grid=(m_rows // tm,),
        in_specs=[row(d_model), o_spec, row(1024), row(1024), row(1024),
                  _const_spec(wa.shape), _const_spec(wm.shape), _const_spec(wo.shape),
                  _const_spec((1, d_model)), _const_spec(wg.shape), _const_spec(wu.shape),
                  _const_spec(wd.shape)],
        out_specs=row(d_model),
        out_shape=jax.ShapeDtypeStruct((m_rows, d_model), F32),
        compiler_params=pltpu.CompilerParams(dimension_semantics=("arbitrary",),
                                             vmem_limit_bytes=VMEM_LIMIT),
        name="merge_ffn",
    )(x2d, o_att, hm, ga, gb, wa, wm, wo, g2, wg, wu, wd)


def _sample_attn_kernel(q0, q1, q2, k0, k1, k2, v0, v1, v2, c0, c1, c2,
                        o0, o1, o2, ot_ref, p0, p1, p2, pn_s, l_s, qkv_t):
    b = pl.program_id(0)
    kv = pl.program_id(1)
    c_refs, o_refs, p_refs = (c0, c1, c2), (o0, o1, o2), (p0, p1, p2)
    sel = lax.broadcasted_iota(jnp.int32, (GROUP_W, LANES), 1) == b
    scale = HEAD_DIM ** -0.5
    q_refs, k_refs, v_refs = (tuple(qkv_t.at[3 * t + g] for g in range(N_GROUPS)) for t in range(3))

    @pl.when(jnp.logical_and(b == 0, kv == 0))
    def _():
        for i, ref in enumerate((q0, q1, q2, k0, k1, k2, v0, v1, v2)):
            qkv_t[i] = ref[...].T
        ot_ref[...] = jnp.zeros_like(ot_ref)

    def column(ref):
        return jnp.sum(jnp.where(sel, ref[...], 0.0), axis=1, keepdims=True)

    def head(col, h):
        return col[h * HEAD_DIM:(h + 1) * HEAD_DIM, :]

    def shift(new_cols):
        for g in range(N_GROUPS):
            wb = c_refs[g].shape[-1]
            last = lax.broadcasted_iota(jnp.int32, (HEAD_DIM, wb), 1) == wb - 1
            for h in range(HEADS):
                rolled = pltpu.roll(c_refs[g][h], wb - 1, 1)
                o_refs[g][h] = jnp.where(last, head(new_cols[g], h), rolled)

    @pl.when(kv == 0)
    def _():
        qc = [column(q_refs[g]) * scale for g in range(N_GROUPS)]
        kn = [column(k_refs[g]) for g in range(N_GROUPS)]
        for h in range(HEADS):
            scores, m_h = [], None
            for g, d in enumerate(DILATIONS):
                wb = c_refs[g].shape[-1]
                qh = head(qc[g], h)
                s = jnp.sum(c_refs[g][h] * qh, axis=0, keepdims=True)
                pos = lax.broadcasted_iota(jnp.int32, (1, wb), 1)
                s = jnp.where((pos & (d - 1)) == 0, s, NEG)
                s_new = jnp.sum(head(kn[g], h) * qh, axis=0, keepdims=True)
                m_g = jnp.maximum(jnp.max(s, axis=1, keepdims=True), s_new)
                m_h = m_g if m_h is None else jnp.maximum(m_h, m_g)
                scores.append((s, s_new))
            l_h = jnp.zeros((1, 1), F32)
            for g in range(N_GROUPS):
                s, s_new = scores[g]
                p = jnp.exp(s - m_h)
                p_new = jnp.exp(s_new - m_h)
                p_refs[g][h:h + 1, :] = p
                pn_s[g * HEADS + h:g * HEADS + h + 1, :] = jnp.broadcast_to(p_new, (1, LANES))
                l_h = l_h + jnp.sum(p, axis=1, keepdims=True) + p_new
            l_s[h:h + 1, :] = jnp.broadcast_to(l_h, (1, LANES))
        shift(kn)

    @pl.when(kv == 1)
    def _():
        vn = [column(v_refs[g]) for g in range(N_GROUPS)]
        cols = []
        for h in range(HEADS):
            acc = jnp.zeros((HEAD_DIM, 1), F32)
            for g in range(N_GROUPS):
                p = p_refs[g][h:h + 1, :]
                acc = acc + jnp.sum(c_refs[g][h] * p, axis=1, keepdims=True)
                acc = acc + pn_s[g * HEADS + h:g * HEADS + h + 1, 0:1] * head(vn[g], h)
            cols.append(acc / l_s[h:h + 1, 0:1])
        ocol = jnp.concatenate(cols, axis=0)
        ot_ref[...] = jnp.where(sel, ocol, ot_ref[...])
        shift(vn)


def _sample_attention(qts, kts, vts, caches_t, db):
    c_specs, scratch = [], []
    for g, d in enumerate(DILATIONS):
        wb = caches_t[g].shape[-1]
        assert wb == SPAN * d
        c_specs.append(pl.BlockSpec((None, None, HEADS, HEAD_DIM, wb), lambda b, kv: (b, kv, 0, 0, 0)))
        scratch.append(pltpu.VMEM((HEADS, wb), F32))
    scratch += [pltpu.VMEM((N_GROUPS * HEADS, LANES), F32), pltpu.VMEM((HEADS, LANES), F32),
                pltpu.VMEM((3 * N_GROUPS, GROUP_W, LANES), F32)]
    outs = pl.pallas_call(
        _sample_attn_kernel,
        grid=(db, 2),
        in_specs=[_const_spec((LANES, GROUP_W))] * 9 + c_specs,
        out_specs=c_specs + [pl.BlockSpec((GROUP_W, LANES), lambda b, kv: (0, 0))],
        out_shape=[jax.ShapeDtypeStruct(c.shape, F32) for c in caches_t]
                  + [jax.ShapeDtypeStruct((GROUP_W, LANES), F32)],
        scratch_shapes=scratch,
        compiler_params=pltpu.CompilerParams(dimension_semantics=("arbitrary", "arbitrary"),
                                             vmem_limit_bytes=VMEM_LIMIT),
        name="sample_attn",
    )(*qts, *kts, *vts, *caches_t)
    return outs[:3], outs[3]


def _sample_mlstm_kernel(q_ref, k_ref, v_ref, om_ref, gt_ref, mt_ref, n_ref, c_ref, rexp_ref,
                         hm_ref, co_ref, nt_ref, mo_ref, *, db):
    nh, dqk = M_HEADS, M_DQK
    hi = lax.Precision.HIGHEST
    rexp = rexp_ref[...]
    q_t = q_ref[...].T * (dqk ** -0.5)
    k_t = k_ref[...].T
    n_t = n_ref[...].T
    ig = gt_ref[0:nh, :]
    lf = _log_sigmoid(gt_ref[nh:2 * nh, :])
    m_prev = mt_ref[...]
    m_new = jnp.maximum(lf + m_prev, ig)
    w8 = jnp.exp(ig - m_new)
    dec8 = jnp.exp(lf + m_prev - m_new)
    head_sum = lambda a: lax.dot_general(rexp, a, (((0,), (0,)), ((), ())), precision=hi,
                                         preferred_element_type=F32)
    expand = lambda a: jnp.dot(rexp, a, precision=hi, preferred_element_type=F32)
    qk8 = head_sum(q_t * k_t)
    qn8 = head_sum(q_t * n_t)
    wqk8 = w8 * qk8
    den8 = dec8 * qn8 + wqk8
    inv8 = 1.0 / jnp.maximum(jnp.abs(den8), jnp.exp(-m_new))
    dec_x = expand(dec8)
    w_x = expand(w8)
    nt_ref[...] = dec_x * n_t + w_x * k_t
    mo_ref[...] = m_new
    wk_x = w_x * k_t
    for b in range(db):
        cb = c_ref[b].reshape(nh * dqk, M_DV)
        v_b = v_ref[b]
        v_x = jnp.concatenate([jnp.broadcast_to(v_b[h:h + 1, :], (dqk, M_DV)) for h in range(nh)],
                              axis=0)
        co_ref[b] = (dec_x[:, b:b + 1] * cb + wk_x[:, b:b + 1] * v_x).reshape(nh, dqk, M_DV)
        qc = jnp.sum((q_t[:, b:b + 1] * cb).reshape(nh, dqk, M_DV), axis=1)
        num = dec8[:, b:b + 1] * qc + wqk8[:, b:b + 1] * v_b
        hm_ref[b] = jax.nn.sigmoid(om_ref[b]) * (num * inv8[:, b:b + 1])


def _sample_mlstm(q_t, k_t, v3, om3, gif_t, m_t, n_pad, c_state, rexp, db):
    vm = pl.BlockSpec(memory_space=pltpu.VMEM)
    return pl.pallas_call(
        functools.partial(_sample_mlstm_kernel, db=db),
        in_specs=[vm] * 9,
        out_specs=[vm] * 4,
        out_shape=[jax.ShapeDtypeStruct((db, M_HEADS, M_DV), F32),
                   jax.ShapeDtypeStruct(c_state.shape, F32),
                   jax.ShapeDtypeStruct((M_HEADS * M_DQK, LANES), F32),
                   jax.ShapeDtypeStruct((M_HEADS, LANES), F32)],
        compiler_params=pltpu.CompilerParams(vmem_limit_bytes=VMEM_LIMIT),
        name="sample_mlstm",
    )(q_t, k_t, v3, om3, gif_t, m_t, n_pad, c_state, rexp)


def _rope_tables(pos):
    half = ROT_DIM // 2
    inv_freq = jnp.exp(-math.log(ROPE_THETA) * jnp.arange(0, ROT_DIM, 2, dtype=F32) / ROT_DIM)
    ang = pos[:, None] * inv_freq[None, :]
    cos, sin = jnp.cos(ang), jnp.sin(ang)
    t = pos.shape[0]
    rest = HEAD_DIM - ROT_DIM
    a = jnp.concatenate([cos, cos, jnp.ones((t, rest), F32)], axis=1)
    bm = jnp.concatenate([-sin, jnp.zeros((t, HEAD_DIM - half), F32)], axis=1)
    bp = jnp.concatenate([jnp.zeros((t, half), F32), sin, jnp.zeros((t, rest), F32)], axis=1)
    return tuple(jnp.concatenate([x, x], axis=1) for x in (a, bm, bp))


def _repack_w_in(w_in):
    d_model = w_in.shape[0]
    aw = _ATT_W
    cols = []
    for g in range(N_GROUPS):
        for base in (0, aw, 2 * aw):
            cols.append(w_in[:, base + g * GROUP_W: base + (g + 1) * GROUP_W])
    gif0 = 3 * aw + 2 * M_HEADS * M_DQK + 2 * M_HEADS * M_DV
    cols.append(w_in[:, 3 * aw:gif0])
    cols.append(w_in[:, gif0 + 2 * M_HEADS:])
    cols.append(w_in[:, gif0:gif0 + 2 * M_HEADS])
    cols.append(jnp.zeros((d_model, LANES - 2 * M_HEADS), w_in.dtype))
    w = jnp.concatenate(cols, axis=1).astype(BF16)
    assert w.shape[1] == _W_COLS
    return w


def kernel(x_prompt, x_sample, cache_kv_w128, cache_kv_w512, cache_kv_w2048, state_mlstm_C, state_mlstm_n, state_mlstm_m, norm1_g, w_in, b_if, q_norm_g, k_norm_g, w_att_out, w_m_out, w_o, norm2_g, w_gate, w_up, w_down):
    nb, seq_len, d_model = x_prompt.shape
    db, dec_seq, _ = x_sample.shape
    assert dec_seq == 1 and d_model == 1024 and seq_len % ATT_STEP == 0 and db <= LANES
    caches = (cache_kv_w128, cache_kv_w512, cache_kv_w2048)
    past_len = 8192

    w_perm = _repack_w_in(w_in)
    g1 = norm1_g.reshape(1, d_model)
    g2 = norm2_g.reshape(1, d_model)
    bif = jnp.concatenate([b_if, jnp.zeros((LANES - b_if.shape[0],), F32)]).reshape(1, LANES)
    qg = jnp.tile(q_norm_g, HEADS).reshape(1, GROUP_W)
    kg = jnp.tile(k_norm_g, HEADS).reshape(1, GROUP_W)
    hid = np.arange(GROUP_W // 2) // HEAD_DIM
    gmat = jnp.asarray(hid[:, None] == hid[None, :], dtype=BF16)
    wa, wm, wo = (w.astype(BF16) for w in (w_att_out, w_m_out, w_o))
    wg, wu, wd = (w.astype(BF16) for w in (w_gate, w_up, w_down))

    m_rows = nb * seq_len
    x2d = x_prompt.reshape(m_rows, d_model)
    tabs_p = _rope_tables(jnp.arange(seq_len, dtype=F32))
    outs = _proj(x2d, seq_len, DILATIONS, 256, False, g1, w_perm, bif, qg, kg, tabs_p, gmat)
    qs, ks, vs = outs[0:3], outs[3:6], outs[6:9]
    qm, km, vm, om, ga, gb, gif = outs[9:16]

    o_att = _attention(qs, ks, vs, seq_len)
    gif_t = gif[:, :2 * M_HEADS].T
    hm, st_p, m_p = _mlstm(qm, km, vm, om, gif, gif_t, nb, seq_len, 256)
    y_prompt = _merge_ffn(x2d, o_att, hm, ga, gb, wa, wm, wo, g2, wg, wu, wd, seq_len, 256)
    y_prompt = y_prompt.reshape(nb, seq_len, d_model)

    tails = _kv_tails(ks, vs, seq_len)
    kv_p = [t.reshape(nb, 2, HEADS, HEAD_DIM, t.shape[-1]).transpose(0, 4, 1, 2, 3) for t in tails]
    c_p = st_p[..., :M_DV].reshape(nb, M_HEADS, M_DQK, M_DV)
    n_p = st_p[..., M_DV].reshape(nb, M_HEADS, M_DQK)
    m_pr = m_p[:, :, 0]

    x_s = jnp.zeros((LANES, d_model), F32).at[:db].set(x_sample.reshape(db, d_model))
    tabs_s = _rope_tables(jnp.full((LANES,), float(past_len), F32))
    outs_s = _proj(x_s, LANES, (1, 1, 1), LANES, True, g1, w_perm, bif, qg, kg, tabs_s, gmat)
    qts, kts, vts = outs_s[0:3], outs_s[3:6], outs_s[6:9]
    qm_t, km_t, vm_s, om_s, ga_s, gb_s, gif_s = outs_s[9:16]
    gif_ts = gif_s.T

    caches_t = [c.transpose(0, 2, 3, 4, 1) for c in caches]
    kv_st, o_att_t = _sample_attention(qts, kts, vts, caches_t, db)
    kv_s = [c.transpose(0, 4, 1, 2, 3) for c in kv_st]

    rexp = jnp.asarray(np.arange(M_HEADS * M_DQK)[:, None] // M_DQK == np.arange(M_HEADS)[None, :],
                       dtype=F32)
    m_t = jnp.zeros((M_HEADS, LANES), F32).at[:, :db].set(state_mlstm_m.T)
    n_pad = jnp.zeros((LANES, M_HEADS * M_DQK), F32).at[:db].set(state_mlstm_n.reshape(db, -1))
    hm_s3, c_s, n_t, m_so = _sample_mlstm(
        qm_t, km_t, vm_s[:db].reshape(db, M_HEADS, M_DV), om_s[:db].reshape(db, M_HEADS, M_DV),
        gif_ts[:2 * M_HEADS], m_t, n_pad, state_mlstm_C, rexp, db)
    n_s = n_t.T[:db].reshape(db, M_HEADS, M_DQK)
    m_s = m_so[:, :db].T

    o_att_sp = o_att_t.T.reshape(LANES, PAIRS, LANES).transpose(1, 0, 2)[None]
    hm_sp = jnp.zeros((LANES, 1024), F32).at[:db].set(hm_s3.reshape(db, 1024))
    y_s = _merge_ffn(x_s, o_att_sp, hm_sp, ga_s, gb_s, wa, wm, wo, g2, wg, wu, wd, LANES, LANES)
    y_sample = y_s[:db].reshape(db, 1, d_model)

    return (y_prompt, y_sample, kv_p[0], kv_p[1], kv_p[2], c_p, n_p, m_pr,
            kv_s[0], kv_s[1], kv_s[2], c_s, n_s, m_s)
```

```python
import functools
import math

import jax
import jax.numpy as jnp
import numpy as np
from jax import lax
from jax.experimental import pallas as pl
from jax.experimental.pallas import tpu as pltpu

F32 = jnp.float32
BF16 = jnp.bfloat16

HEAD_DIM = 64
HEADS = 8
GROUP_W = HEADS * HEAD_DIM
N_GROUPS = 3
WINDOWS = (128, 512, 2048)
DILATIONS = (1, 4, 16)
SPAN = 128
ROT_DIM = 16
ROPE_THETA = 500000.0
M_HEADS = 8
M_DQK = 64
M_DV = 128
NORM_EPS = 1e-6
PAIRS = GROUP_W // 128
NEG = -1e30

LANES = 128
VMEM_LIMIT = 56 * 1024 * 1024

_ATT_W = N_GROUPS * GROUP_W
_C_QM = 3 * _ATT_W
_C_KM = _C_QM + M_HEADS * M_DQK
_C_VM = _C_KM + M_HEADS * M_DQK
_C_OM = _C_VM + M_HEADS * M_DV
_C_GA = _C_OM + M_HEADS * M_DV
_C_GB = _C_GA + 1024
_C_GIF = _C_GB + 1024
_W_COLS = _C_GIF + LANES


def _const_spec(shape):
    nd = len(shape)
    return pl.BlockSpec(shape, lambda *_: (0,) * nd, pipeline_mode=pl.Buffered(1))


def _log_sigmoid(x):
    return jnp.minimum(x, 0.0) - jnp.log1p(jnp.exp(-jnp.abs(x)))


def _proj_kernel(x_ref, g1_ref, w_ref, bif_ref, qg_ref, kg_ref, ra_ref, rm_ref, rp_ref, gm_ref,
                 q0_ref, q1_ref, q2_ref, k0_ref, k1_ref, k2_ref, v0_ref, v1_ref, v2_ref,
                 qm_ref, km_ref, vm_ref, om_ref, ga_ref, gb_ref, gif_ref,
                 hs_ref, *, tm, dils, plain):
    d_model = x_ref.shape[1]
    x = x_ref[...]
    xn = x * lax.rsqrt(jnp.mean(x * x, axis=-1, keepdims=True) + NORM_EPS) * g1_ref[...]
    h_nat = xn.astype(BF16)
    n_slab = d_model // LANES
    if any(d > 1 for d in dils):
        for c in range(n_slab):
            hs_ref[c] = xn[:, c * LANES:(c + 1) * LANES]

    def permuted_h(d):
        if d == 1:
            return h_nat
        n = tm // d
        rows = [jnp.concatenate([hs_ref[c, pl.ds(r, n, stride=d), :] for c in range(n_slab)], axis=1)
                for r in range(d)]
        return jnp.concatenate(rows, axis=0).astype(BF16)

    def permuted_tab(ref, d):
        if d == 1:
            t = ref[...]
        else:
            n = tm // d
            t = jnp.concatenate([ref[pl.ds(r, n, stride=d), :] for r in range(d)], axis=0)
        return jnp.concatenate([t] * PAIRS, axis=1)

    gmat = gm_ref[...]

    def head_sumsq(z):
        zz = (z * z).astype(BF16)
        half = GROUP_W // 2
        return jnp.concatenate(
            [jnp.dot(zz[:, :half], gmat, preferred_element_type=F32),
             jnp.dot(zz[:, half:], gmat, preferred_element_type=F32)], axis=1)

    def norm_rope(z, ss, gain, ra, rm, rp):
        y = z * lax.rsqrt(ss * (1.0 / HEAD_DIM) + NORM_EPS) * gain
        return (y * ra + pltpu.roll(y, GROUP_W - ROT_DIM // 2, 1) * rm
                + pltpu.roll(y, ROT_DIM // 2, 1) * rp)

    def store_group(ref, y, d):
        if plain:
            ref[...] = y.astype(ref.dtype)
            return
        n = tm // d
        for p in range(PAIRS):
            ref[:, p] = y[:, p * LANES:(p + 1) * LANES].reshape(d, n, LANES).astype(ref.dtype)

    q_refs = (q0_ref, q1_ref, q2_ref)
    k_refs = (k0_ref, k1_ref, k2_ref)
    v_refs = (v0_ref, v1_ref, v2_ref)
    for g in range(N_GROUPS):
        d = dils[g]
        hg = permuted_h(d)
        ra, rm, rp = (permuted_tab(r, d) for r in (ra_ref, rm_ref, rp_ref))
        base = g * _ATT_W
        zq = jnp.dot(hg, w_ref[:, base:base + GROUP_W], preferred_element_type=F32)
        zk = jnp.dot(hg, w_ref[:, base + GROUP_W:base + 2 * GROUP_W], preferred_element_type=F32)
        zv = jnp.dot(hg, w_ref[:, base + 2 * GROUP_W:base + 3 * GROUP_W], preferred_element_type=F32)
        ssq, ssk = head_sumsq(zq), head_sumsq(zk)
        store_group(v_refs[g], zv, d)
        store_group(q_refs[g], norm_rope(zq, ssq, qg_ref[...], ra, rm, rp), d)
        store_group(k_refs[g], norm_rope(zk, ssk, kg_ref[...], ra, rm, rp), d)

    for ref, c0, width in ((qm_ref, _C_QM, 512), (km_ref, _C_KM, 512), (vm_ref, _C_VM, 1024),
                           (om_ref, _C_OM, 1024), (ga_ref, _C_GA, 1024), (gb_ref, _C_GB, 1024)):
        for cc in range(0, width, GROUP_W):
            z = jnp.dot(h_nat, w_ref[:, c0 + cc:c0 + cc + GROUP_W], preferred_element_type=F32)
            ref[:, cc:cc + GROUP_W] = z.astype(ref.dtype)
    zg = jnp.dot(h_nat, w_ref[:, _C_GIF:_C_GIF + LANES], preferred_element_type=F32)
    gif_ref[...] = zg + bif_ref[...]


def _proj(x2d, seq_len, dils, tm, plain, g1, w_perm, bif, qg, kg, rope_tabs, gmat):
    m_rows, d_model = x2d.shape
    nb = m_rows // seq_len
    tiles_per_seq = seq_len // tm
    grid = (m_rows // tm,)
    row_spec = lambda w: pl.BlockSpec((tm, w), lambda i: (i, 0))
    tab_spec = pl.BlockSpec((tm, LANES), lambda i: (i % tiles_per_seq, 0))
    sds = jax.ShapeDtypeStruct

    widths = (512, 512, 1024, 1024, 1024, 1024)
    if plain:
        out_shape = ([sds((m_rows, GROUP_W), F32)] * 9 + [sds((m_rows, w), F32) for w in widths]
                     + [sds((m_rows, LANES), F32)])
        out_specs = ([row_spec(GROUP_W)] * 9 + [row_spec(w) for w in widths] + [row_spec(LANES)])
    else:
        def grp_spec(d):
            return pl.BlockSpec((None, d, PAIRS, tm // d, LANES),
                                lambda i: (i // tiles_per_seq, 0, 0, i % tiles_per_seq, 0))
        grp_shape = lambda d: sds((nb, d, PAIRS, seq_len // d, LANES), BF16)
        out_shape = ([grp_shape(d) for d in dils] * 3 + [sds((m_rows, w), BF16) for w in widths]
                     + [sds((m_rows, LANES), F32)])
        out_specs = ([grp_spec(d) for d in dils] * 3 + [row_spec(w) for w in widths]
                     + [row_spec(LANES)])
    in_specs = [row_spec(d_model), _const_spec((1, d_model)), _const_spec(w_perm.shape),
                _const_spec((1, LANES)), _const_spec((1, GROUP_W)), _const_spec((1, GROUP_W)),
                tab_spec, tab_spec, tab_spec, _const_spec(gmat.shape)]
    return pl.pallas_call(
        functools.partial(_proj_kernel, tm=tm, dils=dils, plain=plain),
        grid=grid, in_specs=in_specs, out_specs=out_specs, out_shape=out_shape,
        scratch_shapes=[pltpu.VMEM((d_model // LANES, tm, LANES), F32)],
        compiler_params=pltpu.CompilerParams(dimension_semantics=("arbitrary",),
                                             vmem_limit_bytes=VMEM_LIMIT),
        name="proj",
    )(x2d, g1, w_perm, bif, qg, kg, *rope_tabs, gmat)


ATT_STEP = SPAN * max(DILATIONS)
ATT_UNROLL = 8


def _attn_kernel(q0, k0, v0, kh0, vh0, q1, k1, v1, kh1, vh1, q2, k2, v2, kh2, vh2,
                 o_ref, kb0, vb0, kb1, vb1, kb2, vb2, acc_s, m_s, l_s):
    j = pl.program_id(1)
    for kb, vb, kh, vh, kc, vc in ((kb0, vb0, kh0, vh0, k0, v0), (kb1, vb1, kh1, vh1, k1, v1),
                                   (kb2, vb2, kh2, vh2, k2, v2)):
        kb[:, 0:SPAN, :] = kh[...]
        kb[:, SPAN:, :] = kc[...]
        vb[:, 0:SPAN, :] = vh[...]
        vb[:, SPAN:, :] = vc[...]

    qi = lax.broadcasted_iota(jnp.int32, (SPAN, 2 * SPAN), 0)
    ci = lax.broadcasted_iota(jnp.int32, (SPAN, 2 * SPAN), 1)
    band = (ci >= qi) & (ci <= qi + SPAN)
    bias_band = jnp.where(band, 0.0, NEG).astype(F32)
    bias_first = jnp.where(band & (ci >= SPAN), 0.0, NEG).astype(F32)
    lane_q = lax.broadcasted_iota(jnp.int32, (SPAN, LANES), 1)
    lane_kv = lax.broadcasted_iota(jnp.int32, (2 * SPAN, LANES), 1)
    halves_q = (lane_q < HEAD_DIM, lane_q >= HEAD_DIM)
    halves_kv = (lane_kv < HEAD_DIM, lane_kv >= HEAD_DIM)
    scale = HEAD_DIM ** -0.5

    def blocks(g, q_ref, kb, vb, units):
        loaded = []
        for r, bi, _ in units:
            row0 = pl.multiple_of(bi * SPAN, SPAN)
            q2_ = q_ref[r, pl.ds(row0, SPAN), :]
            kk = kb[r, pl.ds(row0, 2 * SPAN), :]
            first = jnp.logical_and(j == 0, bi == 0)
            bias = jnp.where(first, bias_first, bias_band)
            ss = []
            for hh in range(2):
                qa = (jnp.where(halves_q[hh], q2_, jnp.zeros_like(q2_))
                      * jnp.asarray(scale, q2_.dtype))
                ss.append(lax.dot_general(qa, kk, (((1,), (1,)), ((), ())),
                                          preferred_element_type=F32) + bias)
            loaded.append(ss)
        probs = []
        for ss in loaded:
            mxs = [jnp.max(s, axis=-1, keepdims=True) for s in ss]
            ps = [jnp.exp(s - mx) for s, mx in zip(ss, mxs)]
            ls = [jnp.sum(p, axis=-1, keepdims=True) for p in ps]
            probs.append(([p.astype(BF16) for p in ps], mxs, ls))
        for (r, bi, sl), (ps, mxs, ls) in zip(units, probs):
            row0 = pl.multiple_of(bi * SPAN, SPAN)
            vv = vb[r, pl.ds(row0, 2 * SPAN), :]
            acc = None
            for hh in range(2):
                vh = jnp.where(halves_kv[hh], vv, jnp.zeros_like(vv))
                a = jnp.dot(ps[hh], vh, preferred_element_type=F32)
                acc = a if acc is None else acc + a
            acc_s[g, sl, :] = acc
            m_s[g, sl, :] = jnp.where(halves_q[0], mxs[0], mxs[1])
            l_s[g, sl, :] = jnp.where(halves_q[0], ls[0], ls[1])

    d0, d1, d2 = DILATIONS
    nb0, nb1, nb2 = (ATT_STEP // d // SPAN for d in DILATIONS)
    un = ATT_UNROLL

    def body0(it, c):
        units = []
        for u in range(un):
            bi = it * un + u
            units.append((0, bi, pl.ds(pl.multiple_of(bi * SPAN, SPAN), SPAN)))
        blocks(0, q0, kb0, vb0, units)
        return c
    lax.fori_loop(0, nb0 // un, body0, 0)

    def body1(it, c):
        units = []
        for u in range(un):
            idx = it * un + u
            r, bi = idx // nb1, idx % nb1
            units.append((r, bi, pl.ds(bi * SPAN * d1 + r, SPAN, stride=d1)))
        blocks(1, q1, kb1, vb1, units)
        return c
    lax.fori_loop(0, d1 * nb1 // un, body1, 0)

    def body2(it, c):
        units = []
        for u in range(un):
            r = it * un + u
            units.append((r, 0, pl.ds(r, SPAN, stride=d2)))
        blocks(2, q2, kb2, vb2, units)
        return c
    lax.fori_loop(0, d2 * nb2 // un, body2, 0)

    def combine(ci_, c):
        sl = pl.ds(pl.multiple_of(ci_ * SPAN, SPAN), SPAN)
        ms = [m_s[g, sl, :] for g in range(N_GROUPS)]
        mm = jnp.maximum(jnp.maximum(ms[0], ms[1]), ms[2])
        es = [jnp.exp(m - mm) for m in ms]
        num = es[0] * acc_s[0, sl, :] + es[1] * acc_s[1, sl, :] + es[2] * acc_s[2, sl, :]
        den = es[0] * l_s[0, sl, :] + es[1] * l_s[1, sl, :] + es[2] * l_s[2, sl, :]
        o_ref[sl, :] = (num / den).astype(o_ref.dtype)
        return c
    lax.fori_loop(0, ATT_STEP // SPAN, combine, 0)


def _attention(qs, ks, vs, seq_len):
    nb = qs[0].shape[0]
    steps = seq_len // ATT_STEP
    in_specs, args, scratch = [], [], []
    for g, d in enumerate(DILATIONS):
        rows = ATT_STEP // d
        cur = pl.BlockSpec((None, d, None, rows, LANES), lambda b, j, p: (b, 0, p, j, 0))
        ratio = rows // SPAN
        halo = pl.BlockSpec((None, d, None, SPAN, LANES),
                            lambda b, j, p, ratio=ratio: (b, 0, p, jnp.maximum(j * ratio - 1, 0), 0))
        in_specs += [cur, cur, cur, halo, halo]
        args += [qs[g], ks[g], vs[g], ks[g], vs[g]]
        scratch += [pltpu.VMEM((d, SPAN + rows, LANES), BF16)] * 2
    scratch += [pltpu.VMEM((N_GROUPS, ATT_STEP, LANES), F32)] * 3
    return pl.pallas_call(
        _attn_kernel,
        grid=(nb, steps, PAIRS),
        in_specs=in_specs,
        out_specs=pl.BlockSpec((None, None, ATT_STEP, LANES), lambda b, j, p: (b, p, j, 0)),
        out_shape=jax.ShapeDtypeStruct((nb, PAIRS, seq_len, LANES), BF16),
        scratch_shapes=scratch,
        compiler_params=pltpu.CompilerParams(
            dimension_semantics=("arbitrary", "arbitrary", "arbitrary"),
            vmem_limit_bytes=VMEM_LIMIT),
        name="attn",
    )(*args)


def _kv_tail_kernel(k0, v0, k1, v1, k2, v2, o0, o1, o2, nat_ref):
    for (k_ref, v_ref, o_ref, d) in ((k0, v0, o0, DILATIONS[0]), (k1, v1, o1, DILATIONS[1]),
                                     (k2, v2, o2, DILATIONS[2])):
        for kvi, ref in enumerate((k_ref, v_ref)):
            for p in range(PAIRS):
                rows = slice(p * LANES, (p + 1) * LANES)
                if d == 1:
                    o_ref[kvi, rows, :] = ref[0, p].astype(F32).T
                    continue
                for r in range(d):
                    nat_ref[pl.ds(r, SPAN, stride=d), :] = ref[r, p].astype(F32)
                for c in range(d):
                    o_ref[kvi, rows, c * SPAN:(c + 1) * SPAN] = nat_ref[c * SPAN:(c + 1) * SPAN, :].T


def _kv_tails(ks, vs, seq_len):
    nb = ks[0].shape[0]
    in_specs, args, out_specs, out_shape = [], [], [], []
    for g, d in enumerate(DILATIONS):
        last = seq_len // d // SPAN - 1
        spec = pl.BlockSpec((None, d, PAIRS, SPAN, LANES), lambda b, last=last: (b, 0, 0, last, 0))
        in_specs += [spec, spec]
        args += [ks[g], vs[g]]
        out_specs.append(pl.BlockSpec((None, 2, GROUP_W, SPAN * d), lambda b: (b, 0, 0, 0)))
        out_shape.append(jax.ShapeDtypeStruct((nb, 2, GROUP_W, SPAN * d), F32))
    return pl.pallas_call(
        _kv_tail_kernel,
        grid=(nb,), in_specs=in_specs, out_specs=out_specs, out_shape=out_shape,
        scratch_shapes=[pltpu.VMEM((SPAN * max(DILATIONS), LANES), F32)],
        compiler_params=pltpu.CompilerParams(dimension_semantics=("arbitrary",),
                                             vmem_limit_bytes=VMEM_LIMIT),
        name="kv_tail",
    )(*args)


def _mlstm_kernel(q_ref, k_ref, v_ref, om_ref, gif_ref, gt_ref, hm_ref, st_ref, mo_ref,
                  st_s, m_s, *, lc):
    j = pl.program_id(1)

    @pl.when(j == 0)
    def _():
        st_s[...] = jnp.zeros_like(st_s)
        m_s[...] = jnp.zeros_like(m_s)

    gif = gif_ref[...]
    gt = gt_ref[...]
    lf_c = _log_sigmoid(gif)
    lf_r = _log_sigmoid(gt[M_HEADS:2 * M_HEADS, :])
    row = lax.broadcasted_iota(jnp.int32, (lc, lc), 0)
    col = lax.broadcasted_iota(jnp.int32, (lc, lc), 1)
    causal = col <= row
    tril = causal.astype(F32)
    triu = (row <= col).astype(F32)
    b_c = jnp.dot(tril, lf_c, precision=lax.Precision.HIGHEST, preferred_element_type=F32)
    b_r = jnp.dot(lf_r, triu, precision=lax.Precision.HIGHEST, preferred_element_type=F32)
    lane = lax.broadcasted_iota(jnp.int32, (lc, LANES), 1)
    halves = (lane < M_DQK, lane >= M_DQK)
    sub = lax.broadcasted_iota(jnp.int32, (LANES, 1), 0)
    ones_blk = jnp.ones((lc, LANES), BF16)
    qscale = M_DQK ** -0.5

    states = [st_s[p] for p in range(M_HEADS // 2)]
    phase1 = []
    for h in range(M_HEADS):
        p, hh = divmod(h, 2)
        q2_ = q_ref[:, p * LANES:(p + 1) * LANES]
        k2_ = k_ref[:, p * LANES:(p + 1) * LANES]
        qa = jnp.where(halves[hh], q2_, jnp.zeros_like(q2_)) * jnp.asarray(qscale, q2_.dtype)
        s = lax.dot_general(qa, k2_, (((1,), (1,)), ((), ())), preferred_element_type=F32)
        qc = jnp.dot(qa, states[p].astype(BF16), preferred_element_type=F32)
        phase1.append((s, qc))

    phase2 = []
    for h in range(M_HEADS):
        p, hh = divmod(h, 2)
        s, qc = phase1[h]
        bcol = b_c[:, M_HEADS + h:M_HEADS + h + 1]
        igcol = gif[:, h:h + 1]
        a_row = gt[h:h + 1, :] - b_r[h:h + 1, :]
        amat = jnp.where(causal, a_row, NEG)
        m_prev = m_s[h:h + 1, 0:1]
        g_t = jnp.maximum(m_prev, jnp.max(amat, axis=-1, keepdims=True))
        wqk = (jnp.exp(amat - g_t) * s).astype(BF16)
        dec = jnp.exp(m_prev - g_t)
        m_t = bcol + g_t
        m_new = m_t[lc - 1:lc, :]
        b_last = bcol[lc - 1:lc, :]
        dstate = jnp.exp(b_last + m_prev - m_new)
        ws = jnp.exp(b_last - bcol + igcol - m_new)
        k2_ = k_ref[:, p * LANES:(p + 1) * LANES]
        ka = jnp.where(halves[hh], k2_, jnp.zeros_like(k2_))
        kws = (ka.astype(F32) * ws).astype(BF16)
        m_s[h:h + 1, :] = jnp.broadcast_to(m_new, (1, LANES))
        phase2.append((wqk, dec * qc, jnp.exp(-m_t), dstate, kws))

    upds = []
    for h in range(M_HEADS):
        wqk, dqc, floor, _, kws = phase2[h]
        v1 = jnp.concatenate([v_ref[:, h * M_DV:(h + 1) * M_DV], ones_blk], axis=1)
        num_den = dqc + jnp.dot(wqk, v1, preferred_element_type=F32)
        num = num_den[:, :M_DV]
        den = num_den[:, M_DV:]
        hval = num / jnp.maximum(jnp.abs(den), floor)
        gate = jax.nn.sigmoid(om_ref[:, h * M_DV:(h + 1) * M_DV].astype(F32))
        hm_ref[:, h * M_DV:(h + 1) * M_DV] = (gate * hval).astype(hm_ref.dtype)
        upds.append(lax.dot_general(kws, v1, (((0,), (0,)), ((), ())),
                                    preferred_element_type=F32))
    for p in range(M_HEADS // 2):
        drow = jnp.where(sub < M_DQK, phase2[2 * p][3], phase2[2 * p + 1][3])
        st_s[p] = drow * states[p] + upds[2 * p] + upds[2 * p + 1]

    @pl.when(j == pl.num_programs(1) - 1)
    def _():
        st_ref[...] = st_s[...]
        mo_ref[...] = m_s[...]


def _mlstm(qm, km, vm, om, gif, gif_t, nb, seq_len, lc):
    nc = seq_len // lc
    row = lambda w: pl.BlockSpec((lc, w), lambda b, j: (b * nc + j, 0))
    return pl.pallas_call(
        functools.partial(_mlstm_kernel, lc=lc),
        grid=(nb, nc),
        in_specs=[row(512), row(512), row(1024), row(1024), row(LANES),
                  pl.BlockSpec((2 * M_HEADS, lc), lambda b, j: (0, b * nc + j))],
        out_specs=[row(1024),
                   pl.BlockSpec((None, M_HEADS // 2, LANES, 2 * LANES), lambda b, j: (b, 0, 0, 0)),
                   pl.BlockSpec((None, M_HEADS, LANES), lambda b, j: (b, 0, 0))],
        out_shape=[jax.ShapeDtypeStruct((nb * seq_len, 1024), BF16),
                   jax.ShapeDtypeStruct((nb, M_HEADS // 2, LANES, 2 * LANES), F32),
                   jax.ShapeDtypeStruct((nb, M_HEADS, LANES), F32)],
        scratch_shapes=[pltpu.VMEM((M_HEADS // 2, LANES, 2 * LANES), F32),
                        pltpu.VMEM((M_HEADS, LANES), F32)],
        compiler_params=pltpu.CompilerParams(dimension_semantics=("arbitrary", "arbitrary"),
                                             vmem_limit_bytes=VMEM_LIMIT),
        name="mlstm",
    )(qm, km, vm, om, gif, gif_t)


FF_CHUNK = 256


def _merge_ffn_kernel(x_ref, o_ref, hm_ref, ga_ref, gb_ref, wa_ref, wm_ref, wo_ref, g2_ref,
                      wg_ref, wu_ref, wd_ref, y_ref):
    o_att = jnp.concatenate([o_ref[p] for p in range(PAIRS)], axis=1).astype(BF16)
    ya = jnp.dot(o_att, wa_ref[...], preferred_element_type=F32)
    yb = jnp.dot(hm_ref[...].astype(BF16), wm_ref[...], preferred_element_type=F32)
    mixed = (jax.nn.sigmoid(ga_ref[...].astype(F32)) * ya
             + jax.nn.sigmoid(gb_ref[...].astype(F32)) * yb)
    x2 = x_ref[...] + jnp.dot(mixed.astype(BF16), wo_ref[...], preferred_element_type=F32)
    h2 = (x2 * lax.rsqrt(jnp.mean(x2 * x2, axis=-1, keepdims=True) + NORM_EPS)
          * g2_ref[...]).astype(BF16)
    acc = x2
    d_ff = wg_ref.shape[1]

    def gate_up(c):
        gt = jnp.dot(h2, wg_ref[:, c:c + FF_CHUNK], preferred_element_type=F32)
        up = jnp.dot(h2, wu_ref[:, c:c + FF_CHUNK], preferred_element_type=F32)
        return gt, up

    chunks = list(range(0, d_ff, FF_CHUNK))
    nxt = gate_up(chunks[0])
    for i, c in enumerate(chunks):
        gt, up = nxt
        if i + 1 < len(chunks):
            nxt = gate_up(chunks[i + 1])
        ff = (gt * jax.nn.sigmoid(gt) * up).astype(BF16)
        acc = acc + jnp.dot(ff, wd_ref[c:c + FF_CHUNK, :], preferred_element_type=F32)
    y_ref[...] = acc


def _merge_ffn(x2d, o_att, hm, ga, gb, wa, wm, wo, g2, wg, wu, wd, seq_len, tm):
    m_rows, d_model = x2d.shape
    tiles_per_seq = seq_len // tm
    row = lambda w: pl.BlockSpec((tm, w), lambda i: (i, 0))
    o_spec = pl.BlockSpec((None, PAIRS, tm, LANES),
                          lambda i: (i // tiles_per_seq, 0, i % tiles_per_seq, 0))
    return pl.pallas_call(
        _merge_ffn_kernel,
        grid=(m_rows // tm,),
        in_specs=[row(d_model), o_spec, row(1024), row(1024), row(1024),
                  _const_spec(wa.shape), _const_spec(wm.shape), _const_spec(wo.shape),
                  _const_spec((1, d_model)), _const_spec(wg.shape), _const_spec(wu.shape),
                  _const_spec(wd.shape)],
        out_specs=row(d_model),
        out_shape=jax.ShapeDtypeStruct((m_rows, d_model), F32),
        compiler_params=pltpu.CompilerParams(dimension_semantics=("arbitrary",),
                                             vmem_limit_bytes=VMEM_LIMIT),
        name="merge_ffn",
    )(x2d, o_att, hm, ga, gb, wa, wm, wo, g2, wg, wu, wd)


def _sample_attn_kernel(q0, q1, q2, k0, k1, k2, v0, v1, v2, c0, c1, c2,
                        o0, o1, o2, ot_ref, p0, p1, p2, pn_s, l_s, qkv_t):
    b = pl.program_id(0)
    kv = pl.program_id(1)
    c_refs, o_refs, p_refs = (c0, c1, c2), (o0, o1, o2), (p0, p1, p2)
    sel = lax.broadcasted_iota(jnp.int32, (GROUP_W, LANES), 1) == b
    scale = HEAD_DIM ** -0.5
    q_refs, k_refs, v_refs = (tuple(qkv_t.at[3 * t + g] for g in range(N_GROUPS)) for t in range(3))

    @pl.when(jnp.logical_and(b == 0, kv == 0))
    def _():
        for i, ref in enumerate((q0, q1, q2, k0, k1, k2, v0, v1, v2)):
            qkv_t[i] = ref[...].T
        ot_ref[...] = jnp.zeros_like(ot_ref)

    def column(ref):
        return jnp.sum(jnp.where(sel, ref[...], 0.0), axis=1, keepdims=True)

    def head(col, h):
        return col[h * HEAD_DIM:(h + 1) * HEAD_DIM, :]

    def shift(new_cols):
        for g in range(N_GROUPS):
            wb = c_refs[g].shape[-1]
            last = lax.broadcasted_iota(jnp.int32, (HEAD_DIM, wb), 1) == wb - 1
            for h in range(HEADS):
                rolled = pltpu.roll(c_refs[g][h], wb - 1, 1)
                o_refs[g][h] = jnp.where(last, head(new_cols[g], h), rolled)

    @pl.when(kv == 0)
    def _():
        qc = [column(q_refs[g]) * scale for g in range(N_GROUPS)]
        kn = [column(k_refs[g]) for g in range(N_GROUPS)]
        for h in range(HEADS):
            scores, m_h = [], None
            for g, d in enumerate(DILATIONS):
                wb = c_refs[g].shape[-1]
                qh = head(qc[g], h)
                s = jnp.sum(c_refs[g][h] * qh, axis=0, keepdims=True)
                pos = lax.broadcasted_iota(jnp.int32, (1, wb), 1)
                s = jnp.where((pos & (d - 1)) == 0, s, NEG)
                s_new = jnp.sum(head(kn[g], h) * qh, axis=0, keepdims=True)
                m_g = jnp.maximum(jnp.max(s, axis=1, keepdims=True), s_new)
                m_h = m_g if m_h is None else jnp.maximum(m_h, m_g)
                scores.append((s, s_new))
            l_h = jnp.zeros((1, 1), F32)
            for g in range(N_GROUPS):
                s, s_new = scores[g]
                p = jnp.exp(s - m_h)
                p_new = jnp.exp(s_new - m_h)
                p_refs[g][h:h + 1, :] = p
                pn_s[g * HEADS + h:g * HEADS + h + 1, :] = jnp.broadcast_to(p_new, (1, LANES))
                l_h = l_h + jnp.sum(p, axis=1, keepdims=True) + p_new
            l_s[h:h + 1, :] = jnp.broadcast_to(l_h, (1, LANES))
        shift(kn)

    @pl.when(kv == 1)
    def _():
        vn = [column(v_refs[g]) for g in range(N_GROUPS)]
        cols = []
        for h in range(HEADS):
            acc = jnp.zeros((HEAD_DIM, 1), F32)
            for g in range(N_GROUPS):
                p = p_refs[g][h:h + 1, :]
                acc = acc + jnp.sum(c_refs[g][h] * p, axis=1, keepdims=True)
                acc = acc + pn_s[g * HEADS + h:g * HEADS + h + 1, 0:1] * head(vn[g], h)
            cols.append(acc / l_s[h:h + 1, 0:1])
        ocol = jnp.concatenate(cols, axis=0)
        ot_ref[...] = jnp.where(sel, ocol, ot_ref[...])
        shift(vn)


def _sample_attention(qts, kts, vts, caches_t, db):
    c_specs, scratch = [], []
    for g, d in enumerate(DILATIONS):
        wb = caches_t[g].shape[-1]
        assert wb == SPAN * d
        c_specs.append(pl.BlockSpec((None, None, HEADS, HEAD_DIM, wb), lambda b, kv: (b, kv, 0, 0, 0)))
        scratch.append(pltpu.VMEM((HEADS, wb), F32))
    scratch += [pltpu.VMEM((N_GROUPS * HEADS, LANES), F32), pltpu.VMEM((HEADS, LANES), F32),
                pltpu.VMEM((3 * N_GROUPS, GROUP_W, LANES), F32)]
    outs = pl.pallas_call(
        _sample_attn_kernel,
        grid=(db, 2),
        in_specs=[_const_spec((LANES, GROUP_W))] * 9 + c_specs,
        out_specs=c_specs + [pl.BlockSpec((GROUP_W, LANES), lambda b, kv: (0, 0))],
        out_shape=[jax.ShapeDtypeStruct(c.shape, F32) for c in caches_t]
                  + [jax.ShapeDtypeStruct((GROUP_W, LANES), F32)],
        scratch_shapes=scratch,
        compiler_params=pltpu.CompilerParams(dimension_semantics=("arbitrary", "arbitrary"),
                                             vmem_limit_bytes=VMEM_LIMIT),
        name="sample_attn",
    )(*qts, *kts, *vts, *caches_t)
    return outs[:3], outs[3]


def _sample_mlstm_kernel(q_ref, k_ref, v_ref, om_ref, gt_ref, mt_ref, n_ref, c_ref, rexp_ref,
                         hm_ref, co_ref, nt_ref, mo_ref, *, db):
    nh, dqk = M_HEADS, M_DQK
    hi = lax.Precision.HIGHEST
    rexp = rexp_ref[...]
    q_t = q_ref[...].T * (dqk ** -0.5)
    k_t = k_ref[...].T
    n_t = n_ref[...].T
    ig = gt_ref[0:nh, :]
    lf = _log_sigmoid(gt_ref[nh:2 * nh, :])
    m_prev = mt_ref[...]
    m_new = jnp.maximum(lf + m_prev, ig)
    w8 = jnp.exp(ig - m_new)
    dec8 = jnp.exp(lf + m_prev - m_new)
    head_sum = lambda a: lax.dot_general(rexp, a, (((0,), (0,)), ((), ())), precision=hi,
                                         preferred_element_type=F32)
    expand = lambda a: jnp.dot(rexp, a, precision=hi, preferred_element_type=F32)
    qk8 = head_sum(q_t * k_t)
    qn8 = head_sum(q_t * n_t)
    wqk8 = w8 * qk8
    den8 = dec8 * qn8 + wqk8
    inv8 = 1.0 / jnp.maximum(jnp.abs(den8), jnp.exp(-m_new))
    dec_x = expand(dec8)
    w_x = expand(w8)
    nt_ref[...] = dec_x * n_t + w_x * k_t
    mo_ref[...] = m_new
    wk_x = w_x * k_t
    for b in range(db):
        cb = c_ref[b].reshape(nh * dqk, M_DV)
        v_b = v_ref[b]
        v_x = jnp.concatenate([jnp.broadcast_to(v_b[h:h + 1, :], (dqk, M_DV)) for h in range(nh)],
                              axis=0)
        co_ref[b] = (dec_x[:, b:b + 1] * cb + wk_x[:, b:b + 1] * v_x).reshape(nh, dqk, M_DV)
        qc = jnp.sum((q_t[:, b:b + 1] * cb).reshape(nh, dqk, M_DV), axis=1)
        num = dec8[:, b:b + 1] * qc + wqk8[:, b:b + 1] * v_b
        hm_ref[b] = jax.nn.sigmoid(om_ref[b]) * (num * inv8[:, b:b + 1])


def _sample_mlstm(q_t, k_t, v3, om3, gif_t, m_t, n_pad, c_state, rexp, db):
    vm = pl.BlockSpec(memory_space=pltpu.VMEM)
    return pl.pallas_call(
        functools.partial(_sample_mlstm_kernel, db=db),
        in_specs=[vm] * 9,
        out_specs=[vm] * 4,
        out_shape=[jax.ShapeDtypeStruct((db, M_HEADS, M_DV), F32),
                   jax.ShapeDtypeStruct(c_state.shape, F32),
                   jax.ShapeDtypeStruct((M_HEADS * M_DQK, LANES), F32),
                   jax.ShapeDtypeStruct((M_HEADS, LANES), F32)],
        compiler_params=pltpu.CompilerParams(vmem_limit_bytes=VMEM_LIMIT),
        name="sample_mlstm",
    )(q_t, k_t, v3, om3, gif_t, m_t, n_pad, c_state, rexp)


def _rope_tables(pos):
    half = ROT_DIM // 2
    inv_freq = jnp.exp(-math.log(ROPE_THETA) * jnp.arange(0, ROT_DIM, 2, dtype=F32) / ROT_DIM)
    ang = pos[:, None] * inv_freq[None, :]
    cos, sin = jnp.cos(ang), jnp.sin(ang)
    t = pos.shape[0]
    rest = HEAD_DIM - ROT_DIM
    a = jnp.concatenate([cos, cos, jnp.ones((t, rest), F32)], axis=1)
    bm = jnp.concatenate([-sin, jnp.zeros((t, HEAD_DIM - half), F32)], axis=1)
    bp = jnp.concatenate([jnp.zeros((t, half), F32), sin, jnp.zeros((t, rest), F32)], axis=1)
    return tuple(jnp.concatenate([x, x], axis=1) for x in (a, bm, bp))


def _repack_w_in(w_in):
    d_model = w_in.shape[0]
    aw = _ATT_W
    cols = []
    for g in range(N_GROUPS):
        for base in (0, aw, 2 * aw):
            cols.append(w_in[:, base + g * GROUP_W: base + (g + 1) * GROUP_W])
    gif0 = 3 * aw + 2 * M_HEADS * M_DQK + 2 * M_HEADS * M_DV
    cols.append(w_in[:, 3 * aw:gif0])
    cols.append(w_in[:, gif0 + 2 * M_HEADS:])
    cols.append(w_in[:, gif0:gif0 + 2 * M_HEADS])
    cols.append(jnp.zeros((d_model, LANES - 2 * M_HEADS), w_in.dtype))
    w = jnp.concatenate(cols, axis=1).astype(BF16)
    assert w.shape[1] == _W_COLS
    return w


def kernel(x_prompt, x_sample, cache_kv_w128, cache_kv_w512, cache_kv_w2048, state_mlstm_C, state_mlstm_n, state_mlstm_m, norm1_g, w_in, b_if, q_norm_g, k_norm_g, w_att_out, w_m_out, w_o, norm2_g, w_gate, w_up, w_down):
    nb, seq_len, d_model = x_prompt.shape
    db, dec_seq, _ = x_sample.shape
    assert dec_seq == 1 and d_model == 1024 and seq_len % ATT_STEP == 0 and db <= LANES
    caches = (cache_kv_w128, cache_kv_w512, cache_kv_w2048)
    past_len = 8192

    w_perm = _repack_w_in(w_in)
    g1 = norm1_g.reshape(1, d_model)
    g2 = norm2_g.reshape(1, d_model)
    bif = jnp.concatenate([b_if, jnp.zeros((LANES - b_if.shape[0],), F32)]).reshape(1, LANES)
    qg = jnp.tile(q_norm_g, HEADS).reshape(1, GROUP_W)
    kg = jnp.tile(k_norm_g, HEADS).reshape(1, GROUP_W)
    hid = np.arange(GROUP_W // 2) // HEAD_DIM
    gmat = jnp.asarray(hid[:, None] == hid[None, :], dtype=BF16)
    wa, wm, wo = (w.astype(BF16) for w in (w_att_out, w_m_out, w_o))
    wg, wu, wd = (w.astype(BF16) for w in (w_gate, w_up, w_down))

    m_rows = nb * seq_len
    x2d = x_prompt.reshape(m_rows, d_model)
    tabs_p = _rope_tables(jnp.arange(seq_len, dtype=F32))
    outs = _proj(x2d, seq_len, DILATIONS, 256, False, g1, w_perm, bif, qg, kg, tabs_p, gmat)
    qs, ks, vs = outs[0:3], outs[3:6], outs[6:9]
    qm, km, vm, om, ga, gb, gif = outs[9:16]

    o_att = _attention(qs, ks, vs, seq_len)
    gif_t = gif[:, :2 * M_HEADS].T
    hm, st_p, m_p = _mlstm(qm, km, vm, om, gif, gif_t, nb, seq_len, 256)
    y_prompt = _merge_ffn(x2d, o_att, hm, ga, gb, wa, wm, wo, g2, wg, wu, wd, seq_len, 256)
    y_prompt = y_prompt.reshape(nb, seq_len, d_model)

    tails = _kv_tails(ks, vs, seq_len)
    kv_p = [t.reshape(nb, 2, HEADS, HEAD_DIM, t.shape[-1]).transpose(0, 4, 1, 2, 3) for t in tails]
    c_p = st_p[..., :M_DV].reshape(nb, M_HEADS, M_DQK, M_DV)
    n_p = st_p[..., M_DV].reshape(nb, M_HEADS, M_DQK)
    m_pr = m_p[:, :, 0]

    x_s = jnp.zeros((LANES, d_model), F32).at[:db].set(x_sample.reshape(db, d_model))
    tabs_s = _rope_tables(jnp.full((LANES,), float(past_len), F32))
    outs_s = _proj(x_s, LANES, (1, 1, 1), LANES, True, g1, w_perm, bif, qg, kg, tabs_s, gmat)
    qts, kts, vts = outs_s[0:3], outs_s[3:6], outs_s[6:9]
    qm_t, km_t, vm_s, om_s, ga_s, gb_s, gif_s = outs_s[9:16]
    gif_ts = gif_s.T

    caches_t = [c.transpose(0, 2, 3, 4, 1) for c in caches]
    kv_st, o_att_t = _sample_attention(qts, kts, vts, caches_t, db)
    kv_s = [c.transpose(0, 4, 1, 2, 3) for c in kv_st]

    rexp = jnp.asarray(np.arange(M_HEADS * M_DQK)[:, None] // M_DQK == np.arange(M_HEADS)[None, :],
                       dtype=F32)
    m_t = jnp.zeros((M_HEADS, LANES), F32).at[:, :db].set(state_mlstm_m.T)
    n_pad = jnp.zeros((LANES, M_HEADS * M_DQK), F32).at[:db].set(state_mlstm_n.reshape(db, -1))
    hm_s3, c_s, n_t, m_so = _sample_mlstm(
        qm_t, km_t, vm_s[:db].reshape(db, M_HEADS, M_DV), om_s[:db].reshape(db, M_HEADS, M_DV),
        gif_ts[:2 * M_HEADS], m_t, n_pad, state_mlstm_C, rexp, db)
    n_s = n_t.T[:db].reshape(db, M_HEADS, M_DQK)
    m_s = m_so[:, :db].T

    o_att_sp = o_att_t.T.reshape(LANES, PAIRS, LANES).transpose(1, 0, 2)[None]
    hm_sp = jnp.zeros((LANES, 1024), F32).at[:db].set(hm_s3.reshape(db, 1024))
    y_s = _merge_ffn(x_s, o_att_sp, hm_sp, ga_s, gb_s, wa, wm, wo, g2, wg, wu, wd, LANES, LANES)
    y_sample = y_s[:db].reshape(db, 1, d_model)

    return (y_prompt, y_sample, kv_p[0], kv_p[1], kv_p[2], c_p, n_p, m_pr,
            kv_s[0], kv_s[1], kv_s[2], c_s, n_s, m_s)
```

```python
import functools
import math

import jax
import jax.numpy as jnp
import numpy as np
from jax import lax
from jax.experimental import pallas as pl
from jax.experimental.pallas import tpu as pltpu

F32 = jnp.float32
BF16 = jnp.bfloat16

HEAD_DIM = 64
HEADS = 8
GROUP_W = HEADS * HEAD_DIM
N_GROUPS = 3
WINDOWS = (128, 512, 2048)
DILATIONS = (1, 4, 16)
SPAN = 128
ROT_DIM = 16
ROPE_THETA = 500000.0
M_HEADS = 8
M_DQK = 64
M_DV = 128
NORM_EPS = 1e-6
PAIRS = GROUP_W // 128
NEG = -1e30

LANES = 128
VMEM_LIMIT = 56 * 1024 * 1024

_ATT_W = N_GROUPS * GROUP_W
_C_QM = 3 * _ATT_W
_C_KM = _C_QM + M_HEADS * M_DQK
_C_VM = _C_KM + M_HEADS * M_DQK
_C_OM = _C_VM + M_HEADS * M_DV
_C_GA = _C_OM + M_HEADS * M_DV
_C_GB = _C_GA + 1024
_C_GIF = _C_GB + 1024
_W_COLS = _C_GIF + LANES


def _const_spec(shape):
    nd = len(shape)
    return pl.BlockSpec(shape, lambda *_: (0,) * nd, pipeline_mode=pl.Buffered(1))


def _log_sigmoid(x):
    return jnp.minimum(x, 0.0) - jnp.log1p(jnp.exp(-jnp.abs(x)))


def _proj_kernel(x_ref, g1_ref, w_ref, bif_ref, qg_ref, kg_ref, ra_ref, rm_ref, rp_ref, gm_ref,
                 q0_ref, q1_ref, q2_ref, k0_ref, k1_ref, k2_ref, v0_ref, v1_ref, v2_ref,
                 qm_ref, km_ref, vm_ref, om_ref, ga_ref, gb_ref, gif_ref,
                 hs_ref, *, tm, dils, plain):
    d_model = x_ref.shape[1]
    x = x_ref[...]
    xn = x * lax.rsqrt(jnp.mean(x * x, axis=-1, keepdims=True) + NORM_EPS) * g1_ref[...]
    h_nat = xn.astype(BF16)
    n_slab = d_model // LANES
    if any(d > 1 for d in dils):
        for c in range(n_slab):
            hs_ref[c] = xn[:, c * LANES:(c + 1) * LANES]

    def permuted_h(d):
        if d == 1:
            return h_nat
        n = tm // d
        rows = [jnp.concatenate([hs_ref[c, pl.ds(r, n, stride=d), :] for c in range(n_slab)], axis=1)
                for r in range(d)]
        return jnp.concatenate(rows, axis=0).astype(BF16)

    def permuted_tab(ref, d):
        if d == 1:
            t = ref[...]
        else:
            n = tm // d
            t = jnp.concatenate([ref[pl.ds(r, n, stride=d), :] for r in range(d)], axis=0)
        return jnp.concatenate([t] * PAIRS, axis=1)

    gmat = gm_ref[...]

    def head_sumsq(z):
        zz = (z * z).astype(BF16)
        half = GROUP_W // 2
        return jnp.concatenate(
            [jnp.dot(zz[:, :half], gmat, preferred_element_type=F32),
             jnp.dot(zz[:, half:], gmat, preferred_element_type=F32)], axis=1)

    def norm_rope(z, ss, gain, ra, rm, rp):
        y = z * lax.rsqrt(ss * (1.0 / HEAD_DIM) + NORM_EPS) * gain
        return (y * ra + pltpu.roll(y, GROUP_W - ROT_DIM // 2, 1) * rm
                + pltpu.roll(y, ROT_DIM // 2, 1) * rp)

    def store_group(ref, y, d):
        if plain:
            ref[...] = y.astype(ref.dtype)
            return
        n = tm // d
        for p in range(PAIRS):
            ref[:, p] = y[:, p * LANES:(p + 1) * LANES].reshape(d, n, LANES).astype(ref.dtype)

    q_refs = (q0_ref, q1_ref, q2_ref)
    k_refs = (k0_ref, k1_ref, k2_ref)
    v_refs = (v0_ref, v1_ref, v2_ref)
    for g in range(N_GROUPS):
        d = dils[g]
        hg = permuted_h(d)
        ra, rm, rp = (permuted_tab(r, d) for r in (ra_ref, rm_ref, rp_ref))
        base = g * _ATT_W
        zq = jnp.dot(hg, w_ref[:, base:base + GROUP_W], preferred_element_type=F32)
        zk = jnp.dot(hg, w_ref[:, base + GROUP_W:base + 2 * GROUP_W], preferred_element_type=F32)
        zv = jnp.dot(hg, w_ref[:, base + 2 * GROUP_W:base + 3 * GROUP_W], preferred_element_type=F32)
        ssq, ssk = head_sumsq(zq), head_sumsq(zk)
        store_group(v_refs[g], zv, d)
        store_group(q_refs[g], norm_rope(zq, ssq, qg_ref[...], ra, rm, rp), d)
        store_group(k_refs[g], norm_rope(zk, ssk, kg_ref[...], ra, rm, rp), d)

    for ref, c0, width in ((qm_ref, _C_QM, 512), (km_ref, _C_KM, 512), (vm_ref, _C_VM, 1024),
                           (om_ref, _C_OM, 1024), (ga_ref, _C_GA, 1024), (gb_ref, _C_GB, 1024)):
        for cc in range(0, width, GROUP_W):
            z = jnp.dot(h_nat, w_ref[:, c0 + cc:c0 + cc + GROUP_W], preferred_element_type=F32)
            ref[:, cc:cc + GROUP_W] = z.astype(ref.dtype)
    zg = jnp.dot(h_nat, w_ref[:, _C_GIF:_C_GIF + LANES], preferred_element_type=F32)
    gif_ref[...] = zg + bif_ref[...]


def _proj(x2d, seq_len, dils, tm, plain, g1, w_perm, bif, qg, kg, rope_tabs, gmat):
    m_rows, d_model = x2d.shape
    nb = m_rows // seq_len
    tiles_per_seq = seq_len // tm
    grid = (m_rows // tm,)
    row_spec = lambda w: pl.BlockSpec((tm, w), lambda i: (i, 0))
    tab_spec = pl.BlockSpec((tm, LANES), lambda i: (i % tiles_per_seq, 0))
    sds = jax.ShapeDtypeStruct

    widths = (512, 512, 1024, 1024, 1024, 1024)
    if plain:
        out_shape = ([sds((m_rows, GROUP_W), F32)] * 9 + [sds((m_rows, w), F32) for w in widths]
                     + [sds((m_rows, LANES), F32)])
        out_specs = ([row_spec(GROUP_W)] * 9 + [row_spec(w) for w in widths] + [row_spec(LANES)])
    else:
        def grp_spec(d):
            return pl.BlockSpec((None, d, PAIRS, tm // d, LANES),
                                lambda i: (i // tiles_per_seq, 0, 0, i % tiles_per_seq, 0))
        grp_shape = lambda d: sds((nb, d, PAIRS, seq_len // d, LANES), BF16)
        out_shape = ([grp_shape(d) for d in dils] * 3 + [sds((m_rows, w), BF16) for w in widths]
                     + [sds((m_rows, LANES), F32)])
        out_specs = ([grp_spec(d) for d in dils] * 3 + [row_spec(w) for w in widths]
                     + [row_spec(LANES)])
    in_specs = [row_spec(d_model), _const_spec((1, d_model)), _const_spec(w_perm.shape),
                _const_spec((1, LANES)), _const_spec((1, GROUP_W)), _const_spec((1, GROUP_W)),
                tab_spec, tab_spec, tab_spec, _const_spec(gmat.shape)]
    return pl.pallas_call(
        functools.partial(_proj_kernel, tm=tm, dils=dils, plain=plain),
        grid=grid, in_specs=in_specs, out_specs=out_specs, out_shape=out_shape,
        scratch_shapes=[pltpu.VMEM((d_model // LANES, tm, LANES), F32)],
        compiler_params=pltpu.CompilerParams(dimension_semantics=("arbitrary",),
                                             vmem_limit_bytes=VMEM_LIMIT),
        name="proj",
    )(x2d, g1, w_perm, bif, qg, kg, *rope_tabs, gmat)


ATT_STEP = SPAN * max(DILATIONS)
ATT_UNROLL = 8


def _attn_kernel(q0, k0, v0, kh0, vh0, q1, k1, v1, kh1, vh1, q2, k2, v2, kh2, vh2,
                 o_ref, kb0, vb0, kb1, vb1, kb2, vb2, acc_s, m_s, l_s):
    j = pl.program_id(1)
    for kb, vb, kh, vh, kc, vc in ((kb0, vb0, kh0, vh0, k0, v0), (kb1, vb1, kh1, vh1, k1, v1),
                                   (kb2, vb2, kh2, vh2, k2, v2)):
        kb[:, 0:SPAN, :] = kh[...]
        kb[:, SPAN:, :] = kc[...]
        vb[:, 0:SPAN, :] = vh[...]
        vb[:, SPAN:, :] = vc[...]

    qi = lax.broadcasted_iota(jnp.int32, (SPAN, 2 * SPAN), 0)
    ci = lax.broadcasted_iota(jnp.int32, (SPAN, 2 * SPAN), 1)
    band = (ci >= qi) & (ci <= qi + SPAN)
    bias_band = jnp.where(band, 0.0, NEG).astype(F32)
    bias_first = jnp.where(band & (ci >= SPAN), 0.0, NEG).astype(F32)
    lane_q = lax.broadcasted_iota(jnp.int32, (SPAN, LANES), 1)
    lane_kv = lax.broadcasted_iota(jnp.int32, (2 * SPAN, LANES), 1)
    halves_q = (lane_q < HEAD_DIM, lane_q >= HEAD_DIM)
    halves_kv = (lane_kv < HEAD_DIM, lane_kv >= HEAD_DIM)
    scale = HEAD_DIM ** -0.5

    def blocks(g, q_ref, kb, vb, units):
        loaded = []
        for r, bi, _ in units:
            row0 = pl.multiple_of(bi * SPAN, SPAN)
            q2_ = q_ref[r, pl.ds(row0, SPAN), :]
            kk = kb[r, pl.ds(row0, 2 * SPAN), :]
            first = jnp.logical_and(j == 0, bi == 0)
            bias = jnp.where(first, bias_first, bias_band)
            ss = []
            for hh in range(2):
                qa = (jnp.where(halves_q[hh], q2_, jnp.zeros_like(q2_))
                      * jnp.asarray(scale, q2_.dtype))
                ss.append(lax.dot_general(qa, kk, (((1,), (1,)), ((), ())),
                                          preferred_element_type=F32) + bias)
            loaded.append(ss)
        probs = []
        for ss in loaded:
            mxs = [jnp.max(s, axis=-1, keepdims=True) for s in ss]
            ps = [jnp.exp(s - mx) for s, mx in zip(ss, mxs)]
            ls = [jnp.sum(p, axis=-1, keepdims=True) for p in ps]
            probs.append(([p.astype(BF16) for p in ps], mxs, ls))
        for (r, bi, sl), (ps, mxs, ls) in zip(units, probs):
            row0 = pl.multiple_of(bi * SPAN, SPAN)
            vv = vb[r, pl.ds(row0, 2 * SPAN), :]
            acc = None
            for hh in range(2):
                vh = jnp.where(halves_kv[hh], vv, jnp.zeros_like(vv))
                a = jnp.dot(ps[hh], vh, preferred_element_type=F32)
                acc = a if acc is None else acc + a
            acc_s[g, sl, :] = acc
            m_s[g, sl, :] = jnp.where(halves_q[0], mxs[0], mxs[1])
            l_s[g, sl, :] = jnp.where(halves_q[0], ls[0], ls[1])

    d0, d1, d2 = DILATIONS
    nb0, nb1, nb2 = (ATT_STEP // d // SPAN for d in DILATIONS)
    un = ATT_UNROLL

    def body0(it, c):
        units = []
        for u in range(un):
            bi = it * un + u
            units.append((0, bi, pl.ds(pl.multiple_of(bi * SPAN, SPAN), SPAN)))
        blocks(0, q0, kb0, vb0, units)
        return c
    lax.fori_loop(0, nb0 // un, body0, 0)

    def body1(it, c):
        units = []
        for u in range(un):
            idx = it * un + u
            r, bi = idx // nb1, idx % nb1
            units.append((r, bi, pl.ds(bi * SPAN * d1 + r, SPAN, stride=d1)))
        blocks(1, q1, kb1, vb1, units)
        return c
    lax.fori_loop(0, d1 * nb1 // un, body1, 0)

    def body2(it, c):
        units = []
        for u in range(un):
            r = it * un + u
            units.append((r, 0, pl.ds(r, SPAN, stride=d2)))
        blocks(2, q2, kb2, vb2, units)
        return c
    lax.fori_loop(0, d2 * nb2 // un, body2, 0)

    def combine(ci_, c):
        sl = pl.ds(pl.multiple_of(ci_ * SPAN, SPAN), SPAN)
        ms = [m_s[g, sl, :] for g in range(N_GROUPS)]
        mm = jnp.maximum(jnp.maximum(ms[0], ms[1]), ms[2])
        es = [jnp.exp(m - mm) for m in ms]
        num = es[0] * acc_s[0, sl, :] + es[1] * acc_s[1, sl, :] + es[2] * acc_s[2, sl, :]
        den = es[0] * l_s[0, sl, :] + es[1] * l_s[1, sl, :] + es[2] * l_s[2, sl, :]
        o_ref[sl, :] = (num / den).astype(o_ref.dtype)
        return c
    lax.fori_loop(0, ATT_STEP // SPAN, combine, 0)


def _attention(qs, ks, vs, seq_len):
    nb = qs[0].shape[0]
    steps = seq_len // ATT_STEP
    in_specs, args, scratch = [], [], []
    for g, d in enumerate(DILATIONS):
        rows = ATT_STEP // d
        cur = pl.BlockSpec((None, d, None, rows, LANES), lambda b, j, p: (b, 0, p, j, 0))
        ratio = rows // SPAN
        halo = pl.BlockSpec((None, d, None, SPAN, LANES),
                            lambda b, j, p, ratio=ratio: (b, 0, p, jnp.maximum(j * ratio - 1, 0), 0))
        in_specs += [cur, cur, cur, halo, halo]
        args += [qs[g], ks[g], vs[g], ks[g], vs[g]]
        scratch += [pltpu.VMEM((d, SPAN + rows, LANES), BF16)] * 2
    scratch += [pltpu.VMEM((N_GROUPS, ATT_STEP, LANES), F32)] * 3
    return pl.pallas_call(
        _attn_kernel,
        grid=(nb, steps, PAIRS),
        in_specs=in_specs,
        out_specs=pl.BlockSpec((None, None, ATT_STEP, LANES), lambda b, j, p: (b, p, j, 0)),
        out_shape=jax.ShapeDtypeStruct((nb, PAIRS, seq_len, LANES), BF16),
        scratch_shapes=scratch,
        compiler_params=pltpu.CompilerParams(
            dimension_semantics=("arbitrary", "arbitrary", "arbitrary"),
            vmem_limit_bytes=VMEM_LIMIT),
        name="attn",
    )(*args)


def _kv_tail_kernel(k0, v0, k1, v1, k2, v2, o0, o1, o2, nat_ref):
    for (k_ref, v_ref, o_ref, d) in ((k0, v0, o0, DILATIONS[0]), (k1, v1, o1, DILATIONS[1]),
                                     (k2, v2, o2, DILATIONS[2])):
        for kvi, ref in enumerate((k_ref, v_ref)):
            for p in range(PAIRS):
                rows = slice(p * LANES, (p + 1) * LANES)
                if d == 1:
                    o_ref[kvi, rows, :] = ref[0, p].astype(F32).T
                    continue
                for r in range(d):
                    nat_ref[pl.ds(r, SPAN, stride=d), :] = ref[r, p].astype(F32)
                for c in range(d):
                    o_ref[kvi, rows, c * SPAN:(c + 1) * SPAN] = nat_ref[c * SPAN:(c + 1) * SPAN, :].T


def _kv_tails(ks, vs, seq_len):
    nb = ks[0].shape[0]
    in_specs, args, out_specs, out_shape = [], [], [], []
    for g, d in enumerate(DILATIONS):
        last = seq_len // d // SPAN - 1
        spec = pl.BlockSpec((None, d, PAIRS, SPAN, LANES), lambda b, last=last: (b, 0, 0, last, 0))
        in_specs += [spec, spec]
        args += [ks[g], vs[g]]
        out_specs.append(pl.BlockSpec((None, 2, GROUP_W, SPAN * d), lambda b: (b, 0, 0, 0)))
        out_shape.append(jax.ShapeDtypeStruct((nb, 2, GROUP_W, SPAN * d), F32))
    return pl.pallas_call(
        _kv_tail_kernel,
        grid=(nb,), in_specs=in_specs, out_specs=out_specs, out_shape=out_shape,
        scratch_shapes=[pltpu.VMEM((SPAN * max(DILATIONS), LANES), F32)],
        compiler_params=pltpu.CompilerParams(dimension_semantics=("arbitrary",),
                                             vmem_limit_bytes=VMEM_LIMIT),
        name="kv_tail",
    )(*args)


def _mlstm_kernel(q_ref, k_ref, v_ref, om_ref, gif_ref, gt_ref, hm_ref, st_ref, mo_ref,
                  st_s, m_s, *, lc):
    j = pl.program_id(1)

    @pl.when(j == 0)
    def _():
        st_s[...] = jnp.zeros_like(st_s)
        m_s[...] = jnp.zeros_like(m_s)

    gif = gif_ref[...]
    gt = gt_ref[...]
    lf_c = _log_sigmoid(gif)
    lf_r = _log_sigmoid(gt[M_HEADS:2 * M_HEADS, :])
    row = lax.broadcasted_iota(jnp.int32, (lc, lc), 0)
    col = lax.broadcasted_iota(jnp.int32, (lc, lc), 1)
    causal = col <= row
    tril = causal.astype(F32)
    triu = (row <= col).astype(F32)
    b_c = jnp.dot(tril, lf_c, precision=lax.Precision.HIGHEST, preferred_element_type=F32)
    b_r = jnp.dot(lf_r, triu, precision=lax.Precision.HIGHEST, preferred_element_type=F32)
    lane = lax.broadcasted_iota(jnp.int32, (lc, LANES), 1)
    halves = (lane < M_DQK, lane >= M_DQK)
    sub = lax.broadcasted_iota(jnp.int32, (LANES, 1), 0)
    ones_blk = jnp.ones((lc, LANES), BF16)
    qscale = M_DQK ** -0.5

    states = [st_s[p] for p in range(M_HEADS // 2)]
    phase1 = []
    for h in range(M_HEADS):
        p, hh = divmod(h, 2)
        q2_ = q_ref[:, p * LANES:(p + 1) * LANES]
        k2_ = k_ref[:, p * LANES:(p + 1) * LANES]
        qa = jnp.where(halves[hh], q2_, jnp.zeros_like(q2_)) * jnp.asarray(qscale, q2_.dtype)
        s = lax.dot_general(qa, k2_, (((1,), (1,)), ((), ())), preferred_element_type=F32)
        qc = jnp.dot(qa, states[p].astype(BF16), preferred_element_type=F32)
        phase1.append((s, qc))

    phase2 = []
    for h in range(M_HEADS):
        p, hh = divmod(h, 2)
        s, qc = phase1[h]
        bcol = b_c[:, M_HEADS + h:M_HEADS + h + 1]
        igcol = gif[:, h:h + 1]
        a_row = gt[h:h + 1, :] - b_r[h:h + 1, :]
        amat = jnp.where(causal, a_row, NEG)
        m_prev = m_s[h:h + 1, 0:1]
        g_t = jnp.maximum(m_prev, jnp.max(amat, axis=-1, keepdims=True))
        wqk = (jnp.exp(amat - g_t) * s).astype(BF16)
        dec = jnp.exp(m_prev - g_t)
        m_t = bcol + g_t
        m_new = m_t[lc - 1:lc, :]
        b_last = bcol[lc - 1:lc, :]
        dstate = jnp.exp(b_last + m_prev - m_new)
        ws = jnp.exp(b_last - bcol + igcol - m_new)
        k2_ = k_ref[:, p * LANES:(p + 1) * LANES]
        ka = jnp.where(halves[hh], k2_, jnp.zeros_like(k2_))
        kws = (ka.astype(F32) * ws).astype(BF16)
        m_s[h:h + 1, :] = jnp.broadcast_to(m_new, (1, LANES))
        phase2.append((wqk, dec * qc, jnp.exp(-m_t), dstate, kws))

    upds = []
    for h in range(M_HEADS):
        wqk, dqc, floor, _, kws = phase2[h]
        v1 = jnp.concatenate([v_ref[:, h * M_DV:(h + 1) * M_DV], ones_blk], axis=1)
        num_den = dqc + jnp.dot(wqk, v1, preferred_element_type=F32)
        num = num_den[:, :M_DV]
        den = num_den[:, M_DV:]
        hval = num / jnp.maximum(jnp.abs(den), floor)
        gate = jax.nn.sigmoid(om_ref[:, h * M_DV:(h + 1) * M_DV].astype(F32))
        hm_ref[:, h * M_DV:(h + 1) * M_DV] = (gate * hval).astype(hm_ref.dtype)
        upds.append(lax.dot_general(kws, v1, (((0,), (0,)), ((), ())),
                                    preferred_element_type=F32))
    for p in range(M_HEADS // 2):
        drow = jnp.where(sub < M_DQK, phase2[2 * p][3], phase2[2 * p + 1][3])
        st_s[p] = drow * states[p] + upds[2 * p] + upds[2 * p + 1]

    @pl.when(j == pl.num_programs(1) - 1)
    def _():
        st_ref[...] = st_s[...]
        mo_ref[...] = m_s[...]


def _mlstm(qm, km, vm, om, gif, gif_t, nb, seq_len, lc):
    nc = seq_len // lc
    row = lambda w: pl.BlockSpec((lc, w), lambda b, j: (b * nc + j, 0))
    return pl.pallas_call(
        functools.partial(_mlstm_kernel, lc=lc),
        grid=(nb, nc),
        in_specs=[row(512), row(512), row(1024), row(1024), row(LANES),
                  pl.BlockSpec((2 * M_HEADS, lc), lambda b, j: (0, b * nc + j))],
        out_specs=[row(1024),
                   pl.BlockSpec((None, M_HEADS // 2, LANES, 2 * LANES), lambda b, j: (b, 0, 0, 0)),
                   pl.BlockSpec((None, M_HEADS, LANES), lambda b, j: (b, 0, 0))],
        out_shape=[jax.ShapeDtypeStruct((nb * seq_len, 1024), BF16),
                   jax.ShapeDtypeStruct((nb, M_HEADS // 2, LANES, 2 * LANES), F32),
                   jax.ShapeDtypeStruct((nb, M_HEADS, LANES), F32)],
        scratch_shapes=[pltpu.VMEM((M_HEADS // 2, LANES, 2 * LANES), F32),
                        pltpu.VMEM((M_HEADS, LANES), F32)],
        compiler_params=pltpu.CompilerParams(dimension_semantics=("arbitrary", "arbitrary"),
                                             vmem_limit_bytes=VMEM_LIMIT),
        name="mlstm",
    )(qm, km, vm, om, gif, gif_t)


FF_CHUNK = 256


DEC_HEADS = HEADS // 2
DEC_ROWS = DEC_HEADS * HEAD_DIM


def _decode_attn_step(b, qkv_ref, c_refs, o_refs, ot_ref):
    sel = lax.broadcasted_iota(jnp.int32, (DEC_ROWS, LANES), 1) == b
    scale = HEAD_DIM ** -0.5

    def column(i):
        return jnp.sum(jnp.where(sel, qkv_ref[i], 0.0), axis=1, keepdims=True)

    def head(col, h):
        return col[h * HEAD_DIM:(h + 1) * HEAD_DIM, :]

    qc = [column(g) * scale for g in range(N_GROUPS)]
    kn = [column(N_GROUPS + g) for g in range(N_GROUPS)]
    vn = [column(2 * N_GROUPS + g) for g in range(N_GROUPS)]
    cols = []
    for h in range(DEC_HEADS):
        scores, m_h = [], None
        for g, d in enumerate(DILATIONS):
            wb = c_refs[g].shape[-1]
            qh = head(qc[g], h)
            s = jnp.sum(c_refs[g][0, h] * qh, axis=0, keepdims=True)
            pos = lax.broadcasted_iota(jnp.int32, (1, wb), 1)
            s = jnp.where((pos & (d - 1)) == 0, s, NEG)
            s_new = jnp.sum(head(kn[g], h) * qh, axis=0, keepdims=True)
            m_g = jnp.maximum(jnp.max(s, axis=1, keepdims=True), s_new)
            m_h = m_g if m_h is None else jnp.maximum(m_h, m_g)
            scores.append((s, s_new))
        l_h = jnp.zeros((1, 1), F32)
        acc = jnp.zeros((HEAD_DIM, 1), F32)
        for g in range(N_GROUPS):
            s, s_new = scores[g]
            p = jnp.exp(s - m_h)
            p_new = jnp.exp(s_new - m_h)
            l_h = l_h + jnp.sum(p, axis=1, keepdims=True) + p_new
            acc = acc + jnp.sum(c_refs[g][1, h] * p, axis=1, keepdims=True) + p_new * head(vn[g], h)
        cols.append(acc / l_h)
    ot_ref[...] = jnp.broadcast_to(jnp.concatenate(cols, axis=0), ot_ref.shape)
    for g in range(N_GROUPS):
        wb = c_refs[g].shape[-1]
        last = lax.broadcasted_iota(jnp.int32, (HEAD_DIM, wb), 1) == wb - 1
        for kvi, new in ((0, kn[g]), (1, vn[g])):
            for h in range(DEC_HEADS):
                rolled = pltpu.roll(c_refs[g][kvi, h], wb - 1, 1)
                o_refs[g][kvi, h] = jnp.where(last, head(new, h), rolled)


def _merge_ffn_kernel(x_ref, o_ref, hm_ref, ga_ref, gb_ref, wa_ref, wm_ref, wo_ref, g2_ref,
                      wg_ref, wu_ref, wd_ref, *rest, with_decode):
    if with_decode:
        qkv_ref, c0, c1, c2, y_ref, o0, o1, o2, ot_ref = rest
        _decode_attn_step(pl.program_id(0) // 2, qkv_ref, (c0, c1, c2), (o0, o1, o2), ot_ref)
    else:
        (y_ref,) = rest
    o_att = jnp.concatenate([o_ref[p] for p in range(PAIRS)], axis=1).astype(BF16)
    ya = jnp.dot(o_att, wa_ref[...], preferred_element_type=F32)
    yb = jnp.dot(hm_ref[...].astype(BF16), wm_ref[...], preferred_element_type=F32)
    mixed = (jax.nn.sigmoid(ga_ref[...].astype(F32)) * ya
             + jax.nn.sigmoid(gb_ref[...].astype(F32)) * yb)
    x2 = x_ref[...] + jnp.dot(mixed.astype(BF16), wo_ref[...], preferred_element_type=F32)
    h2 = (x2 * lax.rsqrt(jnp.mean(x2 * x2, axis=-1, keepdims=True) + NORM_EPS)
          * g2_ref[...]).astype(BF16)
    acc = x2
    d_ff = wg_ref.shape[1]

    def gate_up(c):
        gt = jnp.dot(h2, wg_ref[:, c:c + FF_CHUNK], preferred_element_type=F32)
        up = jnp.dot(h2, wu_ref[:, c:c + FF_CHUNK], preferred_element_type=F32)
        return gt, up

    chunks = list(range(0, d_ff, FF_CHUNK))
    nxt = gate_up(chunks[0])
    for i, c in enumerate(chunks):
        gt, up = nxt
        if i + 1 < len(chunks):
            nxt = gate_up(chunks[i + 1])
        ff = (gt * jax.nn.sigmoid(gt) * up).astype(BF16)
        acc = acc + jnp.dot(ff, wd_ref[c:c + FF_CHUNK, :], preferred_element_type=F32)
    y_ref[...] = acc


def _merge_ffn(x2d, o_att, hm, ga, gb, wa, wm, wo, g2, wg, wu, wd, seq_len, tm, decode=None):
    m_rows, d_model = x2d.shape
    tiles_per_seq = seq_len // tm
    steps = m_rows // tm
    row = lambda w: pl.BlockSpec((tm, w), lambda i: (i, 0))
    o_spec = pl.BlockSpec((None, PAIRS, tm, LANES),
                          lambda i: (i // tiles_per_seq, 0, i % tiles_per_seq, 0))
    in_specs = [row(d_model), o_spec, row(1024), row(1024), row(1024),
                _const_spec(wa.shape), _const_spec(wm.shape), _const_spec(wo.shape),
                _const_spec((1, d_model)), _const_spec(wg.shape), _const_spec(wu.shape),
                _const_spec(wd.shape)]
    args = [x2d, o_att, hm, ga, gb, wa, wm, wo, g2, wg, wu, wd]
    out_specs = [row(d_model)]
    out_shape = [jax.ShapeDtypeStruct((m_rows, d_model), F32)]
    if decode is not None:
        qkv_t, caches_t = decode
        db = caches_t[0].shape[0]
        assert steps == 2 * db, (steps, db)
        in_specs.append(pl.BlockSpec((qkv_t.shape[0], DEC_ROWS, LANES), lambda i: (0, i % 2, 0)))
        args.append(qkv_t)
        for c in caches_t:
            spec = pl.BlockSpec((None, 2, DEC_HEADS, HEAD_DIM, c.shape[-1]),
                                lambda i: (i // 2, 0, i % 2, 0, 0))
            in_specs.append(spec)
            args.append(c)
            out_specs.append(spec)
            out_shape.append(jax.ShapeDtypeStruct(c.shape, F32))
        out_specs.append(pl.BlockSpec((None, None, DEC_ROWS, LANES), lambda i: (i // 2, i % 2, 0, 0)))
        out_shape.append(jax.ShapeDtypeStruct((db, 2, DEC_ROWS, LANES), F32))
    outs = pl.pallas_call(
        functools.partial(_merge_ffn_kernel, with_decode=decode is not None),
        grid=(steps,),
        in_specs=in_specs, out_specs=out_specs, out_shape=out_shape,
        compiler_params=pltpu.CompilerParams(dimension_semantics=("arbitrary",),
                                             vmem_limit_bytes=VMEM_LIMIT),
        name="merge_ffn",
    )(*args)
    if decode is None:
        return outs[0]
    return outs[0], outs[1:4], outs[4]


def _sample_prep_kernel(*refs):
    for i, ref in enumerate(refs[:-1]):
        refs[-1][i] = ref[...].T


def _sample_prep(arrs):
    vm = pl.BlockSpec(memory_space=pltpu.VMEM)
    return pl.pallas_call(
        _sample_prep_kernel, in_specs=[vm] * len(arrs), out_specs=vm,
        out_shape=jax.ShapeDtypeStruct((len(arrs), GROUP_W, LANES), F32),
        name="sample_prep",
    )(*arrs)


def _sample_mlstm_kernel(q_ref, k_ref, v_ref, om_ref, gt_ref, mt_ref, n_ref, c_ref, rexp_ref,
                         hm_ref, co_ref, nt_ref, mo_ref, *, db):
    nh, dqk = M_HEADS, M_DQK
    hi = lax.Precision.HIGHEST
    rexp = rexp_ref[...]
    q_t = q_ref[...].T * (dqk ** -0.5)
    k_t = k_ref[...].T
    n_t = n_ref[...].T
    ig = gt_ref[0:nh, :]
    lf = _log_sigmoid(gt_ref[nh:2 * nh, :])
    m_prev = mt_ref[...]
    m_new = jnp.maximum(lf + m_prev, ig)
    w8 = jnp.exp(ig - m_new)
    dec8 = jnp.exp(lf + m_prev - m_new)
    head_sum = lambda a: lax.dot_general(rexp, a, (((0,), (0,)), ((), ())), precision=hi,
                                         preferred_element_type=F32)
    expand = lambda a: jnp.dot(rexp, a, precision=hi, preferred_element_type=F32)
    qk8 = head_sum(q_t * k_t)
    qn8 = head_sum(q_t * n_t)
    wqk8 = w8 * qk8
    den8 = dec8 * qn8 + wqk8
    inv8 = 1.0 / jnp.maximum(jnp.abs(den8), jnp.exp(-m_new))
    dec_x = expand(dec8)
    w_x = expand(w8)
    nt_ref[...] = dec_x * n_t + w_x * k_t
    mo_ref[...] = m_new
    wk_x = w_x * k_t
    for b in range(db):
        cb = c_ref[b].reshape(nh * dqk, M_DV)
        v_b = v_ref[b]
        v_x = jnp.concatenate([jnp.broadcast_to(v_b[h:h + 1, :], (dqk, M_DV)) for h in range(nh)],
                              axis=0)
        co_ref[b] = (dec_x[:, b:b + 1] * cb + wk_x[:, b:b + 1] * v_x).reshape(nh, dqk, M_DV)
        qc = jnp.sum((q_t[:, b:b + 1] * cb).reshape(nh, dqk, M_DV), axis=1)
        num = dec8[:, b:b + 1] * qc + wqk8[:, b:b + 1] * v_b
        hm_ref[b] = jax.nn.sigmoid(om_ref[b]) * (num * inv8[:, b:b + 1])


def _sample_mlstm(q_t, k_t, v3, om3, gif_t, m_t, n_pad, c_state, rexp, db):
    vm = pl.BlockSpec(memory_space=pltpu.VMEM)
    return pl.pallas_call(
        functools.partial(_sample_mlstm_kernel, db=db),
        in_specs=[vm] * 9,
        out_specs=[vm] * 4,
        out_shape=[jax.ShapeDtypeStruct((db, M_HEADS, M_DV), F32),
                   jax.ShapeDtypeStruct(c_state.shape, F32),
                   jax.ShapeDtypeStruct((M_HEADS * M_DQK, LANES), F32),
                   jax.ShapeDtypeStruct((M_HEADS, LANES), F32)],
        compiler_params=pltpu.CompilerParams(vmem_limit_bytes=VMEM_LIMIT),
        name="sample_mlstm",
    )(q_t, k_t, v3, om3, gif_t, m_t, n_pad, c_state, rexp)


def _rope_tables(pos):
    half = ROT_DIM // 2
    inv_freq = jnp.exp(-math.log(ROPE_THETA) * jnp.arange(0, ROT_DIM, 2, dtype=F32) / ROT_DIM)
    ang = pos[:, None] * inv_freq[None, :]
    cos, sin = jnp.cos(ang), jnp.sin(ang)
    t = pos.shape[0]
    rest = HEAD_DIM - ROT_DIM
    a = jnp.concatenate([cos, cos, jnp.ones((t, rest), F32)], axis=1)
    bm = jnp.concatenate([-sin, jnp.zeros((t, HEAD_DIM - half), F32)], axis=1)
    bp = jnp.concatenate([jnp.zeros((t, half), F32), sin, jnp.zeros((t, rest), F32)], axis=1)
    return tuple(jnp.concatenate([x, x], axis=1) for x in (a, bm, bp))


def _repack_w_in(w_in):
    d_model = w_in.shape[0]
    aw = _ATT_W
    cols = []
    for g in range(N_GROUPS):
        for base in (0, aw, 2 * aw):
            cols.append(w_in[:, base + g * GROUP_W: base + (g + 1) * GROUP_W])
    gif0 = 3 * aw + 2 * M_HEADS * M_DQK + 2 * M_HEADS * M_DV
    cols.append(w_in[:, 3 * aw:gif0])
    cols.append(w_in[:, gif0 + 2 * M_HEADS:])
    cols.append(w_in[:, gif0:gif0 + 2 * M_HEADS])
    cols.append(jnp.zeros((d_model, LANES - 2 * M_HEADS), w_in.dtype))
    w = jnp.concatenate(cols, axis=1).astype(BF16)
    assert w.shape[1] == _W_COLS
    return w


def kernel(x_prompt, x_sample, cache_kv_w128, cache_kv_w512, cache_kv_w2048, state_mlstm_C, state_mlstm_n, state_mlstm_m, norm1_g, w_in, b_if, q_norm_g, k_norm_g, w_att_out, w_m_out, w_o, norm2_g, w_gate, w_up, w_down):
    nb, seq_len, d_model = x_prompt.shape
    db, dec_seq, _ = x_sample.shape
    assert dec_seq == 1 and d_model == 1024 and seq_len % ATT_STEP == 0 and db <= LANES
    caches = (cache_kv_w128, cache_kv_w512, cache_kv_w2048)
    past_len = 8192

    w_perm = _repack_w_in(w_in)
    g1 = norm1_g.reshape(1, d_model)
    g2 = norm2_g.reshape(1, d_model)
    bif = jnp.concatenate([b_if, jnp.zeros((LANES - b_if.shape[0],), F32)]).reshape(1, LANES)
    qg = jnp.tile(q_norm_g, HEADS).reshape(1, GROUP_W)
    kg = jnp.tile(k_norm_g, HEADS).reshape(1, GROUP_W)
    hid = np.arange(GROUP_W // 2) // HEAD_DIM
    gmat = jnp.asarray(hid[:, None] == hid[None, :], dtype=BF16)
    wa, wm, wo = (w.astype(BF16) for w in (w_att_out, w_m_out, w_o))
    wg, wu, wd = (w.astype(BF16) for w in (w_gate, w_up, w_down))

    m_rows = nb * seq_len
    x2d = x_prompt.reshape(m_rows, d_model)
    tabs_p = _rope_tables(jnp.arange(seq_len, dtype=F32))
    outs = _proj(x2d, seq_len, DILATIONS, 256, False, g1, w_perm, bif, qg, kg, tabs_p, gmat)
    qs, ks, vs = outs[0:3], outs[3:6], outs[6:9]
    qm, km, vm, om, ga, gb, gif = outs[9:16]

    o_att = _attention(qs, ks, vs, seq_len)
    gif_t = gif[:, :2 * M_HEADS].T
    hm, st_p, m_p = _mlstm(qm, km, vm, om, gif, gif_t, nb, seq_len, 256)

    x_s = jnp.zeros((LANES, d_model), F32).at[:db].set(x_sample.reshape(db, d_model))
    tabs_s = _rope_tables(jnp.full((LANES,), float(past_len), F32))
    outs_s = _proj(x_s, LANES, (1, 1, 1), LANES, True, g1, w_perm, bif, qg, kg, tabs_s, gmat)
    qkv_t = _sample_prep(outs_s[0:9])
    caches_t = [c.transpose(0, 2, 3, 4, 1) for c in caches]
    y_prompt, kv_st, o_att_cols = _merge_ffn(x2d, o_att, hm, ga, gb, wa, wm, wo, g2, wg, wu, wd,
                                             seq_len, 256, decode=(qkv_t, caches_t))
    y_prompt = y_prompt.reshape(nb, seq_len, d_model)
    kv_s = [c.transpose(0, 4, 1, 2, 3) for c in kv_st]

    tails = _kv_tails(ks, vs, seq_len)
    kv_p = [t.reshape(nb, 2, HEADS, HEAD_DIM, t.shape[-1]).transpose(0, 4, 1, 2, 3) for t in tails]
    c_p = st_p[..., :M_DV].reshape(nb, M_HEADS, M_DQK, M_DV)
    n_p = st_p[..., M_DV].reshape(nb, M_HEADS, M_DQK)
    m_pr = m_p[:, :, 0]

    qm_t, km_t, vm_s, om_s, ga_s, gb_s, gif_s = outs_s[9:16]
    gif_ts = gif_s.T

    rexp = jnp.asarray(np.arange(M_HEADS * M_DQK)[:, None] // M_DQK == np.arange(M_HEADS)[None, :],
                       dtype=F32)
    m_t = jnp.zeros((M_HEADS, LANES), F32).at[:, :db].set(state_mlstm_m.T)
    n_pad = jnp.zeros((LANES, M_HEADS * M_DQK), F32).at[:db].set(state_mlstm_n.reshape(db, -1))
    hm_s3, c_s, n_t, m_so = _sample_mlstm(
        qm_t, km_t, vm_s[:db].reshape(db, M_HEADS, M_DV), om_s[:db].reshape(db, M_HEADS, M_DV),
        gif_ts[:2 * M_HEADS], m_t, n_pad, state_mlstm_C, rexp, db)
    n_s = n_t.T[:db].reshape(db, M_HEADS, M_DQK)
    m_s = m_so[:, :db].T

    o_att_s = o_att_cols[:, :, :, 0].reshape(db, PAIRS, LANES)
    o_att_sp = jnp.zeros((PAIRS, LANES, LANES), F32).at[:, :db].set(o_att_s.transpose(1, 0, 2))[None]
    hm_sp = jnp.zeros((LANES, 1024), F32).at[:db].set(hm_s3.reshape(db, 1024))
    y_s = _merge_ffn(x_s, o_att_sp, hm_sp, ga_s, gb_s, wa, wm, wo, g2, wg, wu, wd, LANES, LANES)
    y_sample = y_s[:db].reshape(db, 1, d_model)

    return (y_prompt, y_sample, kv_p[0], kv_p[1], kv_p[2], c_p, n_p, m_pr,
            kv_s[0], kv_s[1], kv_s[2], c_s, n_s, m_s)
```

```python
import functools
import math

import jax
import jax.numpy as jnp
import numpy as np
from jax import lax
from jax.experimental import pallas as pl
from jax.experimental.pallas import tpu as pltpu

F32 = jnp.float32
BF16 = jnp.bfloat16

HEAD_DIM = 64
HEADS = 8
GROUP_W = HEADS * HEAD_DIM
N_GROUPS = 3
WINDOWS = (128, 512, 2048)
DILATIONS = (1, 4, 16)
SPAN = 128
ROT_DIM = 16
ROPE_THETA = 500000.0
M_HEADS = 8
M_DQK = 64
M_DV = 128
NORM_EPS = 1e-6
PAIRS = GROUP_W // 128
NEG = -1e30

LANES = 128
VMEM_LIMIT = 56 * 1024 * 1024

_ATT_W = N_GROUPS * GROUP_W
_C_QM = 3 * _ATT_W
_C_KM = _C_QM + M_HEADS * M_DQK
_C_VM = _C_KM + M_HEADS * M_DQK
_C_OM = _C_VM + M_HEADS * M_DV
_C_GA = _C_OM + M_HEADS * M_DV
_C_GB = _C_GA + 1024
_W_COLS = _C_GB + 1024
_GIF_COLS = 2 * M_HEADS


def _const_spec(shape):
    nd = len(shape)
    return pl.BlockSpec(shape, lambda *_: (0,) * nd, pipeline_mode=pl.Buffered(1))


def _log_sigmoid(x):
    return jnp.minimum(x, 0.0) - jnp.log1p(jnp.exp(-jnp.abs(x)))


def _proj_kernel(x_ref, g1_ref, w_ref, wgif_ref, bif_ref, qg_ref, kg_ref, ra_ref, rm_ref, rp_ref,
                 gm_ref,
                 q0_ref, q1_ref, q2_ref, k0_ref, k1_ref, k2_ref, v0_ref, v1_ref, v2_ref,
                 qm_ref, km_ref, vm_ref, om_ref, ga_ref, gb_ref, gif_ref,
                 hs_ref, *, tm, dils, plain):
    d_model = x_ref.shape[1]
    x = x_ref[...]
    xn = x * lax.rsqrt(jnp.mean(x * x, axis=-1, keepdims=True) + NORM_EPS) * g1_ref[...]
    h_nat = xn.astype(BF16)
    n_slab = d_model // LANES
    if any(d > 1 for d in dils):
        for c in range(n_slab):
            hs_ref[c] = xn[:, c * LANES:(c + 1) * LANES]

    def permuted_h(d):
        if d == 1:
            return h_nat
        n = tm // d
        rows = [jnp.concatenate([hs_ref[c, pl.ds(r, n, stride=d), :] for c in range(n_slab)], axis=1)
                for r in range(d)]
        return jnp.concatenate(rows, axis=0).astype(BF16)

    def permuted_tab(ref, d):
        if d == 1:
            t = ref[...]
        else:
            n = tm // d
            t = jnp.concatenate([ref[pl.ds(r, n, stride=d), :] for r in range(d)], axis=0)
        return jnp.concatenate([t] * PAIRS, axis=1)

    gmat = gm_ref[...]

    def head_sumsq(z):
        zz = (z * z).astype(BF16)
        half = GROUP_W // 2
        return jnp.concatenate(
            [jnp.dot(zz[:, :half], gmat, preferred_element_type=F32),
             jnp.dot(zz[:, half:], gmat, preferred_element_type=F32)], axis=1)

    def norm_rope(z, ss, gain, ra, rm, rp):
        y = z * lax.rsqrt(ss * (1.0 / HEAD_DIM) + NORM_EPS) * gain
        return (y * ra + pltpu.roll(y, GROUP_W - ROT_DIM // 2, 1) * rm
                + pltpu.roll(y, ROT_DIM // 2, 1) * rp)

    def store_group(ref, y, d):
        if plain:
            ref[...] = y.astype(ref.dtype)
            return
        n = tm // d
        for p in range(PAIRS):
            ref[:, p] = y[:, p * LANES:(p + 1) * LANES].reshape(d, n, LANES).astype(ref.dtype)

    q_refs = (q0_ref, q1_ref, q2_ref)
    k_refs = (k0_ref, k1_ref, k2_ref)
    v_refs = (v0_ref, v1_ref, v2_ref)
    for g in range(N_GROUPS):
        d = dils[g]
        hg = permuted_h(d)
        ra, rm, rp = (permuted_tab(r, d) for r in (ra_ref, rm_ref, rp_ref))
        cq, ck, cv = (t * _ATT_W + g * GROUP_W for t in range(3))
        zq = jnp.dot(hg, w_ref[:, cq:cq + GROUP_W], preferred_element_type=F32)
        zk = jnp.dot(hg, w_ref[:, ck:ck + GROUP_W], preferred_element_type=F32)
        zv = jnp.dot(hg, w_ref[:, cv:cv + GROUP_W], preferred_element_type=F32)
        ssq, ssk = head_sumsq(zq), head_sumsq(zk)
        store_group(v_refs[g], zv, d)
        store_group(q_refs[g], norm_rope(zq, ssq, qg_ref[...], ra, rm, rp), d)
        store_group(k_refs[g], norm_rope(zk, ssk, kg_ref[...], ra, rm, rp), d)

    for ref, c0, width in ((qm_ref, _C_QM, 512), (km_ref, _C_KM, 512), (vm_ref, _C_VM, 1024),
                           (om_ref, _C_OM, 1024), (ga_ref, _C_GA, 1024), (gb_ref, _C_GB, 1024)):
        for cc in range(0, width, GROUP_W):
            z = jnp.dot(h_nat, w_ref[:, c0 + cc:c0 + cc + GROUP_W], preferred_element_type=F32)
            ref[:, cc:cc + GROUP_W] = z.astype(ref.dtype)
    zg = jnp.dot(h_nat, wgif_ref[...], preferred_element_type=F32)
    gif_ref[...] = zg + bif_ref[...]


def _proj(x2d, seq_len, dils, tm, plain, g1, w_perm, w_gif, bif, qg, kg, rope_tabs, gmat):
    m_rows, d_model = x2d.shape
    nb = m_rows // seq_len
    tiles_per_seq = seq_len // tm
    grid = (m_rows // tm,)
    row_spec = lambda w: pl.BlockSpec((tm, w), lambda i: (i, 0))
    tab_spec = pl.BlockSpec((tm, LANES), lambda i: (i % tiles_per_seq, 0))
    sds = jax.ShapeDtypeStruct

    widths = (512, 512, 1024, 1024, 1024, 1024)
    if plain:
        out_shape = ([sds((m_rows, GROUP_W), F32)] * 9 + [sds((m_rows, w), F32) for w in widths]
                     + [sds((m_rows, LANES), F32)])
        out_specs = ([row_spec(GROUP_W)] * 9 + [row_spec(w) for w in widths] + [row_spec(LANES)])
    else:
        def grp_spec(d):
            return pl.BlockSpec((None, d, PAIRS, tm // d, LANES),
                                lambda i: (i // tiles_per_seq, 0, 0, i % tiles_per_seq, 0))
        grp_shape = lambda d: sds((nb, d, PAIRS, seq_len // d, LANES), BF16)
        out_shape = ([grp_shape(d) for d in dils] * 3 + [sds((m_rows, w), BF16) for w in widths]
                     + [sds((m_rows, LANES), F32)])
        out_specs = ([grp_spec(d) for d in dils] * 3 + [row_spec(w) for w in widths]
                     + [row_spec(LANES)])
    in_specs = [row_spec(d_model), _const_spec((1, d_model)), _const_spec(w_perm.shape),
                _const_spec(w_gif.shape), _const_spec((1, LANES)), _const_spec((1, GROUP_W)),
                _const_spec((1, GROUP_W)), tab_spec, tab_spec, tab_spec, _const_spec(gmat.shape)]
    return pl.pallas_call(
        functools.partial(_proj_kernel, tm=tm, dils=dils, plain=plain),
        grid=grid, in_specs=in_specs, out_specs=out_specs, out_shape=out_shape,
        scratch_shapes=[pltpu.VMEM((d_model // LANES, tm, LANES), F32)],
        compiler_params=pltpu.CompilerParams(dimension_semantics=("arbitrary",),
                                             vmem_limit_bytes=VMEM_LIMIT),
        name="proj",
    )(x2d, g1, w_perm, w_gif, bif, qg, kg, *rope_tabs, gmat)


ATT_STEP = SPAN * max(DILATIONS)
ATT_UNROLL = 8


def _attn_kernel(q0, k0, v0, kh0, vh0, q1, k1, v1, kh1, vh1, q2, k2, v2, kh2, vh2,
                 o_ref, kb0, vb0, kb1, vb1, kb2, vb2, acc_s, m_s, l_s):
    j = pl.program_id(1)
    for kb, vb, kh, vh, kc, vc in ((kb0, vb0, kh0, vh0, k0, v0), (kb1, vb1, kh1, vh1, k1, v1),
                                   (kb2, vb2, kh2, vh2, k2, v2)):
        kb[:, 0:SPAN, :] = kh[...]
        kb[:, SPAN:, :] = kc[...]
        vb[:, 0:SPAN, :] = vh[...]
        vb[:, SPAN:, :] = vc[...]

    qi = lax.broadcasted_iota(jnp.int32, (SPAN, 2 * SPAN), 0)
    ci = lax.broadcasted_iota(jnp.int32, (SPAN, 2 * SPAN), 1)
    band = (ci >= qi) & (ci <= qi + SPAN)
    bias_band = jnp.where(band, 0.0, NEG).astype(F32)
    bias_first = jnp.where(band & (ci >= SPAN), 0.0, NEG).astype(F32)
    lane_q = lax.broadcasted_iota(jnp.int32, (SPAN, LANES), 1)
    lane_kv = lax.broadcasted_iota(jnp.int32, (2 * SPAN, LANES), 1)
    halves_q = (lane_q < HEAD_DIM, lane_q >= HEAD_DIM)
    halves_kv = (lane_kv < HEAD_DIM, lane_kv >= HEAD_DIM)
    scale = HEAD_DIM ** -0.5

    def blocks(g, q_ref, kb, vb, units):
        loaded = []
        for r, bi, _ in units:
            row0 = pl.multiple_of(bi * SPAN, SPAN)
            q2_ = q_ref[r, pl.ds(row0, SPAN), :]
            kk = kb[r, pl.ds(row0, 2 * SPAN), :]
            first = jnp.logical_and(j == 0, bi == 0)
            bias = jnp.where(first, bias_first, bias_band)
            ss = []
            for hh in range(2):
                qa = (jnp.where(halves_q[hh], q2_, jnp.zeros_like(q2_))
                      * jnp.asarray(scale, q2_.dtype))
                ss.append(lax.dot_general(qa, kk, (((1,), (1,)), ((), ())),
                                          preferred_element_type=F32) + bias)
            loaded.append(ss)
        probs = []
        for ss in loaded:
            mxs = [jnp.max(s, axis=-1, keepdims=True) for s in ss]
            ps = [jnp.exp(s - mx) for s, mx in zip(ss, mxs)]
            ls = [jnp.sum(p, axis=-1, keepdims=True) for p in ps]
            probs.append(([p.astype(BF16) for p in ps], mxs, ls))
        for (r, bi, sl), (ps, mxs, ls) in zip(units, probs):
            row0 = pl.multiple_of(bi * SPAN, SPAN)
            vv = vb[r, pl.ds(row0, 2 * SPAN), :]
            acc = None
            for hh in range(2):
                vh = jnp.where(halves_kv[hh], vv, jnp.zeros_like(vv))
                a = jnp.dot(ps[hh], vh, preferred_element_type=F32)
                acc = a if acc is None else acc + a
            acc_s[g, sl, :] = acc
            m_s[g, sl, :] = jnp.where(halves_q[0], mxs[0], mxs[1])
            l_s[g, sl, :] = jnp.where(halves_q[0], ls[0], ls[1])

    d0, d1, d2 = DILATIONS
    nb0, nb1, nb2 = (ATT_STEP // d // SPAN for d in DILATIONS)
    un = ATT_UNROLL

    def body0(it, c):
        units = []
        for u in range(un):
            bi = it * un + u
            units.append((0, bi, pl.ds(pl.multiple_of(bi * SPAN, SPAN), SPAN)))
        blocks(0, q0, kb0, vb0, units)
        return c
    lax.fori_loop(0, nb0 // un, body0, 0)

    def body1(it, c):
        units = []
        for u in range(un):
            idx = it * un + u
            r, bi = idx // nb1, idx % nb1
            units.append((r, bi, pl.ds(bi * SPAN * d1 + r, SPAN, stride=d1)))
        blocks(1, q1, kb1, vb1, units)
        return c
    lax.fori_loop(0, d1 * nb1 // un, body1, 0)

    def body2(it, c):
        units = []
        for u in range(un):
            r = it * un + u
            units.append((r, 0, pl.ds(r, SPAN, stride=d2)))
        blocks(2, q2, kb2, vb2, units)
        return c
    lax.fori_loop(0, d2 * nb2 // un, body2, 0)

    def combine(ci_, c):
        sl = pl.ds(pl.multiple_of(ci_ * SPAN, SPAN), SPAN)
        ms = [m_s[g, sl, :] for g in range(N_GROUPS)]
        mm = jnp.maximum(jnp.maximum(ms[0], ms[1]), ms[2])
        es = [jnp.exp(m - mm) for m in ms]
        num = es[0] * acc_s[0, sl, :] + es[1] * acc_s[1, sl, :] + es[2] * acc_s[2, sl, :]
        den = es[0] * l_s[0, sl, :] + es[1] * l_s[1, sl, :] + es[2] * l_s[2, sl, :]
        o_ref[sl, :] = (num / den).astype(o_ref.dtype)
        return c
    lax.fori_loop(0, ATT_STEP // SPAN, combine, 0)


def _attention(qs, ks, vs, seq_len):
    nb = qs[0].shape[0]
    steps = seq_len // ATT_STEP
    in_specs, args, scratch = [], [], []
    for g, d in enumerate(DILATIONS):
        rows = ATT_STEP // d
        cur = pl.BlockSpec((None, d, None, rows, LANES), lambda b, j, p: (b, 0, p, j, 0))
        ratio = rows // SPAN
        halo = pl.BlockSpec((None, d, None, SPAN, LANES),
                            lambda b, j, p, ratio=ratio: (b, 0, p, jnp.maximum(j * ratio - 1, 0), 0))
        in_specs += [cur, cur, cur, halo, halo]
        args += [qs[g], ks[g], vs[g], ks[g], vs[g]]
        scratch += [pltpu.VMEM((d, SPAN + rows, LANES), BF16)] * 2
    scratch += [pltpu.VMEM((N_GROUPS, ATT_STEP, LANES), F32)] * 3
    return pl.pallas_call(
        _attn_kernel,
        grid=(nb, steps, PAIRS),
        in_specs=in_specs,
        out_specs=pl.BlockSpec((None, None, ATT_STEP, LANES), lambda b, j, p: (b, p, j, 0)),
        out_shape=jax.ShapeDtypeStruct((nb, PAIRS, seq_len, LANES), BF16),
        scratch_shapes=scratch,
        compiler_params=pltpu.CompilerParams(
            dimension_semantics=("arbitrary", "arbitrary", "arbitrary"),
            vmem_limit_bytes=VMEM_LIMIT),
        name="attn",
    )(*args)


def _kv_tail_kernel(k0, v0, k1, v1, k2, v2, o0, o1, o2, nat_ref):
    for (k_ref, v_ref, o_ref, d) in ((k0, v0, o0, DILATIONS[0]), (k1, v1, o1, DILATIONS[1]),
                                     (k2, v2, o2, DILATIONS[2])):
        for kvi, ref in enumerate((k_ref, v_ref)):
            for p in range(PAIRS):
                rows = slice(p * LANES, (p + 1) * LANES)
                if d == 1:
                    o_ref[kvi, rows, :] = ref[0, p].astype(F32).T
                    continue
                for r in range(d):
                    nat_ref[pl.ds(r, SPAN, stride=d), :] = ref[r, p].astype(F32)
                for c in range(d):
                    o_ref[kvi, rows, c * SPAN:(c + 1) * SPAN] = nat_ref[c * SPAN:(c + 1) * SPAN, :].T


def _kv_tails(ks, vs, seq_len):
    nb = ks[0].shape[0]
    in_specs, args, out_specs, out_shape = [], [], [], []
    for g, d in enumerate(DILATIONS):
        last = seq_len // d // SPAN - 1
        spec = pl.BlockSpec((None, d, PAIRS, SPAN, LANES), lambda b, last=last: (b, 0, 0, last, 0))
        in_specs += [spec, spec]
        args += [ks[g], vs[g]]
        out_specs.append(pl.BlockSpec((None, 2, GROUP_W, SPAN * d), lambda b: (b, 0, 0, 0)))
        out_shape.append(jax.ShapeDtypeStruct((nb, 2, GROUP_W, SPAN * d), F32))
    return pl.pallas_call(
        _kv_tail_kernel,
        grid=(nb,), in_specs=in_specs, out_specs=out_specs, out_shape=out_shape,
        scratch_shapes=[pltpu.VMEM((SPAN * max(DILATIONS), LANES), F32)],
        compiler_params=pltpu.CompilerParams(dimension_semantics=("arbitrary",),
                                             vmem_limit_bytes=VMEM_LIMIT),
        name="kv_tail",
    )(*args)


def _mlstm_kernel(q_ref, k_ref, v_ref, om_ref, gif_ref, gt_ref, hm_ref, st_ref, mo_ref,
                  st_s, m_s, *, lc):
    j = pl.program_id(1)

    @pl.when(j == 0)
    def _():
        st_s[...] = jnp.zeros_like(st_s)
        m_s[...] = jnp.zeros_like(m_s)

    gif = gif_ref[...]
    gt = gt_ref[...]
    lf_c = _log_sigmoid(gif)
    lf_r = _log_sigmoid(gt[M_HEADS:2 * M_HEADS, :])
    row = lax.broadcasted_iota(jnp.int32, (lc, lc), 0)
    col = lax.broadcasted_iota(jnp.int32, (lc, lc), 1)
    causal = col <= row
    tril = causal.astype(F32)
    triu = (row <= col).astype(F32)
    b_c = jnp.dot(tril, lf_c, precision=lax.Precision.HIGHEST, preferred_element_type=F32)
    b_r = jnp.dot(lf_r, triu, precision=lax.Precision.HIGHEST, preferred_element_type=F32)
    lane = lax.broadcasted_iota(jnp.int32, (lc, LANES), 1)
    halves = (lane < M_DQK, lane >= M_DQK)
    sub = lax.broadcasted_iota(jnp.int32, (LANES, 1), 0)
    ones_blk = jnp.ones((lc, LANES), BF16)
    qscale = M_DQK ** -0.5

    states = [st_s[p] for p in range(M_HEADS // 2)]
    phase1 = []
    for h in range(M_HEADS):
        p, hh = divmod(h, 2)
        q2_ = q_ref[:, p * LANES:(p + 1) * LANES]
        k2_ = k_ref[:, p * LANES:(p + 1) * LANES]
        qa = jnp.where(halves[hh], q2_, jnp.zeros_like(q2_)) * jnp.asarray(qscale, q2_.dtype)
        s = lax.dot_general(qa, k2_, (((1,), (1,)), ((), ())), preferred_element_type=F32)
        qc = jnp.dot(qa, states[p].astype(BF16), preferred_element_type=F32)
        phase1.append((s, qc))

    phase2 = []
    for h in range(M_HEADS):
        p, hh = divmod(h, 2)
        s, qc = phase1[h]
        bcol = b_c[:, M_HEADS + h:M_HEADS + h + 1]
        igcol = gif[:, h:h + 1]
        a_row = gt[h:h + 1, :] - b_r[h:h + 1, :]
        amat = jnp.where(causal, a_row, NEG)
        m_prev = m_s[h:h + 1, 0:1]
        g_t = jnp.maximum(m_prev, jnp.max(amat, axis=-1, keepdims=True))
        wqk = (jnp.exp(amat - g_t) * s).astype(BF16)
        dec = jnp.exp(m_prev - g_t)
        m_t = bcol + g_t
        m_new = m_t[lc - 1:lc, :]
        b_last = bcol[lc - 1:lc, :]
        dstate = jnp.exp(b_last + m_prev - m_new)
        ws = jnp.exp(b_last - bcol + igcol - m_new)
        k2_ = k_ref[:, p * LANES:(p + 1) * LANES]
        ka = jnp.where(halves[hh], k2_, jnp.zeros_like(k2_))
        kws = (ka.astype(F32) * ws).astype(BF16)
        m_s[h:h + 1, :] = jnp.broadcast_to(m_new, (1, LANES))
        phase2.append((wqk, dec * qc, jnp.exp(-m_t), dstate, kws))

    upds = []
    for h in range(M_HEADS):
        wqk, dqc, floor, _, kws = phase2[h]
        v1 = jnp.concatenate([v_ref[:, h * M_DV:(h + 1) * M_DV], ones_blk], axis=1)
        num_den = dqc + jnp.dot(wqk, v1, preferred_element_type=F32)
        num = num_den[:, :M_DV]
        den = num_den[:, M_DV:]
        hval = num / jnp.maximum(jnp.abs(den), floor)
        gate = jax.nn.sigmoid(om_ref[:, h * M_DV:(h + 1) * M_DV].astype(F32))
        hm_ref[:, h * M_DV:(h + 1) * M_DV] = (gate * hval).astype(hm_ref.dtype)
        upds.append(lax.dot_general(kws, v1, (((0,), (0,)), ((), ())),
                                    preferred_element_type=F32))
    for p in range(M_HEADS // 2):
        drow = jnp.where(sub < M_DQK, phase2[2 * p][3], phase2[2 * p + 1][3])
        st_s[p] = drow * states[p] + upds[2 * p] + upds[2 * p + 1]

    @pl.when(j == pl.num_programs(1) - 1)
    def _():
        st_ref[...] = st_s[...]
        mo_ref[...] = m_s[...]


def _mlstm(qm, km, vm, om, gif, gif_t, nb, seq_len, lc):
    nc = seq_len // lc
    row = lambda w: pl.BlockSpec((lc, w), lambda b, j: (b * nc + j, 0))
    return pl.pallas_call(
        functools.partial(_mlstm_kernel, lc=lc),
        grid=(nb, nc),
        in_specs=[row(512), row(512), row(1024), row(1024), row(LANES),
                  pl.BlockSpec((2 * M_HEADS, lc), lambda b, j: (0, b * nc + j))],
        out_specs=[row(1024),
                   pl.BlockSpec((None, M_HEADS // 2, LANES, 2 * LANES), lambda b, j: (b, 0, 0, 0)),
                   pl.BlockSpec((None, M_HEADS, LANES), lambda b, j: (b, 0, 0))],
        out_shape=[jax.ShapeDtypeStruct((nb * seq_len, 1024), BF16),
                   jax.ShapeDtypeStruct((nb, M_HEADS // 2, LANES, 2 * LANES), F32),
                   jax.ShapeDtypeStruct((nb, M_HEADS, LANES), F32)],
        scratch_shapes=[pltpu.VMEM((M_HEADS // 2, LANES, 2 * LANES), F32),
                        pltpu.VMEM((M_HEADS, LANES), F32)],
        compiler_params=pltpu.CompilerParams(dimension_semantics=("arbitrary", "arbitrary"),
                                             vmem_limit_bytes=VMEM_LIMIT),
        name="mlstm",
    )(qm, km, vm, om, gif, gif_t)


FF_CHUNK = 256


DEC_HEADS = HEADS // 2
DEC_ROWS = DEC_HEADS * HEAD_DIM


def _decode_attn_step(b, qkv_ref, c_refs, o_refs, ot_ref):
    sel = lax.broadcasted_iota(jnp.int32, (DEC_ROWS, LANES), 1) == b
    scale = HEAD_DIM ** -0.5

    def column(i):
        return jnp.sum(jnp.where(sel, qkv_ref[i], 0.0), axis=1, keepdims=True)

    def head(col, h):
        return col[h * HEAD_DIM:(h + 1) * HEAD_DIM, :]

    qc = [column(g) * scale for g in range(N_GROUPS)]
    kn = [column(N_GROUPS + g) for g in range(N_GROUPS)]
    vn = [column(2 * N_GROUPS + g) for g in range(N_GROUPS)]
    cols = []
    for h in range(DEC_HEADS):
        scores, m_h = [], None
        for g, d in enumerate(DILATIONS):
            wb = c_refs[g].shape[-1]
            qh = head(qc[g], h)
            s = jnp.sum(c_refs[g][0, h] * qh, axis=0, keepdims=True)
            pos = lax.broadcasted_iota(jnp.int32, (1, wb), 1)
            s = jnp.where((pos & (d - 1)) == 0, s, NEG)
            s_new = jnp.sum(head(kn[g], h) * qh, axis=0, keepdims=True)
            m_g = jnp.maximum(jnp.max(s, axis=1, keepdims=True), s_new)
            m_h = m_g if m_h is None else jnp.maximum(m_h, m_g)
            scores.append((s, s_new))
        l_h = jnp.zeros((1, 1), F32)
        acc = jnp.zeros((HEAD_DIM, 1), F32)
        for g in range(N_GROUPS):
            s, s_new = scores[g]
            p = jnp.exp(s - m_h)
            p_new = jnp.exp(s_new - m_h)
            l_h = l_h + jnp.sum(p, axis=1, keepdims=True) + p_new
            acc = acc + jnp.sum(c_refs[g][1, h] * p, axis=1, keepdims=True) + p_new * head(vn[g], h)
        cols.append(acc / l_h)
    ot_ref[...] = jnp.broadcast_to(jnp.concatenate(cols, axis=0), ot_ref.shape)
    for g in range(N_GROUPS):
        wb = c_refs[g].shape[-1]
        last = lax.broadcasted_iota(jnp.int32, (HEAD_DIM, wb), 1) == wb - 1
        for kvi, new in ((0, kn[g]), (1, vn[g])):
            for h in range(DEC_HEADS):
                rolled = pltpu.roll(c_refs[g][kvi, h], wb - 1, 1)
                o_refs[g][kvi, h] = jnp.where(last, head(new, h), rolled)


def _merge_ffn_kernel(x_ref, o_ref, hm_ref, ga_ref, gb_ref, wa_ref, wm_ref, wo_ref, g2_ref,
                      wg_ref, wu_ref, wd_ref, *rest, with_decode):
    if with_decode:
        qkv_ref, c0, c1, c2, y_ref, o0, o1, o2, ot_ref = rest
        _decode_attn_step(pl.program_id(0) // 2, qkv_ref, (c0, c1, c2), (o0, o1, o2), ot_ref)
    else:
        (y_ref,) = rest
    o_att = jnp.concatenate([o_ref[p] for p in range(PAIRS)], axis=1).astype(BF16)
    ya = jnp.dot(o_att, wa_ref[...], preferred_element_type=F32)
    yb = jnp.dot(hm_ref[...].astype(BF16), wm_ref[...], preferred_element_type=F32)
    mixed = (jax.nn.sigmoid(ga_ref[...].astype(F32)) * ya
             + jax.nn.sigmoid(gb_ref[...].astype(F32)) * yb)
    x2 = x_ref[...] + jnp.dot(mixed.astype(BF16), wo_ref[...], preferred_element_type=F32)
    h2 = (x2 * lax.rsqrt(jnp.mean(x2 * x2, axis=-1, keepdims=True) + NORM_EPS)
          * g2_ref[...]).astype(BF16)
    acc = x2
    d_ff = wg_ref.shape[1]

    def gate_up(c):
        gt = jnp.dot(h2, wg_ref[:, c:c + FF_CHUNK], preferred_element_type=F32)
        up = jnp.dot(h2, wu_ref[:, c:c + FF_CHUNK], preferred_element_type=F32)
        return gt, up

    chunks = list(range(0, d_ff, FF_CHUNK))
    nxt = gate_up(chunks[0])
    for i, c in enumerate(chunks):
        gt, up = nxt
        if i + 1 < len(chunks):
            nxt = gate_up(chunks[i + 1])
        ff = (gt * jax.nn.sigmoid(gt) * up).astype(BF16)
        acc = acc + jnp.dot(ff, wd_ref[c:c + FF_CHUNK, :], preferred_element_type=F32)
    y_ref[...] = acc


def _merge_ffn(x2d, o_att, hm, ga, gb, wa, wm, wo, g2, wg, wu, wd, seq_len, tm, decode=None):
    m_rows, d_model = x2d.shape
    tiles_per_seq = seq_len // tm
    steps = m_rows // tm
    row = lambda w: pl.BlockSpec((tm, w), lambda i: (i, 0))
    o_spec = pl.BlockSpec((None, PAIRS, tm, LANES),
                          lambda i: (i // tiles_per_seq, 0, i % tiles_per_seq, 0))
    in_specs = [row(d_model), o_spec, row(1024), row(1024), row(1024),
                _const_spec(wa.shape), _const_spec(wm.shape), _const_spec(wo.shape),
                _const_spec((1, d_model)), _const_spec(wg.shape), _const_spec(wu.shape),
                _const_spec(wd.shape)]
    args = [x2d, o_att, hm, ga, gb, wa, wm, wo, g2, wg, wu, wd]
    out_specs = [row(d_model)]
    out_shape = [jax.ShapeDtypeStruct((m_rows, d_model), F32)]
    if decode is not None:
        qkv_t, caches_t = decode
        db = caches_t[0].shape[0]
        assert steps == 2 * db, (steps, db)
        in_specs.append(pl.BlockSpec((qkv_t.shape[0], DEC_ROWS, LANES), lambda i: (0, i % 2, 0)))
        args.append(qkv_t)
        for c in caches_t:
            spec = pl.BlockSpec((None, 2, DEC_HEADS, HEAD_DIM, c.shape[-1]),
                                lambda i: (i // 2, 0, i % 2, 0, 0))
            in_specs.append(spec)
            args.append(c)
            out_specs.append(spec)
            out_shape.append(jax.ShapeDtypeStruct(c.shape, F32))
        out_specs.append(pl.BlockSpec((None, None, DEC_ROWS, LANES), lambda i: (i // 2, i % 2, 0, 0)))
        out_shape.append(jax.ShapeDtypeStruct((db, 2, DEC_ROWS, LANES), F32))
    outs = pl.pallas_call(
        functools.partial(_merge_ffn_kernel, with_decode=decode is not None),
        grid=(steps,),
        in_specs=in_specs, out_specs=out_specs, out_shape=out_shape,
        compiler_params=pltpu.CompilerParams(dimension_semantics=("arbitrary",),
                                             vmem_limit_bytes=VMEM_LIMIT),
        name="merge_ffn",
    )(*args)
    if decode is None:
        return outs[0]
    return outs[0], outs[1:4], outs[4]


def _sample_prep_kernel(*refs):
    for i, ref in enumerate(refs[:-1]):
        refs[-1][i] = ref[...].T


def _sample_prep(arrs):
    vm = pl.BlockSpec(memory_space=pltpu.VMEM)
    return pl.pallas_call(
        _sample_prep_kernel, in_specs=[vm] * len(arrs), out_specs=vm,
        out_shape=jax.ShapeDtypeStruct((len(arrs), GROUP_W, LANES), F32),
        name="sample_prep",
    )(*arrs)


def _sample_mlstm_kernel(q_ref, k_ref, v_ref, om_ref, gt_ref, mt_ref, n_ref, c_ref, rexp_ref,
                         hm_ref, co_ref, nt_ref, mo_ref, *, db):
    nh, dqk = M_HEADS, M_DQK
    hi = lax.Precision.HIGHEST
    rexp = rexp_ref[...]
    q_t = q_ref[...].T * (dqk ** -0.5)
    k_t = k_ref[...].T
    n_t = n_ref[...].T
    ig = gt_ref[0:nh, :]
    lf = _log_sigmoid(gt_ref[nh:2 * nh, :])
    m_prev = mt_ref[...]
    m_new = jnp.maximum(lf + m_prev, ig)
    w8 = jnp.exp(ig - m_new)
    dec8 = jnp.exp(lf + m_prev - m_new)
    head_sum = lambda a: lax.dot_general(rexp, a, (((0,), (0,)), ((), ())), precision=hi,
                                         preferred_element_type=F32)
    expand = lambda a: jnp.dot(rexp, a, precision=hi, preferred_element_type=F32)
    qk8 = head_sum(q_t * k_t)
    qn8 = head_sum(q_t * n_t)
    wqk8 = w8 * qk8
    den8 = dec8 * qn8 + wqk8
    inv8 = 1.0 / jnp.maximum(jnp.abs(den8), jnp.exp(-m_new))
    dec_x = expand(dec8)
    w_x = expand(w8)
    nt_ref[...] = dec_x * n_t + w_x * k_t
    mo_ref[...] = m_new
    wk_x = w_x * k_t
    for b in range(db):
        cb = c_ref[b].reshape(nh * dqk, M_DV)
        v_b = v_ref[b]
        v_x = jnp.concatenate([jnp.broadcast_to(v_b[h:h + 1, :], (dqk, M_DV)) for h in range(nh)],
                              axis=0)
        co_ref[b] = (dec_x[:, b:b + 1] * cb + wk_x[:, b:b + 1] * v_x).reshape(nh, dqk, M_DV)
        qc = jnp.sum((q_t[:, b:b + 1] * cb).reshape(nh, dqk, M_DV), axis=1)
        num = dec8[:, b:b + 1] * qc + wqk8[:, b:b + 1] * v_b
        hm_ref[b] = jax.nn.sigmoid(om_ref[b]) * (num * inv8[:, b:b + 1])


def _sample_mlstm(q_t, k_t, v3, om3, gif_t, m_t, n_pad, c_state, rexp, db):
    vm = pl.BlockSpec(memory_space=pltpu.VMEM)
    return pl.pallas_call(
        functools.partial(_sample_mlstm_kernel, db=db),
        in_specs=[vm] * 9,
        out_specs=[vm] * 4,
        out_shape=[jax.ShapeDtypeStruct((db, M_HEADS, M_DV), F32),
                   jax.ShapeDtypeStruct(c_state.shape, F32),
                   jax.ShapeDtypeStruct((M_HEADS * M_DQK, LANES), F32),
                   jax.ShapeDtypeStruct((M_HEADS, LANES), F32)],
        compiler_params=pltpu.CompilerParams(vmem_limit_bytes=VMEM_LIMIT),
        name="sample_mlstm",
    )(q_t, k_t, v3, om3, gif_t, m_t, n_pad, c_state, rexp)


def _rope_tables(pos):
    half = ROT_DIM // 2
    pos = np.asarray(pos, np.float32)
    inv_freq = np.exp(np.float32(-math.log(ROPE_THETA))
                      * np.arange(0, ROT_DIM, 2, dtype=np.float32) / np.float32(ROT_DIM))
    ang = (pos[:, None] * inv_freq[None, :]).astype(np.float32)
    cos, sin = np.cos(ang).astype(np.float32), np.sin(ang).astype(np.float32)
    t = pos.shape[0]
    rest = HEAD_DIM - ROT_DIM
    a = np.concatenate([cos, cos, np.ones((t, rest), np.float32)], axis=1)
    bm = np.concatenate([-sin, np.zeros((t, HEAD_DIM - half), np.float32)], axis=1)
    bp = np.concatenate([np.zeros((t, half), np.float32), sin, np.zeros((t, rest), np.float32)],
                        axis=1)
    return tuple(jnp.asarray(np.concatenate([x, x], axis=1)) for x in (a, bm, bp))


W_PREP_BLK = 512


def _w_prep_kernel(wt_ref, o_ref):
    o_ref[...] = wt_ref[...].T.astype(o_ref.dtype)


def _w_prep(w_in):
    d_model = w_in.shape[0]
    wt = w_in.T
    n_head = _C_GA // W_PREP_BLK
    n_blk = _W_COLS // W_PREP_BLK

    sub = 8

    def src_row(j):
        return (j * (W_PREP_BLK // sub) + jnp.where(j < n_head, 0, _GIF_COLS // sub)) * sub

    return pl.pallas_call(
        _w_prep_kernel,
        grid=(n_blk,),
        in_specs=[pl.BlockSpec((pl.Element(W_PREP_BLK), pl.Element(d_model)),
                               lambda j: (src_row(j), 0))],
        out_specs=pl.BlockSpec((d_model, W_PREP_BLK), lambda j: (0, j)),
        out_shape=jax.ShapeDtypeStruct((d_model, _W_COLS), BF16),
        compiler_params=pltpu.CompilerParams(dimension_semantics=("arbitrary",),
                                             vmem_limit_bytes=VMEM_LIMIT),
        name="w_prep",
    )(wt)


def kernel(x_prompt, x_sample, cache_kv_w128, cache_kv_w512, cache_kv_w2048, state_mlstm_C, state_mlstm_n, state_mlstm_m, norm1_g, w_in, b_if, q_norm_g, k_norm_g, w_att_out, w_m_out, w_o, norm2_g, w_gate, w_up, w_down):
    nb, seq_len, d_model = x_prompt.shape
    db, dec_seq, _ = x_sample.shape
    assert dec_seq == 1 and d_model == 1024 and seq_len % ATT_STEP == 0 and db <= LANES
    caches = (cache_kv_w128, cache_kv_w512, cache_kv_w2048)
    past_len = 8192

    w_perm = _w_prep(w_in)
    w_gif = jnp.zeros((d_model, LANES), BF16).at[:, :_GIF_COLS].set(
        w_in[:, _C_GA:_C_GA + _GIF_COLS].astype(BF16))
    g1 = norm1_g.reshape(1, d_model)
    g2 = norm2_g.reshape(1, d_model)
    bif = jnp.concatenate([b_if, jnp.zeros((LANES - b_if.shape[0],), F32)]).reshape(1, LANES)
    qg = jnp.tile(q_norm_g, HEADS).reshape(1, GROUP_W)
    kg = jnp.tile(k_norm_g, HEADS).reshape(1, GROUP_W)
    hid = np.arange(GROUP_W // 2) // HEAD_DIM
    gmat = jnp.asarray(hid[:, None] == hid[None, :], dtype=BF16)
    wa, wm, wo = (w.astype(BF16) for w in (w_att_out, w_m_out, w_o))
    wg, wu, wd = (w.astype(BF16) for w in (w_gate, w_up, w_down))

    m_rows = nb * seq_len
    x2d = x_prompt.reshape(m_rows, d_model)
    tabs_p = _rope_tables(np.arange(seq_len))
    outs = _proj(x2d, seq_len, DILATIONS, 256, False, g1, w_perm, w_gif, bif, qg, kg, tabs_p, gmat)
    qs, ks, vs = outs[0:3], outs[3:6], outs[6:9]
    qm, km, vm, om, ga, gb, gif = outs[9:16]

    o_att = _attention(qs, ks, vs, seq_len)
    gif_t = gif[:, :2 * M_HEADS].T
    hm, st_p, m_p = _mlstm(qm, km, vm, om, gif, gif_t, nb, seq_len, 256)

    x_s = jnp.zeros((LANES, d_model), F32).at[:db].set(x_sample.reshape(db, d_model))
    tabs_s = _rope_tables(np.full((LANES,), past_len))
    outs_s = _proj(x_s, LANES, (1, 1, 1), LANES, True, g1, w_perm, w_gif, bif, qg, kg, tabs_s, gmat)
    qkv_t = _sample_prep(outs_s[0:9])
    caches_t = [c.transpose(0, 2, 3, 4, 1) for c in caches]
    y_prompt, kv_st, o_att_cols = _merge_ffn(x2d, o_att, hm, ga, gb, wa, wm, wo, g2, wg, wu, wd,
                                             seq_len, 256, decode=(qkv_t, caches_t))
    y_prompt = y_prompt.reshape(nb, seq_len, d_model)
    kv_s = [c.transpose(0, 4, 1, 2, 3) for c in kv_st]

    tails = _kv_tails(ks, vs, seq_len)
    kv_p = [t.reshape(nb, 2, HEADS, HEAD_DIM, t.shape[-1]).transpose(0, 4, 1, 2, 3) for t in tails]
    c_p = st_p[..., :M_DV].reshape(nb, M_HEADS, M_DQK, M_DV)
    n_p = st_p[..., M_DV].reshape(nb, M_HEADS, M_DQK)
    m_pr = m_p[:, :, 0]

    qm_t, km_t, vm_s, om_s, ga_s, gb_s, gif_s = outs_s[9:16]
    gif_ts = gif_s.T

    rexp = jnp.asarray(np.arange(M_HEADS * M_DQK)[:, None] // M_DQK == np.arange(M_HEADS)[None, :],
                       dtype=F32)
    m_t = jnp.zeros((M_HEADS, LANES), F32).at[:, :db].set(state_mlstm_m.T)
    n_pad = jnp.zeros((LANES, M_HEADS * M_DQK), F32).at[:db].set(state_mlstm_n.reshape(db, -1))
    hm_s3, c_s, n_t, m_so = _sample_mlstm(
        qm_t, km_t, vm_s[:db].reshape(db, M_HEADS, M_DV), om_s[:db].reshape(db, M_HEADS, M_DV),
        gif_ts[:2 * M_HEADS], m_t, n_pad, state_mlstm_C, rexp, db)
    n_s = n_t.T[:db].reshape(db, M_HEADS, M_DQK)
    m_s = m_so[:, :db].T

    o_att_s = o_att_cols[:, :, :, 0].reshape(db, PAIRS, LANES)
    o_att_sp = jnp.zeros((PAIRS, LANES, LANES), F32).at[:, :db].set(o_att_s.transpose(1, 0, 2))[None]
    hm_sp = jnp.zeros((LANES, 1024), F32).at[:db].set(hm_s3.reshape(db, 1024))
    y_s = _merge_ffn(x_s, o_att_sp, hm_sp, ga_s, gb_s, wa, wm, wo, g2, wg, wu, wd, LANES, LANES)
    y_sample = y_s[:db].reshape(db, 1, d_model)

    return (y_prompt, y_sample, kv_p[0], kv_p[1], kv_p[2], c_p, n_p, m_pr,
            kv_s[0], kv_s[1], kv_s[2], c_s, n_s, m_s)
```

```python
import functools
import math

import jax
import jax.numpy as jnp
import numpy as np
from jax import lax
from jax.experimental import pallas as pl
from jax.experimental.pallas import tpu as pltpu

F32 = jnp.float32
BF16 = jnp.bfloat16

HEAD_DIM = 64
HEADS = 8
GROUP_W = HEADS * HEAD_DIM
N_GROUPS = 3
WINDOWS = (128, 512, 2048)
DILATIONS = (1, 4, 16)
SPAN = 128
ROT_DIM = 16
ROPE_THETA = 500000.0
M_HEADS = 8
M_DQK = 64
M_DV = 128
NORM_EPS = 1e-6
PAIRS = GROUP_W // 128
NEG = -1e30

LANES = 128
VMEM_LIMIT = 56 * 1024 * 1024

_ATT_W = N_GROUPS * GROUP_W
_C_QM = 3 * _ATT_W
_C_KM = _C_QM + M_HEADS * M_DQK
_C_VM = _C_KM + M_HEADS * M_DQK
_C_OM = _C_VM + M_HEADS * M_DV
_C_GA = _C_OM + M_HEADS * M_DV
_C_GB = _C_GA + 1024
_W_COLS = _C_GB + 1024
_GIF_COLS = 2 * M_HEADS


def _const_spec(shape):
    nd = len(shape)
    return pl.BlockSpec(shape, lambda *_: (0,) * nd, pipeline_mode=pl.Buffered(1))


def _log_sigmoid(x):
    return jnp.minimum(x, 0.0) - jnp.log1p(jnp.exp(-jnp.abs(x)))


def _proj_kernel(x_ref, g1_ref, w_ref, wgif_ref, bif_ref, qg_ref, kg_ref, ra_ref, rm_ref, rp_ref,
                 gm_ref,
                 q0_ref, q1_ref, q2_ref, k0_ref, k1_ref, k2_ref, v0_ref, v1_ref, v2_ref,
                 *rest, tm, dils, plain, tiles_per_seq):
    if plain:
        qm_ref, km_ref, vm_ref, om_ref, ga_ref, gb_ref, gif_ref, hs_ref = rest
    else:
        hm_ref, ga_ref, gb_ref, st_ref, mo_ref, hs_ref, st_s, m_s = rest
    d_model = x_ref.shape[1]
    x = x_ref[...]
    xn = x * lax.rsqrt(jnp.mean(x * x, axis=-1, keepdims=True) + NORM_EPS) * g1_ref[...]
    h_nat = xn.astype(BF16)
    n_slab = d_model // LANES
    if any(d > 1 for d in dils):
        for c in range(n_slab):
            hs_ref[c] = xn[:, c * LANES:(c + 1) * LANES]

    def permuted_h(d):
        if d == 1:
            return h_nat
        n = tm // d
        rows = [jnp.concatenate([hs_ref[c, pl.ds(r, n, stride=d), :] for c in range(n_slab)], axis=1)
                for r in range(d)]
        return jnp.concatenate(rows, axis=0).astype(BF16)

    def permuted_tab(ref, d):
        if d == 1:
            t = ref[...]
        else:
            n = tm // d
            t = jnp.concatenate([ref[pl.ds(r, n, stride=d), :] for r in range(d)], axis=0)
        return jnp.concatenate([t] * PAIRS, axis=1)

    gmat = gm_ref[...]

    def head_sumsq(z):
        zz = (z * z).astype(BF16)
        half = GROUP_W // 2
        return jnp.concatenate(
            [jnp.dot(zz[:, :half], gmat, preferred_element_type=F32),
             jnp.dot(zz[:, half:], gmat, preferred_element_type=F32)], axis=1)

    def norm_rope(z, ss, gain, ra, rm, rp):
        y = z * lax.rsqrt(ss * (1.0 / HEAD_DIM) + NORM_EPS) * gain
        return (y * ra + pltpu.roll(y, GROUP_W - ROT_DIM // 2, 1) * rm
                + pltpu.roll(y, ROT_DIM // 2, 1) * rp)

    def store_group(ref, y, d):
        if plain:
            ref[...] = y.astype(ref.dtype)
            return
        n = tm // d
        for p in range(PAIRS):
            ref[:, p] = y[:, p * LANES:(p + 1) * LANES].reshape(d, n, LANES).astype(ref.dtype)

    def seg(c0, width):
        return jnp.dot(h_nat, w_ref[:, c0:c0 + width], preferred_element_type=F32)

    if not plain:
        @pl.when(pl.program_id(0) % tiles_per_seq == 0)
        def _():
            st_s[...] = jnp.zeros_like(st_s)
            m_s[...] = jnp.zeros_like(m_s)

        gif = jnp.dot(h_nat, wgif_ref[...], preferred_element_type=F32) + bif_ref[...]
        m_phase1, m_phase2, m_phase3 = _mlstm_chunk(
            seg(_C_QM, 512).astype(BF16), seg(_C_KM, 512).astype(BF16),
            seg(_C_VM, 1024).astype(BF16), seg(_C_OM, 1024).astype(BF16),
            gif, gif.T[:2 * M_HEADS, :], st_s, m_s, hm_ref, tm)
        m_lo, m_hi = (0, 1), (2, 3)
        m_phase1(m_lo)
        m_phase2(m_lo)

    q_refs = (q0_ref, q1_ref, q2_ref)
    k_refs = (k0_ref, k1_ref, k2_ref)
    v_refs = (v0_ref, v1_ref, v2_ref)
    for g in range(N_GROUPS):
        d = dils[g]
        hg = permuted_h(d)
        ra, rm, rp = (permuted_tab(r, d) for r in (ra_ref, rm_ref, rp_ref))
        cq, ck, cv = (t * _ATT_W + g * GROUP_W for t in range(3))
        zq = jnp.dot(hg, w_ref[:, cq:cq + GROUP_W], preferred_element_type=F32)
        zk = jnp.dot(hg, w_ref[:, ck:ck + GROUP_W], preferred_element_type=F32)
        zv = jnp.dot(hg, w_ref[:, cv:cv + GROUP_W], preferred_element_type=F32)
        ssq, ssk = head_sumsq(zq), head_sumsq(zk)
        store_group(v_refs[g], zv, d)
        store_group(q_refs[g], norm_rope(zq, ssq, qg_ref[...], ra, rm, rp), d)
        store_group(k_refs[g], norm_rope(zk, ssk, kg_ref[...], ra, rm, rp), d)

        if g == 0 and not plain:
            m_phase3(m_lo)
            m_phase1(m_hi)
            m_phase2(m_hi)
        if g == 1 and not plain:
            m_phase3(m_hi)

    if plain:
        segments = ((qm_ref, _C_QM, 512), (km_ref, _C_KM, 512), (vm_ref, _C_VM, 1024),
                    (om_ref, _C_OM, 1024), (ga_ref, _C_GA, 1024), (gb_ref, _C_GB, 1024))
        zg = jnp.dot(h_nat, wgif_ref[...], preferred_element_type=F32)
        gif_ref[...] = zg + bif_ref[...]
    else:
        st_ref[...] = st_s[...]
        mo_ref[...] = m_s[...]
        segments = ((ga_ref, _C_GA, 1024), (gb_ref, _C_GB, 1024))
    for ref, c0, width in segments:
        for cc in range(0, width, GROUP_W):
            ref[:, cc:cc + GROUP_W] = seg(c0 + cc, GROUP_W).astype(ref.dtype)


def _proj(x2d, seq_len, dils, tm, plain, g1, w_perm, w_gif, bif, qg, kg, rope_tabs, gmat):
    m_rows, d_model = x2d.shape
    nb = m_rows // seq_len
    tiles_per_seq = seq_len // tm
    grid = (m_rows // tm,)
    row_spec = lambda w: pl.BlockSpec((tm, w), lambda i: (i, 0))
    tab_spec = pl.BlockSpec((tm, LANES), lambda i: (i % tiles_per_seq, 0))
    sds = jax.ShapeDtypeStruct

    widths = (512, 512, 1024, 1024, 1024, 1024)
    scratch = [pltpu.VMEM((d_model // LANES, tm, LANES), F32)]
    if plain:
        out_shape = ([sds((m_rows, GROUP_W), F32)] * 9 + [sds((m_rows, w), F32) for w in widths]
                     + [sds((m_rows, LANES), F32)])
        out_specs = ([row_spec(GROUP_W)] * 9 + [row_spec(w) for w in widths] + [row_spec(LANES)])
    else:
        def grp_spec(d):
            return pl.BlockSpec((None, d, PAIRS, tm // d, LANES),
                                lambda i: (i // tiles_per_seq, 0, 0, i % tiles_per_seq, 0))
        grp_shape = lambda d: sds((nb, d, PAIRS, seq_len // d, LANES), BF16)
        state_spec = lambda *dims: pl.BlockSpec((None,) + dims,
                                                lambda i: (i // tiles_per_seq,) + (0,) * len(dims))
        st_dims, m_dims = (M_HEADS // 2, LANES, 2 * LANES), (M_HEADS, LANES)
        out_shape = ([grp_shape(d) for d in dils] * 3 + [sds((m_rows, 1024), BF16)] * 3
                     + [sds((nb,) + st_dims, F32), sds((nb,) + m_dims, F32)])
        out_specs = ([grp_spec(d) for d in dils] * 3 + [row_spec(1024)] * 3
                     + [state_spec(*st_dims), state_spec(*m_dims)])
        scratch += [pltpu.VMEM(st_dims, F32), pltpu.VMEM(m_dims, F32)]
    in_specs = [row_spec(d_model), _const_spec((1, d_model)), _const_spec(w_perm.shape),
                _const_spec(w_gif.shape), _const_spec((1, LANES)), _const_spec((1, GROUP_W)),
                _const_spec((1, GROUP_W)), tab_spec, tab_spec, tab_spec, _const_spec(gmat.shape)]
    return pl.pallas_call(
        functools.partial(_proj_kernel, tm=tm, dils=dils, plain=plain,
                          tiles_per_seq=tiles_per_seq),
        grid=grid, in_specs=in_specs, out_specs=out_specs, out_shape=out_shape,
        scratch_shapes=scratch,
        compiler_params=pltpu.CompilerParams(dimension_semantics=("arbitrary",),
                                             vmem_limit_bytes=VMEM_LIMIT),
        name="proj",
    )(x2d, g1, w_perm, w_gif, bif, qg, kg, *rope_tabs, gmat)


ATT_STEP = SPAN * max(DILATIONS)
ATT_UNROLL = 8


def _attn_kernel(q0, k0, v0, kh0, vh0, q1, k1, v1, kh1, vh1, q2, k2, v2, kh2, vh2,
                 o_ref, kb0, vb0, kb1, vb1, kb2, vb2, acc_s, m_s, l_s):
    j = pl.program_id(1)
    for kb, vb, kh, vh, kc, vc in ((kb0, vb0, kh0, vh0, k0, v0), (kb1, vb1, kh1, vh1, k1, v1),
                                   (kb2, vb2, kh2, vh2, k2, v2)):
        kb[:, 0:SPAN, :] = kh[...]
        kb[:, SPAN:, :] = kc[...]
        vb[:, 0:SPAN, :] = vh[...]
        vb[:, SPAN:, :] = vc[...]

    qi = lax.broadcasted_iota(jnp.int32, (SPAN, 2 * SPAN), 0)
    ci = lax.broadcasted_iota(jnp.int32, (SPAN, 2 * SPAN), 1)
    band = (ci >= qi) & (ci <= qi + SPAN)
    bias_band = jnp.where(band, 0.0, NEG).astype(F32)
    bias_first = jnp.where(band & (ci >= SPAN), 0.0, NEG).astype(F32)
    lane_q = lax.broadcasted_iota(jnp.int32, (SPAN, LANES), 1)
    lane_kv = lax.broadcasted_iota(jnp.int32, (2 * SPAN, LANES), 1)
    halves_q = (lane_q < HEAD_DIM, lane_q >= HEAD_DIM)
    halves_kv = (lane_kv < HEAD_DIM, lane_kv >= HEAD_DIM)
    scale = HEAD_DIM ** -0.5

    def blocks(g, q_ref, kb, vb, units):
        loaded = []
        for r, bi, _ in units:
            row0 = pl.multiple_of(bi * SPAN, SPAN)
            q2_ = q_ref[r, pl.ds(row0, SPAN), :]
            kk = kb[r, pl.ds(row0, 2 * SPAN), :]
            first = jnp.logical_and(j == 0, bi == 0)
            bias = jnp.where(first, bias_first, bias_band)
            ss = []
            for hh in range(2):
                qa = (jnp.where(halves_q[hh], q2_, jnp.zeros_like(q2_))
                      * jnp.asarray(scale, q2_.dtype))
                ss.append(lax.dot_general(qa, kk, (((1,), (1,)), ((), ())),
                                          preferred_element_type=F32) + bias)
            loaded.append(ss)
        probs = []
        for ss in loaded:
            mxs = [jnp.max(s, axis=-1, keepdims=True) for s in ss]
            ps = [jnp.exp(s - mx) for s, mx in zip(ss, mxs)]
            ls = [jnp.sum(p, axis=-1, keepdims=True) for p in ps]
            probs.append(([p.astype(BF16) for p in ps], mxs, ls))
        for (r, bi, sl), (ps, mxs, ls) in zip(units, probs):
            row0 = pl.multiple_of(bi * SPAN, SPAN)
            vv = vb[r, pl.ds(row0, 2 * SPAN), :]
            acc = None
            for hh in range(2):
                vh = jnp.where(halves_kv[hh], vv, jnp.zeros_like(vv))
                a = jnp.dot(ps[hh], vh, preferred_element_type=F32)
                acc = a if acc is None else acc + a
            acc_s[g, sl, :] = acc
            m_s[g, sl, :] = jnp.where(halves_q[0], mxs[0], mxs[1])
            l_s[g, sl, :] = jnp.where(halves_q[0], ls[0], ls[1])

    d0, d1, d2 = DILATIONS
    nb0, nb1, nb2 = (ATT_STEP // d // SPAN for d in DILATIONS)
    un = ATT_UNROLL

    def body0(it, c):
        units = []
        for u in range(un):
            bi = it * un + u
            units.append((0, bi, pl.ds(pl.multiple_of(bi * SPAN, SPAN), SPAN)))
        blocks(0, q0, kb0, vb0, units)
        return c
    lax.fori_loop(0, nb0 // un, body0, 0)

    def body1(it, c):
        units = []
        for u in range(un):
            idx = it * un + u
            r, bi = idx // nb1, idx % nb1
            units.append((r, bi, pl.ds(bi * SPAN * d1 + r, SPAN, stride=d1)))
        blocks(1, q1, kb1, vb1, units)
        return c
    lax.fori_loop(0, d1 * nb1 // un, body1, 0)

    def body2(it, c):
        units = []
        for u in range(un):
            r = it * un + u
            units.append((r, 0, pl.ds(r, SPAN, stride=d2)))
        blocks(2, q2, kb2, vb2, units)
        return c
    lax.fori_loop(0, d2 * nb2 // un, body2, 0)

    def combine(ci_, c):
        sl = pl.ds(pl.multiple_of(ci_ * SPAN, SPAN), SPAN)
        ms = [m_s[g, sl, :] for g in range(N_GROUPS)]
        mm = jnp.maximum(jnp.maximum(ms[0], ms[1]), ms[2])
        es = [jnp.exp(m - mm) for m in ms]
        num = es[0] * acc_s[0, sl, :] + es[1] * acc_s[1, sl, :] + es[2] * acc_s[2, sl, :]
        den = es[0] * l_s[0, sl, :] + es[1] * l_s[1, sl, :] + es[2] * l_s[2, sl, :]
        o_ref[sl, :] = (num / den).astype(o_ref.dtype)
        return c
    lax.fori_loop(0, ATT_STEP // SPAN, combine, 0)


def _attention(qs, ks, vs, seq_len):
    nb = qs[0].shape[0]
    steps = seq_len // ATT_STEP
    in_specs, args, scratch = [], [], []
    for g, d in enumerate(DILATIONS):
        rows = ATT_STEP // d
        cur = pl.BlockSpec((None, d, None, rows, LANES), lambda b, j, p: (b, 0, p, j, 0))
        ratio = rows // SPAN
        halo = pl.BlockSpec((None, d, None, SPAN, LANES),
                            lambda b, j, p, ratio=ratio: (b, 0, p, jnp.maximum(j * ratio - 1, 0), 0))
        in_specs += [cur, cur, cur, halo, halo]
        args += [qs[g], ks[g], vs[g], ks[g], vs[g]]
        scratch += [pltpu.VMEM((d, SPAN + rows, LANES), BF16)] * 2
    scratch += [pltpu.VMEM((N_GROUPS, ATT_STEP, LANES), F32)] * 3
    return pl.pallas_call(
        _attn_kernel,
        grid=(nb, steps, PAIRS),
        in_specs=in_specs,
        out_specs=pl.BlockSpec((None, None, ATT_STEP, LANES), lambda b, j, p: (b, p, j, 0)),
        out_shape=jax.ShapeDtypeStruct((nb, PAIRS, seq_len, LANES), BF16),
        scratch_shapes=scratch,
        compiler_params=pltpu.CompilerParams(
            dimension_semantics=("arbitrary", "arbitrary", "arbitrary"),
            vmem_limit_bytes=VMEM_LIMIT),
        name="attn",
    )(*args)


def _kv_tail_kernel(k0, v0, k1, v1, k2, v2, o0, o1, o2, nat_ref):
    for (k_ref, v_ref, o_ref, d) in ((k0, v0, o0, DILATIONS[0]), (k1, v1, o1, DILATIONS[1]),
                                     (k2, v2, o2, DILATIONS[2])):
        for kvi, ref in enumerate((k_ref, v_ref)):
            for p in range(PAIRS):
                rows = slice(p * LANES, (p + 1) * LANES)
                if d == 1:
                    o_ref[kvi, rows, :] = ref[0, p].astype(F32).T
                    continue
                for r in range(d):
                    nat_ref[pl.ds(r, SPAN, stride=d), :] = ref[r, p].astype(F32)
                for c in range(d):
                    o_ref[kvi, rows, c * SPAN:(c + 1) * SPAN] = nat_ref[c * SPAN:(c + 1) * SPAN, :].T


def _kv_tails(ks, vs, seq_len):
    nb = ks[0].shape[0]
    in_specs, args, out_specs, out_shape = [], [], [], []
    for g, d in enumerate(DILATIONS):
        last = seq_len // d // SPAN - 1
        spec = pl.BlockSpec((None, d, PAIRS, SPAN, LANES), lambda b, last=last: (b, 0, 0, last, 0))
        in_specs += [spec, spec]
        args += [ks[g], vs[g]]
        out_specs.append(pl.BlockSpec((None, 2, GROUP_W, SPAN * d), lambda b: (b, 0, 0, 0)))
        out_shape.append(jax.ShapeDtypeStruct((nb, 2, GROUP_W, SPAN * d), F32))
    return pl.pallas_call(
        _kv_tail_kernel,
        grid=(nb,), in_specs=in_specs, out_specs=out_specs, out_shape=out_shape,
        scratch_shapes=[pltpu.VMEM((SPAN * max(DILATIONS), LANES), F32)],
        compiler_params=pltpu.CompilerParams(dimension_semantics=("arbitrary",),
                                             vmem_limit_bytes=VMEM_LIMIT),
        name="kv_tail",
    )(*args)


def _mlstm_chunk(q2d, k2d, v2d, om2d, gif, gt, st_s, m_s, hm_ref, lc):
    lf_c = _log_sigmoid(gif)
    lf_r = _log_sigmoid(gt[M_HEADS:2 * M_HEADS, :])
    row = lax.broadcasted_iota(jnp.int32, (lc, lc), 0)
    col = lax.broadcasted_iota(jnp.int32, (lc, lc), 1)
    causal = col <= row
    tril = causal.astype(F32)
    triu = (row <= col).astype(F32)
    b_c = jnp.dot(tril, lf_c, precision=lax.Precision.HIGHEST, preferred_element_type=F32)
    b_r = jnp.dot(lf_r, triu, precision=lax.Precision.HIGHEST, preferred_element_type=F32)
    lane = lax.broadcasted_iota(jnp.int32, (lc, LANES), 1)
    halves = (lane < M_DQK, lane >= M_DQK)
    sub = lax.broadcasted_iota(jnp.int32, (LANES, 1), 0)
    ones_blk = jnp.ones((lc, LANES), BF16)
    qscale = M_DQK ** -0.5

    states, p1, p2 = {}, {}, {}
    all_pairs = tuple(range(M_HEADS // 2))
    heads_of = lambda pairs: [2 * p + hh for p in pairs for hh in range(2)]

    def phase1(pairs=all_pairs):
        for p in pairs:
            states[p] = st_s[p]
        for h in heads_of(pairs):
            p, hh = divmod(h, 2)
            q2_ = q2d[:, p * LANES:(p + 1) * LANES]
            k2_ = k2d[:, p * LANES:(p + 1) * LANES]
            qa = jnp.where(halves[hh], q2_, jnp.zeros_like(q2_)) * jnp.asarray(qscale, q2_.dtype)
            s = lax.dot_general(qa, k2_, (((1,), (1,)), ((), ())), preferred_element_type=F32)
            qc = jnp.dot(qa, states[p].astype(BF16), preferred_element_type=F32)
            p1[h] = (s, qc)

    def phase2(pairs=all_pairs):
        for h in heads_of(pairs):
            p, hh = divmod(h, 2)
            s, qc = p1[h]
            bcol = b_c[:, M_HEADS + h:M_HEADS + h + 1]
            igcol = gif[:, h:h + 1]
            a_row = gt[h:h + 1, :] - b_r[h:h + 1, :]
            amat = jnp.where(causal, a_row, NEG)
            m_prev = m_s[h:h + 1, 0:1]
            g_t = jnp.maximum(m_prev, jnp.max(amat, axis=-1, keepdims=True))
            wqk = (jnp.exp(amat - g_t) * s).astype(BF16)
            dec = jnp.exp(m_prev - g_t)
            m_t = bcol + g_t
            m_new = m_t[lc - 1:lc, :]
            b_last = bcol[lc - 1:lc, :]
            dstate = jnp.exp(b_last + m_prev - m_new)
            ws = jnp.exp(b_last - bcol + igcol - m_new)
            k2_ = k2d[:, p * LANES:(p + 1) * LANES]
            ka = jnp.where(halves[hh], k2_, jnp.zeros_like(k2_))
            kws = (ka.astype(F32) * ws).astype(BF16)
            m_s[h:h + 1, :] = jnp.broadcast_to(m_new, (1, LANES))
            p2[h] = (wqk, dec * qc, jnp.exp(-m_t), dstate, kws)

    def phase3(pairs=all_pairs):
        upds = {}
        for h in heads_of(pairs):
            wqk, dqc, floor, _, kws = p2[h]
            v1 = jnp.concatenate([v2d[:, h * M_DV:(h + 1) * M_DV], ones_blk], axis=1)
            num_den = dqc + jnp.dot(wqk, v1, preferred_element_type=F32)
            num = num_den[:, :M_DV]
            den = num_den[:, M_DV:]
            hval = num / jnp.maximum(jnp.abs(den), floor)
            gate = jax.nn.sigmoid(om2d[:, h * M_DV:(h + 1) * M_DV].astype(F32))
            hm_ref[:, h * M_DV:(h + 1) * M_DV] = (gate * hval).astype(hm_ref.dtype)
            upds[h] = lax.dot_general(kws, v1, (((0,), (0,)), ((), ())),
                                      preferred_element_type=F32)
        for p in pairs:
            drow = jnp.where(sub < M_DQK, p2[2 * p][3], p2[2 * p + 1][3])
            st_s[p] = drow * states[p] + upds[2 * p] + upds[2 * p + 1]

    return phase1, phase2, phase3


def _mlstm_kernel(q_ref, k_ref, v_ref, om_ref, gif_ref, gt_ref, hm_ref, st_ref, mo_ref,
                  st_s, m_s, *, lc):
    j = pl.program_id(1)

    @pl.when(j == 0)
    def _():
        st_s[...] = jnp.zeros_like(st_s)
        m_s[...] = jnp.zeros_like(m_s)

    phases = _mlstm_chunk(q_ref[...], k_ref[...], v_ref[...], om_ref[...], gif_ref[...],
                          gt_ref[...], st_s, m_s, hm_ref, lc)
    for phase in phases:
        phase()

    @pl.when(j == pl.num_programs(1) - 1)
    def _():
        st_ref[...] = st_s[...]
        mo_ref[...] = m_s[...]


def _mlstm(qm, km, vm, om, gif, gif_t, nb, seq_len, lc):
    nc = seq_len // lc
    row = lambda w: pl.BlockSpec((lc, w), lambda b, j: (b * nc + j, 0))
    return pl.pallas_call(
        functools.partial(_mlstm_kernel, lc=lc),
        grid=(nb, nc),
        in_specs=[row(512), row(512), row(1024), row(1024), row(LANES),
                  pl.BlockSpec((2 * M_HEADS, lc), lambda b, j: (0, b * nc + j))],
        out_specs=[row(1024),
                   pl.BlockSpec((None, M_HEADS // 2, LANES, 2 * LANES), lambda b, j: (b, 0, 0, 0)),
                   pl.BlockSpec((None, M_HEADS, LANES), lambda b, j: (b, 0, 0))],
        out_shape=[jax.ShapeDtypeStruct((nb * seq_len, 1024), BF16),
                   jax.ShapeDtypeStruct((nb, M_HEADS // 2, LANES, 2 * LANES), F32),
                   jax.ShapeDtypeStruct((nb, M_HEADS, LANES), F32)],
        scratch_shapes=[pltpu.VMEM((M_HEADS // 2, LANES, 2 * LANES), F32),
                        pltpu.VMEM((M_HEADS, LANES), F32)],
        compiler_params=pltpu.CompilerParams(dimension_semantics=("arbitrary", "arbitrary"),
                                             vmem_limit_bytes=VMEM_LIMIT),
        name="mlstm",
    )(qm, km, vm, om, gif, gif_t)


FF_CHUNK = 256


DEC_HEADS = HEADS // 2
DEC_ROWS = DEC_HEADS * HEAD_DIM


def _decode_attn_step(b, qkv_ref, c_refs, o_refs, ot_ref):
    sel = lax.broadcasted_iota(jnp.int32, (DEC_ROWS, LANES), 1) == b
    scale = HEAD_DIM ** -0.5

    def column(i):
        return jnp.sum(jnp.where(sel, qkv_ref[i], 0.0), axis=1, keepdims=True)

    def head(col, h):
        return col[h * HEAD_DIM:(h + 1) * HEAD_DIM, :]

    qc = [column(g) * scale for g in range(N_GROUPS)]
    kn = [column(N_GROUPS + g) for g in range(N_GROUPS)]
    vn = [column(2 * N_GROUPS + g) for g in range(N_GROUPS)]
    cols = []
    for h in range(DEC_HEADS):
        scores, m_h = [], None
        for g, d in enumerate(DILATIONS):
            wb = c_refs[g].shape[-1]
            qh = head(qc[g], h)
            s = jnp.sum(c_refs[g][0, h] * qh, axis=0, keepdims=True)
            pos = lax.broadcasted_iota(jnp.int32, (1, wb), 1)
            s = jnp.where((pos & (d - 1)) == 0, s, NEG)
            s_new = jnp.sum(head(kn[g], h) * qh, axis=0, keepdims=True)
            m_g = jnp.maximum(jnp.max(s, axis=1, keepdims=True), s_new)
            m_h = m_g if m_h is None else jnp.maximum(m_h, m_g)
            scores.append((s, s_new))
        l_h = jnp.zeros((1, 1), F32)
        acc = jnp.zeros((HEAD_DIM, 1), F32)
        for g in range(N_GROUPS):
            s, s_new = scores[g]
            p = jnp.exp(s - m_h)
            p_new = jnp.exp(s_new - m_h)
            l_h = l_h + jnp.sum(p, axis=1, keepdims=True) + p_new
            acc = acc + jnp.sum(c_refs[g][1, h] * p, axis=1, keepdims=True) + p_new * head(vn[g], h)
        cols.append(acc / l_h)
    ot_ref[...] = jnp.broadcast_to(jnp.concatenate(cols, axis=0), ot_ref.shape)
    for g in range(N_GROUPS):
        wb = c_refs[g].shape[-1]
        last = lax.broadcasted_iota(jnp.int32, (HEAD_DIM, wb), 1) == wb - 1
        for kvi, new in ((0, kn[g]), (1, vn[g])):
            for h in range(DEC_HEADS):
                rolled = pltpu.roll(c_refs[g][kvi, h], wb - 1, 1)
                o_refs[g][kvi, h] = jnp.where(last, head(new, h), rolled)


def _merge_ffn_kernel(x_ref, o_ref, hm_ref, ga_ref, gb_ref, wa_ref, wm_ref, wo_ref, g2_ref,
                      wg_ref, wu_ref, wd_ref, *rest, with_decode):
    if with_decode:
        qkv_ref, c0, c1, c2, y_ref, o0, o1, o2, ot_ref = rest
        _decode_attn_step(pl.program_id(0) // 2, qkv_ref, (c0, c1, c2), (o0, o1, o2), ot_ref)
    else:
        (y_ref,) = rest
    o_att = jnp.concatenate([o_ref[p] for p in range(PAIRS)], axis=1).astype(BF16)
    ya = jnp.dot(o_att, wa_ref[...], preferred_element_type=F32)
    yb = jnp.dot(hm_ref[...].astype(BF16), wm_ref[...], preferred_element_type=F32)
    mixed = (jax.nn.sigmoid(ga_ref[...].astype(F32)) * ya
             + jax.nn.sigmoid(gb_ref[...].astype(F32)) * yb)
    x2 = x_ref[...] + jnp.dot(mixed.astype(BF16), wo_ref[...], preferred_element_type=F32)
    h2 = (x2 * lax.rsqrt(jnp.mean(x2 * x2, axis=-1, keepdims=True) + NORM_EPS)
          * g2_ref[...]).astype(BF16)
    acc = x2
    d_ff = wg_ref.shape[1]

    def gate_up(c):
        gt = jnp.dot(h2, wg_ref[:, c:c + FF_CHUNK], preferred_element_type=F32)
        up = jnp.dot(h2, wu_ref[:, c:c + FF_CHUNK], preferred_element_type=F32)
        return gt, up

    chunks = list(range(0, d_ff, FF_CHUNK))
    nxt = gate_up(chunks[0])
    for i, c in enumerate(chunks):
        gt, up = nxt
        if i + 1 < len(chunks):
            nxt = gate_up(chunks[i + 1])
        ff = (gt * jax.nn.sigmoid(gt) * up).astype(BF16)
        acc = acc + jnp.dot(ff, wd_ref[c:c + FF_CHUNK, :], preferred_element_type=F32)
    y_ref[...] = acc


def _merge_ffn(x2d, o_att, hm, ga, gb, wa, wm, wo, g2, wg, wu, wd, seq_len, tm, decode=None):
    m_rows, d_model = x2d.shape
    tiles_per_seq = seq_len // tm
    steps = m_rows // tm
    row = lambda w: pl.BlockSpec((tm, w), lambda i: (i, 0))
    o_spec = pl.BlockSpec((None, PAIRS, tm, LANES),
                          lambda i: (i // tiles_per_seq, 0, i % tiles_per_seq, 0))
    in_specs = [row(d_model), o_spec, row(1024), row(1024), row(1024),
                _const_spec(wa.shape), _const_spec(wm.shape), _const_spec(wo.shape),
                _const_spec((1, d_model)), _const_spec(wg.shape), _const_spec(wu.shape),
                _const_spec(wd.shape)]
    args = [x2d, o_att, hm, ga, gb, wa, wm, wo, g2, wg, wu, wd]
    out_specs = [row(d_model)]
    out_shape = [jax.ShapeDtypeStruct((m_rows, d_model), F32)]
    if decode is not None:
        qkv_t, caches_t = decode
        db = caches_t[0].shape[0]
        assert steps == 2 * db, (steps, db)
        in_specs.append(pl.BlockSpec((qkv_t.shape[0], DEC_ROWS, LANES), lambda i: (0, i % 2, 0)))
        args.append(qkv_t)
        for c in caches_t:
            spec = pl.BlockSpec((None, 2, DEC_HEADS, HEAD_DIM, c.shape[-1]),
                                lambda i: (i // 2, 0, i % 2, 0, 0))
            in_specs.append(spec)
            args.append(c)
            out_specs.append(spec)
            out_shape.append(jax.ShapeDtypeStruct(c.shape, F32))
        out_specs.append(pl.BlockSpec((None, None, DEC_ROWS, LANES), lambda i: (i // 2, i % 2, 0, 0)))
        out_shape.append(jax.ShapeDtypeStruct((db, 2, DEC_ROWS, LANES), F32))
    outs = pl.pallas_call(
        functools.partial(_merge_ffn_kernel, with_decode=decode is not None),
        grid=(steps,),
        in_specs=in_specs, out_specs=out_specs, out_shape=out_shape,
        compiler_params=pltpu.CompilerParams(dimension_semantics=("arbitrary",),
                                             vmem_limit_bytes=VMEM_LIMIT),
        name="merge_ffn",
    )(*args)
    if decode is None:
        return outs[0]
    return outs[0], outs[1:4], outs[4]


def _sample_prep_kernel(*refs):
    for i, ref in enumerate(refs[:-1]):
        refs[-1][i] = ref[...].T


def _sample_prep(arrs):
    vm = pl.BlockSpec(memory_space=pltpu.VMEM)
    return pl.pallas_call(
        _sample_prep_kernel, in_specs=[vm] * len(arrs), out_specs=vm,
        out_shape=jax.ShapeDtypeStruct((len(arrs), GROUP_W, LANES), F32),
        name="sample_prep",
    )(*arrs)


def _sample_mlstm_kernel(q_ref, k_ref, v_ref, om_ref, gt_ref, mt_ref, n_ref, c_ref, rexp_ref,
                         hm_ref, co_ref, nt_ref, mo_ref, *, db):
    nh, dqk = M_HEADS, M_DQK
    hi = lax.Precision.HIGHEST
    rexp = rexp_ref[...]
    q_t = q_ref[...].T * (dqk ** -0.5)
    k_t = k_ref[...].T
    n_t = n_ref[...].T
    ig = gt_ref[0:nh, :]
    lf = _log_sigmoid(gt_ref[nh:2 * nh, :])
    m_prev = mt_ref[...]
    m_new = jnp.maximum(lf + m_prev, ig)
    w8 = jnp.exp(ig - m_new)
    dec8 = jnp.exp(lf + m_prev - m_new)
    head_sum = lambda a: lax.dot_general(rexp, a, (((0,), (0,)), ((), ())), precision=hi,
                                         preferred_element_type=F32)
    expand = lambda a: jnp.dot(rexp, a, precision=hi, preferred_element_type=F32)
    qk8 = head_sum(q_t * k_t)
    qn8 = head_sum(q_t * n_t)
    wqk8 = w8 * qk8
    den8 = dec8 * qn8 + wqk8
    inv8 = 1.0 / jnp.maximum(jnp.abs(den8), jnp.exp(-m_new))
    dec_x = expand(dec8)
    w_x = expand(w8)
    nt_ref[...] = dec_x * n_t + w_x * k_t
    mo_ref[...] = m_new
    wk_x = w_x * k_t
    for b in range(db):
        cb = c_ref[b].reshape(nh * dqk, M_DV)
        v_b = v_ref[b]
        v_x = jnp.concatenate([jnp.broadcast_to(v_b[h:h + 1, :], (dqk, M_DV)) for h in range(nh)],
                              axis=0)
        co_ref[b] = (dec_x[:, b:b + 1] * cb + wk_x[:, b:b + 1] * v_x).reshape(nh, dqk, M_DV)
        qc = jnp.sum((q_t[:, b:b + 1] * cb).reshape(nh, dqk, M_DV), axis=1)
        num = dec8[:, b:b + 1] * qc + wqk8[:, b:b + 1] * v_b
        hm_ref[b] = jax.nn.sigmoid(om_ref[b]) * (num * inv8[:, b:b + 1])


def _sample_mlstm(q_t, k_t, v3, om3, gif_t, m_t, n_pad, c_state, rexp, db):
    vm = pl.BlockSpec(memory_space=pltpu.VMEM)
    return pl.pallas_call(
        functools.partial(_sample_mlstm_kernel, db=db),
        in_specs=[vm] * 9,
        out_specs=[vm] * 4,
        out_shape=[jax.ShapeDtypeStruct((db, M_HEADS, M_DV), F32),
                   jax.ShapeDtypeStruct(c_state.shape, F32),
                   jax.ShapeDtypeStruct((M_HEADS * M_DQK, LANES), F32),
                   jax.ShapeDtypeStruct((M_HEADS, LANES), F32)],
        compiler_params=pltpu.CompilerParams(vmem_limit_bytes=VMEM_LIMIT),
        name="sample_mlstm",
    )(q_t, k_t, v3, om3, gif_t, m_t, n_pad, c_state, rexp)


def _rope_tables(pos):
    half = ROT_DIM // 2
    pos = np.asarray(pos, np.float32)
    inv_freq = np.exp(np.float32(-math.log(ROPE_THETA))
                      * np.arange(0, ROT_DIM, 2, dtype=np.float32) / np.float32(ROT_DIM))
    ang = (pos[:, None] * inv_freq[None, :]).astype(np.float32)
    cos, sin = np.cos(ang).astype(np.float32), np.sin(ang).astype(np.float32)
    t = pos.shape[0]
    rest = HEAD_DIM - ROT_DIM
    a = np.concatenate([cos, cos, np.ones((t, rest), np.float32)], axis=1)
    bm = np.concatenate([-sin, np.zeros((t, HEAD_DIM - half), np.float32)], axis=1)
    bp = np.concatenate([np.zeros((t, half), np.float32), sin, np.zeros((t, rest), np.float32)],
                        axis=1)
    return tuple(jnp.asarray(np.concatenate([x, x], axis=1)) for x in (a, bm, bp))


W_PREP_BLK = 512


def _w_prep_kernel(wt_ref, o_ref):
    o_ref[...] = wt_ref[...].T.astype(o_ref.dtype)


def _w_prep(w_in):
    d_model = w_in.shape[0]
    wt = w_in.T
    n_head = _C_GA // W_PREP_BLK
    n_blk = _W_COLS // W_PREP_BLK

    sub = 8

    def src_row(j):
        return (j * (W_PREP_BLK // sub) + jnp.where(j < n_head, 0, _GIF_COLS // sub)) * sub

    return pl.pallas_call(
        _w_prep_kernel,
        grid=(n_blk,),
        in_specs=[pl.BlockSpec((pl.Element(W_PREP_BLK), pl.Element(d_model)),
                               lambda j: (src_row(j), 0))],
        out_specs=pl.BlockSpec((d_model, W_PREP_BLK), lambda j: (0, j)),
        out_shape=jax.ShapeDtypeStruct((d_model, _W_COLS), BF16),
        compiler_params=pltpu.CompilerParams(dimension_semantics=("arbitrary",),
                                             vmem_limit_bytes=VMEM_LIMIT),
        name="w_prep",
    )(wt)


def kernel(x_prompt, x_sample, cache_kv_w128, cache_kv_w512, cache_kv_w2048, state_mlstm_C, state_mlstm_n, state_mlstm_m, norm1_g, w_in, b_if, q_norm_g, k_norm_g, w_att_out, w_m_out, w_o, norm2_g, w_gate, w_up, w_down):
    nb, seq_len, d_model = x_prompt.shape
    db, dec_seq, _ = x_sample.shape
    assert dec_seq == 1 and d_model == 1024 and seq_len % ATT_STEP == 0 and db <= LANES
    caches = (cache_kv_w128, cache_kv_w512, cache_kv_w2048)
    past_len = 8192

    w_perm = _w_prep(w_in)
    w_gif = jnp.zeros((d_model, LANES), BF16).at[:, :_GIF_COLS].set(
        w_in[:, _C_GA:_C_GA + _GIF_COLS].astype(BF16))
    g1 = norm1_g.reshape(1, d_model)
    g2 = norm2_g.reshape(1, d_model)
    bif = jnp.concatenate([b_if, jnp.zeros((LANES - b_if.shape[0],), F32)]).reshape(1, LANES)
    qg = jnp.tile(q_norm_g, HEADS).reshape(1, GROUP_W)
    kg = jnp.tile(k_norm_g, HEADS).reshape(1, GROUP_W)
    hid = np.arange(GROUP_W // 2) // HEAD_DIM
    gmat = jnp.asarray(hid[:, None] == hid[None, :], dtype=BF16)
    wa, wm, wo = (w.astype(BF16) for w in (w_att_out, w_m_out, w_o))
    wg, wu, wd = (w.astype(BF16) for w in (w_gate, w_up, w_down))

    m_rows = nb * seq_len
    x2d = x_prompt.reshape(m_rows, d_model)
    tabs_p = _rope_tables(np.arange(seq_len))
    outs = _proj(x2d, seq_len, DILATIONS, 256, False, g1, w_perm, w_gif, bif, qg, kg, tabs_p, gmat)
    qs, ks, vs = outs[0:3], outs[3:6], outs[6:9]
    hm, ga, gb, st_p, m_p = outs[9:14]

    o_att = _attention(qs, ks, vs, seq_len)

    x_s = jnp.zeros((LANES, d_model), F32).at[:db].set(x_sample.reshape(db, d_model))
    tabs_s = _rope_tables(np.full((LANES,), past_len))
    outs_s = _proj(x_s, LANES, (1, 1, 1), LANES, True, g1, w_perm, w_gif, bif, qg, kg, tabs_s, gmat)
    qkv_t = _sample_prep(outs_s[0:9])
    caches_t = [c.transpose(0, 2, 3, 4, 1) for c in caches]
    y_prompt, kv_st, o_att_cols = _merge_ffn(x2d, o_att, hm, ga, gb, wa, wm, wo, g2, wg, wu, wd,
                                             seq_len, 256, decode=(qkv_t, caches_t))
    y_prompt = y_prompt.reshape(nb, seq_len, d_model)
    kv_s = [c.transpose(0, 4, 1, 2, 3) for c in kv_st]

    tails = _kv_tails(ks, vs, seq_len)
    kv_p = [t.reshape(nb, 2, HEADS, HEAD_DIM, t.shape[-1]).transpose(0, 4, 1, 2, 3) for t in tails]
    c_p = st_p[..., :M_DV].reshape(nb, M_HEADS, M_DQK, M_DV)
    n_p = st_p[..., M_DV].reshape(nb, M_HEADS, M_DQK)
    m_pr = m_p[:, :, 0]

    qm_t, km_t, vm_s, om_s, ga_s, gb_s, gif_s = outs_s[9:16]
    gif_ts = gif_s.T

    rexp = jnp.asarray(np.arange(M_HEADS * M_DQK)[:, None] // M_DQK == np.arange(M_HEADS)[None, :],
                       dtype=F32)
    m_t = jnp.zeros((M_HEADS, LANES), F32).at[:, :db].set(state_mlstm_m.T)
    n_pad = jnp.zeros((LANES, M_HEADS * M_DQK), F32).at[:db].set(state_mlstm_n.reshape(db, -1))
    hm_s3, c_s, n_t, m_so = _sample_mlstm(
        qm_t, km_t, vm_s[:db].reshape(db, M_HEADS, M_DV), om_s[:db].reshape(db, M_HEADS, M_DV),
        gif_ts[:2 * M_HEADS], m_t, n_pad, state_mlstm_C, rexp, db)
    n_s = n_t.T[:db].reshape(db, M_HEADS, M_DQK)
    m_s = m_so[:, :db].T

    o_att_s = o_att_cols[:, :, :, 0].reshape(db, PAIRS, LANES)
    o_att_sp = jnp.zeros((PAIRS, LANES, LANES), F32).at[:, :db].set(o_att_s.transpose(1, 0, 2))[None]
    hm_sp = jnp.zeros((LANES, 1024), F32).at[:db].set(hm_s3.reshape(db, 1024))
    y_s = _merge_ffn(x_s, o_att_sp, hm_sp, ga_s, gb_s, wa, wm, wo, g2, wg, wu, wd, LANES, LANES)
    y_sample = y_s[:db].reshape(db, 1, d_model)

    return (y_prompt, y_sample, kv_p[0], kv_p[1], kv_p[2], c_p, n_p, m_pr,
            kv_s[0], kv_s[1], kv_s[2], c_s, n_s, m_s)
```

```python
import functools
import math

import jax
import jax.numpy as jnp
import numpy as np
from jax import lax
from jax.experimental import pallas as pl
from jax.experimental.pallas import tpu as pltpu

F32 = jnp.float32
BF16 = jnp.bfloat16

HEAD_DIM = 64
HEADS = 8
GROUP_W = HEADS * HEAD_DIM
N_GROUPS = 3
WINDOWS = (128, 512, 2048)
DILATIONS = (1, 4, 16)
SPAN = 128
ROT_DIM = 16
ROPE_THETA = 500000.0
M_HEADS = 8
M_DQK = 64
M_DV = 128
NORM_EPS = 1e-6
PAIRS = GROUP_W // 128
NEG = -1e30

LANES = 128
VMEM_LIMIT = 56 * 1024 * 1024

_ATT_W = N_GROUPS * GROUP_W
_C_QM = 3 * _ATT_W
_C_KM = _C_QM + M_HEADS * M_DQK
_C_VM = _C_KM + M_HEADS * M_DQK
_C_OM = _C_VM + M_HEADS * M_DV
_C_GA = _C_OM + M_HEADS * M_DV
_C_GB = _C_GA + 1024
_W_COLS = _C_GB + 1024
_GIF_COLS = 2 * M_HEADS


def _const_spec(shape):
    nd = len(shape)
    return pl.BlockSpec(shape, lambda *_: (0,) * nd, pipeline_mode=pl.Buffered(1))


def _log_sigmoid(x):
    return jnp.minimum(x, 0.0) - jnp.log1p(jnp.exp(-jnp.abs(x)))


def _proj_kernel(x_ref, g1_ref, w_ref, wgif_ref, bif_ref, qg_ref, kg_ref, ra_ref, rm_ref, rp_ref,
                 gm_ref,
                 q0_ref, q1_ref, q2_ref, k0_ref, k1_ref, k2_ref, v0_ref, v1_ref, v2_ref,
                 *rest, tm, dils, plain, tiles_per_seq):
    if plain:
        qm_ref, km_ref, vm_ref, om_ref, ga_ref, gb_ref, gif_ref, hs_ref = rest
    else:
        hm_ref, ga_ref, gb_ref, st_ref, mo_ref, hs_ref, st_s, m_s = rest
    d_model = x_ref.shape[1]
    x = x_ref[...]
    xn = x * lax.rsqrt(jnp.mean(x * x, axis=-1, keepdims=True) + NORM_EPS) * g1_ref[...]
    h_nat = xn.astype(BF16)
    n_slab = d_model // LANES
    if any(d > 1 for d in dils):
        for c in range(n_slab):
            hs_ref[c] = xn[:, c * LANES:(c + 1) * LANES]

    def permuted_h(d):
        if d == 1:
            return h_nat
        n = tm // d
        rows = [jnp.concatenate([hs_ref[c, pl.ds(r, n, stride=d), :] for c in range(n_slab)], axis=1)
                for r in range(d)]
        return jnp.concatenate(rows, axis=0).astype(BF16)

    def permuted_tab(ref, d):
        if d == 1:
            t = ref[...]
        else:
            n = tm // d
            t = jnp.concatenate([ref[pl.ds(r, n, stride=d), :] for r in range(d)], axis=0)
        return jnp.concatenate([t] * PAIRS, axis=1)

    gmat = gm_ref[...]

    def head_sumsq(z):
        zz = (z * z).astype(BF16)
        half = GROUP_W // 2
        return jnp.concatenate(
            [jnp.dot(zz[:, :half], gmat, preferred_element_type=F32),
             jnp.dot(zz[:, half:], gmat, preferred_element_type=F32)], axis=1)

    def norm_rope(z, ss, gain, ra, rm, rp):
        y = z * lax.rsqrt(ss * (1.0 / HEAD_DIM) + NORM_EPS) * gain
        return (y * ra + pltpu.roll(y, GROUP_W - ROT_DIM // 2, 1) * rm
                + pltpu.roll(y, ROT_DIM // 2, 1) * rp)

    def store_group(ref, y, d):
        if plain:
            ref[...] = y.astype(ref.dtype)
            return
        n = tm // d
        for p in range(PAIRS):
            ref[:, p] = y[:, p * LANES:(p + 1) * LANES].reshape(d, n, LANES).astype(ref.dtype)

    def seg(c0, width):
        return jnp.dot(h_nat, w_ref[:, c0:c0 + width], preferred_element_type=F32)

    if not plain:
        @pl.when(pl.program_id(0) % tiles_per_seq == 0)
        def _():
            st_s[...] = jnp.zeros_like(st_s)
            m_s[...] = jnp.zeros_like(m_s)

        gif = jnp.dot(h_nat, wgif_ref[...], preferred_element_type=F32) + bif_ref[...]
        m_phase1, m_phase2, m_phase3 = _mlstm_chunk(
            seg(_C_QM, 512).astype(BF16), seg(_C_KM, 512).astype(BF16),
            seg(_C_VM, 1024).astype(BF16), seg(_C_OM, 1024).astype(BF16),
            gif, gif.T[:2 * M_HEADS, :], st_s, m_s, hm_ref, tm)
        m_lo, m_hi = (0, 1), (2, 3)
        m_phase1(m_lo)
        m_phase2(m_lo)

    q_refs = (q0_ref, q1_ref, q2_ref)
    k_refs = (k0_ref, k1_ref, k2_ref)
    v_refs = (v0_ref, v1_ref, v2_ref)
    for g in range(N_GROUPS):
        d = dils[g]
        hg = permuted_h(d)
        ra, rm, rp = (permuted_tab(r, d) for r in (ra_ref, rm_ref, rp_ref))
        cq, ck, cv = (t * _ATT_W + g * GROUP_W for t in range(3))
        zq = jnp.dot(hg, w_ref[:, cq:cq + GROUP_W], preferred_element_type=F32)
        zk = jnp.dot(hg, w_ref[:, ck:ck + GROUP_W], preferred_element_type=F32)
        zv = jnp.dot(hg, w_ref[:, cv:cv + GROUP_W], preferred_element_type=F32)
        ssq, ssk = head_sumsq(zq), head_sumsq(zk)
        store_group(v_refs[g], zv, d)
        store_group(q_refs[g], norm_rope(zq, ssq, qg_ref[...], ra, rm, rp), d)
        store_group(k_refs[g], norm_rope(zk, ssk, kg_ref[...], ra, rm, rp), d)

        if g == 0 and not plain:
            m_phase3(m_lo)
            m_phase1(m_hi)
            m_phase2(m_hi)
        if g == 1 and not plain:
            m_phase3(m_hi)

    if plain:
        segments = ((qm_ref, _C_QM, 512), (km_ref, _C_KM, 512), (vm_ref, _C_VM, 1024),
                    (om_ref, _C_OM, 1024), (ga_ref, _C_GA, 1024), (gb_ref, _C_GB, 1024))
        zg = jnp.dot(h_nat, wgif_ref[...], preferred_element_type=F32)
        gif_ref[...] = zg + bif_ref[...]
    else:
        st_ref[...] = st_s[...]
        mo_ref[...] = m_s[...]
        segments = ((ga_ref, _C_GA, 1024), (gb_ref, _C_GB, 1024))
    for ref, c0, width in segments:
        for cc in range(0, width, GROUP_W):
            ref[:, cc:cc + GROUP_W] = seg(c0 + cc, GROUP_W).astype(ref.dtype)


def _proj(x2d, seq_len, dils, tm, plain, g1, w_perm, w_gif, bif, qg, kg, rope_tabs, gmat):
    m_rows, d_model = x2d.shape
    nb = m_rows // seq_len
    tiles_per_seq = seq_len // tm
    grid = (m_rows // tm,)
    row_spec = lambda w: pl.BlockSpec((tm, w), lambda i: (i, 0))
    tab_spec = pl.BlockSpec((tm, LANES), lambda i: (i % tiles_per_seq, 0))
    sds = jax.ShapeDtypeStruct

    widths = (512, 512, 1024, 1024, 1024, 1024)
    scratch = [pltpu.VMEM((d_model // LANES, tm, LANES), F32)]
    if plain:
        out_shape = ([sds((m_rows, GROUP_W), F32)] * 9 + [sds((m_rows, w), F32) for w in widths]
                     + [sds((m_rows, LANES), F32)])
        out_specs = ([row_spec(GROUP_W)] * 9 + [row_spec(w) for w in widths] + [row_spec(LANES)])
    else:
        def grp_spec(d):
            return pl.BlockSpec((None, d, PAIRS, tm // d, LANES),
                                lambda i: (i // tiles_per_seq, 0, 0, i % tiles_per_seq, 0))
        grp_shape = lambda d: sds((nb, d, PAIRS, seq_len // d, LANES), BF16)
        state_spec = lambda *dims: pl.BlockSpec((None,) + dims,
                                                lambda i: (i // tiles_per_seq,) + (0,) * len(dims))
        st_dims, m_dims = (M_HEADS // 2, LANES, 2 * LANES), (M_HEADS, LANES)
        out_shape = ([grp_shape(d) for d in dils] * 3 + [sds((m_rows, 1024), BF16)] * 3
                     + [sds((nb,) + st_dims, F32), sds((nb,) + m_dims, F32)])
        out_specs = ([grp_spec(d) for d in dils] * 3 + [row_spec(1024)] * 3
                     + [state_spec(*st_dims), state_spec(*m_dims)])
        scratch += [pltpu.VMEM(st_dims, F32), pltpu.VMEM(m_dims, F32)]
    in_specs = [row_spec(d_model), _const_spec((1, d_model)), _const_spec(w_perm.shape),
                _const_spec(w_gif.shape), _const_spec((1, LANES)), _const_spec((1, GROUP_W)),
                _const_spec((1, GROUP_W)), tab_spec, tab_spec, tab_spec, _const_spec(gmat.shape)]
    return pl.pallas_call(
        functools.partial(_proj_kernel, tm=tm, dils=dils, plain=plain,
                          tiles_per_seq=tiles_per_seq),
        grid=grid, in_specs=in_specs, out_specs=out_specs, out_shape=out_shape,
        scratch_shapes=scratch,
        compiler_params=pltpu.CompilerParams(dimension_semantics=("arbitrary",),
                                             vmem_limit_bytes=VMEM_LIMIT),
        name="proj",
    )(x2d, g1, w_perm, w_gif, bif, qg, kg, *rope_tabs, gmat)


ATT_STEP = SPAN * max(DILATIONS)
ATT_UNROLL = 8


def _attn_kernel(q0, k0, v0, kh0, vh0, q1, k1, v1, kh1, vh1, q2, k2, v2, kh2, vh2,
                 o_ref, acc_s, m_s, l_s):
    j = pl.program_id(1)

    def window(cur, halo, r, bi):
        here = cur[r, pl.ds(pl.multiple_of(bi * SPAN, SPAN), SPAN), :]
        if cur.shape[1] == SPAN:
            prev = halo[r]
        else:
            before = cur[r, pl.ds(pl.multiple_of(jnp.maximum(bi - 1, 0) * SPAN, SPAN), SPAN), :]
            prev = jnp.where(bi == 0, halo[r], before)
        return jnp.concatenate([prev, here], axis=0)

    qi = lax.broadcasted_iota(jnp.int32, (SPAN, 2 * SPAN), 0)
    ci = lax.broadcasted_iota(jnp.int32, (SPAN, 2 * SPAN), 1)
    band = (ci >= qi) & (ci <= qi + SPAN)
    bias_band = jnp.where(band, 0.0, NEG).astype(F32)
    bias_first = jnp.where(band & (ci >= SPAN), 0.0, NEG).astype(F32)
    lane_q = lax.broadcasted_iota(jnp.int32, (SPAN, LANES), 1)
    lane_kv = lax.broadcasted_iota(jnp.int32, (2 * SPAN, LANES), 1)
    halves_q = (lane_q < HEAD_DIM, lane_q >= HEAD_DIM)
    halves_kv = (lane_kv < HEAD_DIM, lane_kv >= HEAD_DIM)

    def blocks(g, q_ref, kv_refs, units):
        k_cur, k_halo, v_cur, v_halo = kv_refs
        loaded = []
        for r, bi, _ in units:
            row0 = pl.multiple_of(bi * SPAN, SPAN)
            q2_ = q_ref[r, pl.ds(row0, SPAN), :]
            kk = window(k_cur, k_halo, r, bi)
            first = jnp.logical_and(j == 0, bi == 0)
            bias = jnp.where(first, bias_first, bias_band)
            ss = []
            for hh in range(2):
                qa = jnp.where(halves_q[hh], q2_, jnp.zeros_like(q2_))
                ss.append(lax.dot_general(qa, kk, (((1,), (1,)), ((), ())),
                                          preferred_element_type=F32) + bias)
            loaded.append(ss)
        probs = []
        for ss in loaded:
            mxs = [jnp.max(s, axis=-1, keepdims=True) for s in ss]
            ps = [jnp.exp2(s - mx) for s, mx in zip(ss, mxs)]
            ls = [jnp.sum(p, axis=-1, keepdims=True) for p in ps]
            probs.append(([p.astype(BF16) for p in ps], mxs, ls))
        for (r, bi, sl), (ps, mxs, ls) in zip(units, probs):
            vv = window(v_cur, v_halo, r, bi)
            acc = None
            for hh in range(2):
                vh = jnp.where(halves_kv[hh], vv, jnp.zeros_like(vv))
                a = jnp.dot(ps[hh], vh, preferred_element_type=F32)
                acc = a if acc is None else acc + a
            acc_s[g, sl, :] = acc
            m_s[g, sl, :] = jnp.where(halves_q[0], mxs[0], mxs[1])
            l_s[g, sl, :] = jnp.where(halves_q[0], ls[0], ls[1])

    d0, d1, d2 = DILATIONS
    nb0, nb1, nb2 = (ATT_STEP // d // SPAN for d in DILATIONS)
    un = ATT_UNROLL

    def body0(it, c):
        units = []
        for u in range(un):
            bi = it * un + u
            units.append((0, bi, pl.ds(pl.multiple_of(bi * SPAN, SPAN), SPAN)))
        blocks(0, q0, (k0, kh0, v0, vh0), units)
        return c
    lax.fori_loop(0, nb0 // un, body0, 0)

    def body1(it, c):
        units = []
        for u in range(un):
            idx = it * un + u
            r, bi = idx // nb1, idx % nb1
            units.append((r, bi, pl.ds(bi * SPAN * d1 + r, SPAN, stride=d1)))
        blocks(1, q1, (k1, kh1, v1, vh1), units)
        return c
    lax.fori_loop(0, d1 * nb1 // un, body1, 0)

    def body2(it, c):
        units = []
        for u in range(un):
            r = it * un + u
            units.append((r, 0, pl.ds(r, SPAN, stride=d2)))
        blocks(2, q2, (k2, kh2, v2, vh2), units)
        return c
    lax.fori_loop(0, d2 * nb2 // un, body2, 0)

    def combine(ci_, c):
        sl = pl.ds(pl.multiple_of(ci_ * SPAN, SPAN), SPAN)
        ms = [m_s[g, sl, :] for g in range(N_GROUPS)]
        mm = jnp.maximum(jnp.maximum(ms[0], ms[1]), ms[2])
        es = [jnp.exp2(m - mm) for m in ms]
        num = es[0] * acc_s[0, sl, :] + es[1] * acc_s[1, sl, :] + es[2] * acc_s[2, sl, :]
        den = es[0] * l_s[0, sl, :] + es[1] * l_s[1, sl, :] + es[2] * l_s[2, sl, :]
        o_ref[sl, :] = (num / den).astype(o_ref.dtype)
        return c
    lax.fori_loop(0, ATT_STEP // SPAN, combine, 0)


def _attention(qs, ks, vs, seq_len):
    nb = qs[0].shape[0]
    steps = seq_len // ATT_STEP
    in_specs, args, scratch = [], [], []
    for g, d in enumerate(DILATIONS):
        rows = ATT_STEP // d
        cur = pl.BlockSpec((None, d, None, rows, LANES), lambda b, j, p: (b, 0, p, j, 0))
        ratio = rows // SPAN
        halo = pl.BlockSpec((None, d, None, SPAN, LANES),
                            lambda b, j, p, ratio=ratio: (b, 0, p, jnp.maximum(j * ratio - 1, 0), 0))
        in_specs += [cur, cur, cur, halo, halo]
        args += [qs[g], ks[g], vs[g], ks[g], vs[g]]
    scratch += [pltpu.VMEM((N_GROUPS, ATT_STEP, LANES), F32)] * 3
    return pl.pallas_call(
        _attn_kernel,
        grid=(nb, steps, PAIRS),
        in_specs=in_specs,
        out_specs=pl.BlockSpec((None, None, ATT_STEP, LANES), lambda b, j, p: (b, p, j, 0)),
        out_shape=jax.ShapeDtypeStruct((nb, PAIRS, seq_len, LANES), BF16),
        scratch_shapes=scratch,
        compiler_params=pltpu.CompilerParams(
            dimension_semantics=("arbitrary", "arbitrary", "arbitrary"),
            vmem_limit_bytes=VMEM_LIMIT),
        name="attn",
    )(*args)


def _kv_tail_kernel(k0, v0, k1, v1, k2, v2, o0, o1, o2, nat_ref):
    for (k_ref, v_ref, o_ref, d) in ((k0, v0, o0, DILATIONS[0]), (k1, v1, o1, DILATIONS[1]),
                                     (k2, v2, o2, DILATIONS[2])):
        for kvi, ref in enumerate((k_ref, v_ref)):
            for p in range(PAIRS):
                rows = slice(p * LANES, (p + 1) * LANES)
                if d == 1:
                    o_ref[kvi, rows, :] = ref[0, p].astype(F32).T
                    continue
                for r in range(d):
                    nat_ref[pl.ds(r, SPAN, stride=d), :] = ref[r, p].astype(F32)
                for c in range(d):
                    o_ref[kvi, rows, c * SPAN:(c + 1) * SPAN] = nat_ref[c * SPAN:(c + 1) * SPAN, :].T


def _kv_tails(ks, vs, seq_len):
    nb = ks[0].shape[0]
    in_specs, args, out_specs, out_shape = [], [], [], []
    for g, d in enumerate(DILATIONS):
        last = seq_len // d // SPAN - 1
        spec = pl.BlockSpec((None, d, PAIRS, SPAN, LANES), lambda b, last=last: (b, 0, 0, last, 0))
        in_specs += [spec, spec]
        args += [ks[g], vs[g]]
        out_specs.append(pl.BlockSpec((None, 2, GROUP_W, SPAN * d), lambda b: (b, 0, 0, 0)))
        out_shape.append(jax.ShapeDtypeStruct((nb, 2, GROUP_W, SPAN * d), F32))
    return pl.pallas_call(
        _kv_tail_kernel,
        grid=(nb,), in_specs=in_specs, out_specs=out_specs, out_shape=out_shape,
        scratch_shapes=[pltpu.VMEM((SPAN * max(DILATIONS), LANES), F32)],
        compiler_params=pltpu.CompilerParams(dimension_semantics=("arbitrary",),
                                             vmem_limit_bytes=VMEM_LIMIT),
        name="kv_tail",
    )(*args)


def _mlstm_chunk(q2d, k2d, v2d, om2d, gif, gt, st_s, m_s, hm_ref, lc):
    lf_c = _log_sigmoid(gif)
    lf_r = _log_sigmoid(gt[M_HEADS:2 * M_HEADS, :])
    row = lax.broadcasted_iota(jnp.int32, (lc, lc), 0)
    col = lax.broadcasted_iota(jnp.int32, (lc, lc), 1)
    causal = col <= row
    tril = causal.astype(F32)
    triu = (row <= col).astype(F32)
    b_c = jnp.dot(tril, lf_c, precision=lax.Precision.HIGHEST, preferred_element_type=F32)
    b_r = jnp.dot(lf_r, triu, precision=lax.Precision.HIGHEST, preferred_element_type=F32)
    lane = lax.broadcasted_iota(jnp.int32, (lc, LANES), 1)
    halves = (lane < M_DQK, lane >= M_DQK)
    sub = lax.broadcasted_iota(jnp.int32, (LANES, 1), 0)
    ones_blk = jnp.ones((lc, LANES), BF16)
    qscale = M_DQK ** -0.5

    states, p1, p2 = {}, {}, {}
    all_pairs = tuple(range(M_HEADS // 2))
    heads_of = lambda pairs: [2 * p + hh for p in pairs for hh in range(2)]

    def phase1(pairs=all_pairs):
        for p in pairs:
            states[p] = st_s[p]
        for h in heads_of(pairs):
            p, hh = divmod(h, 2)
            q2_ = q2d[:, p * LANES:(p + 1) * LANES]
            k2_ = k2d[:, p * LANES:(p + 1) * LANES]
            qa = jnp.where(halves[hh], q2_, jnp.zeros_like(q2_)) * jnp.asarray(qscale, q2_.dtype)
            s = lax.dot_general(qa, k2_, (((1,), (1,)), ((), ())), preferred_element_type=F32)
            qc = jnp.dot(qa, states[p].astype(BF16), preferred_element_type=F32)
            p1[h] = (s, qc)

    def phase2(pairs=all_pairs):
        for h in heads_of(pairs):
            p, hh = divmod(h, 2)
            s, qc = p1[h]
            bcol = b_c[:, M_HEADS + h:M_HEADS + h + 1]
            igcol = gif[:, h:h + 1]
            a_row = gt[h:h + 1, :] - b_r[h:h + 1, :]
            amat = jnp.where(causal, a_row, NEG)
            m_prev = m_s[h:h + 1, 0:1]
            g_t = jnp.maximum(m_prev, jnp.max(amat, axis=-1, keepdims=True))
            wqk = (jnp.exp(amat - g_t) * s).astype(BF16)
            dec = jnp.exp(m_prev - g_t)
            m_t = bcol + g_t
            m_new = m_t[lc - 1:lc, :]
            b_last = bcol[lc - 1:lc, :]
            dstate = jnp.exp(b_last + m_prev - m_new)
            ws = jnp.exp(b_last - bcol + igcol - m_new)
            k2_ = k2d[:, p * LANES:(p + 1) * LANES]
            ka = jnp.where(halves[hh], k2_, jnp.zeros_like(k2_))
            kws = (ka.astype(F32) * ws).astype(BF16)
            m_s[h:h + 1, :] = jnp.broadcast_to(m_new, (1, LANES))
            p2[h] = (wqk, dec * qc, jnp.exp(-m_t), dstate, kws)

    def phase3(pairs=all_pairs):
        upds = {}
        for h in heads_of(pairs):
            wqk, dqc, floor, _, kws = p2[h]
            v1 = jnp.concatenate([v2d[:, h * M_DV:(h + 1) * M_DV], ones_blk], axis=1)
            num_den = dqc + jnp.dot(wqk, v1, preferred_element_type=F32)
            num = num_den[:, :M_DV]
            den = num_den[:, M_DV:]
            hval = num / jnp.maximum(jnp.abs(den), floor)
            gate = jax.nn.sigmoid(om2d[:, h * M_DV:(h + 1) * M_DV].astype(F32))
            hm_ref[:, h * M_DV:(h + 1) * M_DV] = (gate * hval).astype(hm_ref.dtype)
            upds[h] = lax.dot_general(kws, v1, (((0,), (0,)), ((), ())),
                                      preferred_element_type=F32)
        for p in pairs:
            drow = jnp.where(sub < M_DQK, p2[2 * p][3], p2[2 * p + 1][3])
            st_s[p] = drow * states[p] + upds[2 * p] + upds[2 * p + 1]

    return phase1, phase2, phase3


FF_CHUNK = 256


DEC_HEADS = HEADS // 2
DEC_ROWS = DEC_HEADS * HEAD_DIM


def _decode_attn_step(b, qkv_ref, c_refs, o_refs, ot_ref):
    sel = lax.broadcasted_iota(jnp.int32, (DEC_ROWS, LANES), 1) == b
    scale = HEAD_DIM ** -0.5

    def column(i):
        return jnp.sum(jnp.where(sel, qkv_ref[i], 0.0), axis=1, keepdims=True)

    def head(col, h):
        return col[h * HEAD_DIM:(h + 1) * HEAD_DIM, :]

    qc = [column(g) * scale for g in range(N_GROUPS)]
    kn = [column(N_GROUPS + g) for g in range(N_GROUPS)]
    vn = [column(2 * N_GROUPS + g) for g in range(N_GROUPS)]
    cols = []
    for h in range(DEC_HEADS):
        scores, m_h = [], None
        for g, d in enumerate(DILATIONS):
            wb = c_refs[g].shape[-1]
            qh = head(qc[g], h)
            s = jnp.sum(c_refs[g][0, h] * qh, axis=0, keepdims=True)
            pos = lax.broadcasted_iota(jnp.int32, (1, wb), 1)
            s = jnp.where((pos & (d - 1)) == 0, s, NEG)
            s_new = jnp.sum(head(kn[g], h) * qh, axis=0, keepdims=True)
            m_g = jnp.maximum(jnp.max(s, axis=1, keepdims=True), s_new)
            m_h = m_g if m_h is None else jnp.maximum(m_h, m_g)
            scores.append((s, s_new))
        l_h = jnp.zeros((1, 1), F32)
        acc = jnp.zeros((HEAD_DIM, 1), F32)
        for g in range(N_GROUPS):
            s, s_new = scores[g]
            p = jnp.exp(s - m_h)
            p_new = jnp.exp(s_new - m_h)
            l_h = l_h + jnp.sum(p, axis=1, keepdims=True) + p_new
            acc = acc + jnp.sum(c_refs[g][1, h] * p, axis=1, keepdims=True) + p_new * head(vn[g], h)
        cols.append(acc / l_h)
    ot_ref[...] = jnp.broadcast_to(jnp.concatenate(cols, axis=0), ot_ref.shape)
    for g in range(N_GROUPS):
        wb = c_refs[g].shape[-1]
        last = lax.broadcasted_iota(jnp.int32, (HEAD_DIM, wb), 1) == wb - 1
        for kvi, new in ((0, kn[g]), (1, vn[g])):
            for h in range(DEC_HEADS):
                rolled = pltpu.roll(c_refs[g][kvi, h], wb - 1, 1)
                o_refs[g][kvi, h] = jnp.where(last, head(new, h), rolled)


def _merge_ffn_kernel(x_ref, o_ref, hm_ref, ga_ref, gb_ref, wa_ref, wm_ref, wo_ref, g2_ref,
                      wg_ref, wu_ref, wd_ref, *rest, with_decode):
    if with_decode:
        qkv_ref, c0, c1, c2, y_ref, o0, o1, o2, ot_ref = rest
        _decode_attn_step(pl.program_id(0) // 2, qkv_ref, (c0, c1, c2), (o0, o1, o2), ot_ref)
    else:
        (y_ref,) = rest
    o_att = jnp.concatenate([o_ref[p] for p in range(PAIRS)], axis=1).astype(BF16)
    ya = jnp.dot(o_att, wa_ref[...], preferred_element_type=F32)
    yb = jnp.dot(hm_ref[...].astype(BF16), wm_ref[...], preferred_element_type=F32)
    mixed = (jax.nn.sigmoid(ga_ref[...].astype(F32)) * ya
             + jax.nn.sigmoid(gb_ref[...].astype(F32)) * yb)
    x2 = x_ref[...] + jnp.dot(mixed.astype(BF16), wo_ref[...], preferred_element_type=F32)
    h2 = (x2 * lax.rsqrt(jnp.mean(x2 * x2, axis=-1, keepdims=True) + NORM_EPS)
          * g2_ref[...]).astype(BF16)
    acc = x2
    d_ff = wg_ref.shape[1]

    def gate_up(c):
        gt = jnp.dot(h2, wg_ref[:, c:c + FF_CHUNK], preferred_element_type=F32)
        up = jnp.dot(h2, wu_ref[:, c:c + FF_CHUNK], preferred_element_type=F32)
        return gt, up

    chunks = list(range(0, d_ff, FF_CHUNK))
    nxt = gate_up(chunks[0])
    for i, c in enumerate(chunks):
        gt, up = nxt
        if i + 1 < len(chunks):
            nxt = gate_up(chunks[i + 1])
        ff = (gt * jax.nn.sigmoid(gt) * up).astype(BF16)
        acc = acc + jnp.dot(ff, wd_ref[c:c + FF_CHUNK, :], preferred_element_type=F32)
    y_ref[...] = acc


def _merge_ffn(x2d, o_att, hm, ga, gb, wa, wm, wo, g2, wg, wu, wd, seq_len, tm, decode=None):
    m_rows, d_model = x2d.shape
    tiles_per_seq = seq_len // tm
    steps = m_rows // tm
    row = lambda w: pl.BlockSpec((tm, w), lambda i: (i, 0))
    o_spec = pl.BlockSpec((None, PAIRS, tm, LANES),
                          lambda i: (i // tiles_per_seq, 0, i % tiles_per_seq, 0))
    in_specs = [row(d_model), o_spec, row(1024), row(1024), row(1024),
                _const_spec(wa.shape), _const_spec(wm.shape), _const_spec(wo.shape),
                _const_spec((1, d_model)), _const_spec(wg.shape), _const_spec(wu.shape),
                _const_spec(wd.shape)]
    args = [x2d, o_att, hm, ga, gb, wa, wm, wo, g2, wg, wu, wd]
    out_specs = [row(d_model)]
    out_shape = [jax.ShapeDtypeStruct((m_rows, d_model), F32)]
    if decode is not None:
        qkv_t, caches_t = decode
        db = caches_t[0].shape[0]
        assert steps == 2 * db, (steps, db)
        in_specs.append(pl.BlockSpec((qkv_t.shape[0], DEC_ROWS, LANES), lambda i: (0, i % 2, 0)))
        args.append(qkv_t)
        for c in caches_t:
            spec = pl.BlockSpec((None, 2, DEC_HEADS, HEAD_DIM, c.shape[-1]),
                                lambda i: (i // 2, 0, i % 2, 0, 0))
            in_specs.append(spec)
            args.append(c)
            out_specs.append(spec)
            out_shape.append(jax.ShapeDtypeStruct(c.shape, F32))
        out_specs.append(pl.BlockSpec((None, None, DEC_ROWS, LANES), lambda i: (i // 2, i % 2, 0, 0)))
        out_shape.append(jax.ShapeDtypeStruct((db, 2, DEC_ROWS, LANES), F32))
    outs = pl.pallas_call(
        functools.partial(_merge_ffn_kernel, with_decode=decode is not None),
        grid=(steps,),
        in_specs=in_specs, out_specs=out_specs, out_shape=out_shape,
        compiler_params=pltpu.CompilerParams(dimension_semantics=("arbitrary",),
                                             vmem_limit_bytes=VMEM_LIMIT),
        name="merge_ffn",
    )(*args)
    if decode is None:
        return outs[0]
    return outs[0], outs[1:4], outs[4]


def _sample_prep_kernel(*refs):
    for i, ref in enumerate(refs[:-1]):
        refs[-1][i] = ref[...].T


def _sample_prep(arrs):
    vm = pl.BlockSpec(memory_space=pltpu.VMEM)
    return pl.pallas_call(
        _sample_prep_kernel, in_specs=[vm] * len(arrs), out_specs=vm,
        out_shape=jax.ShapeDtypeStruct((len(arrs), GROUP_W, LANES), F32),
        name="sample_prep",
    )(*arrs)


def _sample_mlstm_kernel(q_ref, k_ref, v_ref, om_ref, gt_ref, mt_ref, n_ref, c_ref, rexp_ref,
                         hm_ref, co_ref, nt_ref, mo_ref, *, db):
    nh, dqk = M_HEADS, M_DQK
    hi = lax.Precision.HIGHEST
    rexp = rexp_ref[...]
    q_t = q_ref[...].T * (dqk ** -0.5)
    k_t = k_ref[...].T
    n_t = n_ref[...].T
    ig = gt_ref[0:nh, :]
    lf = _log_sigmoid(gt_ref[nh:2 * nh, :])
    m_prev = mt_ref[...]
    m_new = jnp.maximum(lf + m_prev, ig)
    w8 = jnp.exp(ig - m_new)
    dec8 = jnp.exp(lf + m_prev - m_new)
    head_sum = lambda a: lax.dot_general(rexp, a, (((0,), (0,)), ((), ())), precision=hi,
                                         preferred_element_type=F32)
    expand = lambda a: jnp.dot(rexp, a, precision=hi, preferred_element_type=F32)
    qk8 = head_sum(q_t * k_t)
    qn8 = head_sum(q_t * n_t)
    wqk8 = w8 * qk8
    den8 = dec8 * qn8 + wqk8
    inv8 = 1.0 / jnp.maximum(jnp.abs(den8), jnp.exp(-m_new))
    dec_x = expand(dec8)
    w_x = expand(w8)
    nt_ref[...] = dec_x * n_t + w_x * k_t
    mo_ref[...] = m_new
    wk_x = w_x * k_t
    for b in range(db):
        cb = c_ref[b].reshape(nh * dqk, M_DV)
        v_b = v_ref[b]
        v_x = jnp.concatenate([jnp.broadcast_to(v_b[h:h + 1, :], (dqk, M_DV)) for h in range(nh)],
                              axis=0)
        co_ref[b] = (dec_x[:, b:b + 1] * cb + wk_x[:, b:b + 1] * v_x).reshape(nh, dqk, M_DV)
        qc = jnp.sum((q_t[:, b:b + 1] * cb).reshape(nh, dqk, M_DV), axis=1)
        num = dec8[:, b:b + 1] * qc + wqk8[:, b:b + 1] * v_b
        hm_ref[b] = jax.nn.sigmoid(om_ref[b]) * (num * inv8[:, b:b + 1])


def _sample_mlstm(q_t, k_t, v3, om3, gif_t, m_t, n_pad, c_state, rexp, db):
    vm = pl.BlockSpec(memory_space=pltpu.VMEM)
    return pl.pallas_call(
        functools.partial(_sample_mlstm_kernel, db=db),
        in_specs=[vm] * 9,
        out_specs=[vm] * 4,
        out_shape=[jax.ShapeDtypeStruct((db, M_HEADS, M_DV), F32),
                   jax.ShapeDtypeStruct(c_state.shape, F32),
                   jax.ShapeDtypeStruct((M_HEADS * M_DQK, LANES), F32),
                   jax.ShapeDtypeStruct((M_HEADS, LANES), F32)],
        compiler_params=pltpu.CompilerParams(vmem_limit_bytes=VMEM_LIMIT),
        name="sample_mlstm",
    )(q_t, k_t, v3, om3, gif_t, m_t, n_pad, c_state, rexp)


def _rope_tables(pos):
    half = ROT_DIM // 2
    pos = np.asarray(pos, np.float32)
    inv_freq = np.exp(np.float32(-math.log(ROPE_THETA))
                      * np.arange(0, ROT_DIM, 2, dtype=np.float32) / np.float32(ROT_DIM))
    ang = (pos[:, None] * inv_freq[None, :]).astype(np.float32)
    cos, sin = np.cos(ang).astype(np.float32), np.sin(ang).astype(np.float32)
    t = pos.shape[0]
    rest = HEAD_DIM - ROT_DIM
    a = np.concatenate([cos, cos, np.ones((t, rest), np.float32)], axis=1)
    bm = np.concatenate([-sin, np.zeros((t, HEAD_DIM - half), np.float32)], axis=1)
    bp = np.concatenate([np.zeros((t, half), np.float32), sin, np.zeros((t, rest), np.float32)],
                        axis=1)
    return tuple(jnp.asarray(np.concatenate([x, x], axis=1)) for x in (a, bm, bp))


W_PREP_BLK = 512


def _w_prep_kernel(wt_ref, o_ref):
    o_ref[...] = wt_ref[...].T.astype(o_ref.dtype)


def _w_prep(w_in):
    d_model = w_in.shape[0]
    wt = w_in.T
    n_head = _C_GA // W_PREP_BLK
    n_blk = _W_COLS // W_PREP_BLK

    sub = 8

    def src_row(j):
        return (j * (W_PREP_BLK // sub) + jnp.where(j < n_head, 0, _GIF_COLS // sub)) * sub

    return pl.pallas_call(
        _w_prep_kernel,
        grid=(n_blk,),
        in_specs=[pl.BlockSpec((pl.Element(W_PREP_BLK), pl.Element(d_model)),
                               lambda j: (src_row(j), 0))],
        out_specs=pl.BlockSpec((d_model, W_PREP_BLK), lambda j: (0, j)),
        out_shape=jax.ShapeDtypeStruct((d_model, _W_COLS), BF16),
        compiler_params=pltpu.CompilerParams(dimension_semantics=("arbitrary",),
                                             vmem_limit_bytes=VMEM_LIMIT),
        name="w_prep",
    )(wt)


def kernel(x_prompt, x_sample, cache_kv_w128, cache_kv_w512, cache_kv_w2048, state_mlstm_C, state_mlstm_n, state_mlstm_m, norm1_g, w_in, b_if, q_norm_g, k_norm_g, w_att_out, w_m_out, w_o, norm2_g, w_gate, w_up, w_down):
    nb, seq_len, d_model = x_prompt.shape
    db, dec_seq, _ = x_sample.shape
    assert dec_seq == 1 and d_model == 1024 and seq_len % ATT_STEP == 0 and db <= LANES
    caches = (cache_kv_w128, cache_kv_w512, cache_kv_w2048)
    past_len = 8192

    w_perm = _w_prep(w_in)
    w_gif = jnp.zeros((d_model, LANES), BF16).at[:, :_GIF_COLS].set(
        w_in[:, _C_GA:_C_GA + _GIF_COLS].astype(BF16))
    g1 = norm1_g.reshape(1, d_model)
    g2 = norm2_g.reshape(1, d_model)
    bif = jnp.concatenate([b_if, jnp.zeros((LANES - b_if.shape[0],), F32)]).reshape(1, LANES)
    qg = jnp.tile(q_norm_g, HEADS).reshape(1, GROUP_W)
    kg = jnp.tile(k_norm_g, HEADS).reshape(1, GROUP_W)
    hid = np.arange(GROUP_W // 2) // HEAD_DIM
    gmat = jnp.asarray(hid[:, None] == hid[None, :], dtype=BF16)
    wa, wm, wo = (w.astype(BF16) for w in (w_att_out, w_m_out, w_o))
    wg, wu, wd = (w.astype(BF16) for w in (w_gate, w_up, w_down))

    m_rows = nb * seq_len
    x2d = x_prompt.reshape(m_rows, d_model)
    tabs_p = _rope_tables(np.arange(seq_len))
    qg_p = qg * (HEAD_DIM ** -0.5 * math.log2(math.e))
    outs = _proj(x2d, seq_len, DILATIONS, 256, False, g1, w_perm, w_gif, bif, qg_p, kg, tabs_p, gmat)
    qs, ks, vs = outs[0:3], outs[3:6], outs[6:9]
    hm, ga, gb, st_p, m_p = outs[9:14]

    o_att = _attention(qs, ks, vs, seq_len)

    x_s = jnp.zeros((LANES, d_model), F32).at[:db].set(x_sample.reshape(db, d_model))
    tabs_s = _rope_tables(np.full((LANES,), past_len))
    outs_s = _proj(x_s, LANES, (1, 1, 1), LANES, True, g1, w_perm, w_gif, bif, qg, kg, tabs_s, gmat)
    qkv_t = _sample_prep(outs_s[0:9])
    caches_t = [c.transpose(0, 2, 3, 4, 1) for c in caches]
    y_prompt, kv_st, o_att_cols = _merge_ffn(x2d, o_att, hm, ga, gb, wa, wm, wo, g2, wg, wu, wd,
                                             seq_len, 256, decode=(qkv_t, caches_t))
    y_prompt = y_prompt.reshape(nb, seq_len, d_model)
    kv_s = [c.transpose(0, 4, 1, 2, 3) for c in kv_st]

    tails = _kv_tails(ks, vs, seq_len)
    kv_p = [t.reshape(nb, 2, HEADS, HEAD_DIM, t.shape[-1]).transpose(0, 4, 1, 2, 3) for t in tails]
    c_p = st_p[..., :M_DV].reshape(nb, M_HEADS, M_DQK, M_DV)
    n_p = st_p[..., M_DV].reshape(nb, M_HEADS, M_DQK)
    m_pr = m_p[:, :, 0]

    qm_t, km_t, vm_s, om_s, ga_s, gb_s, gif_s = outs_s[9:16]
    gif_ts = gif_s.T

    rexp = jnp.asarray(np.arange(M_HEADS * M_DQK)[:, None] // M_DQK == np.arange(M_HEADS)[None, :],
                       dtype=F32)
    m_t = jnp.zeros((M_HEADS, LANES), F32).at[:, :db].set(state_mlstm_m.T)
    n_pad = jnp.zeros((LANES, M_HEADS * M_DQK), F32).at[:db].set(state_mlstm_n.reshape(db, -1))
    hm_s3, c_s, n_t, m_so = _sample_mlstm(
        qm_t, km_t, vm_s[:db].reshape(db, M_HEADS, M_DV), om_s[:db].reshape(db, M_HEADS, M_DV),
        gif_ts[:2 * M_HEADS], m_t, n_pad, state_mlstm_C, rexp, db)
    n_s = n_t.T[:db].reshape(db, M_HEADS, M_DQK)
    m_s = m_so[:, :db].T

    o_att_s = o_att_cols[:, :, :, 0].reshape(db, PAIRS, LANES)
    o_att_sp = jnp.zeros((PAIRS, LANES, LANES), F32).at[:, :db].set(o_att_s.transpose(1, 0, 2))[None]
    hm_sp = jnp.zeros((LANES, 1024), F32).at[:db].set(hm_s3.reshape(db, 1024))
    y_s = _merge_ffn(x_s, o_att_sp, hm_sp, ga_s, gb_s, wa, wm, wo, g2, wg, wu, wd, LANES, LANES)
    y_sample = y_s[:db].reshape(db, 1, d_model)

    return (y_prompt, y_sample, kv_p[0], kv_p[1], kv_p[2], c_p, n_p, m_pr,
            kv_s[0], kv_s[1], kv_s[2], c_s, n_s, m_s)
```

```python
import functools
import math

import jax
import jax.numpy as jnp
import numpy as np
from jax import lax
from jax.experimental import pallas as pl
from jax.experimental.pallas import tpu as pltpu

F32 = jnp.float32
BF16 = jnp.bfloat16

HEAD_DIM = 64
HEADS = 8
GROUP_W = HEADS * HEAD_DIM
N_GROUPS = 3
WINDOWS = (128, 512, 2048)
DILATIONS = (1, 4, 16)
SPAN = 128
ROT_DIM = 16
ROPE_THETA = 500000.0
M_HEADS = 8
M_DQK = 64
M_DV = 128
NORM_EPS = 1e-6
PAIRS = GROUP_W // 128
NEG = -1e30

LANES = 128
VMEM_LIMIT = 56 * 1024 * 1024

_ATT_W = N_GROUPS * GROUP_W
_C_QM = 3 * _ATT_W
_C_KM = _C_QM + M_HEADS * M_DQK
_C_VM = _C_KM + M_HEADS * M_DQK
_C_OM = _C_VM + M_HEADS * M_DV
_C_GA = _C_OM + M_HEADS * M_DV
_C_GB = _C_GA + 1024
_W_COLS = _C_GB + 1024
_GIF_COLS = 2 * M_HEADS


def _const_spec(shape):
    nd = len(shape)
    return pl.BlockSpec(shape, lambda *_: (0,) * nd, pipeline_mode=pl.Buffered(1))


def _log_sigmoid(x):
    return jnp.minimum(x, 0.0) - jnp.log1p(jnp.exp(-jnp.abs(x)))


def _proj_kernel(x_ref, g1_ref, w_ref, wgif_ref, bif_ref, qg_ref, kg_ref, ra_ref, rm_ref, rp_ref,
                 gm_ref,
                 q0_ref, q1_ref, q2_ref, k0_ref, k1_ref, k2_ref, v0_ref, v1_ref, v2_ref,
                 *rest, tm, dils, plain, tiles_per_seq):
    if plain:
        qm_ref, km_ref, vm_ref, om_ref, ga_ref, gb_ref, gif_ref, hs_ref = rest
    else:
        hm_ref, ga_ref, gb_ref, st_ref, mo_ref, hs_ref, st_s, m_s = rest
    d_model = x_ref.shape[1]
    x = x_ref[...]
    xn = x * lax.rsqrt(jnp.mean(x * x, axis=-1, keepdims=True) + NORM_EPS) * g1_ref[...]
    h_nat = xn.astype(BF16)
    n_slab = d_model // LANES
    if any(d > 1 for d in dils):
        for c in range(n_slab):
            hs_ref[c] = xn[:, c * LANES:(c + 1) * LANES]

    def permuted_h(d):
        if d == 1:
            return h_nat
        n = tm // d
        rows = [jnp.concatenate([hs_ref[c, pl.ds(r, n, stride=d), :] for c in range(n_slab)], axis=1)
                for r in range(d)]
        return jnp.concatenate(rows, axis=0).astype(BF16)

    def permuted_tab(ref, d):
        if d == 1:
            t = ref[...]
        else:
            n = tm // d
            t = jnp.concatenate([ref[pl.ds(r, n, stride=d), :] for r in range(d)], axis=0)
        return jnp.concatenate([t] * PAIRS, axis=1)

    gmat = gm_ref[...]

    def head_sumsq(z):
        zz = (z * z).astype(BF16)
        half = GROUP_W // 2
        return jnp.concatenate(
            [jnp.dot(zz[:, :half], gmat, preferred_element_type=F32),
             jnp.dot(zz[:, half:], gmat, preferred_element_type=F32)], axis=1)

    def norm_rope(z, ss, gain, ra, rm, rp):
        y = z * lax.rsqrt(ss * (1.0 / HEAD_DIM) + NORM_EPS) * gain
        return (y * ra + pltpu.roll(y, GROUP_W - ROT_DIM // 2, 1) * rm
                + pltpu.roll(y, ROT_DIM // 2, 1) * rp)

    def store_group(ref, y, d):
        if plain:
            ref[...] = y.astype(ref.dtype)
            return
        n = tm // d
        for p in range(PAIRS):
            ref[:, p] = y[:, p * LANES:(p + 1) * LANES].reshape(d, n, LANES).astype(ref.dtype)

    def seg(c0, width):
        return jnp.dot(h_nat, w_ref[:, c0:c0 + width], preferred_element_type=F32)

    if not plain:
        @pl.when(pl.program_id(0) % tiles_per_seq == 0)
        def _():
            st_s[...] = jnp.zeros_like(st_s)
            m_s[...] = jnp.zeros_like(m_s)

        gif = jnp.dot(h_nat, wgif_ref[...], preferred_element_type=F32) + bif_ref[...]
        m_phase1, m_phase2, m_phase3 = _mlstm_chunk(
            seg(_C_QM, 512).astype(BF16), seg(_C_KM, 512).astype(BF16),
            seg(_C_VM, 1024).astype(BF16), seg(_C_OM, 1024).astype(BF16),
            gif, gif.T[:2 * M_HEADS, :], st_s, m_s, hm_ref, tm)
        m_lo, m_hi = (0, 1), (2, 3)
        m_phase1(m_lo)
        m_phase2(m_lo)

    q_refs = (q0_ref, q1_ref, q2_ref)
    k_refs = (k0_ref, k1_ref, k2_ref)
    v_refs = (v0_ref, v1_ref, v2_ref)
    for g in range(N_GROUPS):
        d = dils[g]
        hg = permuted_h(d)
        ra, rm, rp = (permuted_tab(r, d) for r in (ra_ref, rm_ref, rp_ref))
        cq, ck, cv = (t * _ATT_W + g * GROUP_W for t in range(3))
        zq = jnp.dot(hg, w_ref[:, cq:cq + GROUP_W], preferred_element_type=F32)
        zk = jnp.dot(hg, w_ref[:, ck:ck + GROUP_W], preferred_element_type=F32)
        zv = jnp.dot(hg, w_ref[:, cv:cv + GROUP_W], preferred_element_type=F32)
        ssq, ssk = head_sumsq(zq), head_sumsq(zk)
        store_group(v_refs[g], zv, d)
        store_group(q_refs[g], norm_rope(zq, ssq, qg_ref[...], ra, rm, rp), d)
        store_group(k_refs[g], norm_rope(zk, ssk, kg_ref[...], ra, rm, rp), d)

        if g == 0 and not plain:
            m_phase3(m_lo)
            m_phase1(m_hi)
            m_phase2(m_hi)
        if g == 1 and not plain:
            m_phase3(m_hi)

    if plain:
        segments = ((qm_ref, _C_QM, 512), (km_ref, _C_KM, 512), (vm_ref, _C_VM, 1024),
                    (om_ref, _C_OM, 1024), (ga_ref, _C_GA, 1024), (gb_ref, _C_GB, 1024))
        zg = jnp.dot(h_nat, wgif_ref[...], preferred_element_type=F32)
        gif_ref[...] = zg + bif_ref[...]
    else:
        st_ref[...] = st_s[...]
        mo_ref[...] = m_s[...]
        segments = ((ga_ref, _C_GA, 1024), (gb_ref, _C_GB, 1024))
    for ref, c0, width in segments:
        for cc in range(0, width, GROUP_W):
            ref[:, cc:cc + GROUP_W] = seg(c0 + cc, GROUP_W).astype(ref.dtype)


def _proj(x2d, seq_len, dils, tm, plain, g1, w_perm, w_gif, bif, qg, kg, rope_tabs, gmat):
    m_rows, d_model = x2d.shape
    nb = m_rows // seq_len
    tiles_per_seq = seq_len // tm
    grid = (m_rows // tm,)
    row_spec = lambda w: pl.BlockSpec((tm, w), lambda i: (i, 0))
    tab_spec = pl.BlockSpec((tm, LANES), lambda i: (i % tiles_per_seq, 0))
    sds = jax.ShapeDtypeStruct

    widths = (512, 512, 1024, 1024, 1024, 1024)
    scratch = [pltpu.VMEM((d_model // LANES, tm, LANES), F32)]
    if plain:
        out_shape = ([sds((m_rows, GROUP_W), F32)] * 9 + [sds((m_rows, w), F32) for w in widths]
                     + [sds((m_rows, LANES), F32)])
        out_specs = ([row_spec(GROUP_W)] * 9 + [row_spec(w) for w in widths] + [row_spec(LANES)])
    else:
        def grp_spec(d):
            return pl.BlockSpec((None, d, PAIRS, tm // d, LANES),
                                lambda i: (i // tiles_per_seq, 0, 0, i % tiles_per_seq, 0))
        grp_shape = lambda d: sds((nb, d, PAIRS, seq_len // d, LANES), BF16)
        state_spec = lambda *dims: pl.BlockSpec((None,) + dims,
                                                lambda i: (i // tiles_per_seq,) + (0,) * len(dims))
        st_dims, m_dims = (M_HEADS // 2, LANES, 2 * LANES), (M_HEADS, LANES)
        out_shape = ([grp_shape(d) for d in dils] * 3 + [sds((m_rows, 1024), BF16)] * 3
                     + [sds((nb,) + st_dims, F32), sds((nb,) + m_dims, F32)])
        out_specs = ([grp_spec(d) for d in dils] * 3 + [row_spec(1024)] * 3
                     + [state_spec(*st_dims), state_spec(*m_dims)])
        scratch += [pltpu.VMEM(st_dims, F32), pltpu.VMEM(m_dims, F32)]
    in_specs = [row_spec(d_model), _const_spec((1, d_model)), _const_spec(w_perm.shape),
                _const_spec(w_gif.shape), _const_spec((1, LANES)), _const_spec((1, GROUP_W)),
                _const_spec((1, GROUP_W)), tab_spec, tab_spec, tab_spec, _const_spec(gmat.shape)]
    return pl.pallas_call(
        functools.partial(_proj_kernel, tm=tm, dils=dils, plain=plain,
                          tiles_per_seq=tiles_per_seq),
        grid=grid, in_specs=in_specs, out_specs=out_specs, out_shape=out_shape,
        scratch_shapes=scratch,
        compiler_params=pltpu.CompilerParams(dimension_semantics=("arbitrary",),
                                             vmem_limit_bytes=VMEM_LIMIT),
        name="proj",
    )(x2d, g1, w_perm, w_gif, bif, qg, kg, *rope_tabs, gmat)


ATT_STEP = SPAN * max(DILATIONS)
ATT_UNROLL = 16


def _attn_kernel(q0, k0, v0, kh0, vh0, q1, k1, v1, kh1, vh1, q2, k2, v2, kh2, vh2,
                 *rest, n_cast):
    w_f32, o_ref, w_bf16 = rest[:n_cast], rest[n_cast], rest[n_cast + 1:2 * n_cast + 1]
    acc_s, m_s, l_s = rest[2 * n_cast + 1:]
    j = pl.program_id(1)
    for src, dst in zip(w_f32, w_bf16):
        dst[...] = src[...].astype(dst.dtype)

    def window(cur, halo, r, bi):
        here = cur[r, pl.ds(pl.multiple_of(bi * SPAN, SPAN), SPAN), :]
        if cur.shape[1] == SPAN:
            prev = halo[r]
        else:
            before = cur[r, pl.ds(pl.multiple_of(jnp.maximum(bi - 1, 0) * SPAN, SPAN), SPAN), :]
            prev = jnp.where(bi == 0, halo[r], before)
        return jnp.concatenate([prev, here], axis=0)

    qi = lax.broadcasted_iota(jnp.int32, (SPAN, 2 * SPAN), 0)
    ci = lax.broadcasted_iota(jnp.int32, (SPAN, 2 * SPAN), 1)
    band = (ci >= qi) & (ci <= qi + SPAN)
    bias_band = jnp.where(band, 0.0, NEG).astype(F32)
    bias_first = jnp.where(band & (ci >= SPAN), 0.0, NEG).astype(F32)
    lane_q = lax.broadcasted_iota(jnp.int32, (SPAN, LANES), 1)
    lane_kv = lax.broadcasted_iota(jnp.int32, (2 * SPAN, LANES), 1)
    halves_q = (lane_q < HEAD_DIM, lane_q >= HEAD_DIM)
    halves_kv = (lane_kv < HEAD_DIM, lane_kv >= HEAD_DIM)

    def blocks(g, q_ref, kv_refs, units):
        k_cur, k_halo, v_cur, v_halo = kv_refs
        loaded = []
        for r, bi, _ in units:
            row0 = pl.multiple_of(bi * SPAN, SPAN)
            q2_ = q_ref[r, pl.ds(row0, SPAN), :]
            kk = window(k_cur, k_halo, r, bi)
            first = jnp.logical_and(j == 0, bi == 0)
            bias = jnp.where(first, bias_first, bias_band)
            ss = []
            for hh in range(2):
                qa = jnp.where(halves_q[hh], q2_, jnp.zeros_like(q2_))
                ss.append(lax.dot_general(qa, kk, (((1,), (1,)), ((), ())),
                                          preferred_element_type=F32) + bias)
            loaded.append(ss)
        probs = []
        for ss in loaded:
            mxs = [jnp.max(s, axis=-1, keepdims=True) for s in ss]
            ps = [jnp.exp2(s - mx) for s, mx in zip(ss, mxs)]
            ls = [jnp.sum(p, axis=-1, keepdims=True) for p in ps]
            probs.append(([p.astype(BF16) for p in ps], mxs, ls))
        for (r, bi, sl), (ps, mxs, ls) in zip(units, probs):
            vv = window(v_cur, v_halo, r, bi)
            acc = None
            for hh in range(2):
                vh = jnp.where(halves_kv[hh], vv, jnp.zeros_like(vv))
                a = jnp.dot(ps[hh], vh, preferred_element_type=F32)
                acc = a if acc is None else acc + a
            acc_s[g, sl, :] = acc
            m_s[g, sl, :] = jnp.where(halves_q[0], mxs[0], mxs[1])
            l_s[g, sl, :] = jnp.where(halves_q[0], ls[0], ls[1])

    d0, d1, d2 = DILATIONS
    nb0, nb1, nb2 = (ATT_STEP // d // SPAN for d in DILATIONS)
    un = ATT_UNROLL

    def body0(it, c):
        units = []
        for u in range(un):
            bi = it * un + u
            units.append((0, bi, pl.ds(pl.multiple_of(bi * SPAN, SPAN), SPAN)))
        blocks(0, q0, (k0, kh0, v0, vh0), units)
        return c
    lax.fori_loop(0, nb0 // un, body0, 0)

    def body1(it, c):
        units = []
        for u in range(un):
            idx = it * un + u
            r, bi = idx // nb1, idx % nb1
            units.append((r, bi, pl.ds(bi * SPAN * d1 + r, SPAN, stride=d1)))
        blocks(1, q1, (k1, kh1, v1, vh1), units)
        return c
    lax.fori_loop(0, d1 * nb1 // un, body1, 0)

    def body2(it, c):
        units = []
        for u in range(un):
            r = it * un + u
            units.append((r, 0, pl.ds(r, SPAN, stride=d2)))
        blocks(2, q2, (k2, kh2, v2, vh2), units)
        return c
    lax.fori_loop(0, d2 * nb2 // un, body2, 0)

    def combine(ci_, c):
        sl = pl.ds(pl.multiple_of(ci_ * SPAN, SPAN), SPAN)
        ms = [m_s[g, sl, :] for g in range(N_GROUPS)]
        mm = jnp.maximum(jnp.maximum(ms[0], ms[1]), ms[2])
        es = [jnp.exp2(m - mm) for m in ms]
        num = es[0] * acc_s[0, sl, :] + es[1] * acc_s[1, sl, :] + es[2] * acc_s[2, sl, :]
        den = es[0] * l_s[0, sl, :] + es[1] * l_s[1, sl, :] + es[2] * l_s[2, sl, :]
        o_ref[sl, :] = (num / den).astype(o_ref.dtype)
        return c
    lax.fori_loop(0, ATT_STEP // SPAN, combine, 0)


def _attention(qs, ks, vs, seq_len, cast_ws):
    nb = qs[0].shape[0]
    steps = seq_len // ATT_STEP
    in_specs, args, scratch = [], [], []
    for g, d in enumerate(DILATIONS):
        rows = ATT_STEP // d
        cur = pl.BlockSpec((None, d, None, rows, LANES), lambda b, j, p: (b, 0, p, j, 0))
        ratio = rows // SPAN
        halo = pl.BlockSpec((None, d, None, SPAN, LANES),
                            lambda b, j, p, ratio=ratio: (b, 0, p, jnp.maximum(j * ratio - 1, 0), 0))
        in_specs += [cur, cur, cur, halo, halo]
        args += [qs[g], ks[g], vs[g], ks[g], vs[g]]
    scratch += [pltpu.VMEM((N_GROUPS, ATT_STEP, LANES), F32)] * 3
    n_slices = max(nb * steps * PAIRS // 2, 1)
    slice_of = lambda b, j, p: (((b * steps + j) * PAIRS + p) // 2, 0)
    w_specs = []
    for w in cast_ws:
        assert w.shape[0] % (16 * n_slices) == 0, (w.shape, n_slices)
        w_specs.append(pl.BlockSpec((w.shape[0] // n_slices, w.shape[1]), slice_of))
    outs = pl.pallas_call(
        functools.partial(_attn_kernel, n_cast=len(cast_ws)),
        grid=(nb, steps, PAIRS),
        in_specs=in_specs + w_specs,
        out_specs=[pl.BlockSpec((None, None, ATT_STEP, LANES), lambda b, j, p: (b, p, j, 0))]
                  + w_specs,
        out_shape=[jax.ShapeDtypeStruct((nb, PAIRS, seq_len, LANES), BF16)]
                  + [jax.ShapeDtypeStruct(w.shape, BF16) for w in cast_ws],
        scratch_shapes=scratch,
        compiler_params=pltpu.CompilerParams(
            dimension_semantics=("arbitrary", "arbitrary", "arbitrary"),
            vmem_limit_bytes=VMEM_LIMIT),
        name="attn",
    )(*args, *cast_ws)
    return outs[0], outs[1:]


def _kv_tail_kernel(k0, v0, k1, v1, k2, v2, o0, o1, o2, nat_ref):
    for (k_ref, v_ref, o_ref, d) in ((k0, v0, o0, DILATIONS[0]), (k1, v1, o1, DILATIONS[1]),
                                     (k2, v2, o2, DILATIONS[2])):
        for kvi, ref in enumerate((k_ref, v_ref)):
            for p in range(PAIRS):
                rows = slice(p * LANES, (p + 1) * LANES)
                if d == 1:
                    o_ref[kvi, rows, :] = ref[0, p].astype(F32).T
                    continue
                for r in range(d):
                    nat_ref[pl.ds(r, SPAN, stride=d), :] = ref[r, p].astype(F32)
                for c in range(d):
                    o_ref[kvi, rows, c * SPAN:(c + 1) * SPAN] = nat_ref[c * SPAN:(c + 1) * SPAN, :].T


def _kv_tails(ks, vs, seq_len):
    nb = ks[0].shape[0]
    in_specs, args, out_specs, out_shape = [], [], [], []
    for g, d in enumerate(DILATIONS):
        last = seq_len // d // SPAN - 1
        spec = pl.BlockSpec((None, d, PAIRS, SPAN, LANES), lambda b, last=last: (b, 0, 0, last, 0))
        in_specs += [spec, spec]
        args += [ks[g], vs[g]]
        out_specs.append(pl.BlockSpec((None, 2, GROUP_W, SPAN * d), lambda b: (b, 0, 0, 0)))
        out_shape.append(jax.ShapeDtypeStruct((nb, 2, GROUP_W, SPAN * d), F32))
    return pl.pallas_call(
        _kv_tail_kernel,
        grid=(nb,), in_specs=in_specs, out_specs=out_specs, out_shape=out_shape,
        scratch_shapes=[pltpu.VMEM((SPAN * max(DILATIONS), LANES), F32)],
        compiler_params=pltpu.CompilerParams(dimension_semantics=("arbitrary",),
                                             vmem_limit_bytes=VMEM_LIMIT),
        name="kv_tail",
    )(*args)


def _mlstm_chunk(q2d, k2d, v2d, om2d, gif, gt, st_s, m_s, hm_ref, lc):
    lf_c = _log_sigmoid(gif)
    lf_r = _log_sigmoid(gt[M_HEADS:2 * M_HEADS, :])
    row = lax.broadcasted_iota(jnp.int32, (lc, lc), 0)
    col = lax.broadcasted_iota(jnp.int32, (lc, lc), 1)
    causal = col <= row
    tril = causal.astype(F32)
    triu = (row <= col).astype(F32)
    b_c = jnp.dot(tril, lf_c, precision=lax.Precision.HIGHEST, preferred_element_type=F32)
    b_r = jnp.dot(lf_r, triu, precision=lax.Precision.HIGHEST, preferred_element_type=F32)
    lane = lax.broadcasted_iota(jnp.int32, (lc, LANES), 1)
    halves = (lane < M_DQK, lane >= M_DQK)
    sub = lax.broadcasted_iota(jnp.int32, (LANES, 1), 0)
    ones_blk = jnp.ones((lc, LANES), BF16)
    qscale = M_DQK ** -0.5

    states, p1, p2 = {}, {}, {}
    all_pairs = tuple(range(M_HEADS // 2))
    heads_of = lambda pairs: [2 * p + hh for p in pairs for hh in range(2)]

    def phase1(pairs=all_pairs):
        for p in pairs:
            states[p] = st_s[p]
        for h in heads_of(pairs):
            p, hh = divmod(h, 2)
            q2_ = q2d[:, p * LANES:(p + 1) * LANES]
            k2_ = k2d[:, p * LANES:(p + 1) * LANES]
            qa = jnp.where(halves[hh], q2_, jnp.zeros_like(q2_)) * jnp.asarray(qscale, q2_.dtype)
            s = lax.dot_general(qa, k2_, (((1,), (1,)), ((), ())), preferred_element_type=F32)
            qc = jnp.dot(qa, states[p].astype(BF16), preferred_element_type=F32)
            p1[h] = (s, qc)

    def phase2(pairs=all_pairs):
        for h in heads_of(pairs):
            p, hh = divmod(h, 2)
            s, qc = p1[h]
            bcol = b_c[:, M_HEADS + h:M_HEADS + h + 1]
            igcol = gif[:, h:h + 1]
            a_row = gt[h:h + 1, :] - b_r[h:h + 1, :]
            amat = jnp.where(causal, a_row, NEG)
            m_prev = m_s[h:h + 1, 0:1]
            g_t = jnp.maximum(m_prev, jnp.max(amat, axis=-1, keepdims=True))
            wqk = (jnp.exp(amat - g_t) * s).astype(BF16)
            dec = jnp.exp(m_prev - g_t)
            m_t = bcol + g_t
            m_new = m_t[lc - 1:lc, :]
            b_last = bcol[lc - 1:lc, :]
            dstate = jnp.exp(b_last + m_prev - m_new)
            ws = jnp.exp(b_last - bcol + igcol - m_new)
            k2_ = k2d[:, p * LANES:(p + 1) * LANES]
            ka = jnp.where(halves[hh], k2_, jnp.zeros_like(k2_))
            kws = (ka.astype(F32) * ws).astype(BF16)
            m_s[h:h + 1, :] = jnp.broadcast_to(m_new, (1, LANES))
            p2[h] = (wqk, dec * qc, jnp.exp(-m_t), dstate, kws)

    def phase3(pairs=all_pairs):
        upds = {}
        for h in heads_of(pairs):
            wqk, dqc, floor, _, kws = p2[h]
            v1 = jnp.concatenate([v2d[:, h * M_DV:(h + 1) * M_DV], ones_blk], axis=1)
            num_den = dqc + jnp.dot(wqk, v1, preferred_element_type=F32)
            num = num_den[:, :M_DV]
            den = num_den[:, M_DV:]
            hval = num / jnp.maximum(jnp.abs(den), floor)
            gate = jax.nn.sigmoid(om2d[:, h * M_DV:(h + 1) * M_DV].astype(F32))
            hm_ref[:, h * M_DV:(h + 1) * M_DV] = (gate * hval).astype(hm_ref.dtype)
            upds[h] = lax.dot_general(kws, v1, (((0,), (0,)), ((), ())),
                                      preferred_element_type=F32)
        for p in pairs:
            drow = jnp.where(sub < M_DQK, p2[2 * p][3], p2[2 * p + 1][3])
            st_s[p] = drow * states[p] + upds[2 * p] + upds[2 * p + 1]

    return phase1, phase2, phase3


FF_CHUNK = 256


DEC_HEADS = HEADS // 2
DEC_ROWS = DEC_HEADS * HEAD_DIM


def _decode_attn_step(b, qkv_ref, c_refs, o_refs, ot_ref):
    sel = lax.broadcasted_iota(jnp.int32, (DEC_ROWS, LANES), 1) == b
    scale = HEAD_DIM ** -0.5

    def column(i):
        return jnp.sum(jnp.where(sel, qkv_ref[i], 0.0), axis=1, keepdims=True)

    def head(col, h):
        return col[h * HEAD_DIM:(h + 1) * HEAD_DIM, :]

    qc = [column(g) * scale for g in range(N_GROUPS)]
    kn = [column(N_GROUPS + g) for g in range(N_GROUPS)]
    vn = [column(2 * N_GROUPS + g) for g in range(N_GROUPS)]
    cols = []
    for h in range(DEC_HEADS):
        scores, m_h = [], None
        for g, d in enumerate(DILATIONS):
            wb = c_refs[g].shape[-1]
            qh = head(qc[g], h)
            s = jnp.sum(c_refs[g][0, h] * qh, axis=0, keepdims=True)
            pos = lax.broadcasted_iota(jnp.int32, (1, wb), 1)
            s = jnp.where((pos & (d - 1)) == 0, s, NEG)
            s_new = jnp.sum(head(kn[g], h) * qh, axis=0, keepdims=True)
            m_g = jnp.maximum(jnp.max(s, axis=1, keepdims=True), s_new)
            m_h = m_g if m_h is None else jnp.maximum(m_h, m_g)
            scores.append((s, s_new))
        l_h = jnp.zeros((1, 1), F32)
        acc = jnp.zeros((HEAD_DIM, 1), F32)
        for g in range(N_GROUPS):
            s, s_new = scores[g]
            p = jnp.exp(s - m_h)
            p_new = jnp.exp(s_new - m_h)
            l_h = l_h + jnp.sum(p, axis=1, keepdims=True) + p_new
            acc = acc + jnp.sum(c_refs[g][1, h] * p, axis=1, keepdims=True) + p_new * head(vn[g], h)
        cols.append(acc / l_h)
    ot_ref[...] = jnp.broadcast_to(jnp.concatenate(cols, axis=0), ot_ref.shape)
    for g in range(N_GROUPS):
        wb = c_refs[g].shape[-1]
        last = lax.broadcasted_iota(jnp.int32, (HEAD_DIM, wb), 1) == wb - 1
        for kvi, new in ((0, kn[g]), (1, vn[g])):
            for h in range(DEC_HEADS):
                rolled = pltpu.roll(c_refs[g][kvi, h], wb - 1, 1)
                o_refs[g][kvi, h] = jnp.where(last, head(new, h), rolled)


def _merge_ffn_kernel(x_ref, o_ref, hm_ref, ga_ref, gb_ref, wa_ref, wm_ref, wo_ref, g2_ref,
                      wg_ref, wu_ref, wd_ref, *rest, with_decode):
    if with_decode:
        qkv_ref, c0, c1, c2, y_ref, o0, o1, o2, ot_ref = rest
        _decode_attn_step(pl.program_id(0) // 2, qkv_ref, (c0, c1, c2), (o0, o1, o2), ot_ref)
    else:
        (y_ref,) = rest
    o_att = jnp.concatenate([o_ref[p] for p in range(PAIRS)], axis=1).astype(BF16)
    ya = jnp.dot(o_att, wa_ref[...], preferred_element_type=F32)
    yb = jnp.dot(hm_ref[...].astype(BF16), wm_ref[...], preferred_element_type=F32)
    mixed = (jax.nn.sigmoid(ga_ref[...].astype(F32)) * ya
             + jax.nn.sigmoid(gb_ref[...].astype(F32)) * yb)
    x2 = x_ref[...] + jnp.dot(mixed.astype(BF16), wo_ref[...], preferred_element_type=F32)
    h2 = (x2 * lax.rsqrt(jnp.mean(x2 * x2, axis=-1, keepdims=True) + NORM_EPS)
          * g2_ref[...]).astype(BF16)
    acc = x2
    d_ff = wg_ref.shape[1]

    def gate_up(c):
        gt = jnp.dot(h2, wg_ref[:, c:c + FF_CHUNK], preferred_element_type=F32)
        up = jnp.dot(h2, wu_ref[:, c:c + FF_CHUNK], preferred_element_type=F32)
        return gt, up

    chunks = list(range(0, d_ff, FF_CHUNK))
    nxt = gate_up(chunks[0])
    for i, c in enumerate(chunks):
        gt, up = nxt
        if i + 1 < len(chunks):
            nxt = gate_up(chunks[i + 1])
        ff = (gt * jax.nn.sigmoid(gt) * up).astype(BF16)
        acc = acc + jnp.dot(ff, wd_ref[c:c + FF_CHUNK, :], preferred_element_type=F32)
    y_ref[...] = acc


def _merge_ffn(x2d, o_att, hm, ga, gb, wa, wm, wo, g2, wg, wu, wd, seq_len, tm, decode=None):
    m_rows, d_model = x2d.shape
    tiles_per_seq = seq_len // tm
    steps = m_rows // tm
    row = lambda w: pl.BlockSpec((tm, w), lambda i: (i, 0))
    o_spec = pl.BlockSpec((None, PAIRS, tm, LANES),
                          lambda i: (i // tiles_per_seq, 0, i % tiles_per_seq, 0))
    in_specs = [row(d_model), o_spec, row(1024), row(1024), row(1024),
                _const_spec(wa.shape), _const_spec(wm.shape), _const_spec(wo.shape),
                _const_spec((1, d_model)), _const_spec(wg.shape), _const_spec(wu.shape),
                _const_spec(wd.shape)]
    args = [x2d, o_att, hm, ga, gb, wa, wm, wo, g2, wg, wu, wd]
    out_specs = [row(d_model)]
    out_shape = [jax.ShapeDtypeStruct((m_rows, d_model), F32)]
    if decode is not None:
        qkv_t, caches_t = decode
        db = caches_t[0].shape[0]
        assert steps == 2 * db, (steps, db)
        in_specs.append(pl.BlockSpec((qkv_t.shape[0], DEC_ROWS, LANES), lambda i: (0, i % 2, 0)))
        args.append(qkv_t)
        for c in caches_t:
            spec = pl.BlockSpec((None, 2, DEC_HEADS, HEAD_DIM, c.shape[-1]),
                                lambda i: (i // 2, 0, i % 2, 0, 0))
            in_specs.append(spec)
            args.append(c)
            out_specs.append(spec)
            out_shape.append(jax.ShapeDtypeStruct(c.shape, F32))
        out_specs.append(pl.BlockSpec((None, None, DEC_ROWS, LANES), lambda i: (i // 2, i % 2, 0, 0)))
        out_shape.append(jax.ShapeDtypeStruct((db, 2, DEC_ROWS, LANES), F32))
    outs = pl.pallas_call(
        functools.partial(_merge_ffn_kernel, with_decode=decode is not None),
        grid=(steps,),
        in_specs=in_specs, out_specs=out_specs, out_shape=out_shape,
        compiler_params=pltpu.CompilerParams(dimension_semantics=("arbitrary",),
                                             vmem_limit_bytes=VMEM_LIMIT),
        name="merge_ffn",
    )(*args)
    if decode is None:
        return outs[0]
    return outs[0], outs[1:4], outs[4]


def _sample_prep_kernel(*refs):
    for i, ref in enumerate(refs[:-1]):
        refs[-1][i] = ref[...].T


def _sample_prep(arrs):
    vm = pl.BlockSpec(memory_space=pltpu.VMEM)
    return pl.pallas_call(
        _sample_prep_kernel, in_specs=[vm] * len(arrs), out_specs=vm,
        out_shape=jax.ShapeDtypeStruct((len(arrs), GROUP_W, LANES), F32),
        name="sample_prep",
    )(*arrs)


def _sample_mlstm_kernel(q_ref, k_ref, v_ref, om_ref, gt_ref, mt_ref, n_ref, c_ref, rexp_ref,
                         hm_ref, co_ref, nt_ref, mo_ref, *, db):
    nh, dqk = M_HEADS, M_DQK
    hi = lax.Precision.HIGHEST
    rexp = rexp_ref[...]
    q_t = q_ref[...].T * (dqk ** -0.5)
    k_t = k_ref[...].T
    n_t = n_ref[...].T
    ig = gt_ref[0:nh, :]
    lf = _log_sigmoid(gt_ref[nh:2 * nh, :])
    m_prev = mt_ref[...]
    m_new = jnp.maximum(lf + m_prev, ig)
    w8 = jnp.exp(ig - m_new)
    dec8 = jnp.exp(lf + m_prev - m_new)
    head_sum = lambda a: lax.dot_general(rexp, a, (((0,), (0,)), ((), ())), precision=hi,
                                         preferred_element_type=F32)
    expand = lambda a: jnp.dot(rexp, a, precision=hi, preferred_element_type=F32)
    qk8 = head_sum(q_t * k_t)
    qn8 = head_sum(q_t * n_t)
    wqk8 = w8 * qk8
    den8 = dec8 * qn8 + wqk8
    inv8 = 1.0 / jnp.maximum(jnp.abs(den8), jnp.exp(-m_new))
    dec_x = expand(dec8)
    w_x = expand(w8)
    nt_ref[...] = dec_x * n_t + w_x * k_t
    mo_ref[...] = m_new
    wk_x = w_x * k_t
    for b in range(db):
        cb = c_ref[b].reshape(nh * dqk, M_DV)
        v_b = v_ref[b]
        v_x = jnp.concatenate([jnp.broadcast_to(v_b[h:h + 1, :], (dqk, M_DV)) for h in range(nh)],
                              axis=0)
        co_ref[b] = (dec_x[:, b:b + 1] * cb + wk_x[:, b:b + 1] * v_x).reshape(nh, dqk, M_DV)
        qc = jnp.sum((q_t[:, b:b + 1] * cb).reshape(nh, dqk, M_DV), axis=1)
        num = dec8[:, b:b + 1] * qc + wqk8[:, b:b + 1] * v_b
        hm_ref[b] = jax.nn.sigmoid(om_ref[b]) * (num * inv8[:, b:b + 1])


def _sample_mlstm(q_t, k_t, v3, om3, gif_t, m_t, n_pad, c_state, rexp, db):
    vm = pl.BlockSpec(memory_space=pltpu.VMEM)
    return pl.pallas_call(
        functools.partial(_sample_mlstm_kernel, db=db),
        in_specs=[vm] * 9,
        out_specs=[vm] * 4,
        out_shape=[jax.ShapeDtypeStruct((db, M_HEADS, M_DV), F32),
                   jax.ShapeDtypeStruct(c_state.shape, F32),
                   jax.ShapeDtypeStruct((M_HEADS * M_DQK, LANES), F32),
                   jax.ShapeDtypeStruct((M_HEADS, LANES), F32)],
        compiler_params=pltpu.CompilerParams(vmem_limit_bytes=VMEM_LIMIT),
        name="sample_mlstm",
    )(q_t, k_t, v3, om3, gif_t, m_t, n_pad, c_state, rexp)


def _rope_tables(pos):
    half = ROT_DIM // 2
    pos = np.asarray(pos, np.float32)
    inv_freq = np.exp(np.float32(-math.log(ROPE_THETA))
                      * np.arange(0, ROT_DIM, 2, dtype=np.float32) / np.float32(ROT_DIM))
    ang = (pos[:, None] * inv_freq[None, :]).astype(np.float32)
    cos, sin = np.cos(ang).astype(np.float32), np.sin(ang).astype(np.float32)
    t = pos.shape[0]
    rest = HEAD_DIM - ROT_DIM
    a = np.concatenate([cos, cos, np.ones((t, rest), np.float32)], axis=1)
    bm = np.concatenate([-sin, np.zeros((t, HEAD_DIM - half), np.float32)], axis=1)
    bp = np.concatenate([np.zeros((t, half), np.float32), sin, np.zeros((t, rest), np.float32)],
                        axis=1)
    return tuple(jnp.asarray(np.concatenate([x, x], axis=1)) for x in (a, bm, bp))


W_PREP_BLK = 512


def _w_prep_kernel(wt_ref, o_ref):
    o_ref[...] = wt_ref[...].T.astype(o_ref.dtype)


def _w_prep(w_in):
    d_model = w_in.shape[0]
    wt = w_in.T
    n_head = _C_GA // W_PREP_BLK
    n_blk = _W_COLS // W_PREP_BLK

    sub = 8

    def src_row(j):
        return (j * (W_PREP_BLK // sub) + jnp.where(j < n_head, 0, _GIF_COLS // sub)) * sub

    return pl.pallas_call(
        _w_prep_kernel,
        grid=(n_blk,),
        in_specs=[pl.BlockSpec((pl.Element(W_PREP_BLK), pl.Element(d_model)),
                               lambda j: (src_row(j), 0))],
        out_specs=pl.BlockSpec((d_model, W_PREP_BLK), lambda j: (0, j)),
        out_shape=jax.ShapeDtypeStruct((d_model, _W_COLS), BF16),
        compiler_params=pltpu.CompilerParams(dimension_semantics=("arbitrary",),
                                             vmem_limit_bytes=VMEM_LIMIT),
        name="w_prep",
    )(wt)


def kernel(x_prompt, x_sample, cache_kv_w128, cache_kv_w512, cache_kv_w2048, state_mlstm_C, state_mlstm_n, state_mlstm_m, norm1_g, w_in, b_if, q_norm_g, k_norm_g, w_att_out, w_m_out, w_o, norm2_g, w_gate, w_up, w_down):
    nb, seq_len, d_model = x_prompt.shape
    db, dec_seq, _ = x_sample.shape
    assert dec_seq == 1 and d_model == 1024 and seq_len % ATT_STEP == 0 and db <= LANES
    caches = (cache_kv_w128, cache_kv_w512, cache_kv_w2048)
    past_len = 8192

    w_perm = _w_prep(w_in)
    w_gif = jnp.zeros((d_model, LANES), BF16).at[:, :_GIF_COLS].set(
        w_in[:, _C_GA:_C_GA + _GIF_COLS].astype(BF16))
    g1 = norm1_g.reshape(1, d_model)
    g2 = norm2_g.reshape(1, d_model)
    bif = jnp.concatenate([b_if, jnp.zeros((LANES - b_if.shape[0],), F32)]).reshape(1, LANES)
    qg = jnp.tile(q_norm_g, HEADS).reshape(1, GROUP_W)
    kg = jnp.tile(k_norm_g, HEADS).reshape(1, GROUP_W)
    hid = np.arange(GROUP_W // 2) // HEAD_DIM
    gmat = jnp.asarray(hid[:, None] == hid[None, :], dtype=BF16)

    m_rows = nb * seq_len
    x2d = x_prompt.reshape(m_rows, d_model)
    tabs_p = _rope_tables(np.arange(seq_len))
    qg_p = qg * (HEAD_DIM ** -0.5 * math.log2(math.e))
    outs = _proj(x2d, seq_len, DILATIONS, 256, False, g1, w_perm, w_gif, bif, qg_p, kg, tabs_p, gmat)
    qs, ks, vs = outs[0:3], outs[3:6], outs[6:9]
    hm, ga, gb, st_p, m_p = outs[9:14]

    o_att, (wa, wm, wo, wg, wu, wd) = _attention(
        qs, ks, vs, seq_len, (w_att_out, w_m_out, w_o, w_gate, w_up, w_down))

    x_s = jnp.zeros((LANES, d_model), F32).at[:db].set(x_sample.reshape(db, d_model))
    tabs_s = _rope_tables(np.full((LANES,), past_len))
    outs_s = _proj(x_s, LANES, (1, 1, 1), LANES, True, g1, w_perm, w_gif, bif, qg, kg, tabs_s, gmat)
    qkv_t = _sample_prep(outs_s[0:9])
    caches_t = [c.transpose(0, 2, 3, 4, 1) for c in caches]
    y_prompt, kv_st, o_att_cols = _merge_ffn(x2d, o_att, hm, ga, gb, wa, wm, wo, g2, wg, wu, wd,
                                             seq_len, 256, decode=(qkv_t, caches_t))
    y_prompt = y_prompt.reshape(nb, seq_len, d_model)
    kv_s = [c.transpose(0, 4, 1, 2, 3) for c in kv_st]

    tails = _kv_tails(ks, vs, seq_len)
    kv_p = [t.reshape(nb, 2, HEADS, HEAD_DIM, t.shape[-1]).transpose(0, 4, 1, 2, 3) for t in tails]
    c_p = st_p[..., :M_DV].reshape(nb, M_HEADS, M_DQK, M_DV)
    n_p = st_p[..., M_DV].reshape(nb, M_HEADS, M_DQK)
    m_pr = m_p[:, :, 0]

    qm_t, km_t, vm_s, om_s, ga_s, gb_s, gif_s = outs_s[9:16]
    gif_ts = gif_s.T

    rexp = jnp.asarray(np.arange(M_HEADS * M_DQK)[:, None] // M_DQK == np.arange(M_HEADS)[None, :],
                       dtype=F32)
    m_t = jnp.zeros((M_HEADS, LANES), F32).at[:, :db].set(state_mlstm_m.T)
    n_pad = jnp.zeros((LANES, M_HEADS * M_DQK), F32).at[:db].set(state_mlstm_n.reshape(db, -1))
    hm_s3, c_s, n_t, m_so = _sample_mlstm(
        qm_t, km_t, vm_s[:db].reshape(db, M_HEADS, M_DV), om_s[:db].reshape(db, M_HEADS, M_DV),
        gif_ts[:2 * M_HEADS], m_t, n_pad, state_mlstm_C, rexp, db)
    n_s = n_t.T[:db].reshape(db, M_HEADS, M_DQK)
    m_s = m_so[:, :db].T

    o_att_s = o_att_cols[:, :, :, 0].reshape(db, PAIRS, LANES)
    o_att_sp = jnp.zeros((PAIRS, LANES, LANES), F32).at[:, :db].set(o_att_s.transpose(1, 0, 2))[None]
    hm_sp = jnp.zeros((LANES, 1024), F32).at[:db].set(hm_s3.reshape(db, 1024))
    y_s = _merge_ffn(x_s, o_att_sp, hm_sp, ga_s, gb_s, wa, wm, wo, g2, wg, wu, wd, LANES, LANES)
    y_sample = y_s[:db].reshape(db, 1, d_model)

    return (y_prompt, y_sample, kv_p[0], kv_p[1], kv_p[2], c_p, n_p, m_pr,
            kv_s[0], kv_s[1], kv_s[2], c_s, n_s, m_s)
```

```python
import functools
import math

import jax
import jax.numpy as jnp
import numpy as np
from jax import lax
from jax.experimental import pallas as pl
from jax.experimental.pallas import tpu as pltpu

F32 = jnp.float32
BF16 = jnp.bfloat16

HEAD_DIM = 64
HEADS = 8
GROUP_W = HEADS * HEAD_DIM
N_GROUPS = 3
WINDOWS = (128, 512, 2048)
DILATIONS = (1, 4, 16)
SPAN = 128
ROT_DIM = 16
ROPE_THETA = 500000.0
M_HEADS = 8
M_DQK = 64
M_DV = 128
M_QK_W = M_HEADS * M_DQK
M_V_W = M_HEADS * M_DV
D_MODEL = 1024
PAST_LEN = 8192
NORM_EPS = 1e-6
NEG = -1e30

LANES = 128
PAIRS = GROUP_W // LANES
VMEM_LIMIT = 56 * 1024 * 1024

_ATT_W = N_GROUPS * GROUP_W
_C_QM = 3 * _ATT_W
_C_KM = _C_QM + M_HEADS * M_DQK
_C_VM = _C_KM + M_HEADS * M_DQK
_C_OM = _C_VM + M_HEADS * M_DV
_C_GA = _C_OM + M_HEADS * M_DV
_C_GB = _C_GA + 1024
_W_COLS = _C_GB + 1024
_GIF_COLS = 2 * M_HEADS


def _const_spec(shape):
    nd = len(shape)
    return pl.BlockSpec(shape, lambda *_: (0,) * nd, pipeline_mode=pl.Buffered(1))


def _log_sigmoid(x):
    return jnp.minimum(x, 0.0) - jnp.log1p(jnp.exp(-jnp.abs(x)))


def _proj_kernel(x_ref, g1_ref, w_ref, wgif_ref, bif_ref, qg_ref, kg_ref, ra_ref, rm_ref, rp_ref,
                 gm_ref,
                 q0_ref, q1_ref, q2_ref, k0_ref, k1_ref, k2_ref, v0_ref, v1_ref, v2_ref,
                 *rest, tm, dils, plain, tiles_per_seq):
    if plain:
        qm_ref, km_ref, vm_ref, om_ref, ga_ref, gb_ref, gif_ref, hs_ref = rest
    else:
        hm_ref, ga_ref, gb_ref, st_ref, mo_ref, hs_ref, st_s, m_s = rest
    d_model = x_ref.shape[1]
    x = x_ref[...]
    xn = x * lax.rsqrt(jnp.mean(x * x, axis=-1, keepdims=True) + NORM_EPS) * g1_ref[...]
    h_nat = xn.astype(BF16)
    n_slab = d_model // LANES
    if any(d > 1 for d in dils):
        for c in range(n_slab):
            hs_ref[c] = xn[:, c * LANES:(c + 1) * LANES]

    def permuted_h(d):
        if d == 1:
            return h_nat
        n = tm // d
        rows = [jnp.concatenate([hs_ref[c, pl.ds(r, n, stride=d), :] for c in range(n_slab)], axis=1)
                for r in range(d)]
        return jnp.concatenate(rows, axis=0).astype(BF16)

    def permuted_tab(ref, d):
        if d == 1:
            t = ref[...]
        else:
            n = tm // d
            t = jnp.concatenate([ref[pl.ds(r, n, stride=d), :] for r in range(d)], axis=0)
        return jnp.concatenate([t] * PAIRS, axis=1)

    gmat = gm_ref[...]

    def head_sumsq(z):
        zz = (z * z).astype(BF16)
        half = GROUP_W // 2
        return jnp.concatenate(
            [jnp.dot(zz[:, :half], gmat, preferred_element_type=F32),
             jnp.dot(zz[:, half:], gmat, preferred_element_type=F32)], axis=1)

    def norm_rope(z, ss, gain, ra, rm, rp):
        y = z * lax.rsqrt(ss * (1.0 / HEAD_DIM) + NORM_EPS) * gain
        return (y * ra + pltpu.roll(y, GROUP_W - ROT_DIM // 2, 1) * rm
                + pltpu.roll(y, ROT_DIM // 2, 1) * rp)

    def store_group(ref, y, d):
        if plain:
            ref[...] = y.astype(ref.dtype)
            return
        n = tm // d
        for p in range(PAIRS):
            ref[:, p] = y[:, p * LANES:(p + 1) * LANES].reshape(d, n, LANES).astype(ref.dtype)

    def seg(c0, width):
        return jnp.dot(h_nat, w_ref[:, c0:c0 + width], preferred_element_type=F32)

    if not plain:
        @pl.when(pl.program_id(0) % tiles_per_seq == 0)
        def _():
            st_s[...] = jnp.zeros_like(st_s)
            m_s[...] = jnp.zeros_like(m_s)

        gif = jnp.dot(h_nat, wgif_ref[...], preferred_element_type=F32) + bif_ref[...]
        m_phase1, m_phase2, m_phase3 = _mlstm_chunk(
            seg(_C_QM, M_QK_W).astype(BF16), seg(_C_KM, M_QK_W).astype(BF16),
            seg(_C_VM, M_V_W).astype(BF16), seg(_C_OM, M_V_W).astype(BF16),
            gif, gif.T[:2 * M_HEADS, :], st_s, m_s, hm_ref, tm)
        m_lo, m_hi = (0, 1), (2, 3)
        m_phase1(m_lo)
        m_phase2(m_lo)

    q_refs = (q0_ref, q1_ref, q2_ref)
    k_refs = (k0_ref, k1_ref, k2_ref)
    v_refs = (v0_ref, v1_ref, v2_ref)
    for g in range(N_GROUPS):
        d = dils[g]
        hg = permuted_h(d)
        ra, rm, rp = (permuted_tab(r, d) for r in (ra_ref, rm_ref, rp_ref))
        cq, ck, cv = (t * _ATT_W + g * GROUP_W for t in range(3))
        zq = jnp.dot(hg, w_ref[:, cq:cq + GROUP_W], preferred_element_type=F32)
        zk = jnp.dot(hg, w_ref[:, ck:ck + GROUP_W], preferred_element_type=F32)
        zv = jnp.dot(hg, w_ref[:, cv:cv + GROUP_W], preferred_element_type=F32)
        ssq, ssk = head_sumsq(zq), head_sumsq(zk)
        store_group(v_refs[g], zv, d)
        store_group(q_refs[g], norm_rope(zq, ssq, qg_ref[...], ra, rm, rp), d)
        store_group(k_refs[g], norm_rope(zk, ssk, kg_ref[...], ra, rm, rp), d)

        if g == 0 and not plain:
            m_phase3(m_lo)
            m_phase1(m_hi)
            m_phase2(m_hi)
        if g == 1 and not plain:
            m_phase3(m_hi)

    if plain:
        segments = ((qm_ref, _C_QM, M_QK_W), (km_ref, _C_KM, M_QK_W), (vm_ref, _C_VM, M_V_W),
                    (om_ref, _C_OM, M_V_W), (ga_ref, _C_GA, D_MODEL), (gb_ref, _C_GB, D_MODEL))
        zg = jnp.dot(h_nat, wgif_ref[...], preferred_element_type=F32)
        gif_ref[...] = zg + bif_ref[...]
    else:
        st_ref[...] = st_s[...]
        mo_ref[...] = m_s[...]
        segments = ((ga_ref, _C_GA, D_MODEL), (gb_ref, _C_GB, D_MODEL))
    for ref, c0, width in segments:
        for cc in range(0, width, GROUP_W):
            ref[:, cc:cc + GROUP_W] = seg(c0 + cc, GROUP_W).astype(ref.dtype)


def _proj(x2d, seq_len, dils, tm, plain, g1, w_perm, w_gif, bif, qg, kg, rope_tabs, gmat):
    m_rows, d_model = x2d.shape
    nb = m_rows // seq_len
    tiles_per_seq = seq_len // tm
    grid = (m_rows // tm,)
    row_spec = lambda w: pl.BlockSpec((tm, w), lambda i: (i, 0))
    tab_spec = pl.BlockSpec((tm, LANES), lambda i: (i % tiles_per_seq, 0))
    sds = jax.ShapeDtypeStruct

    widths = (M_QK_W, M_QK_W, M_V_W, M_V_W, D_MODEL, D_MODEL)
    scratch = [pltpu.VMEM((d_model // LANES, tm, LANES), F32)]
    if plain:
        out_shape = ([sds((m_rows, GROUP_W), F32)] * 9 + [sds((m_rows, w), F32) for w in widths]
                     + [sds((m_rows, LANES), F32)])
        out_specs = ([row_spec(GROUP_W)] * 9 + [row_spec(w) for w in widths] + [row_spec(LANES)])
    else:
        def grp_spec(d):
            return pl.BlockSpec((None, d, PAIRS, tm // d, LANES),
                                lambda i: (i // tiles_per_seq, 0, 0, i % tiles_per_seq, 0))
        grp_shape = lambda d: sds((nb, d, PAIRS, seq_len // d, LANES), BF16)
        state_spec = lambda *dims: pl.BlockSpec((None,) + dims,
                                                lambda i: (i // tiles_per_seq,) + (0,) * len(dims))
        st_dims, m_dims = (M_HEADS // 2, LANES, 2 * LANES), (M_HEADS, LANES)
        row_w = (M_V_W, D_MODEL, D_MODEL)
        out_shape = ([grp_shape(d) for d in dils] * 3 + [sds((m_rows, w), BF16) for w in row_w]
                     + [sds((nb,) + st_dims, F32), sds((nb,) + m_dims, F32)])
        out_specs = ([grp_spec(d) for d in dils] * 3 + [row_spec(w) for w in row_w]
                     + [state_spec(*st_dims), state_spec(*m_dims)])
        scratch += [pltpu.VMEM(st_dims, F32), pltpu.VMEM(m_dims, F32)]
    in_specs = [row_spec(d_model), _const_spec((1, d_model)), _const_spec(w_perm.shape),
                _const_spec(w_gif.shape), _const_spec((1, LANES)), _const_spec((1, GROUP_W)),
                _const_spec((1, GROUP_W)), tab_spec, tab_spec, tab_spec, _const_spec(gmat.shape)]
    return pl.pallas_call(
        functools.partial(_proj_kernel, tm=tm, dils=dils, plain=plain,
                          tiles_per_seq=tiles_per_seq),
        grid=grid, in_specs=in_specs, out_specs=out_specs, out_shape=out_shape,
        scratch_shapes=scratch,
        compiler_params=pltpu.CompilerParams(dimension_semantics=("arbitrary",),
                                             vmem_limit_bytes=VMEM_LIMIT),
        name="proj",
    )(x2d, g1, w_perm, w_gif, bif, qg, kg, *rope_tabs, gmat)


ATT_STEP = SPAN * max(DILATIONS)
ATT_UNROLL = 16


def _attn_kernel(q0, k0, v0, kh0, vh0, q1, k1, v1, kh1, vh1, q2, k2, v2, kh2, vh2,
                 *rest, n_cast):
    w_f32, o_ref, w_bf16 = rest[:n_cast], rest[n_cast], rest[n_cast + 1:2 * n_cast + 1]
    acc_s, m_s, l_s = rest[2 * n_cast + 1:]
    j = pl.program_id(1)
    for src, dst in zip(w_f32, w_bf16):
        dst[...] = src[...].astype(dst.dtype)

    def window(cur, halo, r, bi):
        here = cur[r, pl.ds(pl.multiple_of(bi * SPAN, SPAN), SPAN), :]
        if cur.shape[1] == SPAN:
            prev = halo[r]
        else:
            before = cur[r, pl.ds(pl.multiple_of(jnp.maximum(bi - 1, 0) * SPAN, SPAN), SPAN), :]
            prev = jnp.where(bi == 0, halo[r], before)
        return jnp.concatenate([prev, here], axis=0)

    qi = lax.broadcasted_iota(jnp.int32, (SPAN, 2 * SPAN), 0)
    ci = lax.broadcasted_iota(jnp.int32, (SPAN, 2 * SPAN), 1)
    band = (ci >= qi) & (ci <= qi + SPAN)
    bias_band = jnp.where(band, 0.0, NEG).astype(F32)
    bias_first = jnp.where(band & (ci >= SPAN), 0.0, NEG).astype(F32)
    lane_q = lax.broadcasted_iota(jnp.int32, (SPAN, LANES), 1)
    lane_kv = lax.broadcasted_iota(jnp.int32, (2 * SPAN, LANES), 1)
    halves_q = (lane_q < HEAD_DIM, lane_q >= HEAD_DIM)
    halves_kv = (lane_kv < HEAD_DIM, lane_kv >= HEAD_DIM)

    def blocks(g, q_ref, kv_refs, units):
        k_cur, k_halo, v_cur, v_halo = kv_refs
        loaded = []
        for r, bi, _ in units:
            row0 = pl.multiple_of(bi * SPAN, SPAN)
            q2_ = q_ref[r, pl.ds(row0, SPAN), :]
            kk = window(k_cur, k_halo, r, bi)
            first = jnp.logical_and(j == 0, bi == 0)
            bias = jnp.where(first, bias_first, bias_band)
            ss = []
            for hh in range(2):
                qa = jnp.where(halves_q[hh], q2_, jnp.zeros_like(q2_))
                ss.append(lax.dot_general(qa, kk, (((1,), (1,)), ((), ())),
                                          preferred_element_type=F32) + bias)
            loaded.append(ss)
        probs = []
        for ss in loaded:
            mxs = [jnp.max(s, axis=-1, keepdims=True) for s in ss]
            ps = [jnp.exp2(s - mx) for s, mx in zip(ss, mxs)]
            ls = [jnp.sum(p, axis=-1, keepdims=True) for p in ps]
            probs.append(([p.astype(BF16) for p in ps], mxs, ls))
        for (r, bi, sl), (ps, mxs, ls) in zip(units, probs):
            vv = window(v_cur, v_halo, r, bi)
            acc = None
            for hh in range(2):
                vh = jnp.where(halves_kv[hh], vv, jnp.zeros_like(vv))
                a = jnp.dot(ps[hh], vh, preferred_element_type=F32)
                acc = a if acc is None else acc + a
            acc_s[g, sl, :] = acc
            m_s[g, sl, :] = jnp.where(halves_q[0], mxs[0], mxs[1])
            l_s[g, sl, :] = jnp.where(halves_q[0], ls[0], ls[1])

    d0, d1, d2 = DILATIONS
    nb0, nb1, nb2 = (ATT_STEP // d // SPAN for d in DILATIONS)
    un = ATT_UNROLL

    def body0(it, c):
        units = []
        for u in range(un):
            bi = it * un + u
            units.append((0, bi, pl.ds(pl.multiple_of(bi * SPAN, SPAN), SPAN)))
        blocks(0, q0, (k0, kh0, v0, vh0), units)
        return c
    lax.fori_loop(0, nb0 // un, body0, 0)

    def body1(it, c):
        units = []
        for u in range(un):
            idx = it * un + u
            r, bi = idx // nb1, idx % nb1
            units.append((r, bi, pl.ds(bi * SPAN * d1 + r, SPAN, stride=d1)))
        blocks(1, q1, (k1, kh1, v1, vh1), units)
        return c
    lax.fori_loop(0, d1 * nb1 // un, body1, 0)

    def body2(it, c):
        units = []
        for u in range(un):
            r = it * un + u
            units.append((r, 0, pl.ds(r, SPAN, stride=d2)))
        blocks(2, q2, (k2, kh2, v2, vh2), units)
        return c
    lax.fori_loop(0, d2 * nb2 // un, body2, 0)

    def combine(ci_, c):
        sl = pl.ds(pl.multiple_of(ci_ * SPAN, SPAN), SPAN)
        ms = [m_s[g, sl, :] for g in range(N_GROUPS)]
        mm = jnp.maximum(jnp.maximum(ms[0], ms[1]), ms[2])
        es = [jnp.exp2(m - mm) for m in ms]
        num = es[0] * acc_s[0, sl, :] + es[1] * acc_s[1, sl, :] + es[2] * acc_s[2, sl, :]
        den = es[0] * l_s[0, sl, :] + es[1] * l_s[1, sl, :] + es[2] * l_s[2, sl, :]
        o_ref[sl, :] = (num / den).astype(o_ref.dtype)
        return c
    lax.fori_loop(0, ATT_STEP // SPAN, combine, 0)


def _attention(qs, ks, vs, seq_len, cast_ws):
    nb = qs[0].shape[0]
    steps = seq_len // ATT_STEP
    in_specs, args, scratch = [], [], []
    for g, d in enumerate(DILATIONS):
        rows = ATT_STEP // d
        cur = pl.BlockSpec((None, d, None, rows, LANES), lambda b, j, p: (b, 0, p, j, 0))
        ratio = rows // SPAN
        halo = pl.BlockSpec((None, d, None, SPAN, LANES),
                            lambda b, j, p, ratio=ratio: (b, 0, p, jnp.maximum(j * ratio - 1, 0), 0))
        in_specs += [cur, cur, cur, halo, halo]
        args += [qs[g], ks[g], vs[g], ks[g], vs[g]]
    scratch += [pltpu.VMEM((N_GROUPS, ATT_STEP, LANES), F32)] * 3
    n_slices = max(nb * steps * PAIRS // 2, 1)
    slice_of = lambda b, j, p: (((b * steps + j) * PAIRS + p) // 2, 0)
    w_specs = []
    for w in cast_ws:
        assert w.shape[0] % (16 * n_slices) == 0, (w.shape, n_slices)
        w_specs.append(pl.BlockSpec((w.shape[0] // n_slices, w.shape[1]), slice_of))
    outs = pl.pallas_call(
        functools.partial(_attn_kernel, n_cast=len(cast_ws)),
        grid=(nb, steps, PAIRS),
        in_specs=in_specs + w_specs,
        out_specs=[pl.BlockSpec((None, None, ATT_STEP, LANES), lambda b, j, p: (b, p, j, 0))]
                  + w_specs,
        out_shape=[jax.ShapeDtypeStruct((nb, PAIRS, seq_len, LANES), BF16)]
                  + [jax.ShapeDtypeStruct(w.shape, BF16) for w in cast_ws],
        scratch_shapes=scratch,
        compiler_params=pltpu.CompilerParams(
            dimension_semantics=("arbitrary", "arbitrary", "arbitrary"),
            vmem_limit_bytes=VMEM_LIMIT),
        name="attn",
    )(*args, *cast_ws)
    return outs[0], outs[1:]


def _kv_tail_kernel(k0, v0, k1, v1, k2, v2, o0, o1, o2, nat_ref):
    for (k_ref, v_ref, o_ref, d) in ((k0, v0, o0, DILATIONS[0]), (k1, v1, o1, DILATIONS[1]),
                                     (k2, v2, o2, DILATIONS[2])):
        for kvi, ref in enumerate((k_ref, v_ref)):
            for p in range(PAIRS):
                rows = slice(p * LANES, (p + 1) * LANES)
                if d == 1:
                    o_ref[kvi, rows, :] = ref[0, p].astype(F32).T
                    continue
                for r in range(d):
                    nat_ref[pl.ds(r, SPAN, stride=d), :] = ref[r, p].astype(F32)
                for c in range(d):
                    o_ref[kvi, rows, c * SPAN:(c + 1) * SPAN] = nat_ref[c * SPAN:(c + 1) * SPAN, :].T


def _kv_tails(ks, vs, seq_len):
    nb = ks[0].shape[0]
    in_specs, args, out_specs, out_shape = [], [], [], []
    for g, d in enumerate(DILATIONS):
        last = seq_len // d // SPAN - 1
        spec = pl.BlockSpec((None, d, PAIRS, SPAN, LANES), lambda b, last=last: (b, 0, 0, last, 0))
        in_specs += [spec, spec]
        args += [ks[g], vs[g]]
        out_specs.append(pl.BlockSpec((None, 2, GROUP_W, SPAN * d), lambda b: (b, 0, 0, 0)))
        out_shape.append(jax.ShapeDtypeStruct((nb, 2, GROUP_W, SPAN * d), F32))
    return pl.pallas_call(
        _kv_tail_kernel,
        grid=(nb,), in_specs=in_specs, out_specs=out_specs, out_shape=out_shape,
        scratch_shapes=[pltpu.VMEM((SPAN * max(DILATIONS), LANES), F32)],
        compiler_params=pltpu.CompilerParams(dimension_semantics=("arbitrary",),
                                             vmem_limit_bytes=VMEM_LIMIT),
        name="kv_tail",
    )(*args)


def _mlstm_chunk(q2d, k2d, v2d, om2d, gif, gt, st_s, m_s, hm_ref, lc):
    lf_c = _log_sigmoid(gif)
    lf_r = _log_sigmoid(gt[M_HEADS:2 * M_HEADS, :])
    row = lax.broadcasted_iota(jnp.int32, (lc, lc), 0)
    col = lax.broadcasted_iota(jnp.int32, (lc, lc), 1)
    causal = col <= row
    tril = causal.astype(F32)
    triu = (row <= col).astype(F32)
    b_c = jnp.dot(tril, lf_c, precision=lax.Precision.HIGHEST, preferred_element_type=F32)
    b_r = jnp.dot(lf_r, triu, precision=lax.Precision.HIGHEST, preferred_element_type=F32)
    lane = lax.broadcasted_iota(jnp.int32, (lc, LANES), 1)
    halves = (lane < M_DQK, lane >= M_DQK)
    sub = lax.broadcasted_iota(jnp.int32, (LANES, 1), 0)
    ones_blk = jnp.ones((lc, LANES), BF16)
    qscale = M_DQK ** -0.5

    states, p1, p2 = {}, {}, {}
    all_pairs = tuple(range(M_HEADS // 2))
    heads_of = lambda pairs: [2 * p + hh for p in pairs for hh in range(2)]

    def phase1(pairs=all_pairs):
        for p in pairs:
            states[p] = st_s[p]
        for h in heads_of(pairs):
            p, hh = divmod(h, 2)
            q2_ = q2d[:, p * LANES:(p + 1) * LANES]
            k2_ = k2d[:, p * LANES:(p + 1) * LANES]
            qa = jnp.where(halves[hh], q2_, jnp.zeros_like(q2_)) * jnp.asarray(qscale, q2_.dtype)
            s = lax.dot_general(qa, k2_, (((1,), (1,)), ((), ())), preferred_element_type=F32)
            qc = jnp.dot(qa, states[p].astype(BF16), preferred_element_type=F32)
            p1[h] = (s, qc)

    def phase2(pairs=all_pairs):
        for h in heads_of(pairs):
            p, hh = divmod(h, 2)
            s, qc = p1[h]
            bcol = b_c[:, M_HEADS + h:M_HEADS + h + 1]
            igcol = gif[:, h:h + 1]
            a_row = gt[h:h + 1, :] - b_r[h:h + 1, :]
            amat = jnp.where(causal, a_row, NEG)
            m_prev = m_s[h:h + 1, 0:1]
            g_t = jnp.maximum(m_prev, jnp.max(amat, axis=-1, keepdims=True))
            wqk = (jnp.exp(amat - g_t) * s).astype(BF16)
            dec = jnp.exp(m_prev - g_t)
            m_t = bcol + g_t
            m_new = m_t[lc - 1:lc, :]
            b_last = bcol[lc - 1:lc, :]
            dstate = jnp.exp(b_last + m_prev - m_new)
            ws = jnp.exp(b_last - bcol + igcol - m_new)
            k2_ = k2d[:, p * LANES:(p + 1) * LANES]
            ka = jnp.where(halves[hh], k2_, jnp.zeros_like(k2_))
            kws = (ka.astype(F32) * ws).astype(BF16)
            m_s[h:h + 1, :] = jnp.broadcast_to(m_new, (1, LANES))
            p2[h] = (wqk, dec * qc, jnp.exp(-m_t), dstate, kws)

    def phase3(pairs=all_pairs):
        upds = {}
        for h in heads_of(pairs):
            wqk, dqc, floor, _, kws = p2[h]
            v1 = jnp.concatenate([v2d[:, h * M_DV:(h + 1) * M_DV], ones_blk], axis=1)
            num_den = dqc + jnp.dot(wqk, v1, preferred_element_type=F32)
            num = num_den[:, :M_DV]
            den = num_den[:, M_DV:]
            hval = num / jnp.maximum(jnp.abs(den), floor)
            gate = jax.nn.sigmoid(om2d[:, h * M_DV:(h + 1) * M_DV].astype(F32))
            hm_ref[:, h * M_DV:(h + 1) * M_DV] = (gate * hval).astype(hm_ref.dtype)
            upds[h] = lax.dot_general(kws, v1, (((0,), (0,)), ((), ())),
                                      preferred_element_type=F32)
        for p in pairs:
            drow = jnp.where(sub < M_DQK, p2[2 * p][3], p2[2 * p + 1][3])
            st_s[p] = drow * states[p] + upds[2 * p] + upds[2 * p + 1]

    return phase1, phase2, phase3


FF_CHUNK = 256


DEC_HEADS = HEADS // 2
DEC_ROWS = DEC_HEADS * HEAD_DIM


def _decode_attn_step(b, qkv_ref, c_refs, o_refs, ot_ref):
    sel = lax.broadcasted_iota(jnp.int32, (DEC_ROWS, LANES), 1) == b
    scale = HEAD_DIM ** -0.5

    def column(i):
        return jnp.sum(jnp.where(sel, qkv_ref[i], 0.0), axis=1, keepdims=True)

    def head(col, h):
        return col[h * HEAD_DIM:(h + 1) * HEAD_DIM, :]

    qc = [column(g) * scale for g in range(N_GROUPS)]
    kn = [column(N_GROUPS + g) for g in range(N_GROUPS)]
    vn = [column(2 * N_GROUPS + g) for g in range(N_GROUPS)]
    cols = []
    for h in range(DEC_HEADS):
        scores, m_h = [], None
        for g, d in enumerate(DILATIONS):
            wb = c_refs[g].shape[-1]
            qh = head(qc[g], h)
            s = jnp.sum(c_refs[g][0, h] * qh, axis=0, keepdims=True)
            pos = lax.broadcasted_iota(jnp.int32, (1, wb), 1)
            s = jnp.where((pos & (d - 1)) == 0, s, NEG)
            s_new = jnp.sum(head(kn[g], h) * qh, axis=0, keepdims=True)
            m_g = jnp.maximum(jnp.max(s, axis=1, keepdims=True), s_new)
            m_h = m_g if m_h is None else jnp.maximum(m_h, m_g)
            scores.append((s, s_new))
        l_h = jnp.zeros((1, 1), F32)
        acc = jnp.zeros((HEAD_DIM, 1), F32)
        for g in range(N_GROUPS):
            s, s_new = scores[g]
            p = jnp.exp(s - m_h)
            p_new = jnp.exp(s_new - m_h)
            l_h = l_h + jnp.sum(p, axis=1, keepdims=True) + p_new
            acc = acc + jnp.sum(c_refs[g][1, h] * p, axis=1, keepdims=True) + p_new * head(vn[g], h)
        cols.append(acc / l_h)
    ot_ref[...] = jnp.broadcast_to(jnp.concatenate(cols, axis=0), ot_ref.shape)
    for g in range(N_GROUPS):
        wb = c_refs[g].shape[-1]
        last = lax.broadcasted_iota(jnp.int32, (HEAD_DIM, wb), 1) == wb - 1
        for kvi, new in ((0, kn[g]), (1, vn[g])):
            for h in range(DEC_HEADS):
                rolled = pltpu.roll(c_refs[g][kvi, h], wb - 1, 1)
                o_refs[g][kvi, h] = jnp.where(last, head(new, h), rolled)


def _merge_ffn_kernel(x_ref, o_ref, hm_ref, ga_ref, gb_ref, wa_ref, wm_ref, wo_ref, g2_ref,
                      wg_ref, wu_ref, wd_ref, *rest, with_decode):
    if with_decode:
        qkv_ref, c0, c1, c2, y_ref, o0, o1, o2, ot_ref = rest
        _decode_attn_step(pl.program_id(0) // 2, qkv_ref, (c0, c1, c2), (o0, o1, o2), ot_ref)
    else:
        (y_ref,) = rest
    yb = jnp.dot(hm_ref[...].astype(BF16), wm_ref[...], preferred_element_type=F32)
    o_att = jnp.concatenate([o_ref[p] for p in range(PAIRS)], axis=1).astype(BF16)
    ya = jnp.dot(o_att, wa_ref[...], preferred_element_type=F32)
    mixed = (jax.nn.sigmoid(ga_ref[...].astype(F32)) * ya
             + jax.nn.sigmoid(gb_ref[...].astype(F32)) * yb)
    x2 = x_ref[...] + jnp.dot(mixed.astype(BF16), wo_ref[...], preferred_element_type=F32)
    h2 = (x2 * lax.rsqrt(jnp.mean(x2 * x2, axis=-1, keepdims=True) + NORM_EPS)
          * g2_ref[...]).astype(BF16)
    acc = x2
    d_ff = wg_ref.shape[1]

    def gate_up(c):
        gt = jnp.dot(h2, wg_ref[:, c:c + FF_CHUNK], preferred_element_type=F32)
        up = jnp.dot(h2, wu_ref[:, c:c + FF_CHUNK], preferred_element_type=F32)
        return gt, up

    chunks = list(range(0, d_ff, FF_CHUNK))
    nxt = gate_up(chunks[0])
    for i, c in enumerate(chunks):
        gt, up = nxt
        if i + 1 < len(chunks):
            nxt = gate_up(chunks[i + 1])
        ff = (gt * jax.nn.sigmoid(gt) * up).astype(BF16)
        acc = acc + jnp.dot(ff, wd_ref[c:c + FF_CHUNK, :], preferred_element_type=F32)
    y_ref[...] = acc


def _merge_ffn(x2d, o_att, hm, ga, gb, wa, wm, wo, g2, wg, wu, wd, seq_len, tm, decode=None):
    m_rows, d_model = x2d.shape
    tiles_per_seq = seq_len // tm
    steps = m_rows // tm
    row = lambda w: pl.BlockSpec((tm, w), lambda i: (i, 0))
    o_spec = pl.BlockSpec((None, PAIRS, tm, LANES),
                          lambda i: (i // tiles_per_seq, 0, i % tiles_per_seq, 0))
    in_specs = [row(d_model), o_spec, row(M_V_W), row(D_MODEL), row(D_MODEL),
                _const_spec(wa.shape), _const_spec(wm.shape), _const_spec(wo.shape),
                _const_spec((1, d_model)), _const_spec(wg.shape), _const_spec(wu.shape),
                _const_spec(wd.shape)]
    args = [x2d, o_att, hm, ga, gb, wa, wm, wo, g2, wg, wu, wd]
    out_specs = [row(d_model)]
    out_shape = [jax.ShapeDtypeStruct((m_rows, d_model), F32)]
    if decode is not None:
        qkv_t, caches_t = decode
        db = caches_t[0].shape[0]
        assert steps == 2 * db, (steps, db)
        in_specs.append(pl.BlockSpec((qkv_t.shape[0], DEC_ROWS, LANES), lambda i: (0, i % 2, 0)))
        args.append(qkv_t)
        for c in caches_t:
            spec = pl.BlockSpec((None, 2, DEC_HEADS, HEAD_DIM, c.shape[-1]),
                                lambda i: (i // 2, 0, i % 2, 0, 0))
            in_specs.append(spec)
            args.append(c)
            out_specs.append(spec)
            out_shape.append(jax.ShapeDtypeStruct(c.shape, F32))
        out_specs.append(pl.BlockSpec((None, None, DEC_ROWS, LANES), lambda i: (i // 2, i % 2, 0, 0)))
        out_shape.append(jax.ShapeDtypeStruct((db, 2, DEC_ROWS, LANES), F32))
    outs = pl.pallas_call(
        functools.partial(_merge_ffn_kernel, with_decode=decode is not None),
        grid=(steps,),
        in_specs=in_specs, out_specs=out_specs, out_shape=out_shape,
        compiler_params=pltpu.CompilerParams(dimension_semantics=("arbitrary",),
                                             vmem_limit_bytes=VMEM_LIMIT),
        name="merge_ffn",
    )(*args)
    if decode is None:
        return outs[0]
    return outs[0], outs[1:4], outs[4]


def _sample_prep_kernel(*refs):
    for i, ref in enumerate(refs[:-1]):
        refs[-1][i] = ref[...].T


def _sample_prep(arrs):
    vm = pl.BlockSpec(memory_space=pltpu.VMEM)
    return pl.pallas_call(
        _sample_prep_kernel, in_specs=[vm] * len(arrs), out_specs=vm,
        out_shape=jax.ShapeDtypeStruct((len(arrs), GROUP_W, LANES), F32),
        name="sample_prep",
    )(*arrs)


def _sample_mlstm_kernel(q_ref, k_ref, v_ref, om_ref, gt_ref, mt_ref, n_ref, c_ref, rexp_ref,
                         hm_ref, co_ref, nt_ref, mo_ref, *, db):
    nh, dqk = M_HEADS, M_DQK
    hi = lax.Precision.HIGHEST
    rexp = rexp_ref[...]
    q_t = q_ref[...].T * (dqk ** -0.5)
    k_t = k_ref[...].T
    n_t = n_ref[...].T
    ig = gt_ref[0:nh, :]
    lf = _log_sigmoid(gt_ref[nh:2 * nh, :])
    m_prev = mt_ref[...]
    m_new = jnp.maximum(lf + m_prev, ig)
    w8 = jnp.exp(ig - m_new)
    dec8 = jnp.exp(lf + m_prev - m_new)
    head_sum = lambda a: lax.dot_general(rexp, a, (((0,), (0,)), ((), ())), precision=hi,
                                         preferred_element_type=F32)
    expand = lambda a: jnp.dot(rexp, a, precision=hi, preferred_element_type=F32)
    qk8 = head_sum(q_t * k_t)
    qn8 = head_sum(q_t * n_t)
    wqk8 = w8 * qk8
    den8 = dec8 * qn8 + wqk8
    inv8 = 1.0 / jnp.maximum(jnp.abs(den8), jnp.exp(-m_new))
    dec_x = expand(dec8)
    w_x = expand(w8)
    nt_ref[...] = dec_x * n_t + w_x * k_t
    mo_ref[...] = m_new
    wk_x = w_x * k_t
    for b in range(db):
        cb = c_ref[b].reshape(nh * dqk, M_DV)
        v_b = v_ref[b]
        v_x = jnp.concatenate([jnp.broadcast_to(v_b[h:h + 1, :], (dqk, M_DV)) for h in range(nh)],
                              axis=0)
        co_ref[b] = (dec_x[:, b:b + 1] * cb + wk_x[:, b:b + 1] * v_x).reshape(nh, dqk, M_DV)
        qc = jnp.sum((q_t[:, b:b + 1] * cb).reshape(nh, dqk, M_DV), axis=1)
        num = dec8[:, b:b + 1] * qc + wqk8[:, b:b + 1] * v_b
        hm_ref[b] = jax.nn.sigmoid(om_ref[b]) * (num * inv8[:, b:b + 1])


def _sample_mlstm(q_t, k_t, v3, om3, gif_t, m_t, n_pad, c_state, rexp, db):
    vm = pl.BlockSpec(memory_space=pltpu.VMEM)
    return pl.pallas_call(
        functools.partial(_sample_mlstm_kernel, db=db),
        in_specs=[vm] * 9,
        out_specs=[vm] * 4,
        out_shape=[jax.ShapeDtypeStruct((db, M_HEADS, M_DV), F32),
                   jax.ShapeDtypeStruct(c_state.shape, F32),
                   jax.ShapeDtypeStruct((M_HEADS * M_DQK, LANES), F32),
                   jax.ShapeDtypeStruct((M_HEADS, LANES), F32)],
        compiler_params=pltpu.CompilerParams(vmem_limit_bytes=VMEM_LIMIT),
        name="sample_mlstm",
    )(q_t, k_t, v3, om3, gif_t, m_t, n_pad, c_state, rexp)


def _rope_tables(pos):
    half = ROT_DIM // 2
    pos = np.asarray(pos, np.float32)
    inv_freq = np.exp(np.float32(-math.log(ROPE_THETA))
                      * np.arange(0, ROT_DIM, 2, dtype=np.float32) / np.float32(ROT_DIM))
    ang = (pos[:, None] * inv_freq[None, :]).astype(np.float32)
    cos, sin = np.cos(ang).astype(np.float32), np.sin(ang).astype(np.float32)
    t = pos.shape[0]
    rest = HEAD_DIM - ROT_DIM
    a = np.concatenate([cos, cos, np.ones((t, rest), np.float32)], axis=1)
    bm = np.concatenate([-sin, np.zeros((t, HEAD_DIM - half), np.float32)], axis=1)
    bp = np.concatenate([np.zeros((t, half), np.float32), sin, np.zeros((t, rest), np.float32)],
                        axis=1)
    return tuple(jnp.asarray(np.concatenate([x, x], axis=1)) for x in (a, bm, bp))


W_PREP_BLK = 512


def _w_prep_kernel(wt_ref, o_ref):
    o_ref[...] = wt_ref[...].T.astype(o_ref.dtype)


def _w_prep(w_in):
    d_model = w_in.shape[0]
    wt = w_in.T
    n_head = _C_GA // W_PREP_BLK
    n_blk = _W_COLS // W_PREP_BLK

    sub = 8

    def src_row(j):
        return (j * (W_PREP_BLK // sub) + jnp.where(j < n_head, 0, _GIF_COLS // sub)) * sub

    return pl.pallas_call(
        _w_prep_kernel,
        grid=(n_blk,),
        in_specs=[pl.BlockSpec((pl.Element(W_PREP_BLK), pl.Element(d_model)),
                               lambda j: (src_row(j), 0))],
        out_specs=pl.BlockSpec((d_model, W_PREP_BLK), lambda j: (0, j)),
        out_shape=jax.ShapeDtypeStruct((d_model, _W_COLS), BF16),
        compiler_params=pltpu.CompilerParams(dimension_semantics=("arbitrary",),
                                             vmem_limit_bytes=VMEM_LIMIT),
        name="w_prep",
    )(wt)


def kernel(x_prompt, x_sample, cache_kv_w128, cache_kv_w512, cache_kv_w2048, state_mlstm_C, state_mlstm_n, state_mlstm_m, norm1_g, w_in, b_if, q_norm_g, k_norm_g, w_att_out, w_m_out, w_o, norm2_g, w_gate, w_up, w_down):
    nb, seq_len, d_model = x_prompt.shape
    db, dec_seq, _ = x_sample.shape
    assert dec_seq == 1 and d_model == D_MODEL and seq_len % ATT_STEP == 0 and db <= LANES
    caches = (cache_kv_w128, cache_kv_w512, cache_kv_w2048)

    w_perm = _w_prep(w_in)
    w_gif = jnp.zeros((d_model, LANES), BF16).at[:, :_GIF_COLS].set(
        w_in[:, _C_GA:_C_GA + _GIF_COLS].astype(BF16))
    g1 = norm1_g.reshape(1, d_model)
    g2 = norm2_g.reshape(1, d_model)
    bif = jnp.concatenate([b_if, jnp.zeros((LANES - b_if.shape[0],), F32)]).reshape(1, LANES)
    qg = jnp.tile(q_norm_g, HEADS).reshape(1, GROUP_W)
    kg = jnp.tile(k_norm_g, HEADS).reshape(1, GROUP_W)
    hid = np.arange(GROUP_W // 2) // HEAD_DIM
    gmat = jnp.asarray(hid[:, None] == hid[None, :], dtype=BF16)

    m_rows = nb * seq_len
    x2d = x_prompt.reshape(m_rows, d_model)
    tabs_p = _rope_tables(np.arange(seq_len))
    qg_p = qg * (HEAD_DIM ** -0.5 * math.log2(math.e))
    outs = _proj(x2d, seq_len, DILATIONS, 256, False, g1, w_perm, w_gif, bif, qg_p, kg, tabs_p, gmat)
    qs, ks, vs = outs[0:3], outs[3:6], outs[6:9]
    hm, ga, gb, st_p, m_p = outs[9:14]

    o_att, (wa, wm, wo, wg, wu, wd) = _attention(
        qs, ks, vs, seq_len, (w_att_out, w_m_out, w_o, w_gate, w_up, w_down))

    x_s = jnp.zeros((LANES, d_model), F32).at[:db].set(x_sample.reshape(db, d_model))
    tabs_s = _rope_tables(np.full((LANES,), PAST_LEN))
    outs_s = _proj(x_s, LANES, (1, 1, 1), LANES, True, g1, w_perm, w_gif, bif, qg, kg, tabs_s, gmat)
    qkv_t = _sample_prep(outs_s[0:9])
    caches_t = [c.transpose(0, 2, 3, 4, 1) for c in caches]
    y_prompt, kv_st, o_att_cols = _merge_ffn(x2d, o_att, hm, ga, gb, wa, wm, wo, g2, wg, wu, wd,
                                             seq_len, 256, decode=(qkv_t, caches_t))
    y_prompt = y_prompt.reshape(nb, seq_len, d_model)
    kv_s = [c.transpose(0, 4, 1, 2, 3) for c in kv_st]

    tails = _kv_tails(ks, vs, seq_len)
    kv_p = [t.reshape(nb, 2, HEADS, HEAD_DIM, t.shape[-1]).transpose(0, 4, 1, 2, 3) for t in tails]
    c_p = st_p[..., :M_DV].reshape(nb, M_HEADS, M_DQK, M_DV)
    n_p = st_p[..., M_DV].reshape(nb, M_HEADS, M_DQK)
    m_pr = m_p[:, :, 0]

    qm_t, km_t, vm_s, om_s, ga_s, gb_s, gif_s = outs_s[9:16]
    gif_ts = gif_s.T

    rexp = jnp.asarray(np.arange(M_HEADS * M_DQK)[:, None] // M_DQK == np.arange(M_HEADS)[None, :],
                       dtype=F32)
    m_t = jnp.zeros((M_HEADS, LANES), F32).at[:, :db].set(state_mlstm_m.T)
    n_pad = jnp.zeros((LANES, M_HEADS * M_DQK), F32).at[:db].set(state_mlstm_n.reshape(db, -1))
    hm_s3, c_s, n_t, m_so = _sample_mlstm(
        qm_t, km_t, vm_s[:db].reshape(db, M_HEADS, M_DV), om_s[:db].reshape(db, M_HEADS, M_DV),
        gif_ts[:2 * M_HEADS], m_t, n_pad, state_mlstm_C, rexp, db)
    n_s = n_t.T[:db].reshape(db, M_HEADS, M_DQK)
    m_s = m_so[:, :db].T

    o_att_s = o_att_cols[:, :, :, 0].reshape(db, PAIRS, LANES)
    o_att_sp = jnp.zeros((PAIRS, LANES, LANES), F32).at[:, :db].set(o_att_s.transpose(1, 0, 2))[None]
    hm_sp = jnp.zeros((LANES, M_V_W), F32).at[:db].set(hm_s3.reshape(db, M_V_W))
    y_s = _merge_ffn(x_s, o_att_sp, hm_sp, ga_s, gb_s, wa, wm, wo, g2, wg, wu, wd, LANES, LANES)
    y_sample = y_s[:db].reshape(db, 1, d_model)

    return (y_prompt, y_sample, kv_p[0], kv_p[1], kv_p[2], c_p, n_p, m_pr,
            kv_s[0], kv_s[1], kv_s[2], c_s, n_s, m_s)
```

```python
import functools
import math

import jax
import jax.numpy as jnp
import numpy as np
from jax import lax
from jax.experimental import pallas as pl
from jax.experimental.pallas import tpu as pltpu

F32 = jnp.float32
BF16 = jnp.bfloat16

HEAD_DIM = 64
HEADS = 8
GROUP_W = HEADS * HEAD_DIM
N_GROUPS = 3
WINDOWS = (128, 512, 2048)
DILATIONS = (1, 4, 16)
SPAN = 128
ROT_DIM = 16
ROPE_THETA = 500000.0
M_HEADS = 8
M_DQK = 64
M_DV = 128
M_QK_W = M_HEADS * M_DQK
M_V_W = M_HEADS * M_DV
D_MODEL = 1024
PAST_LEN = 8192
NORM_EPS = 1e-6
NEG = -1e30

LANES = 128
PAIRS = GROUP_W // LANES
VMEM_LIMIT = 56 * 1024 * 1024

_ATT_W = N_GROUPS * GROUP_W
_C_QM = 3 * _ATT_W
_C_KM = _C_QM + M_HEADS * M_DQK
_C_VM = _C_KM + M_HEADS * M_DQK
_C_OM = _C_VM + M_HEADS * M_DV
_C_GA = _C_OM + M_HEADS * M_DV
_C_GB = _C_GA + 1024
_W_COLS = _C_GB + 1024
_GIF_COLS = 2 * M_HEADS


def _const_spec(shape):
    nd = len(shape)
    return pl.BlockSpec(shape, lambda *_: (0,) * nd, pipeline_mode=pl.Buffered(1))


def _log_sigmoid(x):
    return jnp.minimum(x, 0.0) - jnp.log1p(jnp.exp(-jnp.abs(x)))


def _proj_kernel(x_ref, g1_ref, w_ref, wgif_ref, bif_ref, qg_ref, kg_ref, ra_ref, rm_ref, rp_ref,
                 gm_ref,
                 q0_ref, q1_ref, q2_ref, k0_ref, k1_ref, k2_ref, v0_ref, v1_ref, v2_ref,
                 *rest, tm, dils, plain, tiles_per_seq):
    if plain:
        qm_ref, km_ref, vm_ref, om_ref, ga_ref, gb_ref, gif_ref, hs_ref = rest
    else:
        hm_ref, ga_ref, gb_ref, st_ref, mo_ref, hs_ref, st_s, m_s = rest
    d_model = x_ref.shape[1]
    x = x_ref[...]
    xn = x * lax.rsqrt(jnp.mean(x * x, axis=-1, keepdims=True) + NORM_EPS) * g1_ref[...]
    h_nat = xn.astype(BF16)
    n_slab = d_model // LANES
    if any(d > 1 for d in dils):
        for c in range(n_slab):
            hs_ref[c] = xn[:, c * LANES:(c + 1) * LANES]

    def permuted_h(d):
        if d == 1:
            return h_nat
        n = tm // d
        rows = [jnp.concatenate([hs_ref[c, pl.ds(r, n, stride=d), :] for c in range(n_slab)], axis=1)
                for r in range(d)]
        return jnp.concatenate(rows, axis=0).astype(BF16)

    def permuted_tab(ref, d):
        if d == 1:
            t = ref[...]
        else:
            n = tm // d
            t = jnp.concatenate([ref[pl.ds(r, n, stride=d), :] for r in range(d)], axis=0)
        return jnp.concatenate([t] * PAIRS, axis=1)

    gmat = gm_ref[...]

    def head_sumsq(z):
        zz = (z * z).astype(BF16)
        half = GROUP_W // 2
        return jnp.concatenate(
            [jnp.dot(zz[:, :half], gmat, preferred_element_type=F32),
             jnp.dot(zz[:, half:], gmat, preferred_element_type=F32)], axis=1)

    def norm_rope(z, ss, gain, ra, rm, rp):
        y = z * lax.rsqrt(ss * (1.0 / HEAD_DIM) + NORM_EPS) * gain
        return (y * ra + pltpu.roll(y, GROUP_W - ROT_DIM // 2, 1) * rm
                + pltpu.roll(y, ROT_DIM // 2, 1) * rp)

    def store_group(ref, y, d):
        if plain:
            ref[...] = y.astype(ref.dtype)
            return
        n = tm // d
        for p in range(PAIRS):
            ref[:, p] = y[:, p * LANES:(p + 1) * LANES].reshape(d, n, LANES).astype(ref.dtype)

    def seg(c0, width):
        return jnp.dot(h_nat, w_ref[:, c0:c0 + width], preferred_element_type=F32)

    if not plain:
        @pl.when(pl.program_id(0) % tiles_per_seq == 0)
        def _():
            st_s[...] = jnp.zeros_like(st_s)
            m_s[...] = jnp.zeros_like(m_s)

        gif = jnp.dot(h_nat, wgif_ref[...], preferred_element_type=F32) + bif_ref[...]
        m_phase1, m_phase2, m_phase3 = _mlstm_chunk(
            seg(_C_QM, M_QK_W).astype(BF16), seg(_C_KM, M_QK_W).astype(BF16),
            seg(_C_VM, M_V_W).astype(BF16), seg(_C_OM, M_V_W).astype(BF16),
            gif, gif.T[:2 * M_HEADS, :], st_s, m_s, hm_ref, tm)
        m_sets = [(0, 1), (2, 3)]

        def m_advance(i):
            if 0 < i <= len(m_sets):
                m_phase3(m_sets[i - 1])
            if i < len(m_sets):
                m_phase1(m_sets[i])
                m_phase2(m_sets[i])

        m_advance(0)

    q_refs = (q0_ref, q1_ref, q2_ref)
    k_refs = (k0_ref, k1_ref, k2_ref)
    v_refs = (v0_ref, v1_ref, v2_ref)
    for g in range(N_GROUPS):
        d = dils[g]
        hg = permuted_h(d)
        ra, rm, rp = (permuted_tab(r, d) for r in (ra_ref, rm_ref, rp_ref))
        cq, ck, cv = (t * _ATT_W + g * GROUP_W for t in range(3))
        zq = jnp.dot(hg, w_ref[:, cq:cq + GROUP_W], preferred_element_type=F32)
        zk = jnp.dot(hg, w_ref[:, ck:ck + GROUP_W], preferred_element_type=F32)
        zv = jnp.dot(hg, w_ref[:, cv:cv + GROUP_W], preferred_element_type=F32)
        ssq, ssk = head_sumsq(zq), head_sumsq(zk)
        store_group(v_refs[g], zv, d)
        store_group(q_refs[g], norm_rope(zq, ssq, qg_ref[...], ra, rm, rp), d)
        store_group(k_refs[g], norm_rope(zk, ssk, kg_ref[...], ra, rm, rp), d)

        if not plain:
            m_advance(g + 1)

    if plain:
        segments = ((qm_ref, _C_QM, M_QK_W), (km_ref, _C_KM, M_QK_W), (vm_ref, _C_VM, M_V_W),
                    (om_ref, _C_OM, M_V_W), (ga_ref, _C_GA, D_MODEL), (gb_ref, _C_GB, D_MODEL))
        zg = jnp.dot(h_nat, wgif_ref[...], preferred_element_type=F32)
        gif_ref[...] = zg + bif_ref[...]
    else:
        segments = ((ga_ref, _C_GA, D_MODEL), (gb_ref, _C_GB, D_MODEL))
    for si, (ref, c0, width) in enumerate(segments):
        for cc in range(0, width, GROUP_W):
            ref[:, cc:cc + GROUP_W] = seg(c0 + cc, GROUP_W).astype(ref.dtype)
        if not plain and si == 0:
            m_advance(N_GROUPS + 1)
            st_ref[...] = st_s[...]
            mo_ref[...] = m_s[...]


def _proj(x2d, seq_len, dils, tm, plain, g1, w_perm, w_gif, bif, qg, kg, rope_tabs, gmat):
    m_rows, d_model = x2d.shape
    nb = m_rows // seq_len
    tiles_per_seq = seq_len // tm
    grid = (m_rows // tm,)
    row_spec = lambda w: pl.BlockSpec((tm, w), lambda i: (i, 0))
    tab_spec = pl.BlockSpec((tm, LANES), lambda i: (i % tiles_per_seq, 0))
    sds = jax.ShapeDtypeStruct

    widths = (M_QK_W, M_QK_W, M_V_W, M_V_W, D_MODEL, D_MODEL)
    scratch = [pltpu.VMEM((d_model // LANES, tm, LANES), F32)]
    if plain:
        out_shape = ([sds((m_rows, GROUP_W), F32)] * 9 + [sds((m_rows, w), F32) for w in widths]
                     + [sds((m_rows, LANES), F32)])
        out_specs = ([row_spec(GROUP_W)] * 9 + [row_spec(w) for w in widths] + [row_spec(LANES)])
    else:
        def grp_spec(d):
            return pl.BlockSpec((None, d, PAIRS, tm // d, LANES),
                                lambda i: (i // tiles_per_seq, 0, 0, i % tiles_per_seq, 0))
        grp_shape = lambda d: sds((nb, d, PAIRS, seq_len // d, LANES), BF16)
        state_spec = lambda *dims: pl.BlockSpec((None,) + dims,
                                                lambda i: (i // tiles_per_seq,) + (0,) * len(dims))
        st_dims, m_dims = (M_HEADS // 2, LANES, 2 * LANES), (M_HEADS, LANES)
        row_w = (M_V_W, D_MODEL, D_MODEL)
        out_shape = ([grp_shape(d) for d in dils] * 3 + [sds((m_rows, w), BF16) for w in row_w]
                     + [sds((nb,) + st_dims, F32), sds((nb,) + m_dims, F32)])
        out_specs = ([grp_spec(d) for d in dils] * 3 + [row_spec(w) for w in row_w]
                     + [state_spec(*st_dims), state_spec(*m_dims)])
        scratch += [pltpu.VMEM(st_dims, F32), pltpu.VMEM(m_dims, F32)]
    in_specs = [row_spec(d_model), _const_spec((1, d_model)), _const_spec(w_perm.shape),
                _const_spec(w_gif.shape), _const_spec((1, LANES)), _const_spec((1, GROUP_W)),
                _const_spec((1, GROUP_W)), tab_spec, tab_spec, tab_spec, _const_spec(gmat.shape)]
    return pl.pallas_call(
        functools.partial(_proj_kernel, tm=tm, dils=dils, plain=plain,
                          tiles_per_seq=tiles_per_seq),
        grid=grid, in_specs=in_specs, out_specs=out_specs, out_shape=out_shape,
        scratch_shapes=scratch,
        compiler_params=pltpu.CompilerParams(dimension_semantics=("arbitrary",),
                                             vmem_limit_bytes=VMEM_LIMIT),
        name="proj",
    )(x2d, g1, w_perm, w_gif, bif, qg, kg, *rope_tabs, gmat)


ATT_STEP = SPAN * max(DILATIONS)
FAR_PITCH = SPAN + 8
ATT_UNROLL = 16


def _attn_kernel(q0, k0, v0, kh0, vh0, q1, k1, v1, kh1, vh1, q2, k2, v2, kh2, vh2,
                 *rest, n_cast):
    w_f32, o_ref, w_bf16 = rest[:n_cast], rest[n_cast], rest[n_cast + 1:2 * n_cast + 1]
    acc_s, m_s, l_s, far_s = rest[2 * n_cast + 1:]
    j = pl.program_id(1)
    for src, dst in zip(w_f32, w_bf16):
        dst[...] = src[...].astype(dst.dtype)

    def window(cur, halo, r, bi):
        here = cur[r, pl.ds(pl.multiple_of(bi * SPAN, SPAN), SPAN), :]
        if cur.shape[1] == SPAN:
            prev = halo[r]
        else:
            before = cur[r, pl.ds(pl.multiple_of(jnp.maximum(bi - 1, 0) * SPAN, SPAN), SPAN), :]
            prev = jnp.where(bi == 0, halo[r], before)
        return jnp.concatenate([prev, here], axis=0)

    qi = lax.broadcasted_iota(jnp.int32, (SPAN, 2 * SPAN), 0)
    ci = lax.broadcasted_iota(jnp.int32, (SPAN, 2 * SPAN), 1)
    band = (ci >= qi) & (ci <= qi + SPAN)
    bias_band = jnp.where(band, 0.0, NEG).astype(F32)
    bias_first = jnp.where(band & (ci >= SPAN), 0.0, NEG).astype(F32)
    lane_q = lax.broadcasted_iota(jnp.int32, (SPAN, LANES), 1)
    lane_kv = lax.broadcasted_iota(jnp.int32, (2 * SPAN, LANES), 1)
    halves_q = (lane_q < HEAD_DIM, lane_q >= HEAD_DIM)
    halves_kv = (lane_kv < HEAD_DIM, lane_kv >= HEAD_DIM)

    def blocks(g, q_ref, kv_refs, units):
        k_cur, k_halo, v_cur, v_halo = kv_refs
        loaded = []
        for r, bi, _ in units:
            row0 = pl.multiple_of(bi * SPAN, SPAN)
            q2_ = q_ref[r, pl.ds(row0, SPAN), :]
            kk = window(k_cur, k_halo, r, bi)
            first = jnp.logical_and(j == 0, bi == 0)
            bias = jnp.where(first, bias_first, bias_band)
            ss = []
            for hh in range(2):
                qa = jnp.where(halves_q[hh], q2_, jnp.zeros_like(q2_))
                ss.append(lax.dot_general(qa, kk, (((1,), (1,)), ((), ())),
                                          preferred_element_type=F32) + bias)
            loaded.append(ss)
        probs = []
        for ss in loaded:
            mxs = [jnp.max(s, axis=-1, keepdims=True) for s in ss]
            ps = [jnp.exp2(s - mx) for s, mx in zip(ss, mxs)]
            ls = [jnp.sum(p, axis=-1, keepdims=True) for p in ps]
            probs.append(([p.astype(BF16) for p in ps], mxs, ls))
        for (r, bi, sl), (ps, mxs, ls) in zip(units, probs):
            vv = window(v_cur, v_halo, r, bi)
            acc = None
            for hh in range(2):
                vh = jnp.where(halves_kv[hh], vv, jnp.zeros_like(vv))
                a = jnp.dot(ps[hh], vh, preferred_element_type=F32)
                acc = a if acc is None else acc + a
            stats = (acc, jnp.where(halves_q[0], mxs[0], mxs[1]),
                     jnp.where(halves_q[0], ls[0], ls[1]))
            if g == N_GROUPS - 1:
                for k, val in enumerate(stats):
                    far_s[k, sl, :] = val
            else:
                for ref, val in zip((acc_s, m_s, l_s), stats):
                    ref[g, sl, :] = val

    d0, d1, d2 = DILATIONS
    nb0, nb1, nb2 = (ATT_STEP // d // SPAN for d in DILATIONS)
    un = ATT_UNROLL

    def body0(it, c):
        units = []
        for u in range(un):
            bi = it * un + u
            units.append((0, bi, pl.ds(pl.multiple_of(bi * SPAN, SPAN), SPAN)))
        blocks(0, q0, (k0, kh0, v0, vh0), units)
        return c
    lax.fori_loop(0, nb0 // un, body0, 0)

    def body1(it, c):
        units = []
        for u in range(un):
            idx = it * un + u
            r, bi = idx // nb1, idx % nb1
            units.append((r, bi, pl.ds(bi * SPAN * d1 + r, SPAN, stride=d1)))
        blocks(1, q1, (k1, kh1, v1, vh1), units)
        return c
    lax.fori_loop(0, d1 * nb1 // un, body1, 0)

    def body2(it, c):
        units = []
        for u in range(un):
            r = it * un + u
            units.append((r, 0, pl.ds(pl.multiple_of(r * FAR_PITCH, 8), SPAN)))
        blocks(2, q2, (k2, kh2, v2, vh2), units)
        return c
    lax.fori_loop(0, d2 * nb2 // un, body2, 0)

    def far_rows(k, ci_):
        sub = 8
        pieces = [far_s[k, pl.ds(a * sub * FAR_PITCH + ci_ * (SPAN // d2) + jj, sub,
                                 stride=FAR_PITCH), :]
                  for jj in range(SPAN // d2) for a in range(d2 // sub)]
        return jnp.concatenate(pieces, axis=0)

    def combine(ci_, c):
        sl = pl.ds(pl.multiple_of(ci_ * SPAN, SPAN), SPAN)
        accs = [acc_s[0, sl, :], acc_s[1, sl, :], far_rows(0, ci_)]
        ms = [m_s[0, sl, :], m_s[1, sl, :], far_rows(1, ci_)]
        ls = [l_s[0, sl, :], l_s[1, sl, :], far_rows(2, ci_)]
        mm = jnp.maximum(jnp.maximum(ms[0], ms[1]), ms[2])
        es = [jnp.exp2(m - mm) for m in ms]
        num = es[0] * accs[0] + es[1] * accs[1] + es[2] * accs[2]
        den = es[0] * ls[0] + es[1] * ls[1] + es[2] * ls[2]
        o_ref[sl, :] = (num / den).astype(o_ref.dtype)
        return c
    lax.fori_loop(0, ATT_STEP // SPAN, combine, 0)


def _attention(qs, ks, vs, seq_len, cast_ws):
    nb = qs[0].shape[0]
    steps = seq_len // ATT_STEP
    in_specs, args, scratch = [], [], []
    for g, d in enumerate(DILATIONS):
        rows = ATT_STEP // d
        cur = pl.BlockSpec((None, d, None, rows, LANES), lambda b, j, p: (b, 0, p, j, 0))
        ratio = rows // SPAN
        halo = pl.BlockSpec((None, d, None, SPAN, LANES),
                            lambda b, j, p, ratio=ratio: (b, 0, p, jnp.maximum(j * ratio - 1, 0), 0))
        in_specs += [cur, cur, cur, halo, halo]
        args += [qs[g], ks[g], vs[g], ks[g], vs[g]]
    scratch += [pltpu.VMEM((N_GROUPS - 1, ATT_STEP, LANES), F32)] * 3
    scratch += [pltpu.VMEM((3, max(DILATIONS) * FAR_PITCH, LANES), F32)]
    n_slices = max(nb * steps * PAIRS // 2, 1)
    slice_of = lambda b, j, p: (((b * steps + j) * PAIRS + p) // 2, 0)
    w_specs = []
    for w in cast_ws:
        assert w.shape[0] % (16 * n_slices) == 0, (w.shape, n_slices)
        w_specs.append(pl.BlockSpec((w.shape[0] // n_slices, w.shape[1]), slice_of))
    outs = pl.pallas_call(
        functools.partial(_attn_kernel, n_cast=len(cast_ws)),
        grid=(nb, steps, PAIRS),
        in_specs=in_specs + w_specs,
        out_specs=[pl.BlockSpec((None, None, ATT_STEP, LANES), lambda b, j, p: (b, p, j, 0))]
                  + w_specs,
        out_shape=[jax.ShapeDtypeStruct((nb, PAIRS, seq_len, LANES), BF16)]
                  + [jax.ShapeDtypeStruct(w.shape, BF16) for w in cast_ws],
        scratch_shapes=scratch,
        compiler_params=pltpu.CompilerParams(
            dimension_semantics=("arbitrary", "arbitrary", "arbitrary"),
            vmem_limit_bytes=VMEM_LIMIT),
        name="attn",
    )(*args, *cast_ws)
    return outs[0], outs[1:]


def _kv_tail_kernel(k0, v0, k1, v1, k2, v2, o0, o1, o2, nat_ref):
    for (k_ref, v_ref, o_ref, d) in ((k0, v0, o0, DILATIONS[0]), (k1, v1, o1, DILATIONS[1]),
                                     (k2, v2, o2, DILATIONS[2])):
        for kvi, ref in enumerate((k_ref, v_ref)):
            for p in range(PAIRS):
                rows = slice(p * LANES, (p + 1) * LANES)
                if d == 1:
                    o_ref[kvi, rows, :] = ref[0, p].astype(F32).T
                    continue
                for r in range(d):
                    nat_ref[pl.ds(r, SPAN, stride=d), :] = ref[r, p].astype(F32)
                for c in range(d):
                    o_ref[kvi, rows, c * SPAN:(c + 1) * SPAN] = nat_ref[c * SPAN:(c + 1) * SPAN, :].T


def _kv_tails(ks, vs, seq_len):
    nb = ks[0].shape[0]
    in_specs, args, out_specs, out_shape = [], [], [], []
    for g, d in enumerate(DILATIONS):
        last = seq_len // d // SPAN - 1
        spec = pl.BlockSpec((None, d, PAIRS, SPAN, LANES), lambda b, last=last: (b, 0, 0, last, 0))
        in_specs += [spec, spec]
        args += [ks[g], vs[g]]
        out_specs.append(pl.BlockSpec((None, 2, GROUP_W, SPAN * d), lambda b: (b, 0, 0, 0)))
        out_shape.append(jax.ShapeDtypeStruct((nb, 2, GROUP_W, SPAN * d), F32))
    return pl.pallas_call(
        _kv_tail_kernel,
        grid=(nb,), in_specs=in_specs, out_specs=out_specs, out_shape=out_shape,
        scratch_shapes=[pltpu.VMEM((SPAN * max(DILATIONS), LANES), F32)],
        compiler_params=pltpu.CompilerParams(dimension_semantics=("arbitrary",),
                                             vmem_limit_bytes=VMEM_LIMIT),
        name="kv_tail",
    )(*args)


def _mlstm_chunk(q2d, k2d, v2d, om2d, gif, gt, st_s, m_s, hm_ref, lc):
    lf_c = _log_sigmoid(gif)
    lf_r = _log_sigmoid(gt[M_HEADS:2 * M_HEADS, :])
    row = lax.broadcasted_iota(jnp.int32, (lc, lc), 0)
    col = lax.broadcasted_iota(jnp.int32, (lc, lc), 1)
    causal = col <= row
    tril = causal.astype(F32)
    triu = (row <= col).astype(F32)
    b_c = jnp.dot(tril, lf_c, precision=lax.Precision.HIGHEST, preferred_element_type=F32)
    b_r = jnp.dot(lf_r, triu, precision=lax.Precision.HIGHEST, preferred_element_type=F32)
    lane = lax.broadcasted_iota(jnp.int32, (lc, LANES), 1)
    halves = (lane < M_DQK, lane >= M_DQK)
    sub = lax.broadcasted_iota(jnp.int32, (LANES, 1), 0)
    ones_blk = jnp.ones((lc, LANES), BF16)
    qscale = M_DQK ** -0.5

    states, p1, p2 = {}, {}, {}
    all_pairs = tuple(range(M_HEADS // 2))
    heads_of = lambda pairs: [2 * p + hh for p in pairs for hh in range(2)]

    def phase1(pairs=all_pairs):
        for p in pairs:
            states[p] = st_s[p]
        for h in heads_of(pairs):
            p, hh = divmod(h, 2)
            q2_ = q2d[:, p * LANES:(p + 1) * LANES]
            k2_ = k2d[:, p * LANES:(p + 1) * LANES]
            qa = jnp.where(halves[hh], q2_, jnp.zeros_like(q2_)) * jnp.asarray(qscale, q2_.dtype)
            s = lax.dot_general(qa, k2_, (((1,), (1,)), ((), ())), preferred_element_type=F32)
            qc = jnp.dot(qa, states[p].astype(BF16), preferred_element_type=F32)
            p1[h] = (s, qc)

    def phase2(pairs=all_pairs):
        for h in heads_of(pairs):
            p, hh = divmod(h, 2)
            s, qc = p1[h]
            bcol = b_c[:, M_HEADS + h:M_HEADS + h + 1]
            igcol = gif[:, h:h + 1]
            a_row = gt[h:h + 1, :] - b_r[h:h + 1, :]
            amat = jnp.where(causal, a_row, NEG)
            m_prev = m_s[h:h + 1, 0:1]
            g_t = jnp.maximum(m_prev, jnp.max(amat, axis=-1, keepdims=True))
            wqk = (jnp.exp(amat - g_t) * s).astype(BF16)
            dec = jnp.exp(m_prev - g_t)
            m_t = bcol + g_t
            m_new = m_t[lc - 1:lc, :]
            b_last = bcol[lc - 1:lc, :]
            dstate = jnp.exp(b_last + m_prev - m_new)
            ws = jnp.exp(b_last - bcol + igcol - m_new)
            k2_ = k2d[:, p * LANES:(p + 1) * LANES]
            ka = jnp.where(halves[hh], k2_, jnp.zeros_like(k2_))
            kws = (ka.astype(F32) * ws).astype(BF16)
            m_s[h:h + 1, :] = jnp.broadcast_to(m_new, (1, LANES))
            p2[h] = (wqk, dec * qc, jnp.exp(-m_t), dstate, kws)

    def phase3(pairs=all_pairs):
        upds = {}
        for h in heads_of(pairs):
            wqk, dqc, floor, _, kws = p2[h]
            v1 = jnp.concatenate([v2d[:, h * M_DV:(h + 1) * M_DV], ones_blk], axis=1)
            num_den = dqc + jnp.dot(wqk, v1, preferred_element_type=F32)
            num = num_den[:, :M_DV]
            den = num_den[:, M_DV:]
            hval = num / jnp.maximum(jnp.abs(den), floor)
            gate = jax.nn.sigmoid(om2d[:, h * M_DV:(h + 1) * M_DV].astype(F32))
            hm_ref[:, h * M_DV:(h + 1) * M_DV] = (gate * hval).astype(hm_ref.dtype)
            upds[h] = lax.dot_general(kws, v1, (((0,), (0,)), ((), ())),
                                      preferred_element_type=F32)
        for p in pairs:
            drow = jnp.where(sub < M_DQK, p2[2 * p][3], p2[2 * p + 1][3])
            st_s[p] = drow * states[p] + upds[2 * p] + upds[2 * p + 1]

    return phase1, phase2, phase3


FF_CHUNK = 256


DEC_HEADS = HEADS // 2
DEC_ROWS = DEC_HEADS * HEAD_DIM


def _decode_attn_step(b, qkv_ref, c_refs, o_refs, ot_ref):
    sel = lax.broadcasted_iota(jnp.int32, (DEC_ROWS, LANES), 1) == b
    scale = HEAD_DIM ** -0.5

    def column(i):
        return jnp.sum(jnp.where(sel, qkv_ref[i], 0.0), axis=1, keepdims=True)

    def head(col, h):
        return col[h * HEAD_DIM:(h + 1) * HEAD_DIM, :]

    qc = [column(g) * scale for g in range(N_GROUPS)]
    kn = [column(N_GROUPS + g) for g in range(N_GROUPS)]
    vn = [column(2 * N_GROUPS + g) for g in range(N_GROUPS)]
    cols = []
    for h in range(DEC_HEADS):
        scores, m_h = [], None
        for g, d in enumerate(DILATIONS):
            wb = c_refs[g].shape[-1]
            qh = head(qc[g], h)
            s = jnp.sum(c_refs[g][0, h] * qh, axis=0, keepdims=True)
            pos = lax.broadcasted_iota(jnp.int32, (1, wb), 1)
            s = jnp.where((pos & (d - 1)) == 0, s, NEG)
            s_new = jnp.sum(head(kn[g], h) * qh, axis=0, keepdims=True)
            m_g = jnp.maximum(jnp.max(s, axis=1, keepdims=True), s_new)
            m_h = m_g if m_h is None else jnp.maximum(m_h, m_g)
            scores.append((s, s_new))
        l_h = jnp.zeros((1, 1), F32)
        acc = jnp.zeros((HEAD_DIM, 1), F32)
        for g in range(N_GROUPS):
            s, s_new = scores[g]
            p = jnp.exp(s - m_h)
            p_new = jnp.exp(s_new - m_h)
            l_h = l_h + jnp.sum(p, axis=1, keepdims=True) + p_new
            acc = acc + jnp.sum(c_refs[g][1, h] * p, axis=1, keepdims=True) + p_new * head(vn[g], h)
        cols.append(acc / l_h)
    ot_ref[...] = jnp.broadcast_to(jnp.concatenate(cols, axis=0), ot_ref.shape)
    for g in range(N_GROUPS):
        wb = c_refs[g].shape[-1]
        last = lax.broadcasted_iota(jnp.int32, (HEAD_DIM, wb), 1) == wb - 1
        for kvi, new in ((0, kn[g]), (1, vn[g])):
            for h in range(DEC_HEADS):
                rolled = pltpu.roll(c_refs[g][kvi, h], wb - 1, 1)
                o_refs[g][kvi, h] = jnp.where(last, head(new, h), rolled)


def _merge_ffn_kernel(x_ref, o_ref, hm_ref, ga_ref, gb_ref, wa_ref, wm_ref, wo_ref, g2_ref,
                      wg_ref, wu_ref, wd_ref, *rest, with_decode):
    if with_decode:
        qkv_ref, c0, c1, c2, y_ref, o0, o1, o2, ot_ref = rest
        _decode_attn_step(pl.program_id(0) // 2, qkv_ref, (c0, c1, c2), (o0, o1, o2), ot_ref)
    else:
        (y_ref,) = rest
    yb = jnp.dot(hm_ref[...].astype(BF16), wm_ref[...], preferred_element_type=F32)
    o_att = jnp.concatenate([o_ref[p] for p in range(PAIRS)], axis=1).astype(BF16)
    ya = jnp.dot(o_att, wa_ref[...], preferred_element_type=F32)
    mixed = (jax.nn.sigmoid(ga_ref[...].astype(F32)) * ya
             + jax.nn.sigmoid(gb_ref[...].astype(F32)) * yb)
    x2 = x_ref[...] + jnp.dot(mixed.astype(BF16), wo_ref[...], preferred_element_type=F32)
    h2 = (x2 * lax.rsqrt(jnp.mean(x2 * x2, axis=-1, keepdims=True) + NORM_EPS)
          * g2_ref[...]).astype(BF16)
    acc = x2
    d_ff = wg_ref.shape[1]

    def gate_up(c):
        gt = jnp.dot(h2, wg_ref[:, c:c + FF_CHUNK], preferred_element_type=F32)
        up = jnp.dot(h2, wu_ref[:, c:c + FF_CHUNK], preferred_element_type=F32)
        return gt, up

    chunks = list(range(0, d_ff, FF_CHUNK))
    nxt = gate_up(chunks[0])
    for i, c in enumerate(chunks):
        gt, up = nxt
        if i + 1 < len(chunks):
            nxt = gate_up(chunks[i + 1])
        ff = (gt * jax.nn.sigmoid(gt) * up).astype(BF16)
        acc = acc + jnp.dot(ff, wd_ref[c:c + FF_CHUNK, :], preferred_element_type=F32)
    y_ref[...] = acc


def _merge_ffn(x2d, o_att, hm, ga, gb, wa, wm, wo, g2, wg, wu, wd, seq_len, tm, decode=None):
    m_rows, d_model = x2d.shape
    tiles_per_seq = seq_len // tm
    steps = m_rows // tm
    row = lambda w: pl.BlockSpec((tm, w), lambda i: (i, 0))
    o_spec = pl.BlockSpec((None, PAIRS, tm, LANES),
                          lambda i: (i // tiles_per_seq, 0, i % tiles_per_seq, 0))
    in_specs = [row(d_model), o_spec, row(M_V_W), row(D_MODEL), row(D_MODEL),
                _const_spec(wa.shape), _const_spec(wm.shape), _const_spec(wo.shape),
                _const_spec((1, d_model)), _const_spec(wg.shape), _const_spec(wu.shape),
                _const_spec(wd.shape)]
    args = [x2d, o_att, hm, ga, gb, wa, wm, wo, g2, wg, wu, wd]
    out_specs = [row(d_model)]
    out_shape = [jax.ShapeDtypeStruct((m_rows, d_model), F32)]
    if decode is not None:
        qkv_t, caches_t = decode
        db = caches_t[0].shape[0]
        assert steps == 2 * db, (steps, db)
        in_specs.append(pl.BlockSpec((qkv_t.shape[0], DEC_ROWS, LANES), lambda i: (0, i % 2, 0)))
        args.append(qkv_t)
        for c in caches_t:
            spec = pl.BlockSpec((None, 2, DEC_HEADS, HEAD_DIM, c.shape[-1]),
                                lambda i: (i // 2, 0, i % 2, 0, 0))
            in_specs.append(spec)
            args.append(c)
            out_specs.append(spec)
            out_shape.append(jax.ShapeDtypeStruct(c.shape, F32))
        out_specs.append(pl.BlockSpec((None, None, DEC_ROWS, LANES), lambda i: (i // 2, i % 2, 0, 0)))
        out_shape.append(jax.ShapeDtypeStruct((db, 2, DEC_ROWS, LANES), F32))
    outs = pl.pallas_call(
        functools.partial(_merge_ffn_kernel, with_decode=decode is not None),
        grid=(steps,),
        in_specs=in_specs, out_specs=out_specs, out_shape=out_shape,
        compiler_params=pltpu.CompilerParams(dimension_semantics=("arbitrary",),
                                             vmem_limit_bytes=VMEM_LIMIT),
        name="merge_ffn",
    )(*args)
    if decode is None:
        return outs[0]
    return outs[0], outs[1:4], outs[4]


def _sample_prep_kernel(*refs):
    for i, ref in enumerate(refs[:-1]):
        refs[-1][i] = ref[...].T


def _sample_prep(arrs):
    vm = pl.BlockSpec(memory_space=pltpu.VMEM)
    return pl.pallas_call(
        _sample_prep_kernel, in_specs=[vm] * len(arrs), out_specs=vm,
        out_shape=jax.ShapeDtypeStruct((len(arrs), GROUP_W, LANES), F32),
        name="sample_prep",
    )(*arrs)


def _sample_mlstm_kernel(q_ref, k_ref, v_ref, om_ref, gt_ref, mt_ref, n_ref, c_ref, rexp_ref,
                         hm_ref, co_ref, nt_ref, mo_ref, *, db):
    nh, dqk = M_HEADS, M_DQK
    hi = lax.Precision.HIGHEST
    rexp = rexp_ref[...]
    q_t = q_ref[...].T * (dqk ** -0.5)
    k_t = k_ref[...].T
    n_t = n_ref[...].T
    ig = gt_ref[0:nh, :]
    lf = _log_sigmoid(gt_ref[nh:2 * nh, :])
    m_prev = mt_ref[...]
    m_new = jnp.maximum(lf + m_prev, ig)
    w8 = jnp.exp(ig - m_new)
    dec8 = jnp.exp(lf + m_prev - m_new)
    head_sum = lambda a: lax.dot_general(rexp, a, (((0,), (0,)), ((), ())), precision=hi,
                                         preferred_element_type=F32)
    expand = lambda a: jnp.dot(rexp, a, precision=hi, preferred_element_type=F32)
    qk8 = head_sum(q_t * k_t)
    qn8 = head_sum(q_t * n_t)
    wqk8 = w8 * qk8
    den8 = dec8 * qn8 + wqk8
    inv8 = 1.0 / jnp.maximum(jnp.abs(den8), jnp.exp(-m_new))
    dec_x = expand(dec8)
    w_x = expand(w8)
    nt_ref[...] = dec_x * n_t + w_x * k_t
    mo_ref[...] = m_new
    wk_x = w_x * k_t
    for b in range(db):
        cb = c_ref[b].reshape(nh * dqk, M_DV)
        v_b = v_ref[b]
        v_x = jnp.concatenate([jnp.broadcast_to(v_b[h:h + 1, :], (dqk, M_DV)) for h in range(nh)],
                              axis=0)
        co_ref[b] = (dec_x[:, b:b + 1] * cb + wk_x[:, b:b + 1] * v_x).reshape(nh, dqk, M_DV)
        qc = jnp.sum((q_t[:, b:b + 1] * cb).reshape(nh, dqk, M_DV), axis=1)
        num = dec8[:, b:b + 1] * qc + wqk8[:, b:b + 1] * v_b
        hm_ref[b] = jax.nn.sigmoid(om_ref[b]) * (num * inv8[:, b:b + 1])


def _sample_mlstm(q_t, k_t, v3, om3, gif_t, m_t, n_pad, c_state, rexp, db):
    vm = pl.BlockSpec(memory_space=pltpu.VMEM)
    return pl.pallas_call(
        functools.partial(_sample_mlstm_kernel, db=db),
        in_specs=[vm] * 9,
        out_specs=[vm] * 4,
        out_shape=[jax.ShapeDtypeStruct((db, M_HEADS, M_DV), F32),
                   jax.ShapeDtypeStruct(c_state.shape, F32),
                   jax.ShapeDtypeStruct((M_HEADS * M_DQK, LANES), F32),
                   jax.ShapeDtypeStruct((M_HEADS, LANES), F32)],
        compiler_params=pltpu.CompilerParams(vmem_limit_bytes=VMEM_LIMIT),
        name="sample_mlstm",
    )(q_t, k_t, v3, om3, gif_t, m_t, n_pad, c_state, rexp)


def _rope_tables(pos):
    half = ROT_DIM // 2
    pos = np.asarray(pos, np.float32)
    inv_freq = np.exp(np.float32(-math.log(ROPE_THETA))
                      * np.arange(0, ROT_DIM, 2, dtype=np.float32) / np.float32(ROT_DIM))
    ang = (pos[:, None] * inv_freq[None, :]).astype(np.float32)
    cos, sin = np.cos(ang).astype(np.float32), np.sin(ang).astype(np.float32)
    t = pos.shape[0]
    rest = HEAD_DIM - ROT_DIM
    a = np.concatenate([cos, cos, np.ones((t, rest), np.float32)], axis=1)
    bm = np.concatenate([-sin, np.zeros((t, HEAD_DIM - half), np.float32)], axis=1)
    bp = np.concatenate([np.zeros((t, half), np.float32), sin, np.zeros((t, rest), np.float32)],
                        axis=1)
    return tuple(jnp.asarray(np.concatenate([x, x], axis=1)) for x in (a, bm, bp))


W_PREP_BLK = 512


def _w_prep_kernel(wt_ref, o_ref):
    o_ref[...] = wt_ref[...].T.astype(o_ref.dtype)


def _w_prep(w_in):
    d_model = w_in.shape[0]
    wt = w_in.T
    n_head = _C_GA // W_PREP_BLK
    n_blk = _W_COLS // W_PREP_BLK

    sub = 8

    def src_row(j):
        return (j * (W_PREP_BLK // sub) + jnp.where(j < n_head, 0, _GIF_COLS // sub)) * sub

    return pl.pallas_call(
        _w_prep_kernel,
        grid=(n_blk,),
        in_specs=[pl.BlockSpec((pl.Element(W_PREP_BLK), pl.Element(d_model)),
                               lambda j: (src_row(j), 0))],
        out_specs=pl.BlockSpec((d_model, W_PREP_BLK), lambda j: (0, j)),
        out_shape=jax.ShapeDtypeStruct((d_model, _W_COLS), BF16),
        compiler_params=pltpu.CompilerParams(dimension_semantics=("arbitrary",),
                                             vmem_limit_bytes=VMEM_LIMIT),
        name="w_prep",
    )(wt)


def kernel(x_prompt, x_sample, cache_kv_w128, cache_kv_w512, cache_kv_w2048, state_mlstm_C, state_mlstm_n, state_mlstm_m, norm1_g, w_in, b_if, q_norm_g, k_norm_g, w_att_out, w_m_out, w_o, norm2_g, w_gate, w_up, w_down):
    nb, seq_len, d_model = x_prompt.shape
    db, dec_seq, _ = x_sample.shape
    assert dec_seq == 1 and d_model == D_MODEL and seq_len % ATT_STEP == 0 and db <= LANES
    caches = (cache_kv_w128, cache_kv_w512, cache_kv_w2048)

    w_perm = _w_prep(w_in)
    w_gif = jnp.zeros((d_model, LANES), BF16).at[:, :_GIF_COLS].set(
        w_in[:, _C_GA:_C_GA + _GIF_COLS].astype(BF16))
    g1 = norm1_g.reshape(1, d_model)
    g2 = norm2_g.reshape(1, d_model)
    bif = jnp.concatenate([b_if, jnp.zeros((LANES - b_if.shape[0],), F32)]).reshape(1, LANES)
    qg = jnp.tile(q_norm_g, HEADS).reshape(1, GROUP_W)
    kg = jnp.tile(k_norm_g, HEADS).reshape(1, GROUP_W)
    hid = np.arange(GROUP_W // 2) // HEAD_DIM
    gmat = jnp.asarray(hid[:, None] == hid[None, :], dtype=BF16)

    m_rows = nb * seq_len
    x2d = x_prompt.reshape(m_rows, d_model)
    tabs_p = _rope_tables(np.arange(seq_len))
    qg_p = qg * (HEAD_DIM ** -0.5 * math.log2(math.e))
    outs = _proj(x2d, seq_len, DILATIONS, 256, False, g1, w_perm, w_gif, bif, qg_p, kg, tabs_p, gmat)
    qs, ks, vs = outs[0:3], outs[3:6], outs[6:9]
    hm, ga, gb, st_p, m_p = outs[9:14]

    o_att, (wa, wm, wo, wg, wu, wd) = _attention(
        qs, ks, vs, seq_len, (w_att_out, w_m_out, w_o, w_gate, w_up, w_down))

    x_s = jnp.zeros((LANES, d_model), F32).at[:db].set(x_sample.reshape(db, d_model))
    tabs_s = _rope_tables(np.full((LANES,), PAST_LEN))
    outs_s = _proj(x_s, LANES, (1, 1, 1), LANES, True, g1, w_perm, w_gif, bif, qg, kg, tabs_s, gmat)
    qkv_t = _sample_prep(outs_s[0:9])
    caches_t = [c.transpose(0, 2, 3, 4, 1) for c in caches]
    y_prompt, kv_st, o_att_cols = _merge_ffn(x2d, o_att, hm, ga, gb, wa, wm, wo, g2, wg, wu, wd,
                                             seq_len, 256, decode=(qkv_t, caches_t))
    y_prompt = y_prompt.reshape(nb, seq_len, d_model)
    kv_s = [c.transpose(0, 4, 1, 2, 3) for c in kv_st]

    tails = _kv_tails(ks, vs, seq_len)
    kv_p = [t.reshape(nb, 2, HEADS, HEAD_DIM, t.shape[-1]).transpose(0, 4, 1, 2, 3) for t in tails]
    c_p = st_p[..., :M_DV].reshape(nb, M_HEADS, M_DQK, M_DV)
    n_p = st_p[..., M_DV].reshape(nb, M_HEADS, M_DQK)
    m_pr = m_p[:, :, 0]

    qm_t, km_t, vm_s, om_s, ga_s, gb_s, gif_s = outs_s[9:16]
    gif_ts = gif_s.T

    rexp = jnp.asarray(np.arange(M_HEADS * M_DQK)[:, None] // M_DQK == np.arange(M_HEADS)[None, :],
                       dtype=F32)
    m_t = jnp.zeros((M_HEADS, LANES), F32).at[:, :db].set(state_mlstm_m.T)
    n_pad = jnp.zeros((LANES, M_HEADS * M_DQK), F32).at[:db].set(state_mlstm_n.reshape(db, -1))
    hm_s3, c_s, n_t, m_so = _sample_mlstm(
        qm_t, km_t, vm_s[:db].reshape(db, M_HEADS, M_DV), om_s[:db].reshape(db, M_HEADS, M_DV),
        gif_ts[:2 * M_HEADS], m_t, n_pad, state_mlstm_C, rexp, db)
    n_s = n_t.T[:db].reshape(db, M_HEADS, M_DQK)
    m_s = m_so[:, :db].T

    o_att_s = o_att_cols[:, :, :, 0].reshape(db, PAIRS, LANES)
    o_att_sp = jnp.zeros((PAIRS, LANES, LANES), F32).at[:, :db].set(o_att_s.transpose(1, 0, 2))[None]
    hm_sp = jnp.zeros((LANES, M_V_W), F32).at[:db].set(hm_s3.reshape(db, M_V_W))
    y_s = _merge_ffn(x_s, o_att_sp, hm_sp, ga_s, gb_s, wa, wm, wo, g2, wg, wu, wd, LANES, LANES)
    y_sample = y_s[:db].reshape(db, 1, d_model)

    return (y_prompt, y_sample, kv_p[0], kv_p[1], kv_p[2], c_p, n_p, m_pr,
            kv_s[0], kv_s[1], kv_s[2], c_s, n_s, m_s)
```

```python
import functools
import math

import jax
import jax.numpy as jnp
import numpy as np
from jax import lax
from jax.experimental import pallas as pl
from jax.experimental.pallas import tpu as pltpu

F32 = jnp.float32
BF16 = jnp.bfloat16

HEAD_DIM = 64
HEADS = 8
GROUP_W = HEADS * HEAD_DIM
N_GROUPS = 3
WINDOWS = (128, 512, 2048)
DILATIONS = (1, 4, 16)
SPAN = 128
ROT_DIM = 16
ROPE_THETA = 500000.0
M_HEADS = 8
M_DQK = 64
M_DV = 128
M_QK_W = M_HEADS * M_DQK
M_V_W = M_HEADS * M_DV
D_MODEL = 1024
PAST_LEN = 8192
NORM_EPS = 1e-6
NEG = -1e30

LANES = 128
PAIRS = GROUP_W // LANES
VMEM_LIMIT = 56 * 1024 * 1024

_ATT_W = N_GROUPS * GROUP_W
_C_QM = 3 * _ATT_W
_C_KM = _C_QM + M_HEADS * M_DQK
_C_VM = _C_KM + M_HEADS * M_DQK
_C_OM = _C_VM + M_HEADS * M_DV
_C_GA = _C_OM + M_HEADS * M_DV
_C_GB = _C_GA + 1024
_W_COLS = _C_GB + 1024
_GIF_COLS = 2 * M_HEADS


def _const_spec(shape):
    nd = len(shape)
    return pl.BlockSpec(shape, lambda *_: (0,) * nd, pipeline_mode=pl.Buffered(1))


def _log_sigmoid(x):
    return jnp.minimum(x, 0.0) - jnp.log1p(jnp.exp(-jnp.abs(x)))


def _proj_kernel(x_ref, g1_ref, w_ref, wgif_ref, bif_ref, qg_ref, kg_ref, ra_ref, rm_ref, rp_ref,
                 gm_ref,
                 q0_ref, q1_ref, q2_ref, k0_ref, k1_ref, k2_ref, v0_ref, v1_ref, v2_ref,
                 *rest, tm, dils, plain, tiles_per_seq):
    if plain:
        qm_ref, km_ref, vm_ref, om_ref, ga_ref, gb_ref, gif_ref, hs_ref = rest
    else:
        hm_ref, ga_ref, gb_ref, st_ref, mo_ref, hs_ref, st_s, m_s = rest
    d_model = x_ref.shape[1]
    x = x_ref[...]
    xn = x * lax.rsqrt(jnp.mean(x * x, axis=-1, keepdims=True) + NORM_EPS) * g1_ref[...]
    h_nat = xn.astype(BF16)
    n_slab = d_model // LANES
    if any(d > 1 for d in dils):
        for c in range(n_slab):
            hs_ref[c] = xn[:, c * LANES:(c + 1) * LANES]

    def permuted_h(d):
        if d == 1:
            return h_nat
        n = tm // d
        rows = [jnp.concatenate([hs_ref[c, pl.ds(r, n, stride=d), :] for c in range(n_slab)], axis=1)
                for r in range(d)]
        return jnp.concatenate(rows, axis=0).astype(BF16)

    def permuted_tab(ref, d):
        if d == 1:
            t = ref[...]
        else:
            n = tm // d
            t = jnp.concatenate([ref[pl.ds(r, n, stride=d), :] for r in range(d)], axis=0)
        return jnp.concatenate([t] * PAIRS, axis=1)

    gmat = gm_ref[...]

    def head_sumsq(z):
        zz = (z * z).astype(BF16)
        half = GROUP_W // 2
        return jnp.concatenate(
            [jnp.dot(zz[:, :half], gmat, preferred_element_type=F32),
             jnp.dot(zz[:, half:], gmat, preferred_element_type=F32)], axis=1)

    def norm_rope(z, ss, gain, ra, rm, rp):
        y = z * lax.rsqrt(ss * (1.0 / HEAD_DIM) + NORM_EPS) * gain
        return (y * ra + pltpu.roll(y, GROUP_W - ROT_DIM // 2, 1) * rm
                + pltpu.roll(y, ROT_DIM // 2, 1) * rp)

    def store_group(ref, y, d):
        if plain:
            ref[...] = y.astype(ref.dtype)
            return
        n = tm // d
        for p in range(PAIRS):
            ref[:, p] = y[:, p * LANES:(p + 1) * LANES].reshape(d, n, LANES).astype(ref.dtype)

    def seg(c0, width):
        return jnp.dot(h_nat, w_ref[:, c0:c0 + width], preferred_element_type=F32)

    if not plain:
        @pl.when(pl.program_id(0) % tiles_per_seq == 0)
        def _():
            st_s[...] = jnp.zeros_like(st_s)
            m_s[...] = jnp.zeros_like(m_s)

        gif = jnp.dot(h_nat, wgif_ref[...], preferred_element_type=F32) + bif_ref[...]
        m_phase1, m_phase2, m_phase3 = _mlstm_chunk(
            seg(_C_QM, M_QK_W).astype(BF16), seg(_C_KM, M_QK_W).astype(BF16),
            seg(_C_VM, M_V_W).astype(BF16), seg(_C_OM, M_V_W).astype(BF16),
            gif, gif.T[:2 * M_HEADS, :], st_s, m_s, hm_ref, tm)
        m_sets = [(0, 1), (2, 3)]

        def m_advance(i):
            if 0 < i <= len(m_sets):
                m_phase3(m_sets[i - 1])
            if i < len(m_sets):
                m_phase1(m_sets[i])
                m_phase2(m_sets[i])

        m_advance(0)

    q_refs = (q0_ref, q1_ref, q2_ref)
    k_refs = (k0_ref, k1_ref, k2_ref)
    v_refs = (v0_ref, v1_ref, v2_ref)
    for g in range(N_GROUPS):
        d = dils[g]
        hg = permuted_h(d)
        ra, rm, rp = (permuted_tab(r, d) for r in (ra_ref, rm_ref, rp_ref))
        cq, ck, cv = (t * _ATT_W + g * GROUP_W for t in range(3))
        zq = jnp.dot(hg, w_ref[:, cq:cq + GROUP_W], preferred_element_type=F32)
        zk = jnp.dot(hg, w_ref[:, ck:ck + GROUP_W], preferred_element_type=F32)
        zv = jnp.dot(hg, w_ref[:, cv:cv + GROUP_W], preferred_element_type=F32)
        ssq, ssk = head_sumsq(zq), head_sumsq(zk)
        store_group(v_refs[g], zv, d)
        store_group(q_refs[g], norm_rope(zq, ssq, qg_ref[...], ra, rm, rp), d)
        store_group(k_refs[g], norm_rope(zk, ssk, kg_ref[...], ra, rm, rp), d)

        if not plain:
            m_advance(g + 1)

    if plain:
        segments = ((qm_ref, _C_QM, M_QK_W), (km_ref, _C_KM, M_QK_W), (vm_ref, _C_VM, M_V_W),
                    (om_ref, _C_OM, M_V_W), (ga_ref, _C_GA, D_MODEL), (gb_ref, _C_GB, D_MODEL))
        zg = jnp.dot(h_nat, wgif_ref[...], preferred_element_type=F32)
        gif_ref[...] = zg + bif_ref[...]
    else:
        segments = ((ga_ref, _C_GA, D_MODEL), (gb_ref, _C_GB, D_MODEL))
    for si, (ref, c0, width) in enumerate(segments):
        for cc in range(0, width, GROUP_W):
            ref[:, cc:cc + GROUP_W] = seg(c0 + cc, GROUP_W).astype(ref.dtype)
        if not plain and si == 0:
            m_advance(N_GROUPS + 1)
            st_ref[...] = st_s[...]
            mo_ref[...] = m_s[...]


def _proj(x2d, seq_len, dils, tm, plain, g1, w_perm, w_gif, bif, qg, kg, rope_tabs, gmat):
    m_rows, d_model = x2d.shape
    nb = m_rows // seq_len
    tiles_per_seq = seq_len // tm
    grid = (m_rows // tm,)
    row_spec = lambda w: pl.BlockSpec((tm, w), lambda i: (i, 0))
    tab_spec = pl.BlockSpec((tm, LANES), lambda i: (i % tiles_per_seq, 0))
    sds = jax.ShapeDtypeStruct

    widths = (M_QK_W, M_QK_W, M_V_W, M_V_W, D_MODEL, D_MODEL)
    scratch = [pltpu.VMEM((d_model // LANES, tm, LANES), F32)]
    if plain:
        out_shape = ([sds((m_rows, GROUP_W), F32)] * 9 + [sds((m_rows, w), F32) for w in widths]
                     + [sds((m_rows, LANES), F32)])
        out_specs = ([row_spec(GROUP_W)] * 9 + [row_spec(w) for w in widths] + [row_spec(LANES)])
    else:
        def grp_spec(d):
            return pl.BlockSpec((None, d, PAIRS, tm // d, LANES),
                                lambda i: (i // tiles_per_seq, 0, 0, i % tiles_per_seq, 0))
        grp_shape = lambda d: sds((nb, d, PAIRS, seq_len // d, LANES), BF16)
        state_spec = lambda *dims: pl.BlockSpec((None,) + dims,
                                                lambda i: (i // tiles_per_seq,) + (0,) * len(dims))
        st_dims, m_dims = (M_HEADS // 2, LANES, 2 * LANES), (M_HEADS, LANES)
        row_w = (M_V_W, D_MODEL, D_MODEL)
        out_shape = ([grp_shape(d) for d in dils] * 3 + [sds((m_rows, w), BF16) for w in row_w]
                     + [sds((nb,) + st_dims, F32), sds((nb,) + m_dims, F32)])
        out_specs = ([grp_spec(d) for d in dils] * 3 + [row_spec(w) for w in row_w]
                     + [state_spec(*st_dims), state_spec(*m_dims)])
        scratch += [pltpu.VMEM(st_dims, F32), pltpu.VMEM(m_dims, F32)]
    in_specs = [row_spec(d_model), _const_spec((1, d_model)), _const_spec(w_perm.shape),
                _const_spec(w_gif.shape), _const_spec((1, LANES)), _const_spec((1, GROUP_W)),
                _const_spec((1, GROUP_W)), tab_spec, tab_spec, tab_spec, _const_spec(gmat.shape)]
    return pl.pallas_call(
        functools.partial(_proj_kernel, tm=tm, dils=dils, plain=plain,
                          tiles_per_seq=tiles_per_seq),
        grid=grid, in_specs=in_specs, out_specs=out_specs, out_shape=out_shape,
        scratch_shapes=scratch,
        compiler_params=pltpu.CompilerParams(dimension_semantics=("arbitrary",),
                                             vmem_limit_bytes=VMEM_LIMIT),
        name="proj",
    )(x2d, g1, w_perm, w_gif, bif, qg, kg, *rope_tabs, gmat)


ATT_STEP = SPAN * max(DILATIONS)
FAR_PITCH = SPAN + 8


def _attn_kernel(q0, k0, v0, kh0, vh0, q1, k1, v1, kh1, vh1, q2, k2, v2, kh2, vh2,
                 *rest, n_cast):
    w_f32, o_ref, w_bf16 = rest[:n_cast], rest[n_cast], rest[n_cast + 1:2 * n_cast + 1]
    acc_s, m_s, l_s, far_s = rest[2 * n_cast + 1:]
    j = pl.program_id(1)
    for src, dst in zip(w_f32, w_bf16):
        dst[...] = src[...].astype(dst.dtype)

    def window(cur, halo, r, bi):
        prev = halo[r] if bi == 0 else cur[r, (bi - 1) * SPAN:bi * SPAN, :]
        return jnp.concatenate([prev, cur[r, bi * SPAN:(bi + 1) * SPAN, :]], axis=0)

    qi = lax.broadcasted_iota(jnp.int32, (SPAN, 2 * SPAN), 0)
    ci = lax.broadcasted_iota(jnp.int32, (SPAN, 2 * SPAN), 1)
    band = (ci >= qi) & (ci <= qi + SPAN)
    bias_band = jnp.where(band, 0.0, NEG).astype(F32)
    bias_first = jnp.where(band & (ci >= SPAN), 0.0, NEG).astype(F32)
    lane_q = lax.broadcasted_iota(jnp.int32, (SPAN, LANES), 1)
    lane_kv = lax.broadcasted_iota(jnp.int32, (2 * SPAN, LANES), 1)
    halves_q = (lane_q < HEAD_DIM, lane_q >= HEAD_DIM)
    halves_kv = (lane_kv < HEAD_DIM, lane_kv >= HEAD_DIM)

    d0, d1, d2 = DILATIONS
    units = (
        [(0, bi, pl.ds(bi * SPAN, SPAN)) for bi in range(ATT_STEP // d0 // SPAN)],
        [(r, bi, pl.ds(bi * SPAN * d1 + r, SPAN, stride=d1))
         for r in range(d1) for bi in range(ATT_STEP // d1 // SPAN)],
        [(r, 0, pl.ds(r * FAR_PITCH, SPAN)) for r in range(d2)],
    )
    q_refs = (q0, q1, q2)
    kv_refs = ((k0, kh0, v0, vh0), (k1, kh1, v1, vh1), (k2, kh2, v2, vh2))

    def scores(g):
        k_cur, k_halo = kv_refs[g][:2]
        out = []
        for r, bi, _ in units[g]:
            q2_ = q_refs[g][r, bi * SPAN:(bi + 1) * SPAN, :]
            kk = window(k_cur, k_halo, r, bi)
            bias = jnp.where(j == 0, bias_first, bias_band) if bi == 0 else bias_band
            ss = []
            for hh in range(2):
                qa = jnp.where(halves_q[hh], q2_, jnp.zeros_like(q2_))
                ss.append(lax.dot_general(qa, kk, (((1,), (1,)), ((), ())),
                                          preferred_element_type=F32) + bias)
            out.append(ss)
        return out

    def finish(g, all_ss):
        v_cur, v_halo = kv_refs[g][2:]
        probs = []
        for ss in all_ss:
            mxs = [jnp.max(s, axis=-1, keepdims=True) for s in ss]
            ps = [jnp.exp2(s - mx) for s, mx in zip(ss, mxs)]
            ls = [jnp.sum(p, axis=-1, keepdims=True) for p in ps]
            probs.append(([p.astype(BF16) for p in ps], mxs, ls))
        for (r, bi, sl), (ps, mxs, ls) in zip(units[g], probs):
            vv = window(v_cur, v_halo, r, bi)
            acc = None
            for hh in range(2):
                vh = jnp.where(halves_kv[hh], vv, jnp.zeros_like(vv))
                a = jnp.dot(ps[hh], vh, preferred_element_type=F32)
                acc = a if acc is None else acc + a
            stats = (acc, jnp.where(halves_q[0], mxs[0], mxs[1]),
                     jnp.where(halves_q[0], ls[0], ls[1]))
            if g == N_GROUPS - 1:
                for k, val in enumerate(stats):
                    far_s[k, sl, :] = val
            else:
                for ref, val in zip((acc_s, m_s, l_s), stats):
                    ref[g, sl, :] = val

    for g in range(N_GROUPS):
        finish(g, scores(g))

    def far_rows(k, ci_):
        sub = 8
        pieces = [far_s[k, pl.ds(a * sub * FAR_PITCH + ci_ * (SPAN // d2) + jj, sub,
                                 stride=FAR_PITCH), :]
                  for jj in range(SPAN // d2) for a in range(d2 // sub)]
        return jnp.concatenate(pieces, axis=0)

    def combine(ci_, c):
        sl = pl.ds(pl.multiple_of(ci_ * SPAN, SPAN), SPAN)
        accs = [acc_s[0, sl, :], acc_s[1, sl, :], far_rows(0, ci_)]
        ms = [m_s[0, sl, :], m_s[1, sl, :], far_rows(1, ci_)]
        ls = [l_s[0, sl, :], l_s[1, sl, :], far_rows(2, ci_)]
        mm = jnp.maximum(jnp.maximum(ms[0], ms[1]), ms[2])
        es = [jnp.exp2(m - mm) for m in ms]
        num = es[0] * accs[0] + es[1] * accs[1] + es[2] * accs[2]
        den = es[0] * ls[0] + es[1] * ls[1] + es[2] * ls[2]
        o_ref[sl, :] = (num / den).astype(o_ref.dtype)
        return c
    lax.fori_loop(0, ATT_STEP // SPAN, combine, 0)


def _attention(qs, ks, vs, seq_len, cast_ws):
    nb = qs[0].shape[0]
    steps = seq_len // ATT_STEP
    in_specs, args, scratch = [], [], []
    for g, d in enumerate(DILATIONS):
        rows = ATT_STEP // d
        cur = pl.BlockSpec((None, d, None, rows, LANES), lambda b, j, p: (b, 0, p, j, 0))
        ratio = rows // SPAN
        halo = pl.BlockSpec((None, d, None, SPAN, LANES),
                            lambda b, j, p, ratio=ratio: (b, 0, p, jnp.maximum(j * ratio - 1, 0), 0))
        in_specs += [cur, cur, cur, halo, halo]
        args += [qs[g], ks[g], vs[g], ks[g], vs[g]]
    scratch += [pltpu.VMEM((N_GROUPS - 1, ATT_STEP, LANES), F32)] * 3
    scratch += [pltpu.VMEM((3, max(DILATIONS) * FAR_PITCH, LANES), F32)]
    n_slices = max(nb * steps * PAIRS // 2, 1)
    slice_of = lambda b, j, p: (((b * steps + j) * PAIRS + p) // 2, 0)
    w_specs = []
    for w in cast_ws:
        assert w.shape[0] % (16 * n_slices) == 0, (w.shape, n_slices)
        w_specs.append(pl.BlockSpec((w.shape[0] // n_slices, w.shape[1]), slice_of))
    outs = pl.pallas_call(
        functools.partial(_attn_kernel, n_cast=len(cast_ws)),
        grid=(nb, steps, PAIRS),
        in_specs=in_specs + w_specs,
        out_specs=[pl.BlockSpec((None, None, ATT_STEP, LANES), lambda b, j, p: (b, p, j, 0))]
                  + w_specs,
        out_shape=[jax.ShapeDtypeStruct((nb, PAIRS, seq_len, LANES), BF16)]
                  + [jax.ShapeDtypeStruct(w.shape, BF16) for w in cast_ws],
        scratch_shapes=scratch,
        compiler_params=pltpu.CompilerParams(
            dimension_semantics=("arbitrary", "arbitrary", "arbitrary"),
            vmem_limit_bytes=VMEM_LIMIT),
        name="attn",
    )(*args, *cast_ws)
    return outs[0], outs[1:]


def _kv_tail_kernel(k0, v0, k1, v1, k2, v2, o0, o1, o2, nat_ref):
    for (k_ref, v_ref, o_ref, d) in ((k0, v0, o0, DILATIONS[0]), (k1, v1, o1, DILATIONS[1]),
                                     (k2, v2, o2, DILATIONS[2])):
        for kvi, ref in enumerate((k_ref, v_ref)):
            for p in range(PAIRS):
                rows = slice(p * LANES, (p + 1) * LANES)
                if d == 1:
                    o_ref[kvi, rows, :] = ref[0, p].astype(F32).T
                    continue
                for r in range(d):
                    nat_ref[pl.ds(r, SPAN, stride=d), :] = ref[r, p].astype(F32)
                for c in range(d):
                    o_ref[kvi, rows, c * SPAN:(c + 1) * SPAN] = nat_ref[c * SPAN:(c + 1) * SPAN, :].T


def _kv_tails(ks, vs, seq_len):
    nb = ks[0].shape[0]
    in_specs, args, out_specs, out_shape = [], [], [], []
    for g, d in enumerate(DILATIONS):
        last = seq_len // d // SPAN - 1
        spec = pl.BlockSpec((None, d, PAIRS, SPAN, LANES), lambda b, last=last: (b, 0, 0, last, 0))
        in_specs += [spec, spec]
        args += [ks[g], vs[g]]
        out_specs.append(pl.BlockSpec((None, 2, GROUP_W, SPAN * d), lambda b: (b, 0, 0, 0)))
        out_shape.append(jax.ShapeDtypeStruct((nb, 2, GROUP_W, SPAN * d), F32))
    return pl.pallas_call(
        _kv_tail_kernel,
        grid=(nb,), in_specs=in_specs, out_specs=out_specs, out_shape=out_shape,
        scratch_shapes=[pltpu.VMEM((SPAN * max(DILATIONS), LANES), F32)],
        compiler_params=pltpu.CompilerParams(dimension_semantics=("arbitrary",),
                                             vmem_limit_bytes=VMEM_LIMIT),
        name="kv_tail",
    )(*args)


def _mlstm_chunk(q2d, k2d, v2d, om2d, gif, gt, st_s, m_s, hm_ref, lc):
    lf_c = _log_sigmoid(gif)
    lf_r = _log_sigmoid(gt[M_HEADS:2 * M_HEADS, :])
    row = lax.broadcasted_iota(jnp.int32, (lc, lc), 0)
    col = lax.broadcasted_iota(jnp.int32, (lc, lc), 1)
    causal = col <= row
    tril = causal.astype(F32)
    triu = (row <= col).astype(F32)
    b_c = jnp.dot(tril, lf_c, precision=lax.Precision.HIGHEST, preferred_element_type=F32)
    b_r = jnp.dot(lf_r, triu, precision=lax.Precision.HIGHEST, preferred_element_type=F32)
    lane = lax.broadcasted_iota(jnp.int32, (lc, LANES), 1)
    halves = (lane < M_DQK, lane >= M_DQK)
    sub = lax.broadcasted_iota(jnp.int32, (LANES, 1), 0)
    ones_blk = jnp.ones((lc, LANES), BF16)
    qscale = M_DQK ** -0.5

    states, p1, p2 = {}, {}, {}
    all_pairs = tuple(range(M_HEADS // 2))
    heads_of = lambda pairs: [2 * p + hh for p in pairs for hh in range(2)]

    def phase1(pairs=all_pairs):
        for p in pairs:
            states[p] = st_s[p]
        for h in heads_of(pairs):
            p, hh = divmod(h, 2)
            q2_ = q2d[:, p * LANES:(p + 1) * LANES]
            k2_ = k2d[:, p * LANES:(p + 1) * LANES]
            qa = jnp.where(halves[hh], q2_, jnp.zeros_like(q2_)) * jnp.asarray(qscale, q2_.dtype)
            s = lax.dot_general(qa, k2_, (((1,), (1,)), ((), ())), preferred_element_type=F32)
            qc = jnp.dot(qa, states[p].astype(BF16), preferred_element_type=F32)
            p1[h] = (s, qc)

    def phase2(pairs=all_pairs):
        for h in heads_of(pairs):
            p, hh = divmod(h, 2)
            s, qc = p1[h]
            bcol = b_c[:, M_HEADS + h:M_HEADS + h + 1]
            igcol = gif[:, h:h + 1]
            a_row = gt[h:h + 1, :] - b_r[h:h + 1, :]
            amat = jnp.where(causal, a_row, NEG)
            m_prev = m_s[h:h + 1, 0:1]
            g_t = jnp.maximum(m_prev, jnp.max(amat, axis=-1, keepdims=True))
            wqk = (jnp.exp(amat - g_t) * s).astype(BF16)
            dec = jnp.exp(m_prev - g_t)
            m_t = bcol + g_t
            m_new = m_t[lc - 1:lc, :]
            b_last = bcol[lc - 1:lc, :]
            dstate = jnp.exp(b_last + m_prev - m_new)
            ws = jnp.exp(b_last - bcol + igcol - m_new)
            k2_ = k2d[:, p * LANES:(p + 1) * LANES]
            ka = jnp.where(halves[hh], k2_, jnp.zeros_like(k2_))
            kws = (ka.astype(F32) * ws).astype(BF16)
            m_s[h:h + 1, :] = jnp.broadcast_to(m_new, (1, LANES))
            p2[h] = (wqk, dec * qc, jnp.exp(-m_t), dstate, kws)

    def phase3(pairs=all_pairs):
        upds = {}
        for h in heads_of(pairs):
            wqk, dqc, floor, _, kws = p2[h]
            v1 = jnp.concatenate([v2d[:, h * M_DV:(h + 1) * M_DV], ones_blk], axis=1)
            num_den = dqc + jnp.dot(wqk, v1, preferred_element_type=F32)
            num = num_den[:, :M_DV]
            den = num_den[:, M_DV:]
            hval = num / jnp.maximum(jnp.abs(den), floor)
            gate = jax.nn.sigmoid(om2d[:, h * M_DV:(h + 1) * M_DV].astype(F32))
            hm_ref[:, h * M_DV:(h + 1) * M_DV] = (gate * hval).astype(hm_ref.dtype)
            upds[h] = lax.dot_general(kws, v1, (((0,), (0,)), ((), ())),
                                      preferred_element_type=F32)
        for p in pairs:
            drow = jnp.where(sub < M_DQK, p2[2 * p][3], p2[2 * p + 1][3])
            st_s[p] = drow * states[p] + upds[2 * p] + upds[2 * p + 1]

    return phase1, phase2, phase3


FF_CHUNK = 256


DEC_HEADS = HEADS // 2
DEC_ROWS = DEC_HEADS * HEAD_DIM


def _decode_attn_step(b, qkv_ref, c_refs, o_refs, ot_ref):
    sel = lax.broadcasted_iota(jnp.int32, (DEC_ROWS, LANES), 1) == b
    scale = HEAD_DIM ** -0.5

    def column(i):
        return jnp.sum(jnp.where(sel, qkv_ref[i], 0.0), axis=1, keepdims=True)

    def head(col, h):
        return col[h * HEAD_DIM:(h + 1) * HEAD_DIM, :]

    qc = [column(g) * scale for g in range(N_GROUPS)]
    kn = [column(N_GROUPS + g) for g in range(N_GROUPS)]
    vn = [column(2 * N_GROUPS + g) for g in range(N_GROUPS)]
    cols = []
    for h in range(DEC_HEADS):
        scores, m_h = [], None
        for g, d in enumerate(DILATIONS):
            wb = c_refs[g].shape[-1]
            qh = head(qc[g], h)
            s = jnp.sum(c_refs[g][0, h] * qh, axis=0, keepdims=True)
            pos = lax.broadcasted_iota(jnp.int32, (1, wb), 1)
            s = jnp.where((pos & (d - 1)) == 0, s, NEG)
            s_new = jnp.sum(head(kn[g], h) * qh, axis=0, keepdims=True)
            m_g = jnp.maximum(jnp.max(s, axis=1, keepdims=True), s_new)
            m_h = m_g if m_h is None else jnp.maximum(m_h, m_g)
            scores.append((s, s_new))
        l_h = jnp.zeros((1, 1), F32)
        acc = jnp.zeros((HEAD_DIM, 1), F32)
        for g in range(N_GROUPS):
            s, s_new = scores[g]
            p = jnp.exp(s - m_h)
            p_new = jnp.exp(s_new - m_h)
            l_h = l_h + jnp.sum(p, axis=1, keepdims=True) + p_new
            acc = acc + jnp.sum(c_refs[g][1, h] * p, axis=1, keepdims=True) + p_new * head(vn[g], h)
        cols.append(acc / l_h)
    ot_ref[...] = jnp.broadcast_to(jnp.concatenate(cols, axis=0), ot_ref.shape)
    for g in range(N_GROUPS):
        wb = c_refs[g].shape[-1]
        last = lax.broadcasted_iota(jnp.int32, (HEAD_DIM, wb), 1) == wb - 1
        for kvi, new in ((0, kn[g]), (1, vn[g])):
            for h in range(DEC_HEADS):
                rolled = pltpu.roll(c_refs[g][kvi, h], wb - 1, 1)
                o_refs[g][kvi, h] = jnp.where(last, head(new, h), rolled)


def _merge_ffn_kernel(x_ref, o_ref, hm_ref, ga_ref, gb_ref, wa_ref, wm_ref, wo_ref, g2_ref,
                      wg_ref, wu_ref, wd_ref, *rest, with_decode):
    if with_decode:
        qkv_ref, c0, c1, c2, y_ref, o0, o1, o2, ot_ref = rest
        _decode_attn_step(pl.program_id(0) // 2, qkv_ref, (c0, c1, c2), (o0, o1, o2), ot_ref)
    else:
        (y_ref,) = rest
    yb = jnp.dot(hm_ref[...].astype(BF16), wm_ref[...], preferred_element_type=F32)
    o_att = jnp.concatenate([o_ref[p] for p in range(PAIRS)], axis=1).astype(BF16)
    ya = jnp.dot(o_att, wa_ref[...], preferred_element_type=F32)
    mixed = (jax.nn.sigmoid(ga_ref[...].astype(F32)) * ya
             + jax.nn.sigmoid(gb_ref[...].astype(F32)) * yb)
    x2 = x_ref[...] + jnp.dot(mixed.astype(BF16), wo_ref[...], preferred_element_type=F32)
    h2 = (x2 * lax.rsqrt(jnp.mean(x2 * x2, axis=-1, keepdims=True) + NORM_EPS)
          * g2_ref[...]).astype(BF16)
    acc = x2
    d_ff = wg_ref.shape[1]

    def gate_up(c):
        gt = jnp.dot(h2, wg_ref[:, c:c + FF_CHUNK], preferred_element_type=F32)
        up = jnp.dot(h2, wu_ref[:, c:c + FF_CHUNK], preferred_element_type=F32)
        return gt, up

    chunks = list(range(0, d_ff, FF_CHUNK))
    nxt = gate_up(chunks[0])
    for i, c in enumerate(chunks):
        gt, up = nxt
        if i + 1 < len(chunks):
            nxt = gate_up(chunks[i + 1])
        ff = (gt * jax.nn.sigmoid(gt) * up).astype(BF16)
        acc = acc + jnp.dot(ff, wd_ref[c:c + FF_CHUNK, :], preferred_element_type=F32)
    y_ref[...] = acc


def _merge_ffn(x2d, o_att, hm, ga, gb, wa, wm, wo, g2, wg, wu, wd, seq_len, tm, decode=None):
    m_rows, d_model = x2d.shape
    tiles_per_seq = seq_len // tm
    steps = m_rows // tm
    row = lambda w: pl.BlockSpec((tm, w), lambda i: (i, 0))
    o_spec = pl.BlockSpec((None, PAIRS, tm, LANES),
                          lambda i: (i // tiles_per_seq, 0, i % tiles_per_seq, 0))
    in_specs = [row(d_model), o_spec, row(M_V_W), row(D_MODEL), row(D_MODEL),
                _const_spec(wa.shape), _const_spec(wm.shape), _const_spec(wo.shape),
                _const_spec((1, d_model)), _const_spec(wg.shape), _const_spec(wu.shape),
                _const_spec(wd.shape)]
    args = [x2d, o_att, hm, ga, gb, wa, wm, wo, g2, wg, wu, wd]
    out_specs = [row(d_model)]
    out_shape = [jax.ShapeDtypeStruct((m_rows, d_model), F32)]
    if decode is not None:
        qkv_t, caches_t = decode
        db = caches_t[0].shape[0]
        assert steps == 2 * db, (steps, db)
        in_specs.append(pl.BlockSpec((qkv_t.shape[0], DEC_ROWS, LANES), lambda i: (0, i % 2, 0)))
        args.append(qkv_t)
        for c in caches_t:
            spec = pl.BlockSpec((None, 2, DEC_HEADS, HEAD_DIM, c.shape[-1]),
                                lambda i: (i // 2, 0, i % 2, 0, 0))
            in_specs.append(spec)
            args.append(c)
            out_specs.append(spec)
            out_shape.append(jax.ShapeDtypeStruct(c.shape, F32))
        out_specs.append(pl.BlockSpec((None, None, DEC_ROWS, LANES), lambda i: (i // 2, i % 2, 0, 0)))
        out_shape.append(jax.ShapeDtypeStruct((db, 2, DEC_ROWS, LANES), F32))
    outs = pl.pallas_call(
        functools.partial(_merge_ffn_kernel, with_decode=decode is not None),
        grid=(steps,),
        in_specs=in_specs, out_specs=out_specs, out_shape=out_shape,
        compiler_params=pltpu.CompilerParams(dimension_semantics=("arbitrary",),
                                             vmem_limit_bytes=VMEM_LIMIT),
        name="merge_ffn",
    )(*args)
    if decode is None:
        return outs[0]
    return outs[0], outs[1:4], outs[4]


def _sample_prep_kernel(*refs):
    for i, ref in enumerate(refs[:-1]):
        refs[-1][i] = ref[...].T


def _sample_prep(arrs):
    vm = pl.BlockSpec(memory_space=pltpu.VMEM)
    return pl.pallas_call(
        _sample_prep_kernel, in_specs=[vm] * len(arrs), out_specs=vm,
        out_shape=jax.ShapeDtypeStruct((len(arrs), GROUP_W, LANES), F32),
        name="sample_prep",
    )(*arrs)


def _sample_mlstm_kernel(q_ref, k_ref, v_ref, om_ref, gt_ref, mt_ref, n_ref, c_ref, rexp_ref,
                         hm_ref, co_ref, nt_ref, mo_ref, *, db):
    nh, dqk = M_HEADS, M_DQK
    hi = lax.Precision.HIGHEST
    rexp = rexp_ref[...]
    q_t = q_ref[...].T * (dqk ** -0.5)
    k_t = k_ref[...].T
    n_t = n_ref[...].T
    ig = gt_ref[0:nh, :]
    lf = _log_sigmoid(gt_ref[nh:2 * nh, :])
    m_prev = mt_ref[...]
    m_new = jnp.maximum(lf + m_prev, ig)
    w8 = jnp.exp(ig - m_new)
    dec8 = jnp.exp(lf + m_prev - m_new)
    head_sum = lambda a: lax.dot_general(rexp, a, (((0,), (0,)), ((), ())), precision=hi,
                                         preferred_element_type=F32)
    expand = lambda a: jnp.dot(rexp, a, precision=hi, preferred_element_type=F32)
    qk8 = head_sum(q_t * k_t)
    qn8 = head_sum(q_t * n_t)
    wqk8 = w8 * qk8
    den8 = dec8 * qn8 + wqk8
    inv8 = 1.0 / jnp.maximum(jnp.abs(den8), jnp.exp(-m_new))
    dec_x = expand(dec8)
    w_x = expand(w8)
    nt_ref[...] = dec_x * n_t + w_x * k_t
    mo_ref[...] = m_new
    wk_x = w_x * k_t
    for b in range(db):
        cb = c_ref[b].reshape(nh * dqk, M_DV)
        v_b = v_ref[b]
        v_x = jnp.concatenate([jnp.broadcast_to(v_b[h:h + 1, :], (dqk, M_DV)) for h in range(nh)],
                              axis=0)
        co_ref[b] = (dec_x[:, b:b + 1] * cb + wk_x[:, b:b + 1] * v_x).reshape(nh, dqk, M_DV)
        qc = jnp.sum((q_t[:, b:b + 1] * cb).reshape(nh, dqk, M_DV), axis=1)
        num = dec8[:, b:b + 1] * qc + wqk8[:, b:b + 1] * v_b
        hm_ref[b] = jax.nn.sigmoid(om_ref[b]) * (num * inv8[:, b:b + 1])


def _sample_mlstm(q_t, k_t, v3, om3, gif_t, m_t, n_pad, c_state, rexp, db):
    vm = pl.BlockSpec(memory_space=pltpu.VMEM)
    return pl.pallas_call(
        functools.partial(_sample_mlstm_kernel, db=db),
        in_specs=[vm] * 9,
        out_specs=[vm] * 4,
        out_shape=[jax.ShapeDtypeStruct((db, M_HEADS, M_DV), F32),
                   jax.ShapeDtypeStruct(c_state.shape, F32),
                   jax.ShapeDtypeStruct((M_HEADS * M_DQK, LANES), F32),
                   jax.ShapeDtypeStruct((M_HEADS, LANES), F32)],
        compiler_params=pltpu.CompilerParams(vmem_limit_bytes=VMEM_LIMIT),
        name="sample_mlstm",
    )(q_t, k_t, v3, om3, gif_t, m_t, n_pad, c_state, rexp)


def _rope_tables(pos):
    half = ROT_DIM // 2
    pos = np.asarray(pos, np.float32)
    inv_freq = np.exp(np.float32(-math.log(ROPE_THETA))
                      * np.arange(0, ROT_DIM, 2, dtype=np.float32) / np.float32(ROT_DIM))
    ang = (pos[:, None] * inv_freq[None, :]).astype(np.float32)
    cos, sin = np.cos(ang).astype(np.float32), np.sin(ang).astype(np.float32)
    t = pos.shape[0]
    rest = HEAD_DIM - ROT_DIM
    a = np.concatenate([cos, cos, np.ones((t, rest), np.float32)], axis=1)
    bm = np.concatenate([-sin, np.zeros((t, HEAD_DIM - half), np.float32)], axis=1)
    bp = np.concatenate([np.zeros((t, half), np.float32), sin, np.zeros((t, rest), np.float32)],
                        axis=1)
    return tuple(jnp.asarray(np.concatenate([x, x], axis=1)) for x in (a, bm, bp))


W_PREP_BLK = 512


def _w_prep_kernel(wt_ref, o_ref):
    o_ref[...] = wt_ref[...].T.astype(o_ref.dtype)


def _w_prep(w_in):
    d_model = w_in.shape[0]
    wt = w_in.T
    n_head = _C_GA // W_PREP_BLK
    n_blk = _W_COLS // W_PREP_BLK

    sub = 8

    def src_row(j):
        return (j * (W_PREP_BLK // sub) + jnp.where(j < n_head, 0, _GIF_COLS // sub)) * sub

    return pl.pallas_call(
        _w_prep_kernel,
        grid=(n_blk,),
        in_specs=[pl.BlockSpec((pl.Element(W_PREP_BLK), pl.Element(d_model)),
                               lambda j: (src_row(j), 0))],
        out_specs=pl.BlockSpec((d_model, W_PREP_BLK), lambda j: (0, j)),
        out_shape=jax.ShapeDtypeStruct((d_model, _W_COLS), BF16),
        compiler_params=pltpu.CompilerParams(dimension_semantics=("arbitrary",),
                                             vmem_limit_bytes=VMEM_LIMIT),
        name="w_prep",
    )(wt)


def kernel(x_prompt, x_sample, cache_kv_w128, cache_kv_w512, cache_kv_w2048, state_mlstm_C, state_mlstm_n, state_mlstm_m, norm1_g, w_in, b_if, q_norm_g, k_norm_g, w_att_out, w_m_out, w_o, norm2_g, w_gate, w_up, w_down):
    nb, seq_len, d_model = x_prompt.shape
    db, dec_seq, _ = x_sample.shape
    assert dec_seq == 1 and d_model == D_MODEL and seq_len % ATT_STEP == 0 and db <= LANES
    caches = (cache_kv_w128, cache_kv_w512, cache_kv_w2048)

    w_perm = _w_prep(w_in)
    w_gif = jnp.zeros((d_model, LANES), BF16).at[:, :_GIF_COLS].set(
        w_in[:, _C_GA:_C_GA + _GIF_COLS].astype(BF16))
    g1 = norm1_g.reshape(1, d_model)
    g2 = norm2_g.reshape(1, d_model)
    bif = jnp.concatenate([b_if, jnp.zeros((LANES - b_if.shape[0],), F32)]).reshape(1, LANES)
    qg = jnp.tile(q_norm_g, HEADS).reshape(1, GROUP_W)
    kg = jnp.tile(k_norm_g, HEADS).reshape(1, GROUP_W)
    hid = np.arange(GROUP_W // 2) // HEAD_DIM
    gmat = jnp.asarray(hid[:, None] == hid[None, :], dtype=BF16)

    m_rows = nb * seq_len
    x2d = x_prompt.reshape(m_rows, d_model)
    tabs_p = _rope_tables(np.arange(seq_len))
    qg_p = qg * (HEAD_DIM ** -0.5 * math.log2(math.e))
    outs = _proj(x2d, seq_len, DILATIONS, 256, False, g1, w_perm, w_gif, bif, qg_p, kg, tabs_p, gmat)
    qs, ks, vs = outs[0:3], outs[3:6], outs[6:9]
    hm, ga, gb, st_p, m_p = outs[9:14]

    o_att, (wa, wm, wo, wg, wu, wd) = _attention(
        qs, ks, vs, seq_len, (w_att_out, w_m_out, w_o, w_gate, w_up, w_down))

    x_s = jnp.zeros((LANES, d_model), F32).at[:db].set(x_sample.reshape(db, d_model))
    tabs_s = _rope_tables(np.full((LANES,), PAST_LEN))
    outs_s = _proj(x_s, LANES, (1, 1, 1), LANES, True, g1, w_perm, w_gif, bif, qg, kg, tabs_s, gmat)
    qkv_t = _sample_prep(outs_s[0:9])
    caches_t = [c.transpose(0, 2, 3, 4, 1) for c in caches]
    y_prompt, kv_st, o_att_cols = _merge_ffn(x2d, o_att, hm, ga, gb, wa, wm, wo, g2, wg, wu, wd,
                                             seq_len, 256, decode=(qkv_t, caches_t))
    y_prompt = y_prompt.reshape(nb, seq_len, d_model)
    kv_s = [c.transpose(0, 4, 1, 2, 3) for c in kv_st]

    tails = _kv_tails(ks, vs, seq_len)
    kv_p = [t.reshape(nb, 2, HEADS, HEAD_DIM, t.shape[-1]).transpose(0, 4, 1, 2, 3) for t in tails]
    c_p = st_p[..., :M_DV].reshape(nb, M_HEADS, M_DQK, M_DV)
    n_p = st_p[..., M_DV].reshape(nb, M_HEADS, M_DQK)
    m_pr = m_p[:, :, 0]

    qm_t, km_t, vm_s, om_s, ga_s, gb_s, gif_s = outs_s[9:16]
    gif_ts = gif_s.T

    rexp = jnp.asarray(np.arange(M_HEADS * M_DQK)[:, None] // M_DQK == np.arange(M_HEADS)[None, :],
                       dtype=F32)
    m_t = jnp.zeros((M_HEADS, LANES), F32).at[:, :db].set(state_mlstm_m.T)
    n_pad = jnp.zeros((LANES, M_HEADS * M_DQK), F32).at[:db].set(state_mlstm_n.reshape(db, -1))
    hm_s3, c_s, n_t, m_so = _sample_mlstm(
        qm_t, km_t, vm_s[:db].reshape(db, M_HEADS, M_DV), om_s[:db].reshape(db, M_HEADS, M_DV),
        gif_ts[:2 * M_HEADS], m_t, n_pad, state_mlstm_C, rexp, db)
    n_s = n_t.T[:db].reshape(db, M_HEADS, M_DQK)
    m_s = m_so[:, :db].T

    o_att_s = o_att_cols[:, :, :, 0].reshape(db, PAIRS, LANES)
    o_att_sp = jnp.zeros((PAIRS, LANES, LANES), F32).at[:, :db].set(o_att_s.transpose(1, 0, 2))[None]
    hm_sp = jnp.zeros((LANES, M_V_W), F32).at[:db].set(hm_s3.reshape(db, M_V_W))
    y_s = _merge_ffn(x_s, o_att_sp, hm_sp, ga_s, gb_s, wa, wm, wo, g2, wg, wu, wd, LANES, LANES)
    y_sample = y_s[:db].reshape(db, 1, d_model)

    return (y_prompt, y_sample, kv_p[0], kv_p[1], kv_p[2], c_p, n_p, m_pr,
            kv_s[0], kv_s[1], kv_s[2], c_s, n_s, m_s)
```

```python
import functools
import math

import jax
import jax.numpy as jnp
import numpy as np
from jax import lax
from jax.experimental import pallas as pl
from jax.experimental.pallas import tpu as pltpu

F32 = jnp.float32
BF16 = jnp.bfloat16

HEAD_DIM = 64
HEADS = 8
GROUP_W = HEADS * HEAD_DIM
N_GROUPS = 3
WINDOWS = (128, 512, 2048)
DILATIONS = (1, 4, 16)
SPAN = 128
ROT_DIM = 16
ROPE_THETA = 500000.0
M_HEADS = 8
M_DQK = 64
M_DV = 128
M_QK_W = M_HEADS * M_DQK
M_V_W = M_HEADS * M_DV
D_MODEL = 1024
PAST_LEN = 8192
NORM_EPS = 1e-6
NEG = -1e30

LANES = 128
PAIRS = GROUP_W // LANES
VMEM_LIMIT = 56 * 1024 * 1024

_ATT_W = N_GROUPS * GROUP_W
_C_QM = 3 * _ATT_W
_C_KM = _C_QM + M_HEADS * M_DQK
_C_VM = _C_KM + M_HEADS * M_DQK
_C_OM = _C_VM + M_HEADS * M_DV
_C_GA = _C_OM + M_HEADS * M_DV
_C_GB = _C_GA + 1024
_W_COLS = _C_GB + 1024
_GIF_COLS = 2 * M_HEADS


def _const_spec(shape):
    nd = len(shape)
    return pl.BlockSpec(shape, lambda *_: (0,) * nd, pipeline_mode=pl.Buffered(1))


def _log_sigmoid(x):
    return jnp.minimum(x, 0.0) - jnp.log1p(jnp.exp(-jnp.abs(x)))


MLSTM_CHUNKS = 2


def _proj_kernel(x_ref, g1_ref, w_ref, wgif_ref, bif_ref, qg_ref, kg_ref, ra_ref, rm_ref, rp_ref,
                 gm_ref,
                 q0_ref, q1_ref, q2_ref, k0_ref, k1_ref, k2_ref, v0_ref, v1_ref, v2_ref,
                 *rest, tm, dils, plain, tiles_per_seq):
    if plain:
        qm_ref, km_ref, vm_ref, om_ref, ga_ref, gb_ref, gif_ref, hs_ref = rest
    else:
        hm_ref, ga_ref, gb_ref, st_ref, mo_ref, hs_ref, st_s, m_s = rest
    d_model = x_ref.shape[1]
    x = x_ref[...]
    xn = x * lax.rsqrt(jnp.mean(x * x, axis=-1, keepdims=True) + NORM_EPS) * g1_ref[...]
    h_nat = xn.astype(BF16)
    n_slab = d_model // LANES
    if any(d > 1 for d in dils):
        for c in range(n_slab):
            hs_ref[c] = xn[:, c * LANES:(c + 1) * LANES]

    def permuted_h(d):
        if d == 1:
            return h_nat
        n = tm // d
        rows = [jnp.concatenate([hs_ref[c, pl.ds(r, n, stride=d), :] for c in range(n_slab)], axis=1)
                for r in range(d)]
        return jnp.concatenate(rows, axis=0).astype(BF16)

    def permuted_tab(ref, d):
        if d == 1:
            t = ref[...]
        else:
            n = tm // d
            t = jnp.concatenate([ref[pl.ds(r, n, stride=d), :] for r in range(d)], axis=0)
        return jnp.concatenate([t] * PAIRS, axis=1)

    gmat = gm_ref[...]

    def head_sumsq(z):
        zz = (z * z).astype(BF16)
        half = GROUP_W // 2
        return jnp.concatenate(
            [jnp.dot(zz[:, :half], gmat, preferred_element_type=F32),
             jnp.dot(zz[:, half:], gmat, preferred_element_type=F32)], axis=1)

    def norm_rope(z, ss, gain, ra, rm, rp):
        y = z * lax.rsqrt(ss * (1.0 / HEAD_DIM) + NORM_EPS) * gain
        return (y * ra + pltpu.roll(y, GROUP_W - ROT_DIM // 2, 1) * rm
                + pltpu.roll(y, ROT_DIM // 2, 1) * rp)

    def store_group(ref, y, d):
        if plain:
            ref[...] = y.astype(ref.dtype)
            return
        n = tm // d
        for p in range(PAIRS):
            ref[:, p] = y[:, p * LANES:(p + 1) * LANES].reshape(d, n, LANES).astype(ref.dtype)

    def seg(c0, width):
        return jnp.dot(h_nat, w_ref[:, c0:c0 + width], preferred_element_type=F32)

    if not plain:
        @pl.when(pl.program_id(0) % tiles_per_seq == 0)
        def _():
            st_s[...] = jnp.zeros_like(st_s)
            m_s[...] = jnp.zeros_like(m_s)

        gif = jnp.dot(h_nat, wgif_ref[...], preferred_element_type=F32) + bif_ref[...]
        qm_b, km_b = seg(_C_QM, M_QK_W).astype(BF16), seg(_C_KM, M_QK_W).astype(BF16)
        vm_b, om_b = seg(_C_VM, M_V_W).astype(BF16), seg(_C_OM, M_V_W).astype(BF16)
        gif_t = gif.T[:2 * M_HEADS, :]
        lc = tm // MLSTM_CHUNKS
        m_chunks = []
        for c in range(MLSTM_CHUNKS):
            rows = slice(c * lc, (c + 1) * lc)
            m_chunks.append(functools.partial(
                _mlstm_chunk, qm_b[rows], km_b[rows], vm_b[rows], om_b[rows], gif[rows],
                gif_t[:, rows], st_s, m_s, hm_ref.at[rows], lc))
        m_live, m_started = [], {}
        m_units = [(c, pairs) for c in range(MLSTM_CHUNKS) for pairs in ((0, 1), (2, 3))]

        def m_advance(i):
            if 0 < i <= len(m_units):
                phases, pairs = m_live.pop()
                phases[2](pairs)
            if i < len(m_units):
                c, pairs = m_units[i]
                if c not in m_started:
                    m_started[c] = m_chunks[c]()
                phases = m_started[c]
                phases[0](pairs)
                phases[1](pairs)
                m_live.append((phases, pairs))

        m_advance(0)

    q_refs = (q0_ref, q1_ref, q2_ref)
    k_refs = (k0_ref, k1_ref, k2_ref)
    v_refs = (v0_ref, v1_ref, v2_ref)
    for g in range(N_GROUPS):
        d = dils[g]
        hg = permuted_h(d)
        ra, rm, rp = (permuted_tab(r, d) for r in (ra_ref, rm_ref, rp_ref))
        cq, ck, cv = (t * _ATT_W + g * GROUP_W for t in range(3))
        zq = jnp.dot(hg, w_ref[:, cq:cq + GROUP_W], preferred_element_type=F32)
        zk = jnp.dot(hg, w_ref[:, ck:ck + GROUP_W], preferred_element_type=F32)
        zv = jnp.dot(hg, w_ref[:, cv:cv + GROUP_W], preferred_element_type=F32)
        ssq, ssk = head_sumsq(zq), head_sumsq(zk)
        store_group(v_refs[g], zv, d)
        store_group(q_refs[g], norm_rope(zq, ssq, qg_ref[...], ra, rm, rp), d)
        store_group(k_refs[g], norm_rope(zk, ssk, kg_ref[...], ra, rm, rp), d)

        if not plain:
            m_advance(g + 1)

    if plain:
        segments = ((qm_ref, _C_QM, M_QK_W), (km_ref, _C_KM, M_QK_W), (vm_ref, _C_VM, M_V_W),
                    (om_ref, _C_OM, M_V_W), (ga_ref, _C_GA, D_MODEL), (gb_ref, _C_GB, D_MODEL))
        zg = jnp.dot(h_nat, wgif_ref[...], preferred_element_type=F32)
        gif_ref[...] = zg + bif_ref[...]
    else:
        segments = ((ga_ref, _C_GA, D_MODEL), (gb_ref, _C_GB, D_MODEL))
    for si, (ref, c0, width) in enumerate(segments):
        for cc in range(0, width, GROUP_W):
            ref[:, cc:cc + GROUP_W] = seg(c0 + cc, GROUP_W).astype(ref.dtype)
        if not plain and si == 0:
            m_advance(N_GROUPS + 1)
            st_ref[...] = st_s[...]
            mo_ref[...] = m_s[...]


def _proj(x2d, seq_len, dils, tm, plain, g1, w_perm, w_gif, bif, qg, kg, rope_tabs, gmat):
    m_rows, d_model = x2d.shape
    nb = m_rows // seq_len
    tiles_per_seq = seq_len // tm
    grid = (m_rows // tm,)
    row_spec = lambda w: pl.BlockSpec((tm, w), lambda i: (i, 0))
    tab_spec = pl.BlockSpec((tm, LANES), lambda i: (i % tiles_per_seq, 0))
    sds = jax.ShapeDtypeStruct

    widths = (M_QK_W, M_QK_W, M_V_W, M_V_W, D_MODEL, D_MODEL)
    scratch = [pltpu.VMEM((d_model // LANES, tm, LANES), F32)]
    if plain:
        out_shape = ([sds((m_rows, GROUP_W), F32)] * 9 + [sds((m_rows, w), F32) for w in widths]
                     + [sds((m_rows, LANES), F32)])
        out_specs = ([row_spec(GROUP_W)] * 9 + [row_spec(w) for w in widths] + [row_spec(LANES)])
    else:
        def grp_spec(d):
            return pl.BlockSpec((None, d, PAIRS, tm // d, LANES),
                                lambda i: (i // tiles_per_seq, 0, 0, i % tiles_per_seq, 0))
        grp_shape = lambda d: sds((nb, d, PAIRS, seq_len // d, LANES), BF16)
        state_spec = lambda *dims: pl.BlockSpec((None,) + dims,
                                                lambda i: (i // tiles_per_seq,) + (0,) * len(dims))
        st_dims, m_dims = (M_HEADS // 2, LANES, 2 * LANES), (M_HEADS, LANES)
        row_w = (M_V_W, D_MODEL, D_MODEL)
        out_shape = ([grp_shape(d) for d in dils] * 3 + [sds((m_rows, w), BF16) for w in row_w]
                     + [sds((nb,) + st_dims, F32), sds((nb,) + m_dims, F32)])
        out_specs = ([grp_spec(d) for d in dils] * 3 + [row_spec(w) for w in row_w]
                     + [state_spec(*st_dims), state_spec(*m_dims)])
        scratch += [pltpu.VMEM(st_dims, F32), pltpu.VMEM(m_dims, F32)]
    in_specs = [row_spec(d_model), _const_spec((1, d_model)), _const_spec(w_perm.shape),
                _const_spec(w_gif.shape), _const_spec((1, LANES)), _const_spec((1, GROUP_W)),
                _const_spec((1, GROUP_W)), tab_spec, tab_spec, tab_spec, _const_spec(gmat.shape)]
    return pl.pallas_call(
        functools.partial(_proj_kernel, tm=tm, dils=dils, plain=plain,
                          tiles_per_seq=tiles_per_seq),
        grid=grid, in_specs=in_specs, out_specs=out_specs, out_shape=out_shape,
        scratch_shapes=scratch,
        compiler_params=pltpu.CompilerParams(dimension_semantics=("arbitrary",),
                                             vmem_limit_bytes=VMEM_LIMIT),
        name="proj",
    )(x2d, g1, w_perm, w_gif, bif, qg, kg, *rope_tabs, gmat)


ATT_STEP = SPAN * max(DILATIONS)
FAR_PITCH = SPAN + 8


def _attn_kernel(q0, k0, v0, kh0, vh0, q1, k1, v1, kh1, vh1, q2, k2, v2, kh2, vh2,
                 *rest, n_cast):
    w_f32, o_ref, w_bf16 = rest[:n_cast], rest[n_cast], rest[n_cast + 1:2 * n_cast + 1]
    acc_s, m_s, l_s, far_s = rest[2 * n_cast + 1:]
    j = pl.program_id(1)
    for src, dst in zip(w_f32, w_bf16):
        dst[...] = src[...].astype(dst.dtype)

    def window(cur, halo, r, bi):
        prev = halo[r] if bi == 0 else cur[r, (bi - 1) * SPAN:bi * SPAN, :]
        return jnp.concatenate([prev, cur[r, bi * SPAN:(bi + 1) * SPAN, :]], axis=0)

    qi = lax.broadcasted_iota(jnp.int32, (SPAN, 2 * SPAN), 0)
    ci = lax.broadcasted_iota(jnp.int32, (SPAN, 2 * SPAN), 1)
    band = (ci >= qi) & (ci <= qi + SPAN)
    bias_band = jnp.where(band, 0.0, NEG).astype(F32)
    bias_first = jnp.where(band & (ci >= SPAN), 0.0, NEG).astype(F32)
    lane_q = lax.broadcasted_iota(jnp.int32, (SPAN, LANES), 1)
    lane_kv = lax.broadcasted_iota(jnp.int32, (2 * SPAN, LANES), 1)
    halves_q = (lane_q < HEAD_DIM, lane_q >= HEAD_DIM)
    halves_kv = (lane_kv < HEAD_DIM, lane_kv >= HEAD_DIM)

    d0, d1, d2 = DILATIONS
    units = (
        [(0, bi, pl.ds(bi * SPAN, SPAN)) for bi in range(ATT_STEP // d0 // SPAN)],
        [(r, bi, pl.ds(bi * SPAN * d1 + r, SPAN, stride=d1))
         for r in range(d1) for bi in range(ATT_STEP // d1 // SPAN)],
        [(r, 0, pl.ds(r * FAR_PITCH, SPAN)) for r in range(d2)],
    )
    q_refs = (q0, q1, q2)
    kv_refs = ((k0, kh0, v0, vh0), (k1, kh1, v1, vh1), (k2, kh2, v2, vh2))

    def scores(g):
        k_cur, k_halo = kv_refs[g][:2]
        out = []
        for r, bi, _ in units[g]:
            q2_ = q_refs[g][r, bi * SPAN:(bi + 1) * SPAN, :]
            kk = window(k_cur, k_halo, r, bi)
            bias = jnp.where(j == 0, bias_first, bias_band) if bi == 0 else bias_band
            ss = []
            for hh in range(2):
                qa = jnp.where(halves_q[hh], q2_, jnp.zeros_like(q2_))
                ss.append(lax.dot_general(qa, kk, (((1,), (1,)), ((), ())),
                                          preferred_element_type=F32) + bias)
            out.append(ss)
        return out

    def finish(g, all_ss):
        v_cur, v_halo = kv_refs[g][2:]
        probs = []
        for ss in all_ss:
            mxs = [jnp.max(s, axis=-1, keepdims=True) for s in ss]
            ps = [jnp.exp2(s - mx) for s, mx in zip(ss, mxs)]
            ls = [jnp.sum(p, axis=-1, keepdims=True) for p in ps]
            probs.append(([p.astype(BF16) for p in ps], mxs, ls))
        for (r, bi, sl), (ps, mxs, ls) in zip(units[g], probs):
            vv = window(v_cur, v_halo, r, bi)
            acc = None
            for hh in range(2):
                vh = jnp.where(halves_kv[hh], vv, jnp.zeros_like(vv))
                a = jnp.dot(ps[hh], vh, preferred_element_type=F32)
                acc = a if acc is None else acc + a
            stats = (acc, jnp.where(halves_q[0], mxs[0], mxs[1]),
                     jnp.where(halves_q[0], ls[0], ls[1]))
            if g == N_GROUPS - 1:
                for k, val in enumerate(stats):
                    far_s[k, sl, :] = val
            else:
                for ref, val in zip((acc_s, m_s, l_s), stats):
                    ref[g, sl, :] = val

    for g in range(N_GROUPS):
        finish(g, scores(g))

    def far_rows(k, ci_):
        sub = 8
        pieces = [far_s[k, pl.ds(a * sub * FAR_PITCH + ci_ * (SPAN // d2) + jj, sub,
                                 stride=FAR_PITCH), :]
                  for jj in range(SPAN // d2) for a in range(d2 // sub)]
        return jnp.concatenate(pieces, axis=0)

    def combine(ci_, c):
        sl = pl.ds(pl.multiple_of(ci_ * SPAN, SPAN), SPAN)
        accs = [acc_s[0, sl, :], acc_s[1, sl, :], far_rows(0, ci_)]
        ms = [m_s[0, sl, :], m_s[1, sl, :], far_rows(1, ci_)]
        ls = [l_s[0, sl, :], l_s[1, sl, :], far_rows(2, ci_)]
        mm = jnp.maximum(jnp.maximum(ms[0], ms[1]), ms[2])
        es = [jnp.exp2(m - mm) for m in ms]
        num = es[0] * accs[0] + es[1] * accs[1] + es[2] * accs[2]
        den = es[0] * ls[0] + es[1] * ls[1] + es[2] * ls[2]
        o_ref[sl, :] = (num / den).astype(o_ref.dtype)
        return c
    lax.fori_loop(0, ATT_STEP // SPAN, combine, 0)


def _attention(qs, ks, vs, seq_len, cast_ws):
    nb = qs[0].shape[0]
    steps = seq_len // ATT_STEP
    in_specs, args, scratch = [], [], []
    for g, d in enumerate(DILATIONS):
        rows = ATT_STEP // d
        cur = pl.BlockSpec((None, d, None, rows, LANES), lambda b, j, p: (b, 0, p, j, 0))
        ratio = rows // SPAN
        halo = pl.BlockSpec((None, d, None, SPAN, LANES),
                            lambda b, j, p, ratio=ratio: (b, 0, p, jnp.maximum(j * ratio - 1, 0), 0))
        in_specs += [cur, cur, cur, halo, halo]
        args += [qs[g], ks[g], vs[g], ks[g], vs[g]]
    scratch += [pltpu.VMEM((N_GROUPS - 1, ATT_STEP, LANES), F32)] * 3
    scratch += [pltpu.VMEM((3, max(DILATIONS) * FAR_PITCH, LANES), F32)]
    n_slices = max(nb * steps * PAIRS // 2, 1)
    slice_of = lambda b, j, p: (((b * steps + j) * PAIRS + p) // 2, 0)
    w_specs = []
    for w in cast_ws:
        assert w.shape[0] % (16 * n_slices) == 0, (w.shape, n_slices)
        w_specs.append(pl.BlockSpec((w.shape[0] // n_slices, w.shape[1]), slice_of))
    outs = pl.pallas_call(
        functools.partial(_attn_kernel, n_cast=len(cast_ws)),
        grid=(nb, steps, PAIRS),
        in_specs=in_specs + w_specs,
        out_specs=[pl.BlockSpec((None, None, ATT_STEP, LANES), lambda b, j, p: (b, p, j, 0))]
                  + w_specs,
        out_shape=[jax.ShapeDtypeStruct((nb, PAIRS, seq_len, LANES), BF16)]
                  + [jax.ShapeDtypeStruct(w.shape, BF16) for w in cast_ws],
        scratch_shapes=scratch,
        compiler_params=pltpu.CompilerParams(
            dimension_semantics=("arbitrary", "arbitrary", "arbitrary"),
            vmem_limit_bytes=VMEM_LIMIT),
        name="attn",
    )(*args, *cast_ws)
    return outs[0], outs[1:]


def _kv_tail_kernel(k0, v0, k1, v1, k2, v2, o0, o1, o2, nat_ref):
    for (k_ref, v_ref, o_ref, d) in ((k0, v0, o0, DILATIONS[0]), (k1, v1, o1, DILATIONS[1]),
                                     (k2, v2, o2, DILATIONS[2])):
        for kvi, ref in enumerate((k_ref, v_ref)):
            for p in range(PAIRS):
                rows = slice(p * LANES, (p + 1) * LANES)
                if d == 1:
                    o_ref[kvi, rows, :] = ref[0, p].astype(F32).T
                    continue
                for r in range(d):
                    nat_ref[pl.ds(r, SPAN, stride=d), :] = ref[r, p].astype(F32)
                for c in range(d):
                    o_ref[kvi, rows, c * SPAN:(c + 1) * SPAN] = nat_ref[c * SPAN:(c + 1) * SPAN, :].T


def _kv_tails(ks, vs, seq_len):
    nb = ks[0].shape[0]
    in_specs, args, out_specs, out_shape = [], [], [], []
    for g, d in enumerate(DILATIONS):
        last = seq_len // d // SPAN - 1
        spec = pl.BlockSpec((None, d, PAIRS, SPAN, LANES), lambda b, last=last: (b, 0, 0, last, 0))
        in_specs += [spec, spec]
        args += [ks[g], vs[g]]
        out_specs.append(pl.BlockSpec((None, 2, GROUP_W, SPAN * d), lambda b: (b, 0, 0, 0)))
        out_shape.append(jax.ShapeDtypeStruct((nb, 2, GROUP_W, SPAN * d), F32))
    return pl.pallas_call(
        _kv_tail_kernel,
        grid=(nb,), in_specs=in_specs, out_specs=out_specs, out_shape=out_shape,
        scratch_shapes=[pltpu.VMEM((SPAN * max(DILATIONS), LANES), F32)],
        compiler_params=pltpu.CompilerParams(dimension_semantics=("arbitrary",),
                                             vmem_limit_bytes=VMEM_LIMIT),
        name="kv_tail",
    )(*args)


def _mlstm_chunk(q2d, k2d, v2d, om2d, gif, gt, st_s, m_s, hm_ref, lc):
    lf_c = _log_sigmoid(gif)
    lf_r = _log_sigmoid(gt[M_HEADS:2 * M_HEADS, :])
    row = lax.broadcasted_iota(jnp.int32, (lc, lc), 0)
    col = lax.broadcasted_iota(jnp.int32, (lc, lc), 1)
    causal = col <= row
    tril = causal.astype(F32)
    triu = (row <= col).astype(F32)
    b_c = jnp.dot(tril, lf_c, precision=lax.Precision.HIGHEST, preferred_element_type=F32)
    b_r = jnp.dot(lf_r, triu, precision=lax.Precision.HIGHEST, preferred_element_type=F32)
    lane = lax.broadcasted_iota(jnp.int32, (lc, LANES), 1)
    halves = (lane < M_DQK, lane >= M_DQK)
    sub = lax.broadcasted_iota(jnp.int32, (LANES, 1), 0)
    ones_blk = jnp.ones((lc, LANES), BF16)
    qscale = M_DQK ** -0.5

    states, p1, p2 = {}, {}, {}
    all_pairs = tuple(range(M_HEADS // 2))
    heads_of = lambda pairs: [2 * p + hh for p in pairs for hh in range(2)]

    def phase1(pairs=all_pairs):
        for p in pairs:
            states[p] = st_s[p]
        for h in heads_of(pairs):
            p, hh = divmod(h, 2)
            q2_ = q2d[:, p * LANES:(p + 1) * LANES]
            k2_ = k2d[:, p * LANES:(p + 1) * LANES]
            qa = jnp.where(halves[hh], q2_, jnp.zeros_like(q2_)) * jnp.asarray(qscale, q2_.dtype)
            s = lax.dot_general(qa, k2_, (((1,), (1,)), ((), ())), preferred_element_type=F32)
            qc = jnp.dot(qa, states[p].astype(BF16), preferred_element_type=F32)
            p1[h] = (s, qc)

    def phase2(pairs=all_pairs):
        for h in heads_of(pairs):
            p, hh = divmod(h, 2)
            s, qc = p1[h]
            bcol = b_c[:, M_HEADS + h:M_HEADS + h + 1]
            igcol = gif[:, h:h + 1]
            a_row = gt[h:h + 1, :] - b_r[h:h + 1, :]
            amat = jnp.where(causal, a_row, NEG)
            m_prev = m_s[h:h + 1, 0:1]
            g_t = jnp.maximum(m_prev, jnp.max(amat, axis=-1, keepdims=True))
            wqk = (jnp.exp(amat - g_t) * s).astype(BF16)
            dec = jnp.exp(m_prev - g_t)
            m_t = bcol + g_t
            m_new = m_t[lc - 1:lc, :]
            b_last = bcol[lc - 1:lc, :]
            dstate = jnp.exp(b_last + m_prev - m_new)
            ws = jnp.exp(b_last - bcol + igcol - m_new)
            k2_ = k2d[:, p * LANES:(p + 1) * LANES]
            ka = jnp.where(halves[hh], k2_, jnp.zeros_like(k2_))
            kws = (ka.astype(F32) * ws).astype(BF16)
            m_s[h:h + 1, :] = jnp.broadcast_to(m_new, (1, LANES))
            p2[h] = (wqk, dec * qc, jnp.exp(-m_t), dstate, kws)

    def phase3(pairs=all_pairs):
        upds = {}
        for h in heads_of(pairs):
            wqk, dqc, floor, _, kws = p2[h]
            v1 = jnp.concatenate([v2d[:, h * M_DV:(h + 1) * M_DV], ones_blk], axis=1)
            num_den = dqc + jnp.dot(wqk, v1, preferred_element_type=F32)
            num = num_den[:, :M_DV]
            den = num_den[:, M_DV:]
            hval = num / jnp.maximum(jnp.abs(den), floor)
            gate = jax.nn.sigmoid(om2d[:, h * M_DV:(h + 1) * M_DV].astype(F32))
            hm_ref[:, h * M_DV:(h + 1) * M_DV] = (gate * hval).astype(hm_ref.dtype)
            upds[h] = lax.dot_general(kws, v1, (((0,), (0,)), ((), ())),
                                      preferred_element_type=F32)
        for p in pairs:
            drow = jnp.where(sub < M_DQK, p2[2 * p][3], p2[2 * p + 1][3])
            st_s[p] = drow * states[p] + upds[2 * p] + upds[2 * p + 1]

    return phase1, phase2, phase3


FF_CHUNK = 256


DEC_HEADS = HEADS // 2
DEC_ROWS = DEC_HEADS * HEAD_DIM


def _decode_attn_step(b, qkv_ref, c_refs, o_refs, ot_ref):
    sel = lax.broadcasted_iota(jnp.int32, (DEC_ROWS, LANES), 1) == b
    scale = HEAD_DIM ** -0.5

    def column(i):
        return jnp.sum(jnp.where(sel, qkv_ref[i], 0.0), axis=1, keepdims=True)

    def head(col, h):
        return col[h * HEAD_DIM:(h + 1) * HEAD_DIM, :]

    qc = [column(g) * scale for g in range(N_GROUPS)]
    kn = [column(N_GROUPS + g) for g in range(N_GROUPS)]
    vn = [column(2 * N_GROUPS + g) for g in range(N_GROUPS)]
    cols = []
    for h in range(DEC_HEADS):
        scores, m_h = [], None
        for g, d in enumerate(DILATIONS):
            wb = c_refs[g].shape[-1]
            qh = head(qc[g], h)
            s = jnp.sum(c_refs[g][0, h] * qh, axis=0, keepdims=True)
            pos = lax.broadcasted_iota(jnp.int32, (1, wb), 1)
            s = jnp.where((pos & (d - 1)) == 0, s, NEG)
            s_new = jnp.sum(head(kn[g], h) * qh, axis=0, keepdims=True)
            m_g = jnp.maximum(jnp.max(s, axis=1, keepdims=True), s_new)
            m_h = m_g if m_h is None else jnp.maximum(m_h, m_g)
            scores.append((s, s_new))
        l_h = jnp.zeros((1, 1), F32)
        acc = jnp.zeros((HEAD_DIM, 1), F32)
        for g in range(N_GROUPS):
            s, s_new = scores[g]
            p = jnp.exp(s - m_h)
            p_new = jnp.exp(s_new - m_h)
            l_h = l_h + jnp.sum(p, axis=1, keepdims=True) + p_new
            acc = acc + jnp.sum(c_refs[g][1, h] * p, axis=1, keepdims=True) + p_new * head(vn[g], h)
        cols.append(acc / l_h)
    ot_ref[...] = jnp.broadcast_to(jnp.concatenate(cols, axis=0), ot_ref.shape)
    for g in range(N_GROUPS):
        wb = c_refs[g].shape[-1]
        last = lax.broadcasted_iota(jnp.int32, (HEAD_DIM, wb), 1) == wb - 1
        for kvi, new in ((0, kn[g]), (1, vn[g])):
            for h in range(DEC_HEADS):
                rolled = pltpu.roll(c_refs[g][kvi, h], wb - 1, 1)
                o_refs[g][kvi, h] = jnp.where(last, head(new, h), rolled)


def _merge_ffn_kernel(x_ref, o_ref, hm_ref, ga_ref, gb_ref, wa_ref, wm_ref, wo_ref, g2_ref,
                      wg_ref, wu_ref, wd_ref, *rest, with_decode):
    if with_decode:
        qkv_ref, c0, c1, c2, y_ref, o0, o1, o2, ot_ref = rest
        _decode_attn_step(pl.program_id(0) // 2, qkv_ref, (c0, c1, c2), (o0, o1, o2), ot_ref)
    else:
        (y_ref,) = rest
    yb = jnp.dot(hm_ref[...].astype(BF16), wm_ref[...], preferred_element_type=F32)
    o_att = jnp.concatenate([o_ref[p] for p in range(PAIRS)], axis=1).astype(BF16)
    ya = jnp.dot(o_att, wa_ref[...], preferred_element_type=F32)
    mixed = (jax.nn.sigmoid(ga_ref[...].astype(F32)) * ya
             + jax.nn.sigmoid(gb_ref[...].astype(F32)) * yb)
    x2 = x_ref[...] + jnp.dot(mixed.astype(BF16), wo_ref[...], preferred_element_type=F32)
    h2 = (x2 * lax.rsqrt(jnp.mean(x2 * x2, axis=-1, keepdims=True) + NORM_EPS)
          * g2_ref[...]).astype(BF16)
    acc = x2
    d_ff = wg_ref.shape[1]

    def gate_up(c):
        gt = jnp.dot(h2, wg_ref[:, c:c + FF_CHUNK], preferred_element_type=F32)
        up = jnp.dot(h2, wu_ref[:, c:c + FF_CHUNK], preferred_element_type=F32)
        return gt, up

    chunks = list(range(0, d_ff, FF_CHUNK))
    nxt = gate_up(chunks[0])
    for i, c in enumerate(chunks):
        gt, up = nxt
        if i + 1 < len(chunks):
            nxt = gate_up(chunks[i + 1])
        ff = (gt * jax.nn.sigmoid(gt) * up).astype(BF16)
        acc = acc + jnp.dot(ff, wd_ref[c:c + FF_CHUNK, :], preferred_element_type=F32)
    y_ref[...] = acc


def _merge_ffn(x2d, o_att, hm, ga, gb, wa, wm, wo, g2, wg, wu, wd, seq_len, tm, decode=None):
    m_rows, d_model = x2d.shape
    tiles_per_seq = seq_len // tm
    steps = m_rows // tm
    row = lambda w: pl.BlockSpec((tm, w), lambda i: (i, 0))
    o_spec = pl.BlockSpec((None, PAIRS, tm, LANES),
                          lambda i: (i // tiles_per_seq, 0, i % tiles_per_seq, 0))
    in_specs = [row(d_model), o_spec, row(M_V_W), row(D_MODEL), row(D_MODEL),
                _const_spec(wa.shape), _const_spec(wm.shape), _const_spec(wo.shape),
                _const_spec((1, d_model)), _const_spec(wg.shape), _const_spec(wu.shape),
                _const_spec(wd.shape)]
    args = [x2d, o_att, hm, ga, gb, wa, wm, wo, g2, wg, wu, wd]
    out_specs = [row(d_model)]
    out_shape = [jax.ShapeDtypeStruct((m_rows, d_model), F32)]
    if decode is not None:
        qkv_t, caches_t = decode
        db = caches_t[0].shape[0]
        assert steps == 2 * db, (steps, db)
        in_specs.append(pl.BlockSpec((qkv_t.shape[0], DEC_ROWS, LANES), lambda i: (0, i % 2, 0)))
        args.append(qkv_t)
        for c in caches_t:
            spec = pl.BlockSpec((None, 2, DEC_HEADS, HEAD_DIM, c.shape[-1]),
                                lambda i: (i // 2, 0, i % 2, 0, 0))
            in_specs.append(spec)
            args.append(c)
            out_specs.append(spec)
            out_shape.append(jax.ShapeDtypeStruct(c.shape, F32))
        out_specs.append(pl.BlockSpec((None, None, DEC_ROWS, LANES), lambda i: (i // 2, i % 2, 0, 0)))
        out_shape.append(jax.ShapeDtypeStruct((db, 2, DEC_ROWS, LANES), F32))
    outs = pl.pallas_call(
        functools.partial(_merge_ffn_kernel, with_decode=decode is not None),
        grid=(steps,),
        in_specs=in_specs, out_specs=out_specs, out_shape=out_shape,
        compiler_params=pltpu.CompilerParams(dimension_semantics=("arbitrary",),
                                             vmem_limit_bytes=VMEM_LIMIT),
        name="merge_ffn",
    )(*args)
    if decode is None:
        return outs[0]
    return outs[0], outs[1:4], outs[4]


def _sample_prep_kernel(*refs):
    for i, ref in enumerate(refs[:-1]):
        refs[-1][i] = ref[...].T


def _sample_prep(arrs):
    vm = pl.BlockSpec(memory_space=pltpu.VMEM)
    return pl.pallas_call(
        _sample_prep_kernel, in_specs=[vm] * len(arrs), out_specs=vm,
        out_shape=jax.ShapeDtypeStruct((len(arrs), GROUP_W, LANES), F32),
        name="sample_prep",
    )(*arrs)


def _sample_mlstm_kernel(q_ref, k_ref, v_ref, om_ref, gt_ref, mt_ref, n_ref, c_ref, rexp_ref,
                         hm_ref, co_ref, nt_ref, mo_ref, *, db):
    nh, dqk = M_HEADS, M_DQK
    hi = lax.Precision.HIGHEST
    rexp = rexp_ref[...]
    q_t = q_ref[...].T * (dqk ** -0.5)
    k_t = k_ref[...].T
    n_t = n_ref[...].T
    ig = gt_ref[0:nh, :]
    lf = _log_sigmoid(gt_ref[nh:2 * nh, :])
    m_prev = mt_ref[...]
    m_new = jnp.maximum(lf + m_prev, ig)
    w8 = jnp.exp(ig - m_new)
    dec8 = jnp.exp(lf + m_prev - m_new)
    head_sum = lambda a: lax.dot_general(rexp, a, (((0,), (0,)), ((), ())), precision=hi,
                                         preferred_element_type=F32)
    expand = lambda a: jnp.dot(rexp, a, precision=hi, preferred_element_type=F32)
    qk8 = head_sum(q_t * k_t)
    qn8 = head_sum(q_t * n_t)
    wqk8 = w8 * qk8
    den8 = dec8 * qn8 + wqk8
    inv8 = 1.0 / jnp.maximum(jnp.abs(den8), jnp.exp(-m_new))
    dec_x = expand(dec8)
    w_x = expand(w8)
    nt_ref[...] = dec_x * n_t + w_x * k_t
    mo_ref[...] = m_new
    wk_x = w_x * k_t
    for b in range(db):
        cb = c_ref[b].reshape(nh * dqk, M_DV)
        v_b = v_ref[b]
        v_x = jnp.concatenate([jnp.broadcast_to(v_b[h:h + 1, :], (dqk, M_DV)) for h in range(nh)],
                              axis=0)
        co_ref[b] = (dec_x[:, b:b + 1] * cb + wk_x[:, b:b + 1] * v_x).reshape(nh, dqk, M_DV)
        qc = jnp.sum((q_t[:, b:b + 1] * cb).reshape(nh, dqk, M_DV), axis=1)
        num = dec8[:, b:b + 1] * qc + wqk8[:, b:b + 1] * v_b
        hm_ref[b] = jax.nn.sigmoid(om_ref[b]) * (num * inv8[:, b:b + 1])


def _sample_mlstm(q_t, k_t, v3, om3, gif_t, m_t, n_pad, c_state, rexp, db):
    vm = pl.BlockSpec(memory_space=pltpu.VMEM)
    return pl.pallas_call(
        functools.partial(_sample_mlstm_kernel, db=db),
        in_specs=[vm] * 9,
        out_specs=[vm] * 4,
        out_shape=[jax.ShapeDtypeStruct((db, M_HEADS, M_DV), F32),
                   jax.ShapeDtypeStruct(c_state.shape, F32),
                   jax.ShapeDtypeStruct((M_HEADS * M_DQK, LANES), F32),
                   jax.ShapeDtypeStruct((M_HEADS, LANES), F32)],
        compiler_params=pltpu.CompilerParams(vmem_limit_bytes=VMEM_LIMIT),
        name="sample_mlstm",
    )(q_t, k_t, v3, om3, gif_t, m_t, n_pad, c_state, rexp)


def _rope_tables(pos):
    half = ROT_DIM // 2
    pos = np.asarray(pos, np.float32)
    inv_freq = np.exp(np.float32(-math.log(ROPE_THETA))
                      * np.arange(0, ROT_DIM, 2, dtype=np.float32) / np.float32(ROT_DIM))
    ang = (pos[:, None] * inv_freq[None, :]).astype(np.float32)
    cos, sin = np.cos(ang).astype(np.float32), np.sin(ang).astype(np.float32)
    t = pos.shape[0]
    rest = HEAD_DIM - ROT_DIM
    a = np.concatenate([cos, cos, np.ones((t, rest), np.float32)], axis=1)
    bm = np.concatenate([-sin, np.zeros((t, HEAD_DIM - half), np.float32)], axis=1)
    bp = np.concatenate([np.zeros((t, half), np.float32), sin, np.zeros((t, rest), np.float32)],
                        axis=1)
    return tuple(jnp.asarray(np.concatenate([x, x], axis=1)) for x in (a, bm, bp))


W_PREP_BLK = 512


def _w_prep_kernel(wt_ref, o_ref):
    o_ref[...] = wt_ref[...].T.astype(o_ref.dtype)


def _w_prep(w_in):
    d_model = w_in.shape[0]
    wt = w_in.T
    n_head = _C_GA // W_PREP_BLK
    n_blk = _W_COLS // W_PREP_BLK

    sub = 8

    def src_row(j):
        return (j * (W_PREP_BLK // sub) + jnp.where(j < n_head, 0, _GIF_COLS // sub)) * sub

    return pl.pallas_call(
        _w_prep_kernel,
        grid=(n_blk,),
        in_specs=[pl.BlockSpec((pl.Element(W_PREP_BLK), pl.Element(d_model)),
                               lambda j: (src_row(j), 0))],
        out_specs=pl.BlockSpec((d_model, W_PREP_BLK), lambda j: (0, j)),
        out_shape=jax.ShapeDtypeStruct((d_model, _W_COLS), BF16),
        compiler_params=pltpu.CompilerParams(dimension_semantics=("arbitrary",),
                                             vmem_limit_bytes=VMEM_LIMIT),
        name="w_prep",
    )(wt)


def kernel(x_prompt, x_sample, cache_kv_w128, cache_kv_w512, cache_kv_w2048, state_mlstm_C, state_mlstm_n, state_mlstm_m, norm1_g, w_in, b_if, q_norm_g, k_norm_g, w_att_out, w_m_out, w_o, norm2_g, w_gate, w_up, w_down):
    nb, seq_len, d_model = x_prompt.shape
    db, dec_seq, _ = x_sample.shape
    assert dec_seq == 1 and d_model == D_MODEL and seq_len % ATT_STEP == 0 and db <= LANES
    caches = (cache_kv_w128, cache_kv_w512, cache_kv_w2048)

    w_perm = _w_prep(w_in)
    w_gif = jnp.zeros((d_model, LANES), BF16).at[:, :_GIF_COLS].set(
        w_in[:, _C_GA:_C_GA + _GIF_COLS].astype(BF16))
    g1 = norm1_g.reshape(1, d_model)
    g2 = norm2_g.reshape(1, d_model)
    bif = jnp.concatenate([b_if, jnp.zeros((LANES - b_if.shape[0],), F32)]).reshape(1, LANES)
    qg = jnp.tile(q_norm_g, HEADS).reshape(1, GROUP_W)
    kg = jnp.tile(k_norm_g, HEADS).reshape(1, GROUP_W)
    hid = np.arange(GROUP_W // 2) // HEAD_DIM
    gmat = jnp.asarray(hid[:, None] == hid[None, :], dtype=BF16)

    m_rows = nb * seq_len
    x2d = x_prompt.reshape(m_rows, d_model)
    tabs_p = _rope_tables(np.arange(seq_len))
    qg_p = qg * (HEAD_DIM ** -0.5 * math.log2(math.e))
    outs = _proj(x2d, seq_len, DILATIONS, 256, False, g1, w_perm, w_gif, bif, qg_p, kg, tabs_p, gmat)
    qs, ks, vs = outs[0:3], outs[3:6], outs[6:9]
    hm, ga, gb, st_p, m_p = outs[9:14]

    o_att, (wa, wm, wo, wg, wu, wd) = _attention(
        qs, ks, vs, seq_len, (w_att_out, w_m_out, w_o, w_gate, w_up, w_down))

    x_s = jnp.zeros((LANES, d_model), F32).at[:db].set(x_sample.reshape(db, d_model))
    tabs_s = _rope_tables(np.full((LANES,), PAST_LEN))
    outs_s = _proj(x_s, LANES, (1, 1, 1), LANES, True, g1, w_perm, w_gif, bif, qg, kg, tabs_s, gmat)
    qkv_t = _sample_prep(outs_s[0:9])
    caches_t = [c.transpose(0, 2, 3, 4, 1) for c in caches]
    y_prompt, kv_st, o_att_cols = _merge_ffn(x2d, o_att, hm, ga, gb, wa, wm, wo, g2, wg, wu, wd,
                                             seq_len, 256, decode=(qkv_t, caches_t))
    y_prompt = y_prompt.reshape(nb, seq_len, d_model)
    kv_s = [c.transpose(0, 4, 1, 2, 3) for c in kv_st]

    tails = _kv_tails(ks, vs, seq_len)
    kv_p = [t.reshape(nb, 2, HEADS, HEAD_DIM, t.shape[-1]).transpose(0, 4, 1, 2, 3) for t in tails]
    c_p = st_p[..., :M_DV].reshape(nb, M_HEADS, M_DQK, M_DV)
    n_p = st_p[..., M_DV].reshape(nb, M_HEADS, M_DQK)
    m_pr = m_p[:, :, 0]

    qm_t, km_t, vm_s, om_s, ga_s, gb_s, gif_s = outs_s[9:16]
    gif_ts = gif_s.T

    rexp = jnp.asarray(np.arange(M_HEADS * M_DQK)[:, None] // M_DQK == np.arange(M_HEADS)[None, :],
                       dtype=F32)
    m_t = jnp.zeros((M_HEADS, LANES), F32).at[:, :db].set(state_mlstm_m.T)
    n_pad = jnp.zeros((LANES, M_HEADS * M_DQK), F32).at[:db].set(state_mlstm_n.reshape(db, -1))
    hm_s3, c_s, n_t, m_so = _sample_mlstm(
        qm_t, km_t, vm_s[:db].reshape(db, M_HEADS, M_DV), om_s[:db].reshape(db, M_HEADS, M_DV),
        gif_ts[:2 * M_HEADS], m_t, n_pad, state_mlstm_C, rexp, db)
    n_s = n_t.T[:db].reshape(db, M_HEADS, M_DQK)
    m_s = m_so[:, :db].T

    o_att_s = o_att_cols[:, :, :, 0].reshape(db, PAIRS, LANES)
    o_att_sp = jnp.zeros((PAIRS, LANES, LANES), F32).at[:, :db].set(o_att_s.transpose(1, 0, 2))[None]
    hm_sp = jnp.zeros((LANES, M_V_W), F32).at[:db].set(hm_s3.reshape(db, M_V_W))
    y_s = _merge_ffn(x_s, o_att_sp, hm_sp, ga_s, gb_s, wa, wm, wo, g2, wg, wu, wd, LANES, LANES)
    y_sample = y_s[:db].reshape(db, 1, d_model)

    return (y_prompt, y_sample, kv_p[0], kv_p[1], kv_p[2], c_p, n_p, m_pr,
            kv_s[0], kv_s[1], kv_s[2], c_s, n_s, m_s)
```

```python
import functools
import math

import jax
import jax.numpy as jnp
import numpy as np
from jax import lax
from jax.experimental import pallas as pl
from jax.experimental.pallas import tpu as pltpu

F32 = jnp.float32
BF16 = jnp.bfloat16

HEAD_DIM = 64
HEADS = 8
GROUP_W = HEADS * HEAD_DIM
N_GROUPS = 3
WINDOWS = (128, 512, 2048)
DILATIONS = (1, 4, 16)
SPAN = 128
ROT_DIM = 16
ROPE_THETA = 500000.0
M_HEADS = 8
M_DQK = 64
M_DV = 128
M_QK_W = M_HEADS * M_DQK
M_V_W = M_HEADS * M_DV
D_MODEL = 1024
PAST_LEN = 8192
NORM_EPS = 1e-6
NEG = -1e30

LANES = 128
PAIRS = GROUP_W // LANES
VMEM_LIMIT = 56 * 1024 * 1024

_ATT_W = N_GROUPS * GROUP_W
_C_QM = 3 * _ATT_W
_C_KM = _C_QM + M_QK_W
_C_VM = _C_KM + M_QK_W
_C_OM = _C_VM + M_V_W
_C_GA = _C_OM + M_V_W
_C_GB = _C_GA + D_MODEL
_W_COLS = _C_GB + D_MODEL
_GIF_COLS = 2 * M_HEADS


def _const_spec(shape):
    nd = len(shape)
    return pl.BlockSpec(shape, lambda *_: (0,) * nd, pipeline_mode=pl.Buffered(1))


def _log_sigmoid(x):
    return jnp.minimum(x, 0.0) - jnp.log1p(jnp.exp(-jnp.abs(x)))


MLSTM_CHUNKS = 2


def _proj_kernel(x_ref, g1_ref, w_ref, wgif_ref, bif_ref, qg_ref, kg_ref, ra_ref, rm_ref, rp_ref,
                 gm_ref,
                 q0_ref, q1_ref, q2_ref, k0_ref, k1_ref, k2_ref, v0_ref, v1_ref, v2_ref,
                 *rest, tm, dils, plain, tiles_per_seq):
    if plain:
        qm_ref, km_ref, vm_ref, om_ref, ga_ref, gb_ref, gif_ref, hs_ref = rest
    else:
        hm_ref, ga_ref, gb_ref, st_ref, mo_ref, hs_ref, st_s, m_s = rest
    d_model = x_ref.shape[1]
    x = x_ref[...]
    xn = x * lax.rsqrt(jnp.mean(x * x, axis=-1, keepdims=True) + NORM_EPS) * g1_ref[...]
    h_nat = xn.astype(BF16)
    n_slab = d_model // LANES
    if any(d > 1 for d in dils):
        for c in range(n_slab):
            hs_ref[c] = xn[:, c * LANES:(c + 1) * LANES]

    def permuted_h(d):
        if d == 1:
            return h_nat
        n = tm // d
        rows = [jnp.concatenate([hs_ref[c, pl.ds(r, n, stride=d), :] for c in range(n_slab)], axis=1)
                for r in range(d)]
        return jnp.concatenate(rows, axis=0).astype(BF16)

    def permuted_tab(ref, d):
        if d == 1:
            t = ref[...]
        else:
            n = tm // d
            t = jnp.concatenate([ref[pl.ds(r, n, stride=d), :] for r in range(d)], axis=0)
        return jnp.concatenate([t] * PAIRS, axis=1)

    gmat = gm_ref[...]

    def head_sumsq(z):
        zz = (z * z).astype(BF16)
        half = GROUP_W // 2
        return jnp.concatenate(
            [jnp.dot(zz[:, :half], gmat, preferred_element_type=F32),
             jnp.dot(zz[:, half:], gmat, preferred_element_type=F32)], axis=1)

    def norm_rope(z, ss, gain, ra, rm, rp):
        y = z * lax.rsqrt(ss * (1.0 / HEAD_DIM) + NORM_EPS) * gain
        return (y * ra + pltpu.roll(y, GROUP_W - ROT_DIM // 2, 1) * rm
                + pltpu.roll(y, ROT_DIM // 2, 1) * rp)

    def store_group(ref, y, d):
        if plain:
            ref[...] = y.astype(ref.dtype)
            return
        n = tm // d
        for p in range(PAIRS):
            ref[:, p] = y[:, p * LANES:(p + 1) * LANES].reshape(d, n, LANES).astype(ref.dtype)

    def seg(c0, width):
        return jnp.dot(h_nat, w_ref[:, c0:c0 + width], preferred_element_type=F32)

    if not plain:
        @pl.when(pl.program_id(0) % tiles_per_seq == 0)
        def _():
            st_s[...] = jnp.zeros_like(st_s)
            m_s[...] = jnp.zeros_like(m_s)

        gif = jnp.dot(h_nat, wgif_ref[...], preferred_element_type=F32) + bif_ref[...]
        qm_b, km_b = seg(_C_QM, M_QK_W).astype(BF16), seg(_C_KM, M_QK_W).astype(BF16)
        vm_b, om_b = seg(_C_VM, M_V_W).astype(BF16), seg(_C_OM, M_V_W).astype(BF16)
        gif_t = gif.T[:2 * M_HEADS, :]
        lc = tm // MLSTM_CHUNKS
        m_chunks = []
        for c in range(MLSTM_CHUNKS):
            rows = slice(c * lc, (c + 1) * lc)
            m_chunks.append(functools.partial(
                _mlstm_chunk, qm_b[rows], km_b[rows], vm_b[rows], om_b[rows], gif[rows],
                gif_t[:, rows], st_s, m_s, hm_ref.at[rows], lc))
        m_live, m_started = [], {}
        m_units = [(c, pairs) for c in range(MLSTM_CHUNKS) for pairs in ((0, 1), (2, 3))]

        def m_advance(i):
            if 0 < i <= len(m_units):
                phases, pairs = m_live.pop()
                phases[2](pairs)
            if i < len(m_units):
                c, pairs = m_units[i]
                if c not in m_started:
                    m_started[c] = m_chunks[c]()
                phases = m_started[c]
                phases[0](pairs)
                phases[1](pairs)
                m_live.append((phases, pairs))

        m_advance(0)

    q_refs = (q0_ref, q1_ref, q2_ref)
    k_refs = (k0_ref, k1_ref, k2_ref)
    v_refs = (v0_ref, v1_ref, v2_ref)
    for g in range(N_GROUPS):
        d = dils[g]
        hg = permuted_h(d)
        ra, rm, rp = (permuted_tab(r, d) for r in (ra_ref, rm_ref, rp_ref))
        cq, ck, cv = (t * _ATT_W + g * GROUP_W for t in range(3))
        zq = jnp.dot(hg, w_ref[:, cq:cq + GROUP_W], preferred_element_type=F32)
        zk = jnp.dot(hg, w_ref[:, ck:ck + GROUP_W], preferred_element_type=F32)
        zv = jnp.dot(hg, w_ref[:, cv:cv + GROUP_W], preferred_element_type=F32)
        ssq, ssk = head_sumsq(zq), head_sumsq(zk)
        store_group(v_refs[g], zv, d)
        store_group(q_refs[g], norm_rope(zq, ssq, qg_ref[...], ra, rm, rp), d)
        store_group(k_refs[g], norm_rope(zk, ssk, kg_ref[...], ra, rm, rp), d)

        if not plain:
            m_advance(g + 1)

    if plain:
        segments = ((qm_ref, _C_QM, M_QK_W), (km_ref, _C_KM, M_QK_W), (vm_ref, _C_VM, M_V_W),
                    (om_ref, _C_OM, M_V_W), (ga_ref, _C_GA, D_MODEL), (gb_ref, _C_GB, D_MODEL))
        zg = jnp.dot(h_nat, wgif_ref[...], preferred_element_type=F32)
        gif_ref[...] = zg + bif_ref[...]
    else:
        segments = ((ga_ref, _C_GA, D_MODEL), (gb_ref, _C_GB, D_MODEL))
    for si, (ref, c0, width) in enumerate(segments):
        for cc in range(0, width, GROUP_W):
            ref[:, cc:cc + GROUP_W] = seg(c0 + cc, GROUP_W).astype(ref.dtype)
        if not plain and si == 0:
            m_advance(N_GROUPS + 1)
            st_ref[...] = st_s[...]
            mo_ref[...] = m_s[...]


def _proj(x2d, seq_len, dils, tm, plain, g1, w_perm, w_gif, bif, qg, kg, rope_tabs, gmat):
    m_rows, d_model = x2d.shape
    nb = m_rows // seq_len
    tiles_per_seq = seq_len // tm
    grid = (m_rows // tm,)
    row_spec = lambda w: pl.BlockSpec((tm, w), lambda i: (i, 0))
    tab_spec = pl.BlockSpec((tm, LANES), lambda i: (i % tiles_per_seq, 0))
    sds = jax.ShapeDtypeStruct

    widths = (M_QK_W, M_QK_W, M_V_W, M_V_W, D_MODEL, D_MODEL)
    scratch = [pltpu.VMEM((d_model // LANES, tm, LANES), F32)]
    if plain:
        out_shape = ([sds((m_rows, GROUP_W), F32)] * 9 + [sds((m_rows, w), F32) for w in widths]
                     + [sds((m_rows, LANES), F32)])
        out_specs = ([row_spec(GROUP_W)] * 9 + [row_spec(w) for w in widths] + [row_spec(LANES)])
    else:
        def grp_spec(d):
            return pl.BlockSpec((None, d, PAIRS, tm // d, LANES),
                                lambda i: (i // tiles_per_seq, 0, 0, i % tiles_per_seq, 0))
        grp_shape = lambda d: sds((nb, d, PAIRS, seq_len // d, LANES), BF16)
        state_spec = lambda *dims: pl.BlockSpec((None,) + dims,
                                                lambda i: (i // tiles_per_seq,) + (0,) * len(dims))
        st_dims, m_dims = (M_HEADS // 2, LANES, 2 * LANES), (M_HEADS, LANES)
        row_w = (M_V_W, D_MODEL, D_MODEL)
        out_shape = ([grp_shape(d) for d in dils] * 3 + [sds((m_rows, w), BF16) for w in row_w]
                     + [sds((nb,) + st_dims, F32), sds((nb,) + m_dims, F32)])
        out_specs = ([grp_spec(d) for d in dils] * 3 + [row_spec(w) for w in row_w]
                     + [state_spec(*st_dims), state_spec(*m_dims)])
        scratch += [pltpu.VMEM(st_dims, F32), pltpu.VMEM(m_dims, F32)]
    in_specs = [row_spec(d_model), _const_spec((1, d_model)), _const_spec(w_perm.shape),
                _const_spec(w_gif.shape), _const_spec((1, LANES)), _const_spec((1, GROUP_W)),
                _const_spec((1, GROUP_W)), tab_spec, tab_spec, tab_spec, _const_spec(gmat.shape)]
    return pl.pallas_call(
        functools.partial(_proj_kernel, tm=tm, dils=dils, plain=plain,
                          tiles_per_seq=tiles_per_seq),
        grid=grid, in_specs=in_specs, out_specs=out_specs, out_shape=out_shape,
        scratch_shapes=scratch,
        compiler_params=pltpu.CompilerParams(dimension_semantics=("arbitrary",),
                                             vmem_limit_bytes=VMEM_LIMIT),
        name="proj",
    )(x2d, g1, w_perm, w_gif, bif, qg, kg, *rope_tabs, gmat)


ATT_STEP = SPAN * max(DILATIONS)
FAR_PITCH = SPAN + 8


def _attn_kernel(q0, k0, v0, kh0, vh0, q1, k1, v1, kh1, vh1, q2, k2, v2, kh2, vh2,
                 *rest, n_cast):
    w_f32, o_ref, w_bf16 = rest[:n_cast], rest[n_cast], rest[n_cast + 1:2 * n_cast + 1]
    acc_s, m_s, l_s, far_s = rest[2 * n_cast + 1:]
    j = pl.program_id(1)
    for src, dst in zip(w_f32, w_bf16):
        dst[...] = src[...].astype(dst.dtype)

    def window(cur, halo, r, bi):
        prev = halo[r] if bi == 0 else cur[r, (bi - 1) * SPAN:bi * SPAN, :]
        return jnp.concatenate([prev, cur[r, bi * SPAN:(bi + 1) * SPAN, :]], axis=0)

    qi = lax.broadcasted_iota(jnp.int32, (SPAN, 2 * SPAN), 0)
    ci = lax.broadcasted_iota(jnp.int32, (SPAN, 2 * SPAN), 1)
    band = (ci >= qi) & (ci <= qi + SPAN)
    bias_band = jnp.where(band, 0.0, NEG).astype(F32)
    bias_first = jnp.where(band & (ci >= SPAN), 0.0, NEG).astype(F32)
    lane_q = lax.broadcasted_iota(jnp.int32, (SPAN, LANES), 1)
    lane_kv = lax.broadcasted_iota(jnp.int32, (2 * SPAN, LANES), 1)
    halves_q = (lane_q < HEAD_DIM, lane_q >= HEAD_DIM)
    halves_kv = (lane_kv < HEAD_DIM, lane_kv >= HEAD_DIM)

    d0, d1, d2 = DILATIONS
    units = (
        [(0, bi, pl.ds(bi * SPAN, SPAN)) for bi in range(ATT_STEP // d0 // SPAN)],
        [(r, bi, pl.ds(bi * SPAN * d1 + r, SPAN, stride=d1))
         for r in range(d1) for bi in range(ATT_STEP // d1 // SPAN)],
        [(r, 0, pl.ds(r * FAR_PITCH, SPAN)) for r in range(d2)],
    )
    q_refs = (q0, q1, q2)
    kv_refs = ((k0, kh0, v0, vh0), (k1, kh1, v1, vh1), (k2, kh2, v2, vh2))

    def scores(g):
        k_cur, k_halo = kv_refs[g][:2]
        out = []
        for r, bi, _ in units[g]:
            q2_ = q_refs[g][r, bi * SPAN:(bi + 1) * SPAN, :]
            kk = window(k_cur, k_halo, r, bi)
            bias = jnp.where(j == 0, bias_first, bias_band) if bi == 0 else bias_band
            ss = []
            for hh in range(2):
                qa = jnp.where(halves_q[hh], q2_, jnp.zeros_like(q2_))
                ss.append(lax.dot_general(qa, kk, (((1,), (1,)), ((), ())),
                                          preferred_element_type=F32) + bias)
            out.append(ss)
        return out

    def finish(g, all_ss):
        v_cur, v_halo = kv_refs[g][2:]
        probs = []
        for ss in all_ss:
            mxs = [jnp.max(s, axis=-1, keepdims=True) for s in ss]
            ps = [jnp.exp2(s - mx) for s, mx in zip(ss, mxs)]
            ls = [jnp.sum(p, axis=-1, keepdims=True) for p in ps]
            probs.append(([p.astype(BF16) for p in ps], mxs, ls))
        for (r, bi, sl), (ps, mxs, ls) in zip(units[g], probs):
            vv = window(v_cur, v_halo, r, bi)
            acc = None
            for hh in range(2):
                vh = jnp.where(halves_kv[hh], vv, jnp.zeros_like(vv))
                a = jnp.dot(ps[hh], vh, preferred_element_type=F32)
                acc = a if acc is None else acc + a
            stats = (acc, jnp.where(halves_q[0], mxs[0], mxs[1]),
                     jnp.where(halves_q[0], ls[0], ls[1]))
            if g == N_GROUPS - 1:
                for k, val in enumerate(stats):
                    far_s[k, sl, :] = val
            else:
                for ref, val in zip((acc_s, m_s, l_s), stats):
                    ref[g, sl, :] = val

    for g in range(N_GROUPS):
        finish(g, scores(g))

    def far_rows(k, ci_):
        sub = 8
        pieces = [far_s[k, pl.ds(a * sub * FAR_PITCH + ci_ * (SPAN // d2) + jj, sub,
                                 stride=FAR_PITCH), :]
                  for jj in range(SPAN // d2) for a in range(d2 // sub)]
        return jnp.concatenate(pieces, axis=0)

    def combine(ci_, c):
        sl = pl.ds(pl.multiple_of(ci_ * SPAN, SPAN), SPAN)
        accs = [acc_s[0, sl, :], acc_s[1, sl, :], far_rows(0, ci_)]
        ms = [m_s[0, sl, :], m_s[1, sl, :], far_rows(1, ci_)]
        ls = [l_s[0, sl, :], l_s[1, sl, :], far_rows(2, ci_)]
        mm = jnp.maximum(jnp.maximum(ms[0], ms[1]), ms[2])
        es = [jnp.exp2(m - mm) for m in ms]
        num = es[0] * accs[0] + es[1] * accs[1] + es[2] * accs[2]
        den = es[0] * ls[0] + es[1] * ls[1] + es[2] * ls[2]
        o_ref[sl, :] = (num / den).astype(o_ref.dtype)
        return c
    lax.fori_loop(0, ATT_STEP // SPAN, combine, 0)


def _attention(qs, ks, vs, seq_len, cast_ws):
    nb = qs[0].shape[0]
    steps = seq_len // ATT_STEP
    in_specs, args, scratch = [], [], []
    for g, d in enumerate(DILATIONS):
        rows = ATT_STEP // d
        cur = pl.BlockSpec((None, d, None, rows, LANES), lambda b, j, p: (b, 0, p, j, 0))
        ratio = rows // SPAN
        halo = pl.BlockSpec((None, d, None, SPAN, LANES),
                            lambda b, j, p, ratio=ratio: (b, 0, p, jnp.maximum(j * ratio - 1, 0), 0))
        in_specs += [cur, cur, cur, halo, halo]
        args += [qs[g], ks[g], vs[g], ks[g], vs[g]]
    scratch += [pltpu.VMEM((N_GROUPS - 1, ATT_STEP, LANES), F32)] * 3
    scratch += [pltpu.VMEM((3, max(DILATIONS) * FAR_PITCH, LANES), F32)]
    n_slices = max(nb * steps * PAIRS // 2, 1)
    slice_of = lambda b, j, p: (((b * steps + j) * PAIRS + p) // 2, 0)
    w_specs = []
    for w in cast_ws:
        assert w.shape[0] % (16 * n_slices) == 0, (w.shape, n_slices)
        w_specs.append(pl.BlockSpec((w.shape[0] // n_slices, w.shape[1]), slice_of))
    outs = pl.pallas_call(
        functools.partial(_attn_kernel, n_cast=len(cast_ws)),
        grid=(nb, steps, PAIRS),
        in_specs=in_specs + w_specs,
        out_specs=[pl.BlockSpec((None, None, ATT_STEP, LANES), lambda b, j, p: (b, p, j, 0))]
                  + w_specs,
        out_shape=[jax.ShapeDtypeStruct((nb, PAIRS, seq_len, LANES), BF16)]
                  + [jax.ShapeDtypeStruct(w.shape, BF16) for w in cast_ws],
        scratch_shapes=scratch,
        compiler_params=pltpu.CompilerParams(
            dimension_semantics=("arbitrary", "arbitrary", "arbitrary"),
            vmem_limit_bytes=VMEM_LIMIT),
        name="attn",
    )(*args, *cast_ws)
    return outs[0], outs[1:]


def _kv_tail_kernel(k0, v0, k1, v1, k2, v2, o0, o1, o2, nat_ref):
    for (k_ref, v_ref, o_ref, d) in ((k0, v0, o0, DILATIONS[0]), (k1, v1, o1, DILATIONS[1]),
                                     (k2, v2, o2, DILATIONS[2])):
        for kvi, ref in enumerate((k_ref, v_ref)):
            for p in range(PAIRS):
                rows = slice(p * LANES, (p + 1) * LANES)
                if d == 1:
                    o_ref[kvi, rows, :] = ref[0, p].astype(F32).T
                    continue
                for r in range(d):
                    nat_ref[pl.ds(r, SPAN, stride=d), :] = ref[r, p].astype(F32)
                for c in range(d):
                    o_ref[kvi, rows, c * SPAN:(c + 1) * SPAN] = nat_ref[c * SPAN:(c + 1) * SPAN, :].T


def _kv_tails(ks, vs, seq_len):
    nb = ks[0].shape[0]
    in_specs, args, out_specs, out_shape = [], [], [], []
    for g, d in enumerate(DILATIONS):
        last = seq_len // d // SPAN - 1
        spec = pl.BlockSpec((None, d, PAIRS, SPAN, LANES), lambda b, last=last: (b, 0, 0, last, 0))
        in_specs += [spec, spec]
        args += [ks[g], vs[g]]
        out_specs.append(pl.BlockSpec((None, 2, GROUP_W, SPAN * d), lambda b: (b, 0, 0, 0)))
        out_shape.append(jax.ShapeDtypeStruct((nb, 2, GROUP_W, SPAN * d), F32))
    return pl.pallas_call(
        _kv_tail_kernel,
        grid=(nb,), in_specs=in_specs, out_specs=out_specs, out_shape=out_shape,
        scratch_shapes=[pltpu.VMEM((SPAN * max(DILATIONS), LANES), F32)],
        compiler_params=pltpu.CompilerParams(dimension_semantics=("arbitrary",),
                                             vmem_limit_bytes=VMEM_LIMIT),
        name="kv_tail",
    )(*args)


def _mlstm_chunk(q2d, k2d, v2d, om2d, gif, gt, st_s, m_s, hm_ref, lc):
    lf_c = _log_sigmoid(gif)
    lf_r = _log_sigmoid(gt[M_HEADS:2 * M_HEADS, :])
    row = lax.broadcasted_iota(jnp.int32, (lc, lc), 0)
    col = lax.broadcasted_iota(jnp.int32, (lc, lc), 1)
    causal = col <= row
    tril = causal.astype(F32)
    triu = (row <= col).astype(F32)
    b_c = jnp.dot(tril, lf_c, precision=lax.Precision.HIGHEST, preferred_element_type=F32)
    b_r = jnp.dot(lf_r, triu, precision=lax.Precision.HIGHEST, preferred_element_type=F32)
    lane = lax.broadcasted_iota(jnp.int32, (lc, LANES), 1)
    halves = (lane < M_DQK, lane >= M_DQK)
    sub = lax.broadcasted_iota(jnp.int32, (LANES, 1), 0)
    ones_blk = jnp.ones((lc, LANES), BF16)
    qscale = M_DQK ** -0.5

    states, p1, p2 = {}, {}, {}
    all_pairs = tuple(range(M_HEADS // 2))
    heads_of = lambda pairs: [2 * p + hh for p in pairs for hh in range(2)]

    def phase1(pairs=all_pairs):
        for p in pairs:
            states[p] = st_s[p]
        for h in heads_of(pairs):
            p, hh = divmod(h, 2)
            q2_ = q2d[:, p * LANES:(p + 1) * LANES]
            k2_ = k2d[:, p * LANES:(p + 1) * LANES]
            qa = jnp.where(halves[hh], q2_, jnp.zeros_like(q2_)) * jnp.asarray(qscale, q2_.dtype)
            s = lax.dot_general(qa, k2_, (((1,), (1,)), ((), ())), preferred_element_type=F32)
            qc = jnp.dot(qa, states[p].astype(BF16), preferred_element_type=F32)
            p1[h] = (s, qc)

    def phase2(pairs=all_pairs):
        for h in heads_of(pairs):
            p, hh = divmod(h, 2)
            s, qc = p1[h]
            bcol = b_c[:, M_HEADS + h:M_HEADS + h + 1]
            igcol = gif[:, h:h + 1]
            a_row = gt[h:h + 1, :] - b_r[h:h + 1, :]
            amat = jnp.where(causal, a_row, NEG)
            m_prev = m_s[h:h + 1, 0:1]
            g_t = jnp.maximum(m_prev, jnp.max(amat, axis=-1, keepdims=True))
            wqk = (jnp.exp(amat - g_t) * s).astype(BF16)
            dec = jnp.exp(m_prev - g_t)
            m_t = bcol + g_t
            m_new = m_t[lc - 1:lc, :]
            b_last = bcol[lc - 1:lc, :]
            dstate = jnp.exp(b_last + m_prev - m_new)
            ws = jnp.exp(b_last - bcol + igcol - m_new)
            k2_ = k2d[:, p * LANES:(p + 1) * LANES]
            ka = jnp.where(halves[hh], k2_, jnp.zeros_like(k2_))
            kws = (ka.astype(F32) * ws).astype(BF16)
            m_s[h:h + 1, :] = jnp.broadcast_to(m_new, (1, LANES))
            p2[h] = (wqk, dec * qc, jnp.exp(-m_t), dstate, kws)

    def phase3(pairs=all_pairs):
        upds = {}
        for h in heads_of(pairs):
            wqk, dqc, floor, _, kws = p2[h]
            v1 = jnp.concatenate([v2d[:, h * M_DV:(h + 1) * M_DV], ones_blk], axis=1)
            num_den = dqc + jnp.dot(wqk, v1, preferred_element_type=F32)
            num = num_den[:, :M_DV]
            den = num_den[:, M_DV:]
            hval = num / jnp.maximum(jnp.abs(den), floor)
            gate = jax.nn.sigmoid(om2d[:, h * M_DV:(h + 1) * M_DV].astype(F32))
            hm_ref[:, h * M_DV:(h + 1) * M_DV] = (gate * hval).astype(hm_ref.dtype)
            upds[h] = lax.dot_general(kws, v1, (((0,), (0,)), ((), ())),
                                      preferred_element_type=F32)
        for p in pairs:
            drow = jnp.where(sub < M_DQK, p2[2 * p][3], p2[2 * p + 1][3])
            st_s[p] = drow * states[p] + upds[2 * p] + upds[2 * p + 1]

    return phase1, phase2, phase3


FF_CHUNK = 256


DEC_HEADS = HEADS // 2
DEC_ROWS = DEC_HEADS * HEAD_DIM


def _decode_attn_step(b, qkv_ref, c_refs, o_refs, ot_ref):
    sel = lax.broadcasted_iota(jnp.int32, (DEC_ROWS, LANES), 1) == b
    scale = HEAD_DIM ** -0.5

    def column(i):
        return jnp.sum(jnp.where(sel, qkv_ref[i], 0.0), axis=1, keepdims=True)

    def head(col, h):
        return col[h * HEAD_DIM:(h + 1) * HEAD_DIM, :]

    qc = [column(g) * scale for g in range(N_GROUPS)]
    kn = [column(N_GROUPS + g) for g in range(N_GROUPS)]
    vn = [column(2 * N_GROUPS + g) for g in range(N_GROUPS)]
    cols = []
    for h in range(DEC_HEADS):
        scores, m_h = [], None
        for g, d in enumerate(DILATIONS):
            wb = c_refs[g].shape[-1]
            qh = head(qc[g], h)
            s = jnp.sum(c_refs[g][0, h] * qh, axis=0, keepdims=True)
            pos = lax.broadcasted_iota(jnp.int32, (1, wb), 1)
            s = jnp.where((pos & (d - 1)) == 0, s, NEG)
            s_new = jnp.sum(head(kn[g], h) * qh, axis=0, keepdims=True)
            m_g = jnp.maximum(jnp.max(s, axis=1, keepdims=True), s_new)
            m_h = m_g if m_h is None else jnp.maximum(m_h, m_g)
            scores.append((s, s_new))
        l_h = jnp.zeros((1, 1), F32)
        acc = jnp.zeros((HEAD_DIM, 1), F32)
        for g in range(N_GROUPS):
            s, s_new = scores[g]
            p = jnp.exp(s - m_h)
            p_new = jnp.exp(s_new - m_h)
            l_h = l_h + jnp.sum(p, axis=1, keepdims=True) + p_new
            acc = acc + jnp.sum(c_refs[g][1, h] * p, axis=1, keepdims=True) + p_new * head(vn[g], h)
        cols.append(acc / l_h)
    ot_ref[...] = jnp.broadcast_to(jnp.concatenate(cols, axis=0), ot_ref.shape)
    for g in range(N_GROUPS):
        wb = c_refs[g].shape[-1]
        last = lax.broadcasted_iota(jnp.int32, (HEAD_DIM, wb), 1) == wb - 1
        for kvi, new in ((0, kn[g]), (1, vn[g])):
            for h in range(DEC_HEADS):
                rolled = pltpu.roll(c_refs[g][kvi, h], wb - 1, 1)
                o_refs[g][kvi, h] = jnp.where(last, head(new, h), rolled)


def _merge_ffn_kernel(x_ref, o_ref, hm_ref, ga_ref, gb_ref, wa_ref, wm_ref, wo_ref, g2_ref,
                      wg_ref, wu_ref, wd_ref, *rest, with_decode):
    if with_decode:
        qkv_ref, c0, c1, c2, y_ref, o0, o1, o2, ot_ref = rest
        _decode_attn_step(pl.program_id(0) // 2, qkv_ref, (c0, c1, c2), (o0, o1, o2), ot_ref)
    else:
        (y_ref,) = rest
    yb = jnp.dot(hm_ref[...].astype(BF16), wm_ref[...], preferred_element_type=F32)
    o_att = jnp.concatenate([o_ref[p] for p in range(PAIRS)], axis=1).astype(BF16)
    ya = jnp.dot(o_att, wa_ref[...], preferred_element_type=F32)
    mixed = (jax.nn.sigmoid(ga_ref[...].astype(F32)) * ya
             + jax.nn.sigmoid(gb_ref[...].astype(F32)) * yb)
    x2 = x_ref[...] + jnp.dot(mixed.astype(BF16), wo_ref[...], preferred_element_type=F32)
    h2 = (x2 * lax.rsqrt(jnp.mean(x2 * x2, axis=-1, keepdims=True) + NORM_EPS)
          * g2_ref[...]).astype(BF16)
    acc = x2
    d_ff = wg_ref.shape[1]

    def gate_up(c):
        gt = jnp.dot(h2, wg_ref[:, c:c + FF_CHUNK], preferred_element_type=F32)
        up = jnp.dot(h2, wu_ref[:, c:c + FF_CHUNK], preferred_element_type=F32)
        return gt, up

    chunks = list(range(0, d_ff, FF_CHUNK))
    nxt = gate_up(chunks[0])
    for i, c in enumerate(chunks):
        gt, up = nxt
        if i + 1 < len(chunks):
            nxt = gate_up(chunks[i + 1])
        ff = (gt * jax.nn.sigmoid(gt) * up).astype(BF16)
        acc = acc + jnp.dot(ff, wd_ref[c:c + FF_CHUNK, :], preferred_element_type=F32)
    y_ref[...] = acc


def _merge_ffn(x2d, o_att, hm, ga, gb, wa, wm, wo, g2, wg, wu, wd, seq_len, tm, decode=None):
    m_rows, d_model = x2d.shape
    tiles_per_seq = seq_len // tm
    steps = m_rows // tm
    row = lambda w: pl.BlockSpec((tm, w), lambda i: (i, 0))
    o_spec = pl.BlockSpec((None, PAIRS, tm, LANES),
                          lambda i: (i // tiles_per_seq, 0, i % tiles_per_seq, 0))
    in_specs = [row(d_model), o_spec, row(M_V_W), row(D_MODEL), row(D_MODEL),
                _const_spec(wa.shape), _const_spec(wm.shape), _const_spec(wo.shape),
                _const_spec((1, d_model)), _const_spec(wg.shape), _const_spec(wu.shape),
                _const_spec(wd.shape)]
    args = [x2d, o_att, hm, ga, gb, wa, wm, wo, g2, wg, wu, wd]
    out_specs = [row(d_model)]
    out_shape = [jax.ShapeDtypeStruct((m_rows, d_model), F32)]
    if decode is not None:
        qkv_t, caches_t = decode
        db = caches_t[0].shape[0]
        assert steps == 2 * db, (steps, db)
        in_specs.append(pl.BlockSpec((qkv_t.shape[0], DEC_ROWS, LANES), lambda i: (0, i % 2, 0)))
        args.append(qkv_t)
        for c in caches_t:
            spec = pl.BlockSpec((None, 2, DEC_HEADS, HEAD_DIM, c.shape[-1]),
                                lambda i: (i // 2, 0, i % 2, 0, 0))
            in_specs.append(spec)
            args.append(c)
            out_specs.append(spec)
            out_shape.append(jax.ShapeDtypeStruct(c.shape, F32))
        out_specs.append(pl.BlockSpec((None, None, DEC_ROWS, LANES), lambda i: (i // 2, i % 2, 0, 0)))
        out_shape.append(jax.ShapeDtypeStruct((db, 2, DEC_ROWS, LANES), F32))
    outs = pl.pallas_call(
        functools.partial(_merge_ffn_kernel, with_decode=decode is not None),
        grid=(steps,),
        in_specs=in_specs, out_specs=out_specs, out_shape=out_shape,
        compiler_params=pltpu.CompilerParams(dimension_semantics=("arbitrary",),
                                             vmem_limit_bytes=VMEM_LIMIT),
        name="merge_ffn",
    )(*args)
    if decode is None:
        return outs[0]
    return outs[0], outs[1:4], outs[4]


def _sample_prep_kernel(*refs):
    for i, ref in enumerate(refs[:-1]):
        refs[-1][i] = ref[...].T


def _sample_prep(arrs):
    vm = pl.BlockSpec(memory_space=pltpu.VMEM)
    return pl.pallas_call(
        _sample_prep_kernel, in_specs=[vm] * len(arrs), out_specs=vm,
        out_shape=jax.ShapeDtypeStruct((len(arrs), GROUP_W, LANES), F32),
        name="sample_prep",
    )(*arrs)


def _sample_mlstm_kernel(q_ref, k_ref, v_ref, om_ref, gt_ref, mt_ref, n_ref, c_ref, rexp_ref,
                         hm_ref, co_ref, nt_ref, mo_ref, *, db):
    nh, dqk = M_HEADS, M_DQK
    hi = lax.Precision.HIGHEST
    rexp = rexp_ref[...]
    q_t = q_ref[...].T * (dqk ** -0.5)
    k_t = k_ref[...].T
    n_t = n_ref[...].T
    ig = gt_ref[0:nh, :]
    lf = _log_sigmoid(gt_ref[nh:2 * nh, :])
    m_prev = mt_ref[...]
    m_new = jnp.maximum(lf + m_prev, ig)
    w8 = jnp.exp(ig - m_new)
    dec8 = jnp.exp(lf + m_prev - m_new)
    head_sum = lambda a: lax.dot_general(rexp, a, (((0,), (0,)), ((), ())), precision=hi,
                                         preferred_element_type=F32)
    expand = lambda a: jnp.dot(rexp, a, precision=hi, preferred_element_type=F32)
    qk8 = head_sum(q_t * k_t)
    qn8 = head_sum(q_t * n_t)
    wqk8 = w8 * qk8
    den8 = dec8 * qn8 + wqk8
    inv8 = 1.0 / jnp.maximum(jnp.abs(den8), jnp.exp(-m_new))
    dec_x = expand(dec8)
    w_x = expand(w8)
    nt_ref[...] = dec_x * n_t + w_x * k_t
    mo_ref[...] = m_new
    wk_x = w_x * k_t
    for b in range(db):
        cb = c_ref[b].reshape(nh * dqk, M_DV)
        v_b = v_ref[b]
        v_x = jnp.concatenate([jnp.broadcast_to(v_b[h:h + 1, :], (dqk, M_DV)) for h in range(nh)],
                              axis=0)
        co_ref[b] = (dec_x[:, b:b + 1] * cb + wk_x[:, b:b + 1] * v_x).reshape(nh, dqk, M_DV)
        qc = jnp.sum((q_t[:, b:b + 1] * cb).reshape(nh, dqk, M_DV), axis=1)
        num = dec8[:, b:b + 1] * qc + wqk8[:, b:b + 1] * v_b
        hm_ref[b] = jax.nn.sigmoid(om_ref[b]) * (num * inv8[:, b:b + 1])


def _sample_mlstm(q_t, k_t, v3, om3, gif_t, m_t, n_pad, c_state, rexp, db):
    vm = pl.BlockSpec(memory_space=pltpu.VMEM)
    return pl.pallas_call(
        functools.partial(_sample_mlstm_kernel, db=db),
        in_specs=[vm] * 9,
        out_specs=[vm] * 4,
        out_shape=[jax.ShapeDtypeStruct((db, M_HEADS, M_DV), F32),
                   jax.ShapeDtypeStruct(c_state.shape, F32),
                   jax.ShapeDtypeStruct((M_HEADS * M_DQK, LANES), F32),
                   jax.ShapeDtypeStruct((M_HEADS, LANES), F32)],
        compiler_params=pltpu.CompilerParams(vmem_limit_bytes=VMEM_LIMIT),
        name="sample_mlstm",
    )(q_t, k_t, v3, om3, gif_t, m_t, n_pad, c_state, rexp)


def _rope_tables(pos):
    half = ROT_DIM // 2
    pos = np.asarray(pos, np.float32)
    inv_freq = np.exp(np.float32(-math.log(ROPE_THETA))
                      * np.arange(0, ROT_DIM, 2, dtype=np.float32) / np.float32(ROT_DIM))
    ang = (pos[:, None] * inv_freq[None, :]).astype(np.float32)
    cos, sin = np.cos(ang).astype(np.float32), np.sin(ang).astype(np.float32)
    t = pos.shape[0]
    rest = HEAD_DIM - ROT_DIM
    a = np.concatenate([cos, cos, np.ones((t, rest), np.float32)], axis=1)
    bm = np.concatenate([-sin, np.zeros((t, HEAD_DIM - half), np.float32)], axis=1)
    bp = np.concatenate([np.zeros((t, half), np.float32), sin, np.zeros((t, rest), np.float32)],
                        axis=1)
    return tuple(jnp.asarray(np.concatenate([x, x], axis=1)) for x in (a, bm, bp))


W_PREP_BLK = 512


def _w_prep_kernel(wt_ref, o_ref):
    o_ref[...] = wt_ref[...].T.astype(o_ref.dtype)


def _w_prep(w_in):
    d_model = w_in.shape[0]
    wt = w_in.T
    n_head = _C_GA // W_PREP_BLK
    n_blk = _W_COLS // W_PREP_BLK

    sub = 8

    def src_row(j):
        return (j * (W_PREP_BLK // sub) + jnp.where(j < n_head, 0, _GIF_COLS // sub)) * sub

    return pl.pallas_call(
        _w_prep_kernel,
        grid=(n_blk,),
        in_specs=[pl.BlockSpec((pl.Element(W_PREP_BLK), pl.Element(d_model)),
                               lambda j: (src_row(j), 0))],
        out_specs=pl.BlockSpec((d_model, W_PREP_BLK), lambda j: (0, j)),
        out_shape=jax.ShapeDtypeStruct((d_model, _W_COLS), BF16),
        compiler_params=pltpu.CompilerParams(dimension_semantics=("arbitrary",),
                                             vmem_limit_bytes=VMEM_LIMIT),
        name="w_prep",
    )(wt)


def kernel(x_prompt, x_sample, cache_kv_w128, cache_kv_w512, cache_kv_w2048, state_mlstm_C, state_mlstm_n, state_mlstm_m, norm1_g, w_in, b_if, q_norm_g, k_norm_g, w_att_out, w_m_out, w_o, norm2_g, w_gate, w_up, w_down):
    nb, seq_len, d_model = x_prompt.shape
    db, dec_seq, _ = x_sample.shape
    assert dec_seq == 1 and d_model == D_MODEL and seq_len % ATT_STEP == 0 and db <= LANES
    caches = (cache_kv_w128, cache_kv_w512, cache_kv_w2048)

    w_perm = _w_prep(w_in)
    w_gif = jnp.zeros((d_model, LANES), BF16).at[:, :_GIF_COLS].set(
        w_in[:, _C_GA:_C_GA + _GIF_COLS].astype(BF16))
    g1 = norm1_g.reshape(1, d_model)
    g2 = norm2_g.reshape(1, d_model)
    bif = jnp.concatenate([b_if, jnp.zeros((LANES - b_if.shape[0],), F32)]).reshape(1, LANES)
    qg = jnp.tile(q_norm_g, HEADS).reshape(1, GROUP_W)
    kg = jnp.tile(k_norm_g, HEADS).reshape(1, GROUP_W)
    hid = np.arange(GROUP_W // 2) // HEAD_DIM
    gmat = jnp.asarray(hid[:, None] == hid[None, :], dtype=BF16)

    m_rows = nb * seq_len
    x2d = x_prompt.reshape(m_rows, d_model)
    tabs_p = _rope_tables(np.arange(seq_len))
    qg_p = qg * (HEAD_DIM ** -0.5 * math.log2(math.e))
    outs = _proj(x2d, seq_len, DILATIONS, 256, False, g1, w_perm, w_gif, bif, qg_p, kg, tabs_p, gmat)
    qs, ks, vs = outs[0:3], outs[3:6], outs[6:9]
    hm, ga, gb, st_p, m_p = outs[9:14]

    o_att, (wa, wm, wo, wg, wu, wd) = _attention(
        qs, ks, vs, seq_len, (w_att_out, w_m_out, w_o, w_gate, w_up, w_down))

    x_s = jnp.zeros((LANES, d_model), F32).at[:db].set(x_sample.reshape(db, d_model))
    tabs_s = _rope_tables(np.full((LANES,), PAST_LEN))
    outs_s = _proj(x_s, LANES, (1, 1, 1), LANES, True, g1, w_perm, w_gif, bif, qg, kg, tabs_s, gmat)
    qkv_t = _sample_prep(outs_s[0:9])
    caches_t = [c.transpose(0, 2, 3, 4, 1) for c in caches]
    y_prompt, kv_st, o_att_cols = _merge_ffn(x2d, o_att, hm, ga, gb, wa, wm, wo, g2, wg, wu, wd,
                                             seq_len, 256, decode=(qkv_t, caches_t))
    y_prompt = y_prompt.reshape(nb, seq_len, d_model)
    kv_s = [c.transpose(0, 4, 1, 2, 3) for c in kv_st]

    tails = _kv_tails(ks, vs, seq_len)
    kv_p = [t.reshape(nb, 2, HEADS, HEAD_DIM, t.shape[-1]).transpose(0, 4, 1, 2, 3) for t in tails]
    c_p = st_p[..., :M_DV].reshape(nb, M_HEADS, M_DQK, M_DV)
    n_p = st_p[..., M_DV].reshape(nb, M_HEADS, M_DQK)
    m_pr = m_p[:, :, 0]

    qm_t, km_t, vm_s, om_s, ga_s, gb_s, gif_s = outs_s[9:16]
    gif_ts = gif_s.T

    rexp = jnp.asarray(np.arange(M_HEADS * M_DQK)[:, None] // M_DQK == np.arange(M_HEADS)[None, :],
                       dtype=F32)
    m_t = jnp.zeros((M_HEADS, LANES), F32).at[:, :db].set(state_mlstm_m.T)
    n_pad = jnp.zeros((LANES, M_HEADS * M_DQK), F32).at[:db].set(state_mlstm_n.reshape(db, -1))
    hm_s3, c_s, n_t, m_so = _sample_mlstm(
        qm_t, km_t, vm_s[:db].reshape(db, M_HEADS, M_DV), om_s[:db].reshape(db, M_HEADS, M_DV),
        gif_ts[:2 * M_HEADS], m_t, n_pad, state_mlstm_C, rexp, db)
    n_s = n_t.T[:db].reshape(db, M_HEADS, M_DQK)
    m_s = m_so[:, :db].T

    o_att_s = o_att_cols[:, :, :, 0].reshape(db, PAIRS, LANES)
    o_att_sp = jnp.zeros((PAIRS, LANES, LANES), F32).at[:, :db].set(o_att_s.transpose(1, 0, 2))[None]
    hm_sp = jnp.zeros((LANES, M_V_W), F32).at[:db].set(hm_s3.reshape(db, M_V_W))
    y_s = _merge_ffn(x_s, o_att_sp, hm_sp, ga_s, gb_s, wa, wm, wo, g2, wg, wu, wd, LANES, LANES)
    y_sample = y_s[:db].reshape(db, 1, d_model)

    return (y_prompt, y_sample, kv_p[0], kv_p[1], kv_p[2], c_p, n_p, m_pr,
            kv_s[0], kv_s[1], kv_s[2], c_s, n_s, m_s)
```

```python
import functools
import math

import jax
import jax.numpy as jnp
import numpy as np
from jax import lax
from jax.experimental import pallas as pl
from jax.experimental.pallas import tpu as pltpu

F32 = jnp.float32
BF16 = jnp.bfloat16

HEAD_DIM = 64
HEADS = 8
GROUP_W = HEADS * HEAD_DIM
N_GROUPS = 3
WINDOWS = (128, 512, 2048)
DILATIONS = (1, 4, 16)
SPAN = 128
ROT_DIM = 16
ROPE_THETA = 500000.0
M_HEADS = 8
M_DQK = 64
M_DV = 128
M_QK_W = M_HEADS * M_DQK
M_V_W = M_HEADS * M_DV
D_MODEL = 1024
PAST_LEN = 8192
NORM_EPS = 1e-6
NEG = -1e30

LANES = 128
PAIRS = GROUP_W // LANES
VMEM_LIMIT = 56 * 1024 * 1024

_ATT_W = N_GROUPS * GROUP_W
_C_QM = 3 * _ATT_W
_C_KM = _C_QM + M_QK_W
_C_VM = _C_KM + M_QK_W
_C_OM = _C_VM + M_V_W
_C_GA = _C_OM + M_V_W
_C_GB = _C_GA + D_MODEL
_W_COLS = _C_GB + D_MODEL
_GIF_COLS = 2 * M_HEADS


def _const_spec(shape):
    nd = len(shape)
    return pl.BlockSpec(shape, lambda *_: (0,) * nd, pipeline_mode=pl.Buffered(1))


def _log_sigmoid(x):
    return jnp.minimum(x, 0.0) - jnp.log1p(jnp.exp(-jnp.abs(x)))


MLSTM_CHUNKS = 2


def _proj_kernel(x_ref, g1_ref, w_ref, wgif_ref, bif_ref, qg_ref, kg_ref, ra_ref, rm_ref, rp_ref,
                 gm_ref,
                 q0_ref, q1_ref, q2_ref, k0_ref, k1_ref, k2_ref, v0_ref, v1_ref, v2_ref,
                 *rest, tm, dils, plain, tiles_per_seq):
    if plain:
        qm_ref, km_ref, vm_ref, om_ref, ga_ref, gb_ref, gif_ref, hs_ref = rest
    else:
        hm_ref, ga_ref, gb_ref, st_ref, mo_ref, hs_ref, st_s, m_s = rest
    d_model = x_ref.shape[1]
    x = x_ref[...]
    xn = x * lax.rsqrt(jnp.mean(x * x, axis=-1, keepdims=True) + NORM_EPS) * g1_ref[...]
    h_nat = xn.astype(BF16)
    n_slab = d_model // LANES
    if any(d > 1 for d in dils):
        for c in range(n_slab):
            hs_ref[c] = xn[:, c * LANES:(c + 1) * LANES]

    def permuted_h(d):
        if d == 1:
            return h_nat
        n = tm // d
        if d >= 8:
            ri = lax.broadcasted_iota(jnp.int32, (tm, tm), 0)
            ci = lax.broadcasted_iota(jnp.int32, (tm, tm), 1)
            perm = (ci == (ri % n) * d + ri // n).astype(BF16)
            return jnp.dot(perm, h_nat, preferred_element_type=F32).astype(BF16)
        rows = [jnp.concatenate([hs_ref[c, pl.ds(r, n, stride=d), :] for c in range(n_slab)], axis=1)
                for r in range(d)]
        return jnp.concatenate(rows, axis=0).astype(BF16)

    def permuted_tab(ref, d):
        if d == 1:
            t = ref[...]
        else:
            n = tm // d
            t = jnp.concatenate([ref[pl.ds(r, n, stride=d), :] for r in range(d)], axis=0)
        return jnp.concatenate([t] * PAIRS, axis=1)

    gmat = gm_ref[...]

    def head_sumsq(z):
        zz = (z * z).astype(BF16)
        half = GROUP_W // 2
        return jnp.concatenate(
            [jnp.dot(zz[:, :half], gmat, preferred_element_type=F32),
             jnp.dot(zz[:, half:], gmat, preferred_element_type=F32)], axis=1)

    def norm_rope(z, ss, gain, ra, rm, rp):
        y = z * lax.rsqrt(ss * (1.0 / HEAD_DIM) + NORM_EPS) * gain
        return (y * ra + pltpu.roll(y, GROUP_W - ROT_DIM // 2, 1) * rm
                + pltpu.roll(y, ROT_DIM // 2, 1) * rp)

    def store_group(ref, y, d):
        if plain:
            ref[...] = y.astype(ref.dtype)
            return
        n = tm // d
        for p in range(PAIRS):
            ref[:, p] = y[:, p * LANES:(p + 1) * LANES].reshape(d, n, LANES).astype(ref.dtype)

    def seg(c0, width):
        return jnp.dot(h_nat, w_ref[:, c0:c0 + width], preferred_element_type=F32)

    if not plain:
        @pl.when(pl.program_id(0) % tiles_per_seq == 0)
        def _():
            st_s[...] = jnp.zeros_like(st_s)
            m_s[...] = jnp.zeros_like(m_s)

        gif = jnp.dot(h_nat, wgif_ref[...], preferred_element_type=F32) + bif_ref[...]
        qm_b, km_b = seg(_C_QM, M_QK_W).astype(BF16), seg(_C_KM, M_QK_W).astype(BF16)
        vm_b, om_b = seg(_C_VM, M_V_W).astype(BF16), seg(_C_OM, M_V_W).astype(BF16)
        gif_t = gif.T[:2 * M_HEADS, :]
        lc = tm // MLSTM_CHUNKS
        m_chunks = []
        for c in range(MLSTM_CHUNKS):
            rows = slice(c * lc, (c + 1) * lc)
            m_chunks.append(functools.partial(
                _mlstm_chunk, qm_b[rows], km_b[rows], vm_b[rows], om_b[rows], gif[rows],
                gif_t[:, rows], st_s, m_s, hm_ref.at[rows], lc))
        m_live, m_started = [], {}
        m_units = [(c, pairs) for c in range(MLSTM_CHUNKS) for pairs in ((0, 1), (2, 3))]

        def m_advance(i):
            if 0 < i <= len(m_units):
                phases, pairs = m_live.pop()
                phases[2](pairs)
            if i < len(m_units):
                c, pairs = m_units[i]
                if c not in m_started:
                    m_started[c] = m_chunks[c]()
                phases = m_started[c]
                phases[0](pairs)
                phases[1](pairs)
                m_live.append((phases, pairs))

        m_advance(0)

    q_refs = (q0_ref, q1_ref, q2_ref)
    k_refs = (k0_ref, k1_ref, k2_ref)
    v_refs = (v0_ref, v1_ref, v2_ref)
    for g in range(N_GROUPS):
        d = dils[g]
        hg = permuted_h(d)
        ra, rm, rp = (permuted_tab(r, d) for r in (ra_ref, rm_ref, rp_ref))
        cq, ck, cv = (t * _ATT_W + g * GROUP_W for t in range(3))
        zq = jnp.dot(hg, w_ref[:, cq:cq + GROUP_W], preferred_element_type=F32)
        zk = jnp.dot(hg, w_ref[:, ck:ck + GROUP_W], preferred_element_type=F32)
        zv = jnp.dot(hg, w_ref[:, cv:cv + GROUP_W], preferred_element_type=F32)
        ssq, ssk = head_sumsq(zq), head_sumsq(zk)
        store_group(v_refs[g], zv, d)
        store_group(q_refs[g], norm_rope(zq, ssq, qg_ref[...], ra, rm, rp), d)
        store_group(k_refs[g], norm_rope(zk, ssk, kg_ref[...], ra, rm, rp), d)

        if not plain:
            m_advance(g + 1)

    if plain:
        segments = ((qm_ref, _C_QM, M_QK_W), (km_ref, _C_KM, M_QK_W), (vm_ref, _C_VM, M_V_W),
                    (om_ref, _C_OM, M_V_W), (ga_ref, _C_GA, D_MODEL), (gb_ref, _C_GB, D_MODEL))
        zg = jnp.dot(h_nat, wgif_ref[...], preferred_element_type=F32)
        gif_ref[...] = zg + bif_ref[...]
    else:
        segments = ((ga_ref, _C_GA, D_MODEL), (gb_ref, _C_GB, D_MODEL))
    for si, (ref, c0, width) in enumerate(segments):
        for cc in range(0, width, GROUP_W):
            ref[:, cc:cc + GROUP_W] = seg(c0 + cc, GROUP_W).astype(ref.dtype)
        if not plain and si == 0:
            m_advance(N_GROUPS + 1)
            st_ref[...] = st_s[...]
            mo_ref[...] = m_s[...]


def _proj(x2d, seq_len, dils, tm, plain, g1, w_perm, w_gif, bif, qg, kg, rope_tabs, gmat):
    m_rows, d_model = x2d.shape
    nb = m_rows // seq_len
    tiles_per_seq = seq_len // tm
    grid = (m_rows // tm,)
    row_spec = lambda w: pl.BlockSpec((tm, w), lambda i: (i, 0))
    tab_spec = pl.BlockSpec((tm, LANES), lambda i: (i % tiles_per_seq, 0))
    sds = jax.ShapeDtypeStruct

    widths = (M_QK_W, M_QK_W, M_V_W, M_V_W, D_MODEL, D_MODEL)
    scratch = [pltpu.VMEM((d_model // LANES, tm, LANES), F32)]
    if plain:
        out_shape = ([sds((m_rows, GROUP_W), F32)] * 9 + [sds((m_rows, w), F32) for w in widths]
                     + [sds((m_rows, LANES), F32)])
        out_specs = ([row_spec(GROUP_W)] * 9 + [row_spec(w) for w in widths] + [row_spec(LANES)])
    else:
        def grp_spec(d):
            return pl.BlockSpec((None, d, PAIRS, tm // d, LANES),
                                lambda i: (i // tiles_per_seq, 0, 0, i % tiles_per_seq, 0))
        grp_shape = lambda d: sds((nb, d, PAIRS, seq_len // d, LANES), BF16)
        state_spec = lambda *dims: pl.BlockSpec((None,) + dims,
                                                lambda i: (i // tiles_per_seq,) + (0,) * len(dims))
        st_dims, m_dims = (M_HEADS // 2, LANES, 2 * LANES), (M_HEADS, LANES)
        row_w = (M_V_W, D_MODEL, D_MODEL)
        out_shape = ([grp_shape(d) for d in dils] * 3 + [sds((m_rows, w), BF16) for w in row_w]
                     + [sds((nb,) + st_dims, F32), sds((nb,) + m_dims, F32)])
        out_specs = ([grp_spec(d) for d in dils] * 3 + [row_spec(w) for w in row_w]
                     + [state_spec(*st_dims), state_spec(*m_dims)])
        scratch += [pltpu.VMEM(st_dims, F32), pltpu.VMEM(m_dims, F32)]
    in_specs = [row_spec(d_model), _const_spec((1, d_model)), _const_spec(w_perm.shape),
                _const_spec(w_gif.shape), _const_spec((1, LANES)), _const_spec((1, GROUP_W)),
                _const_spec((1, GROUP_W)), tab_spec, tab_spec, tab_spec, _const_spec(gmat.shape)]
    return pl.pallas_call(
        functools.partial(_proj_kernel, tm=tm, dils=dils, plain=plain,
                          tiles_per_seq=tiles_per_seq),
        grid=grid, in_specs=in_specs, out_specs=out_specs, out_shape=out_shape,
        scratch_shapes=scratch,
        compiler_params=pltpu.CompilerParams(dimension_semantics=("arbitrary",),
                                             vmem_limit_bytes=VMEM_LIMIT),
        name="proj",
    )(x2d, g1, w_perm, w_gif, bif, qg, kg, *rope_tabs, gmat)


ATT_STEP = SPAN * max(DILATIONS)
FAR_PITCH = SPAN + 8


def _attn_kernel(q0, k0, v0, kh0, vh0, q1, k1, v1, kh1, vh1, q2, k2, v2, kh2, vh2,
                 *rest, n_cast):
    w_f32, o_ref, w_bf16 = rest[:n_cast], rest[n_cast], rest[n_cast + 1:2 * n_cast + 1]
    acc_s, m_s, l_s, far_s = rest[2 * n_cast + 1:]
    j = pl.program_id(1)
    for src, dst in zip(w_f32, w_bf16):
        dst[...] = src[...].astype(dst.dtype)

    def window(cur, halo, r, bi):
        prev = halo[r] if bi == 0 else cur[r, (bi - 1) * SPAN:bi * SPAN, :]
        return jnp.concatenate([prev, cur[r, bi * SPAN:(bi + 1) * SPAN, :]], axis=0)

    qi = lax.broadcasted_iota(jnp.int32, (SPAN, 2 * SPAN), 0)
    ci = lax.broadcasted_iota(jnp.int32, (SPAN, 2 * SPAN), 1)
    band = (ci >= qi) & (ci <= qi + SPAN)
    bias_band = jnp.where(band, 0.0, NEG).astype(F32)
    bias_first = jnp.where(band & (ci >= SPAN), 0.0, NEG).astype(F32)
    lane_q = lax.broadcasted_iota(jnp.int32, (SPAN, LANES), 1)
    lane_kv = lax.broadcasted_iota(jnp.int32, (2 * SPAN, LANES), 1)
    halves_q = (lane_q < HEAD_DIM, lane_q >= HEAD_DIM)
    halves_kv = (lane_kv < HEAD_DIM, lane_kv >= HEAD_DIM)

    d0, d1, d2 = DILATIONS
    units = (
        [(0, bi, pl.ds(bi * SPAN, SPAN)) for bi in range(ATT_STEP // d0 // SPAN)],
        [(r, bi, pl.ds(bi * SPAN * d1 + r, SPAN, stride=d1))
         for r in range(d1) for bi in range(ATT_STEP // d1 // SPAN)],
        [(r, 0, pl.ds(r * FAR_PITCH, SPAN)) for r in range(d2)],
    )
    q_refs = (q0, q1, q2)
    kv_refs = ((k0, kh0, v0, vh0), (k1, kh1, v1, vh1), (k2, kh2, v2, vh2))

    def scores(g):
        k_cur, k_halo = kv_refs[g][:2]
        out = []
        for r, bi, _ in units[g]:
            q2_ = q_refs[g][r, bi * SPAN:(bi + 1) * SPAN, :]
            kk = window(k_cur, k_halo, r, bi)
            bias = jnp.where(j == 0, bias_first, bias_band) if bi == 0 else bias_band
            ss = []
            for hh in range(2):
                qa = jnp.where(halves_q[hh], q2_, jnp.zeros_like(q2_))
                ss.append(lax.dot_general(qa, kk, (((1,), (1,)), ((), ())),
                                          preferred_element_type=F32) + bias)
            out.append(ss)
        return out

    def finish(g, all_ss):
        v_cur, v_halo = kv_refs[g][2:]
        probs = []
        for ss in all_ss:
            mxs = [jnp.max(s, axis=-1, keepdims=True) for s in ss]
            ps = [jnp.exp2(s - mx) for s, mx in zip(ss, mxs)]
            ls = [jnp.sum(p, axis=-1, keepdims=True) for p in ps]
            probs.append(([p.astype(BF16) for p in ps], mxs, ls))
        for (r, bi, sl), (ps, mxs, ls) in zip(units[g], probs):
            vv = window(v_cur, v_halo, r, bi)
            acc = None
            for hh in range(2):
                vh = jnp.where(halves_kv[hh], vv, jnp.zeros_like(vv))
                a = jnp.dot(ps[hh], vh, preferred_element_type=F32)
                acc = a if acc is None else acc + a
            stats = (acc, jnp.where(halves_q[0], mxs[0], mxs[1]),
                     jnp.where(halves_q[0], ls[0], ls[1]))
            if g == N_GROUPS - 1:
                for k, val in enumerate(stats):
                    far_s[k, sl, :] = val
            else:
                for ref, val in zip((acc_s, m_s, l_s), stats):
                    ref[g, sl, :] = val

    for g in range(N_GROUPS):
        finish(g, scores(g))

    def far_rows(k, ci_):
        sub = 8
        pieces = [far_s[k, pl.ds(a * sub * FAR_PITCH + ci_ * (SPAN // d2) + jj, sub,
                                 stride=FAR_PITCH), :]
                  for jj in range(SPAN // d2) for a in range(d2 // sub)]
        return jnp.concatenate(pieces, axis=0)

    def combine(ci_, c):
        sl = pl.ds(pl.multiple_of(ci_ * SPAN, SPAN), SPAN)
        accs = [acc_s[0, sl, :], acc_s[1, sl, :], far_rows(0, ci_)]
        ms = [m_s[0, sl, :], m_s[1, sl, :], far_rows(1, ci_)]
        ls = [l_s[0, sl, :], l_s[1, sl, :], far_rows(2, ci_)]
        mm = jnp.maximum(jnp.maximum(ms[0], ms[1]), ms[2])
        es = [jnp.exp2(m - mm) for m in ms]
        num = es[0] * accs[0] + es[1] * accs[1] + es[2] * accs[2]
        den = es[0] * ls[0] + es[1] * ls[1] + es[2] * ls[2]
        o_ref[sl, :] = (num / den).astype(o_ref.dtype)
        return c
    lax.fori_loop(0, ATT_STEP // SPAN, combine, 0)


def _attention(qs, ks, vs, seq_len, cast_ws):
    nb = qs[0].shape[0]
    steps = seq_len // ATT_STEP
    in_specs, args, scratch = [], [], []
    for g, d in enumerate(DILATIONS):
        rows = ATT_STEP // d
        cur = pl.BlockSpec((None, d, None, rows, LANES), lambda b, j, p: (b, 0, p, j, 0))
        ratio = rows // SPAN
        halo = pl.BlockSpec((None, d, None, SPAN, LANES),
                            lambda b, j, p, ratio=ratio: (b, 0, p, jnp.maximum(j * ratio - 1, 0), 0))
        in_specs += [cur, cur, cur, halo, halo]
        args += [qs[g], ks[g], vs[g], ks[g], vs[g]]
    scratch += [pltpu.VMEM((N_GROUPS - 1, ATT_STEP, LANES), F32)] * 3
    scratch += [pltpu.VMEM((3, max(DILATIONS) * FAR_PITCH, LANES), F32)]
    n_slices = max(nb * steps * PAIRS // 2, 1)
    slice_of = lambda b, j, p: (((b * steps + j) * PAIRS + p) // 2, 0)
    w_specs = []
    for w in cast_ws:
        assert w.shape[0] % (16 * n_slices) == 0, (w.shape, n_slices)
        w_specs.append(pl.BlockSpec((w.shape[0] // n_slices, w.shape[1]), slice_of))
    outs = pl.pallas_call(
        functools.partial(_attn_kernel, n_cast=len(cast_ws)),
        grid=(nb, steps, PAIRS),
        in_specs=in_specs + w_specs,
        out_specs=[pl.BlockSpec((None, None, ATT_STEP, LANES), lambda b, j, p: (b, p, j, 0))]
                  + w_specs,
        out_shape=[jax.ShapeDtypeStruct((nb, PAIRS, seq_len, LANES), BF16)]
                  + [jax.ShapeDtypeStruct(w.shape, BF16) for w in cast_ws],
        scratch_shapes=scratch,
        compiler_params=pltpu.CompilerParams(
            dimension_semantics=("arbitrary", "arbitrary", "arbitrary"),
            vmem_limit_bytes=VMEM_LIMIT),
        name="attn",
    )(*args, *cast_ws)
    return outs[0], outs[1:]


def _kv_tail_kernel(k0, v0, k1, v1, k2, v2, o0, o1, o2, nat_ref):
    for (k_ref, v_ref, o_ref, d) in ((k0, v0, o0, DILATIONS[0]), (k1, v1, o1, DILATIONS[1]),
                                     (k2, v2, o2, DILATIONS[2])):
        for kvi, ref in enumerate((k_ref, v_ref)):
            for p in range(PAIRS):
                rows = slice(p * LANES, (p + 1) * LANES)
                if d == 1:
                    o_ref[kvi, rows, :] = ref[0, p].astype(F32).T
                    continue
                for r in range(d):
                    nat_ref[pl.ds(r, SPAN, stride=d), :] = ref[r, p].astype(F32)
                for c in range(d):
                    o_ref[kvi, rows, c * SPAN:(c + 1) * SPAN] = nat_ref[c * SPAN:(c + 1) * SPAN, :].T


def _kv_tails(ks, vs, seq_len):
    nb = ks[0].shape[0]
    in_specs, args, out_specs, out_shape = [], [], [], []
    for g, d in enumerate(DILATIONS):
        last = seq_len // d // SPAN - 1
        spec = pl.BlockSpec((None, d, PAIRS, SPAN, LANES), lambda b, last=last: (b, 0, 0, last, 0))
        in_specs += [spec, spec]
        args += [ks[g], vs[g]]
        out_specs.append(pl.BlockSpec((None, 2, GROUP_W, SPAN * d), lambda b: (b, 0, 0, 0)))
        out_shape.append(jax.ShapeDtypeStruct((nb, 2, GROUP_W, SPAN * d), F32))
    return pl.pallas_call(
        _kv_tail_kernel,
        grid=(nb,), in_specs=in_specs, out_specs=out_specs, out_shape=out_shape,
        scratch_shapes=[pltpu.VMEM((SPAN * max(DILATIONS), LANES), F32)],
        compiler_params=pltpu.CompilerParams(dimension_semantics=("arbitrary",),
                                             vmem_limit_bytes=VMEM_LIMIT),
        name="kv_tail",
    )(*args)


def _mlstm_chunk(q2d, k2d, v2d, om2d, gif, gt, st_s, m_s, hm_ref, lc):
    lf_c = _log_sigmoid(gif)
    lf_r = _log_sigmoid(gt[M_HEADS:2 * M_HEADS, :])
    row = lax.broadcasted_iota(jnp.int32, (lc, lc), 0)
    col = lax.broadcasted_iota(jnp.int32, (lc, lc), 1)
    causal = col <= row
    tril = causal.astype(F32)
    triu = (row <= col).astype(F32)
    b_c = jnp.dot(tril, lf_c, precision=lax.Precision.HIGHEST, preferred_element_type=F32)
    b_r = jnp.dot(lf_r, triu, precision=lax.Precision.HIGHEST, preferred_element_type=F32)
    lane = lax.broadcasted_iota(jnp.int32, (lc, LANES), 1)
    halves = (lane < M_DQK, lane >= M_DQK)
    sub = lax.broadcasted_iota(jnp.int32, (LANES, 1), 0)
    ones_blk = jnp.ones((lc, LANES), BF16)
    qscale = M_DQK ** -0.5

    states, p1, p2 = {}, {}, {}
    all_pairs = tuple(range(M_HEADS // 2))
    heads_of = lambda pairs: [2 * p + hh for p in pairs for hh in range(2)]

    def phase1(pairs=all_pairs):
        for p in pairs:
            states[p] = st_s[p]
        for h in heads_of(pairs):
            p, hh = divmod(h, 2)
            q2_ = q2d[:, p * LANES:(p + 1) * LANES]
            k2_ = k2d[:, p * LANES:(p + 1) * LANES]
            qa = jnp.where(halves[hh], q2_, jnp.zeros_like(q2_)) * jnp.asarray(qscale, q2_.dtype)
            s = lax.dot_general(qa, k2_, (((1,), (1,)), ((), ())), preferred_element_type=F32)
            qc = jnp.dot(qa, states[p].astype(BF16), preferred_element_type=F32)
            p1[h] = (s, qc)

    def phase2(pairs=all_pairs):
        for h in heads_of(pairs):
            p, hh = divmod(h, 2)
            s, qc = p1[h]
            bcol = b_c[:, M_HEADS + h:M_HEADS + h + 1]
            igcol = gif[:, h:h + 1]
            a_row = gt[h:h + 1, :] - b_r[h:h + 1, :]
            amat = jnp.where(causal, a_row, NEG)
            m_prev = m_s[h:h + 1, 0:1]
            g_t = jnp.maximum(m_prev, jnp.max(amat, axis=-1, keepdims=True))
            wqk = (jnp.exp(amat - g_t) * s).astype(BF16)
            dec = jnp.exp(m_prev - g_t)
            m_t = bcol + g_t
            m_new = m_t[lc - 1:lc, :]
            b_last = bcol[lc - 1:lc, :]
            dstate = jnp.exp(b_last + m_prev - m_new)
            ws = jnp.exp(b_last - bcol + igcol - m_new)
            k2_ = k2d[:, p * LANES:(p + 1) * LANES]
            ka = jnp.where(halves[hh], k2_, jnp.zeros_like(k2_))
            kws = (ka.astype(F32) * ws).astype(BF16)
            m_s[h:h + 1, :] = jnp.broadcast_to(m_new, (1, LANES))
            p2[h] = (wqk, dec * qc, jnp.exp(-m_t), dstate, kws)

    def phase3(pairs=all_pairs):
        upds = {}
        for h in heads_of(pairs):
            wqk, dqc, floor, _, kws = p2[h]
            v1 = jnp.concatenate([v2d[:, h * M_DV:(h + 1) * M_DV], ones_blk], axis=1)
            num_den = dqc + jnp.dot(wqk, v1, preferred_element_type=F32)
            num = num_den[:, :M_DV]
            den = num_den[:, M_DV:]
            hval = num / jnp.maximum(jnp.abs(den), floor)
            gate = jax.nn.sigmoid(om2d[:, h * M_DV:(h + 1) * M_DV].astype(F32))
            hm_ref[:, h * M_DV:(h + 1) * M_DV] = (gate * hval).astype(hm_ref.dtype)
            upds[h] = lax.dot_general(kws, v1, (((0,), (0,)), ((), ())),
                                      preferred_element_type=F32)
        for p in pairs:
            drow = jnp.where(sub < M_DQK, p2[2 * p][3], p2[2 * p + 1][3])
            st_s[p] = drow * states[p] + upds[2 * p] + upds[2 * p + 1]

    return phase1, phase2, phase3


FF_CHUNK = 256


DEC_HEADS = HEADS // 2
DEC_ROWS = DEC_HEADS * HEAD_DIM


def _decode_attn_step(b, qkv_ref, c_refs, o_refs, ot_ref):
    sel = lax.broadcasted_iota(jnp.int32, (DEC_ROWS, LANES), 1) == b
    scale = HEAD_DIM ** -0.5

    def column(i):
        return jnp.sum(jnp.where(sel, qkv_ref[i], 0.0), axis=1, keepdims=True)

    def head(col, h):
        return col[h * HEAD_DIM:(h + 1) * HEAD_DIM, :]

    qc = [column(g) * scale for g in range(N_GROUPS)]
    kn = [column(N_GROUPS + g) for g in range(N_GROUPS)]
    vn = [column(2 * N_GROUPS + g) for g in range(N_GROUPS)]
    cols = []
    for h in range(DEC_HEADS):
        scores, m_h = [], None
        for g, d in enumerate(DILATIONS):
            wb = c_refs[g].shape[-1]
            qh = head(qc[g], h)
            s = jnp.sum(c_refs[g][0, h] * qh, axis=0, keepdims=True)
            pos = lax.broadcasted_iota(jnp.int32, (1, wb), 1)
            s = jnp.where((pos & (d - 1)) == 0, s, NEG)
            s_new = jnp.sum(head(kn[g], h) * qh, axis=0, keepdims=True)
            m_g = jnp.maximum(jnp.max(s, axis=1, keepdims=True), s_new)
            m_h = m_g if m_h is None else jnp.maximum(m_h, m_g)
            scores.append((s, s_new))
        l_h = jnp.zeros((1, 1), F32)
        acc = jnp.zeros((HEAD_DIM, 1), F32)
        for g in range(N_GROUPS):
            s, s_new = scores[g]
            p = jnp.exp(s - m_h)
            p_new = jnp.exp(s_new - m_h)
            l_h = l_h + jnp.sum(p, axis=1, keepdims=True) + p_new
            acc = acc + jnp.sum(c_refs[g][1, h] * p, axis=1, keepdims=True) + p_new * head(vn[g], h)
        cols.append(acc / l_h)
    ot_ref[...] = jnp.broadcast_to(jnp.concatenate(cols, axis=0), ot_ref.shape)
    for g in range(N_GROUPS):
        wb = c_refs[g].shape[-1]
        last = lax.broadcasted_iota(jnp.int32, (HEAD_DIM, wb), 1) == wb - 1
        for kvi, new in ((0, kn[g]), (1, vn[g])):
            for h in range(DEC_HEADS):
                rolled = pltpu.roll(c_refs[g][kvi, h], wb - 1, 1)
                o_refs[g][kvi, h] = jnp.where(last, head(new, h), rolled)


def _merge_ffn_kernel(x_ref, o_ref, hm_ref, ga_ref, gb_ref, wa_ref, wm_ref, wo_ref, g2_ref,
                      wg_ref, wu_ref, wd_ref, *rest, with_decode):
    if with_decode:
        qkv_ref, c0, c1, c2, y_ref, o0, o1, o2, ot_ref = rest
        _decode_attn_step(pl.program_id(0) // 2, qkv_ref, (c0, c1, c2), (o0, o1, o2), ot_ref)
    else:
        (y_ref,) = rest
    yb = jnp.dot(hm_ref[...].astype(BF16), wm_ref[...], preferred_element_type=F32)
    o_att = jnp.concatenate([o_ref[p] for p in range(PAIRS)], axis=1).astype(BF16)
    ya = jnp.dot(o_att, wa_ref[...], preferred_element_type=F32)
    mixed = (jax.nn.sigmoid(ga_ref[...].astype(F32)) * ya
             + jax.nn.sigmoid(gb_ref[...].astype(F32)) * yb)
    x2 = x_ref[...] + jnp.dot(mixed.astype(BF16), wo_ref[...], preferred_element_type=F32)
    h2 = (x2 * lax.rsqrt(jnp.mean(x2 * x2, axis=-1, keepdims=True) + NORM_EPS)
          * g2_ref[...]).astype(BF16)
    acc = x2
    d_ff = wg_ref.shape[1]

    def gate_up(c):
        gt = jnp.dot(h2, wg_ref[:, c:c + FF_CHUNK], preferred_element_type=F32)
        up = jnp.dot(h2, wu_ref[:, c:c + FF_CHUNK], preferred_element_type=F32)
        return gt, up

    chunks = list(range(0, d_ff, FF_CHUNK))
    nxt = gate_up(chunks[0])
    for i, c in enumerate(chunks):
        gt, up = nxt
        if i + 1 < len(chunks):
            nxt = gate_up(chunks[i + 1])
        ff = (gt * jax.nn.sigmoid(gt) * up).astype(BF16)
        acc = acc + jnp.dot(ff, wd_ref[c:c + FF_CHUNK, :], preferred_element_type=F32)
    y_ref[...] = acc


def _merge_ffn(x2d, o_att, hm, ga, gb, wa, wm, wo, g2, wg, wu, wd, seq_len, tm, decode=None):
    m_rows, d_model = x2d.shape
    tiles_per_seq = seq_len // tm
    steps = m_rows // tm
    row = lambda w: pl.BlockSpec((tm, w), lambda i: (i, 0))
    o_spec = pl.BlockSpec((None, PAIRS, tm, LANES),
                          lambda i: (i // tiles_per_seq, 0, i % tiles_per_seq, 0))
    in_specs = [row(d_model), o_spec, row(M_V_W), row(D_MODEL), row(D_MODEL),
                _const_spec(wa.shape), _const_spec(wm.shape), _const_spec(wo.shape),
                _const_spec((1, d_model)), _const_spec(wg.shape), _const_spec(wu.shape),
                _const_spec(wd.shape)]
    args = [x2d, o_att, hm, ga, gb, wa, wm, wo, g2, wg, wu, wd]
    out_specs = [row(d_model)]
    out_shape = [jax.ShapeDtypeStruct((m_rows, d_model), F32)]
    if decode is not None:
        qkv_t, caches_t = decode
        db = caches_t[0].shape[0]
        assert steps == 2 * db, (steps, db)
        in_specs.append(pl.BlockSpec((qkv_t.shape[0], DEC_ROWS, LANES), lambda i: (0, i % 2, 0)))
        args.append(qkv_t)
        for c in caches_t:
            spec = pl.BlockSpec((None, 2, DEC_HEADS, HEAD_DIM, c.shape[-1]),
                                lambda i: (i // 2, 0, i % 2, 0, 0))
            in_specs.append(spec)
            args.append(c)
            out_specs.append(spec)
            out_shape.append(jax.ShapeDtypeStruct(c.shape, F32))
        out_specs.append(pl.BlockSpec((None, None, DEC_ROWS, LANES), lambda i: (i // 2, i % 2, 0, 0)))
        out_shape.append(jax.ShapeDtypeStruct((db, 2, DEC_ROWS, LANES), F32))
    outs = pl.pallas_call(
        functools.partial(_merge_ffn_kernel, with_decode=decode is not None),
        grid=(steps,),
        in_specs=in_specs, out_specs=out_specs, out_shape=out_shape,
        compiler_params=pltpu.CompilerParams(dimension_semantics=("arbitrary",),
                                             vmem_limit_bytes=VMEM_LIMIT),
        name="merge_ffn",
    )(*args)
    if decode is None:
        return outs[0]
    return outs[0], outs[1:4], outs[4]


def _sample_prep_kernel(*refs):
    for i, ref in enumerate(refs[:-1]):
        refs[-1][i] = ref[...].T


def _sample_prep(arrs):
    vm = pl.BlockSpec(memory_space=pltpu.VMEM)
    return pl.pallas_call(
        _sample_prep_kernel, in_specs=[vm] * len(arrs), out_specs=vm,
        out_shape=jax.ShapeDtypeStruct((len(arrs), GROUP_W, LANES), F32),
        name="sample_prep",
    )(*arrs)


def _sample_mlstm_kernel(q_ref, k_ref, v_ref, om_ref, gt_ref, mt_ref, n_ref, c_ref, rexp_ref,
                         hm_ref, co_ref, nt_ref, mo_ref, *, db):
    nh, dqk = M_HEADS, M_DQK
    hi = lax.Precision.HIGHEST
    rexp = rexp_ref[...]
    q_t = q_ref[...].T * (dqk ** -0.5)
    k_t = k_ref[...].T
    n_t = n_ref[...].T
    ig = gt_ref[0:nh, :]
    lf = _log_sigmoid(gt_ref[nh:2 * nh, :])
    m_prev = mt_ref[...]
    m_new = jnp.maximum(lf + m_prev, ig)
    w8 = jnp.exp(ig - m_new)
    dec8 = jnp.exp(lf + m_prev - m_new)
    head_sum = lambda a: lax.dot_general(rexp, a, (((0,), (0,)), ((), ())), precision=hi,
                                         preferred_element_type=F32)
    expand = lambda a: jnp.dot(rexp, a, precision=hi, preferred_element_type=F32)
    qk8 = head_sum(q_t * k_t)
    qn8 = head_sum(q_t * n_t)
    wqk8 = w8 * qk8
    den8 = dec8 * qn8 + wqk8
    inv8 = 1.0 / jnp.maximum(jnp.abs(den8), jnp.exp(-m_new))
    dec_x = expand(dec8)
    w_x = expand(w8)
    nt_ref[...] = dec_x * n_t + w_x * k_t
    mo_ref[...] = m_new
    wk_x = w_x * k_t
    for b in range(db):
        cb = c_ref[b].reshape(nh * dqk, M_DV)
        v_b = v_ref[b]
        v_x = jnp.concatenate([jnp.broadcast_to(v_b[h:h + 1, :], (dqk, M_DV)) for h in range(nh)],
                              axis=0)
        co_ref[b] = (dec_x[:, b:b + 1] * cb + wk_x[:, b:b + 1] * v_x).reshape(nh, dqk, M_DV)
        qc = jnp.sum((q_t[:, b:b + 1] * cb).reshape(nh, dqk, M_DV), axis=1)
        num = dec8[:, b:b + 1] * qc + wqk8[:, b:b + 1] * v_b
        hm_ref[b] = jax.nn.sigmoid(om_ref[b]) * (num * inv8[:, b:b + 1])


def _sample_mlstm(q_t, k_t, v3, om3, gif_t, m_t, n_pad, c_state, rexp, db):
    vm = pl.BlockSpec(memory_space=pltpu.VMEM)
    return pl.pallas_call(
        functools.partial(_sample_mlstm_kernel, db=db),
        in_specs=[vm] * 9,
        out_specs=[vm] * 4,
        out_shape=[jax.ShapeDtypeStruct((db, M_HEADS, M_DV), F32),
                   jax.ShapeDtypeStruct(c_state.shape, F32),
                   jax.ShapeDtypeStruct((M_HEADS * M_DQK, LANES), F32),
                   jax.ShapeDtypeStruct((M_HEADS, LANES), F32)],
        compiler_params=pltpu.CompilerParams(vmem_limit_bytes=VMEM_LIMIT),
        name="sample_mlstm",
    )(q_t, k_t, v3, om3, gif_t, m_t, n_pad, c_state, rexp)


def _rope_tables(pos):
    half = ROT_DIM // 2
    pos = np.asarray(pos, np.float32)
    inv_freq = np.exp(np.float32(-math.log(ROPE_THETA))
                      * np.arange(0, ROT_DIM, 2, dtype=np.float32) / np.float32(ROT_DIM))
    ang = (pos[:, None] * inv_freq[None, :]).astype(np.float32)
    cos, sin = np.cos(ang).astype(np.float32), np.sin(ang).astype(np.float32)
    t = pos.shape[0]
    rest = HEAD_DIM - ROT_DIM
    a = np.concatenate([cos, cos, np.ones((t, rest), np.float32)], axis=1)
    bm = np.concatenate([-sin, np.zeros((t, HEAD_DIM - half), np.float32)], axis=1)
    bp = np.concatenate([np.zeros((t, half), np.float32), sin, np.zeros((t, rest), np.float32)],
                        axis=1)
    return tuple(jnp.asarray(np.concatenate([x, x], axis=1)) for x in (a, bm, bp))


W_PREP_BLK = 512


def _w_prep_kernel(wt_ref, o_ref):
    o_ref[...] = wt_ref[...].T.astype(o_ref.dtype)


def _w_prep(w_in):
    d_model = w_in.shape[0]
    wt = w_in.T
    n_head = _C_GA // W_PREP_BLK
    n_blk = _W_COLS // W_PREP_BLK

    sub = 8

    def src_row(j):
        return (j * (W_PREP_BLK // sub) + jnp.where(j < n_head, 0, _GIF_COLS // sub)) * sub

    return pl.pallas_call(
        _w_prep_kernel,
        grid=(n_blk,),
        in_specs=[pl.BlockSpec((pl.Element(W_PREP_BLK), pl.Element(d_model)),
                               lambda j: (src_row(j), 0))],
        out_specs=pl.BlockSpec((d_model, W_PREP_BLK), lambda j: (0, j)),
        out_shape=jax.ShapeDtypeStruct((d_model, _W_COLS), BF16),
        compiler_params=pltpu.CompilerParams(dimension_semantics=("arbitrary",),
                                             vmem_limit_bytes=VMEM_LIMIT),
        name="w_prep",
    )(wt)


def kernel(x_prompt, x_sample, cache_kv_w128, cache_kv_w512, cache_kv_w2048, state_mlstm_C, state_mlstm_n, state_mlstm_m, norm1_g, w_in, b_if, q_norm_g, k_norm_g, w_att_out, w_m_out, w_o, norm2_g, w_gate, w_up, w_down):
    nb, seq_len, d_model = x_prompt.shape
    db, dec_seq, _ = x_sample.shape
    assert dec_seq == 1 and d_model == D_MODEL and seq_len % ATT_STEP == 0 and db <= LANES
    caches = (cache_kv_w128, cache_kv_w512, cache_kv_w2048)

    w_perm = _w_prep(w_in)
    w_gif = jnp.zeros((d_model, LANES), BF16).at[:, :_GIF_COLS].set(
        w_in[:, _C_GA:_C_GA + _GIF_COLS].astype(BF16))
    g1 = norm1_g.reshape(1, d_model)
    g2 = norm2_g.reshape(1, d_model)
    bif = jnp.concatenate([b_if, jnp.zeros((LANES - b_if.shape[0],), F32)]).reshape(1, LANES)
    qg = jnp.tile(q_norm_g, HEADS).reshape(1, GROUP_W)
    kg = jnp.tile(k_norm_g, HEADS).reshape(1, GROUP_W)
    hid = np.arange(GROUP_W // 2) // HEAD_DIM
    gmat = jnp.asarray(hid[:, None] == hid[None, :], dtype=BF16)

    m_rows = nb * seq_len
    x2d = x_prompt.reshape(m_rows, d_model)
    tabs_p = _rope_tables(np.arange(seq_len))
    qg_p = qg * (HEAD_DIM ** -0.5 * math.log2(math.e))
    outs = _proj(x2d, seq_len, DILATIONS, 256, False, g1, w_perm, w_gif, bif, qg_p, kg, tabs_p, gmat)
    qs, ks, vs = outs[0:3], outs[3:6], outs[6:9]
    hm, ga, gb, st_p, m_p = outs[9:14]

    o_att, (wa, wm, wo, wg, wu, wd) = _attention(
        qs, ks, vs, seq_len, (w_att_out, w_m_out, w_o, w_gate, w_up, w_down))

    x_s = jnp.zeros((LANES, d_model), F32).at[:db].set(x_sample.reshape(db, d_model))
    tabs_s = _rope_tables(np.full((LANES,), PAST_LEN))
    outs_s = _proj(x_s, LANES, (1, 1, 1), LANES, True, g1, w_perm, w_gif, bif, qg, kg, tabs_s, gmat)
    qkv_t = _sample_prep(outs_s[0:9])
    caches_t = [c.transpose(0, 2, 3, 4, 1) for c in caches]
    y_prompt, kv_st, o_att_cols = _merge_ffn(x2d, o_att, hm, ga, gb, wa, wm, wo, g2, wg, wu, wd,
                                             seq_len, 256, decode=(qkv_t, caches_t))
    y_prompt = y_prompt.reshape(nb, seq_len, d_model)
    kv_s = [c.transpose(0, 4, 1, 2, 3) for c in kv_st]

    tails = _kv_tails(ks, vs, seq_len)
    kv_p = [t.reshape(nb, 2, HEADS, HEAD_DIM, t.shape[-1]).transpose(0, 4, 1, 2, 3) for t in tails]
    c_p = st_p[..., :M_DV].reshape(nb, M_HEADS, M_DQK, M_DV)
    n_p = st_p[..., M_DV].reshape(nb, M_HEADS, M_DQK)
    m_pr = m_p[:, :, 0]

    qm_t, km_t, vm_s, om_s, ga_s, gb_s, gif_s = outs_s[9:16]
    gif_ts = gif_s.T

    rexp = jnp.asarray(np.arange(M_HEADS * M_DQK)[:, None] // M_DQK == np.arange(M_HEADS)[None, :],
                       dtype=F32)
    m_t = jnp.zeros((M_HEADS, LANES), F32).at[:, :db].set(state_mlstm_m.T)
    n_pad = jnp.zeros((LANES, M_HEADS * M_DQK), F32).at[:db].set(state_mlstm_n.reshape(db, -1))
    hm_s3, c_s, n_t, m_so = _sample_mlstm(
        qm_t, km_t, vm_s[:db].reshape(db, M_HEADS, M_DV), om_s[:db].reshape(db, M_HEADS, M_DV),
        gif_ts[:2 * M_HEADS], m_t, n_pad, state_mlstm_C, rexp, db)
    n_s = n_t.T[:db].reshape(db, M_HEADS, M_DQK)
    m_s = m_so[:, :db].T

    o_att_s = o_att_cols[:, :, :, 0].reshape(db, PAIRS, LANES)
    o_att_sp = jnp.zeros((PAIRS, LANES, LANES), F32).at[:, :db].set(o_att_s.transpose(1, 0, 2))[None]
    hm_sp = jnp.zeros((LANES, M_V_W), F32).at[:db].set(hm_s3.reshape(db, M_V_W))
    y_s = _merge_ffn(x_s, o_att_sp, hm_sp, ga_s, gb_s, wa, wm, wo, g2, wg, wu, wd, LANES, LANES)
    y_sample = y_s[:db].reshape(db, 1, d_model)

    return (y_prompt, y_sample, kv_p[0], kv_p[1], kv_p[2], c_p, n_p, m_pr,
            kv_s[0], kv_s[1], kv_s[2], c_s, n_s, m_s)
```

```python
import functools
import math

import jax
import jax.numpy as jnp
import numpy as np
from jax import lax
from jax.experimental import pallas as pl
from jax.experimental.pallas import tpu as pltpu

F32 = jnp.float32
BF16 = jnp.bfloat16

HEAD_DIM = 64
HEADS = 8
GROUP_W = HEADS * HEAD_DIM
N_GROUPS = 3
WINDOWS = (128, 512, 2048)
DILATIONS = (1, 4, 16)
SPAN = 128
ROT_DIM = 16
ROPE_THETA = 500000.0
M_HEADS = 8
M_DQK = 64
M_DV = 128
M_QK_W = M_HEADS * M_DQK
M_V_W = M_HEADS * M_DV
D_MODEL = 1024
PAST_LEN = 8192
NORM_EPS = 1e-6
NEG = -1e30

LANES = 128
PAIRS = GROUP_W // LANES
VMEM_LIMIT = 56 * 1024 * 1024

_ATT_W = N_GROUPS * GROUP_W
_C_QM = 3 * _ATT_W
_C_KM = _C_QM + M_QK_W
_C_VM = _C_KM + M_QK_W
_C_OM = _C_VM + M_V_W
_C_GA = _C_OM + M_V_W
_C_GB = _C_GA + D_MODEL
_W_COLS = _C_GB + D_MODEL
_GIF_COLS = 2 * M_HEADS


def _const_spec(shape):
    nd = len(shape)
    return pl.BlockSpec(shape, lambda *_: (0,) * nd, pipeline_mode=pl.Buffered(1))


def _log_sigmoid(x):
    return jnp.minimum(x, 0.0) - jnp.log1p(jnp.exp(-jnp.abs(x)))


MLSTM_CHUNKS = 2


def _proj_kernel(x_ref, g1_ref, w_ref, wgif_ref, bif_ref, qg_ref, kg_ref, ra_ref, rm_ref, rp_ref,
                 gm_ref,
                 q0_ref, q1_ref, q2_ref, k0_ref, k1_ref, k2_ref, v0_ref, v1_ref, v2_ref,
                 *rest, tm, dils, plain, tiles_per_seq):
    if plain:
        qm_ref, km_ref, vm_ref, om_ref, ga_ref, gb_ref, gif_ref, hs_ref = rest
    else:
        hm_ref, ga_ref, gb_ref, st_ref, mo_ref, hs_ref, st_s, m_s = rest
    d_model = x_ref.shape[1]
    x = x_ref[...]
    xn = x * lax.rsqrt(jnp.mean(x * x, axis=-1, keepdims=True) + NORM_EPS) * g1_ref[...]
    h_nat = xn.astype(BF16)
    n_slab = d_model // LANES
    if any(d > 1 for d in dils):
        for c in range(n_slab):
            hs_ref[c] = xn[:, c * LANES:(c + 1) * LANES]

    def permuted_h(d):
        if d == 1:
            return h_nat
        n = tm // d
        rows = [jnp.concatenate([hs_ref[c, pl.ds(r, n, stride=d), :] for c in range(n_slab)], axis=1)
                for r in range(d)]
        return jnp.concatenate(rows, axis=0).astype(BF16)

    def permuted_tab(ref, d):
        if d == 1:
            t = ref[...]
        else:
            n = tm // d
            t = jnp.concatenate([ref[pl.ds(r, n, stride=d), :] for r in range(d)], axis=0)
        return jnp.concatenate([t] * PAIRS, axis=1)

    gmat = gm_ref[...]

    def head_sumsq(z):
        zz = (z * z).astype(BF16)
        half = GROUP_W // 2
        return jnp.concatenate(
            [jnp.dot(zz[:, :half], gmat, preferred_element_type=F32),
             jnp.dot(zz[:, half:], gmat, preferred_element_type=F32)], axis=1)

    def norm_rope(z, ss, gain, ra, rm, rp):
        y = z * lax.rsqrt(ss * (1.0 / HEAD_DIM) + NORM_EPS) * gain
        return (y * ra + pltpu.roll(y, GROUP_W - ROT_DIM // 2, 1) * rm
                + pltpu.roll(y, ROT_DIM // 2, 1) * rp)

    def store_group(ref, y, d):
        if plain:
            ref[...] = y.astype(ref.dtype)
            return
        n = tm // d
        for p in range(PAIRS):
            ref[:, p] = y[:, p * LANES:(p + 1) * LANES].reshape(d, n, LANES).astype(ref.dtype)

    def seg(c0, width):
        return jnp.dot(h_nat, w_ref[:, c0:c0 + width], preferred_element_type=F32)

    if not plain:
        @pl.when(pl.program_id(0) % tiles_per_seq == 0)
        def _():
            st_s[...] = jnp.zeros_like(st_s)
            m_s[...] = jnp.zeros_like(m_s)

        gif = jnp.dot(h_nat, wgif_ref[...], preferred_element_type=F32) + bif_ref[...]
        qm_b, km_b = seg(_C_QM, M_QK_W).astype(BF16), seg(_C_KM, M_QK_W).astype(BF16)
        vm_b, om_b = seg(_C_VM, M_V_W).astype(BF16), seg(_C_OM, M_V_W).astype(BF16)
        gif_t = gif.T[:2 * M_HEADS, :]
        lc = tm // MLSTM_CHUNKS
        m_chunks = []
        for c in range(MLSTM_CHUNKS):
            rows = slice(c * lc, (c + 1) * lc)
            m_chunks.append(functools.partial(
                _mlstm_chunk, qm_b[rows], km_b[rows], vm_b[rows], om_b[rows], gif[rows],
                gif_t[:, rows], st_s, m_s, hm_ref.at[rows], lc))
        m_live, m_started = [], {}
        m_units = [(c, pairs) for c in range(MLSTM_CHUNKS) for pairs in ((0, 1), (2, 3))]

        def m_advance(i):
            if 0 < i <= len(m_units):
                phases, pairs = m_live.pop()
                phases[2](pairs)
            if i < len(m_units):
                c, pairs = m_units[i]
                if c not in m_started:
                    m_started[c] = m_chunks[c]()
                phases = m_started[c]
                phases[0](pairs)
                phases[1](pairs)
                m_live.append((phases, pairs))

        m_advance(0)

    q_refs = (q0_ref, q1_ref, q2_ref)
    k_refs = (k0_ref, k1_ref, k2_ref)
    v_refs = (v0_ref, v1_ref, v2_ref)
    for g in range(N_GROUPS):
        d = dils[g]
        hg = permuted_h(d)
        ra, rm, rp = (permuted_tab(r, d) for r in (ra_ref, rm_ref, rp_ref))
        cq, ck, cv = (t * _ATT_W + g * GROUP_W for t in range(3))
        zq = jnp.dot(hg, w_ref[:, cq:cq + GROUP_W], preferred_element_type=F32)
        zk = jnp.dot(hg, w_ref[:, ck:ck + GROUP_W], preferred_element_type=F32)
        zv = jnp.dot(hg, w_ref[:, cv:cv + GROUP_W], preferred_element_type=F32)
        ssq, ssk = head_sumsq(zq), head_sumsq(zk)
        store_group(v_refs[g], zv, d)
        store_group(q_refs[g], norm_rope(zq, ssq, qg_ref[...], ra, rm, rp), d)
        store_group(k_refs[g], norm_rope(zk, ssk, kg_ref[...], ra, rm, rp), d)

        if not plain:
            m_advance(g + 1)

    if plain:
        segments = ((qm_ref, _C_QM, M_QK_W), (km_ref, _C_KM, M_QK_W), (vm_ref, _C_VM, M_V_W),
                    (om_ref, _C_OM, M_V_W), (ga_ref, _C_GA, D_MODEL), (gb_ref, _C_GB, D_MODEL))
        zg = jnp.dot(h_nat, wgif_ref[...], preferred_element_type=F32)
        gif_ref[...] = zg + bif_ref[...]
    else:
        segments = ((ga_ref, _C_GA, D_MODEL), (gb_ref, _C_GB, D_MODEL))
    for si, (ref, c0, width) in enumerate(segments):
        for cc in range(0, width, GROUP_W):
            ref[:, cc:cc + GROUP_W] = seg(c0 + cc, GROUP_W).astype(ref.dtype)
        if not plain and si == 0:
            m_advance(N_GROUPS + 1)
            st_ref[...] = st_s[...]
            mo_ref[...] = m_s[...]


def _proj(x2d, seq_len, dils, tm, plain, g1, w_perm, w_gif, bif, qg, kg, rope_tabs, gmat):
    m_rows, d_model = x2d.shape
    nb = m_rows // seq_len
    tiles_per_seq = seq_len // tm
    grid = (m_rows // tm,)
    row_spec = lambda w: pl.BlockSpec((tm, w), lambda i: (i, 0))
    tab_spec = pl.BlockSpec((tm, LANES), lambda i: (i % tiles_per_seq, 0))
    sds = jax.ShapeDtypeStruct

    widths = (M_QK_W, M_QK_W, M_V_W, M_V_W, D_MODEL, D_MODEL)
    scratch = [pltpu.VMEM((d_model // LANES, tm, LANES), F32)]
    if plain:
        out_shape = ([sds((m_rows, GROUP_W), F32)] * 9 + [sds((m_rows, w), F32) for w in widths]
                     + [sds((m_rows, LANES), F32)])
        out_specs = ([row_spec(GROUP_W)] * 9 + [row_spec(w) for w in widths] + [row_spec(LANES)])
    else:
        def grp_spec(d):
            return pl.BlockSpec((None, d, PAIRS, tm // d, LANES),
                                lambda i: (i // tiles_per_seq, 0, 0, i % tiles_per_seq, 0))
        grp_shape = lambda d: sds((nb, d, PAIRS, seq_len // d, LANES), BF16)
        state_spec = lambda *dims: pl.BlockSpec((None,) + dims,
                                                lambda i: (i // tiles_per_seq,) + (0,) * len(dims))
        st_dims, m_dims = (M_HEADS // 2, LANES, 2 * LANES), (M_HEADS, LANES)
        row_w = (M_V_W, D_MODEL, D_MODEL)
        out_shape = ([grp_shape(d) for d in dils] * 3 + [sds((m_rows, w), BF16) for w in row_w]
                     + [sds((nb,) + st_dims, F32), sds((nb,) + m_dims, F32)])
        out_specs = ([grp_spec(d) for d in dils] * 3 + [row_spec(w) for w in row_w]
                     + [state_spec(*st_dims), state_spec(*m_dims)])
        scratch += [pltpu.VMEM(st_dims, F32), pltpu.VMEM(m_dims, F32)]
    in_specs = [row_spec(d_model), _const_spec((1, d_model)), _const_spec(w_perm.shape),
                _const_spec(w_gif.shape), _const_spec((1, LANES)), _const_spec((1, GROUP_W)),
                _const_spec((1, GROUP_W)), tab_spec, tab_spec, tab_spec, _const_spec(gmat.shape)]
    return pl.pallas_call(
        functools.partial(_proj_kernel, tm=tm, dils=dils, plain=plain,
                          tiles_per_seq=tiles_per_seq),
        grid=grid, in_specs=in_specs, out_specs=out_specs, out_shape=out_shape,
        scratch_shapes=scratch,
        compiler_params=pltpu.CompilerParams(dimension_semantics=("arbitrary",),
                                             vmem_limit_bytes=VMEM_LIMIT),
        name="proj",
    )(x2d, g1, w_perm, w_gif, bif, qg, kg, *rope_tabs, gmat)


ATT_STEP = SPAN * max(DILATIONS)
FAR_PITCH = SPAN + 8


def _attn_kernel(q0, k0, v0, kh0, vh0, q1, k1, v1, kh1, vh1, q2, k2, v2, kh2, vh2,
                 *rest, n_cast):
    w_f32, o_ref, w_bf16 = rest[:n_cast], rest[n_cast], rest[n_cast + 1:2 * n_cast + 1]
    acc_s, m_s, l_s, far_s = rest[2 * n_cast + 1:]
    j = pl.program_id(1)
    for src, dst in zip(w_f32, w_bf16):
        dst[...] = src[...].astype(dst.dtype)

    def window(cur, halo, r, bi):
        prev = halo[r] if bi == 0 else cur[r, (bi - 1) * SPAN:bi * SPAN, :]
        return jnp.concatenate([prev, cur[r, bi * SPAN:(bi + 1) * SPAN, :]], axis=0)

    qi = lax.broadcasted_iota(jnp.int32, (SPAN, 2 * SPAN), 0)
    ci = lax.broadcasted_iota(jnp.int32, (SPAN, 2 * SPAN), 1)
    band = (ci >= qi) & (ci <= qi + SPAN)
    bias_band = jnp.where(band, 0.0, NEG).astype(F32)
    bias_first = jnp.where(band & (ci >= SPAN), 0.0, NEG).astype(F32)
    lane_q = lax.broadcasted_iota(jnp.int32, (SPAN, LANES), 1)
    lane_kv = lax.broadcasted_iota(jnp.int32, (2 * SPAN, LANES), 1)
    halves_q = (lane_q < HEAD_DIM, lane_q >= HEAD_DIM)
    halves_kv = (lane_kv < HEAD_DIM, lane_kv >= HEAD_DIM)

    d0, d1, d2 = DILATIONS
    units = (
        [(0, bi, pl.ds(bi * SPAN, SPAN)) for bi in range(ATT_STEP // d0 // SPAN)],
        [(r, bi, pl.ds(bi * SPAN * d1 + r, SPAN, stride=d1))
         for r in range(d1) for bi in range(ATT_STEP // d1 // SPAN)],
        [(r, 0, pl.ds(r * FAR_PITCH, SPAN)) for r in range(d2)],
    )
    q_refs = (q0, q1, q2)
    kv_refs = ((k0, kh0, v0, vh0), (k1, kh1, v1, vh1), (k2, kh2, v2, vh2))

    def scores(g):
        k_cur, k_halo = kv_refs[g][:2]
        out = []
        for r, bi, _ in units[g]:
            q2_ = q_refs[g][r, bi * SPAN:(bi + 1) * SPAN, :]
            kk = window(k_cur, k_halo, r, bi)
            bias = jnp.where(j == 0, bias_first, bias_band) if bi == 0 else bias_band
            ss = []
            for hh in range(2):
                qa = jnp.where(halves_q[hh], q2_, jnp.zeros_like(q2_))
                ss.append(lax.dot_general(qa, kk, (((1,), (1,)), ((), ())),
                                          preferred_element_type=F32) + bias)
            out.append(ss)
        return out

    def finish(g, all_ss):
        v_cur, v_halo = kv_refs[g][2:]
        probs = []
        for ss in all_ss:
            mxs = [jnp.max(s, axis=-1, keepdims=True) for s in ss]
            ps = [jnp.exp2(s - mx) for s, mx in zip(ss, mxs)]
            ls = [jnp.sum(p, axis=-1, keepdims=True) for p in ps]
            probs.append(([p.astype(BF16) for p in ps], mxs, ls))
        for (r, bi, sl), (ps, mxs, ls) in zip(units[g], probs):
            vv = window(v_cur, v_halo, r, bi)
            acc = None
            for hh in range(2):
                vh = jnp.where(halves_kv[hh], vv, jnp.zeros_like(vv))
                a = jnp.dot(ps[hh], vh, preferred_element_type=F32)
                acc = a if acc is None else acc + a
            stats = (acc, jnp.where(halves_q[0], mxs[0], mxs[1]),
                     jnp.where(halves_q[0], ls[0], ls[1]))
            if g == N_GROUPS - 1:
                for k, val in enumerate(stats):
                    far_s[k, sl, :] = val
            else:
                for ref, val in zip((acc_s, m_s, l_s), stats):
                    ref[g, sl, :] = val

    for g in range(N_GROUPS):
        finish(g, scores(g))

    def far_rows(k, ci_):
        sub = 8
        pieces = [far_s[k, pl.ds(a * sub * FAR_PITCH + ci_ * (SPAN // d2) + jj, sub,
                                 stride=FAR_PITCH), :]
                  for jj in range(SPAN // d2) for a in range(d2 // sub)]
        return jnp.concatenate(pieces, axis=0)

    def combine(ci_, c):
        sl = pl.ds(pl.multiple_of(ci_ * SPAN, SPAN), SPAN)
        accs = [acc_s[0, sl, :], acc_s[1, sl, :], far_rows(0, ci_)]
        ms = [m_s[0, sl, :], m_s[1, sl, :], far_rows(1, ci_)]
        ls = [l_s[0, sl, :], l_s[1, sl, :], far_rows(2, ci_)]
        mm = jnp.maximum(jnp.maximum(ms[0], ms[1]), ms[2])
        es = [jnp.exp2(m - mm) for m in ms]
        num = es[0] * accs[0] + es[1] * accs[1] + es[2] * accs[2]
        den = es[0] * ls[0] + es[1] * ls[1] + es[2] * ls[2]
        o_ref[sl, :] = (num / den).astype(o_ref.dtype)
        return c
    lax.fori_loop(0, ATT_STEP // SPAN, combine, 0, unroll=4)


def _attention(qs, ks, vs, seq_len, cast_ws):
    nb = qs[0].shape[0]
    steps = seq_len // ATT_STEP
    in_specs, args, scratch = [], [], []
    for g, d in enumerate(DILATIONS):
        rows = ATT_STEP // d
        cur = pl.BlockSpec((None, d, None, rows, LANES), lambda b, j, p: (b, 0, p, j, 0))
        ratio = rows // SPAN
        halo = pl.BlockSpec((None, d, None, SPAN, LANES),
                            lambda b, j, p, ratio=ratio: (b, 0, p, jnp.maximum(j * ratio - 1, 0), 0))
        in_specs += [cur, cur, cur, halo, halo]
        args += [qs[g], ks[g], vs[g], ks[g], vs[g]]
    scratch += [pltpu.VMEM((N_GROUPS - 1, ATT_STEP, LANES), F32)] * 3
    scratch += [pltpu.VMEM((3, max(DILATIONS) * FAR_PITCH, LANES), F32)]
    n_slices = max(nb * steps * PAIRS // 2, 1)
    slice_of = lambda b, j, p: (((b * steps + j) * PAIRS + p) // 2, 0)
    w_specs = []
    for w in cast_ws:
        assert w.shape[0] % (16 * n_slices) == 0, (w.shape, n_slices)
        w_specs.append(pl.BlockSpec((w.shape[0] // n_slices, w.shape[1]), slice_of))
    outs = pl.pallas_call(
        functools.partial(_attn_kernel, n_cast=len(cast_ws)),
        grid=(nb, steps, PAIRS),
        in_specs=in_specs + w_specs,
        out_specs=[pl.BlockSpec((None, None, ATT_STEP, LANES), lambda b, j, p: (b, p, j, 0))]
                  + w_specs,
        out_shape=[jax.ShapeDtypeStruct((nb, PAIRS, seq_len, LANES), BF16)]
                  + [jax.ShapeDtypeStruct(w.shape, BF16) for w in cast_ws],
        scratch_shapes=scratch,
        compiler_params=pltpu.CompilerParams(
            dimension_semantics=("arbitrary", "arbitrary", "arbitrary"),
            vmem_limit_bytes=VMEM_LIMIT),
        name="attn",
    )(*args, *cast_ws)
    return outs[0], outs[1:]


def _kv_tail_kernel(k0, v0, k1, v1, k2, v2, o0, o1, o2, nat_ref):
    for (k_ref, v_ref, o_ref, d) in ((k0, v0, o0, DILATIONS[0]), (k1, v1, o1, DILATIONS[1]),
                                     (k2, v2, o2, DILATIONS[2])):
        for kvi, ref in enumerate((k_ref, v_ref)):
            for p in range(PAIRS):
                rows = slice(p * LANES, (p + 1) * LANES)
                if d == 1:
                    o_ref[kvi, rows, :] = ref[0, p].astype(F32).T
                    continue
                for r in range(d):
                    nat_ref[pl.ds(r, SPAN, stride=d), :] = ref[r, p].astype(F32)
                for c in range(d):
                    o_ref[kvi, rows, c * SPAN:(c + 1) * SPAN] = nat_ref[c * SPAN:(c + 1) * SPAN, :].T


def _kv_tails(ks, vs, seq_len):
    nb = ks[0].shape[0]
    in_specs, args, out_specs, out_shape = [], [], [], []
    for g, d in enumerate(DILATIONS):
        last = seq_len // d // SPAN - 1
        spec = pl.BlockSpec((None, d, PAIRS, SPAN, LANES), lambda b, last=last: (b, 0, 0, last, 0))
        in_specs += [spec, spec]
        args += [ks[g], vs[g]]
        out_specs.append(pl.BlockSpec((None, 2, GROUP_W, SPAN * d), lambda b: (b, 0, 0, 0)))
        out_shape.append(jax.ShapeDtypeStruct((nb, 2, GROUP_W, SPAN * d), F32))
    return pl.pallas_call(
        _kv_tail_kernel,
        grid=(nb,), in_specs=in_specs, out_specs=out_specs, out_shape=out_shape,
        scratch_shapes=[pltpu.VMEM((SPAN * max(DILATIONS), LANES), F32)],
        compiler_params=pltpu.CompilerParams(dimension_semantics=("arbitrary",),
                                             vmem_limit_bytes=VMEM_LIMIT),
        name="kv_tail",
    )(*args)


def _mlstm_chunk(q2d, k2d, v2d, om2d, gif, gt, st_s, m_s, hm_ref, lc):
    lf_c = _log_sigmoid(gif)
    lf_r = _log_sigmoid(gt[M_HEADS:2 * M_HEADS, :])
    row = lax.broadcasted_iota(jnp.int32, (lc, lc), 0)
    col = lax.broadcasted_iota(jnp.int32, (lc, lc), 1)
    causal = col <= row
    tril = causal.astype(F32)
    triu = (row <= col).astype(F32)
    b_c = jnp.dot(tril, lf_c, precision=lax.Precision.HIGHEST, preferred_element_type=F32)
    b_r = jnp.dot(lf_r, triu, precision=lax.Precision.HIGHEST, preferred_element_type=F32)
    lane = lax.broadcasted_iota(jnp.int32, (lc, LANES), 1)
    halves = (lane < M_DQK, lane >= M_DQK)
    sub = lax.broadcasted_iota(jnp.int32, (LANES, 1), 0)
    ones_blk = jnp.ones((lc, LANES), BF16)
    qscale = M_DQK ** -0.5

    states, p1, p2 = {}, {}, {}
    all_pairs = tuple(range(M_HEADS // 2))
    heads_of = lambda pairs: [2 * p + hh for p in pairs for hh in range(2)]

    def phase1(pairs=all_pairs):
        for p in pairs:
            states[p] = st_s[p]
        for h in heads_of(pairs):
            p, hh = divmod(h, 2)
            q2_ = q2d[:, p * LANES:(p + 1) * LANES]
            k2_ = k2d[:, p * LANES:(p + 1) * LANES]
            qa = jnp.where(halves[hh], q2_, jnp.zeros_like(q2_)) * jnp.asarray(qscale, q2_.dtype)
            s = lax.dot_general(qa, k2_, (((1,), (1,)), ((), ())), preferred_element_type=F32)
            qc = jnp.dot(qa, states[p].astype(BF16), preferred_element_type=F32)
            p1[h] = (s, qc)

    def phase2(pairs=all_pairs):
        for h in heads_of(pairs):
            p, hh = divmod(h, 2)
            s, qc = p1[h]
            bcol = b_c[:, M_HEADS + h:M_HEADS + h + 1]
            igcol = gif[:, h:h + 1]
            a_row = gt[h:h + 1, :] - b_r[h:h + 1, :]
            amat = jnp.where(causal, a_row, NEG)
            m_prev = m_s[h:h + 1, 0:1]
            g_t = jnp.maximum(m_prev, jnp.max(amat, axis=-1, keepdims=True))
            wqk = (jnp.exp(amat - g_t) * s).astype(BF16)
            dec = jnp.exp(m_prev - g_t)
            m_t = bcol + g_t
            m_new = m_t[lc - 1:lc, :]
            b_last = bcol[lc - 1:lc, :]
            dstate = jnp.exp(b_last + m_prev - m_new)
            ws = jnp.exp(b_last - bcol + igcol - m_new)
            k2_ = k2d[:, p * LANES:(p + 1) * LANES]
            ka = jnp.where(halves[hh], k2_, jnp.zeros_like(k2_))
            kws = (ka.astype(F32) * ws).astype(BF16)
            m_s[h:h + 1, :] = jnp.broadcast_to(m_new, (1, LANES))
            p2[h] = (wqk, dec * qc, jnp.exp(-m_t), dstate, kws)

    def phase3(pairs=all_pairs):
        upds = {}
        for h in heads_of(pairs):
            wqk, dqc, floor, _, kws = p2[h]
            v1 = jnp.concatenate([v2d[:, h * M_DV:(h + 1) * M_DV], ones_blk], axis=1)
            num_den = dqc + jnp.dot(wqk, v1, preferred_element_type=F32)
            num = num_den[:, :M_DV]
            den = num_den[:, M_DV:]
            hval = num / jnp.maximum(jnp.abs(den), floor)
            gate = jax.nn.sigmoid(om2d[:, h * M_DV:(h + 1) * M_DV].astype(F32))
            hm_ref[:, h * M_DV:(h + 1) * M_DV] = (gate * hval).astype(hm_ref.dtype)
            upds[h] = lax.dot_general(kws, v1, (((0,), (0,)), ((), ())),
                                      preferred_element_type=F32)
        for p in pairs:
            drow = jnp.where(sub < M_DQK, p2[2 * p][3], p2[2 * p + 1][3])
            st_s[p] = drow * states[p] + upds[2 * p] + upds[2 * p + 1]

    return phase1, phase2, phase3


FF_CHUNK = 256


DEC_HEADS = HEADS // 2
DEC_ROWS = DEC_HEADS * HEAD_DIM


def _decode_attn_step(b, qkv_ref, c_refs, o_refs, ot_ref):
    sel = lax.broadcasted_iota(jnp.int32, (DEC_ROWS, LANES), 1) == b
    scale = HEAD_DIM ** -0.5

    def column(i):
        return jnp.sum(jnp.where(sel, qkv_ref[i], 0.0), axis=1, keepdims=True)

    def head(col, h):
        return col[h * HEAD_DIM:(h + 1) * HEAD_DIM, :]

    qc = [column(g) * scale for g in range(N_GROUPS)]
    kn = [column(N_GROUPS + g) for g in range(N_GROUPS)]
    vn = [column(2 * N_GROUPS + g) for g in range(N_GROUPS)]
    cols = []
    for h in range(DEC_HEADS):
        scores, m_h = [], None
        for g, d in enumerate(DILATIONS):
            wb = c_refs[g].shape[-1]
            qh = head(qc[g], h)
            s = jnp.sum(c_refs[g][0, h] * qh, axis=0, keepdims=True)
            pos = lax.broadcasted_iota(jnp.int32, (1, wb), 1)
            s = jnp.where((pos & (d - 1)) == 0, s, NEG)
            s_new = jnp.sum(head(kn[g], h) * qh, axis=0, keepdims=True)
            m_g = jnp.maximum(jnp.max(s, axis=1, keepdims=True), s_new)
            m_h = m_g if m_h is None else jnp.maximum(m_h, m_g)
            scores.append((s, s_new))
        l_h = jnp.zeros((1, 1), F32)
        acc = jnp.zeros((HEAD_DIM, 1), F32)
        for g in range(N_GROUPS):
            s, s_new = scores[g]
            p = jnp.exp(s - m_h)
            p_new = jnp.exp(s_new - m_h)
            l_h = l_h + jnp.sum(p, axis=1, keepdims=True) + p_new
            acc = acc + jnp.sum(c_refs[g][1, h] * p, axis=1, keepdims=True) + p_new * head(vn[g], h)
        cols.append(acc / l_h)
    ot_ref[...] = jnp.broadcast_to(jnp.concatenate(cols, axis=0), ot_ref.shape)
    for g in range(N_GROUPS):
        wb = c_refs[g].shape[-1]
        last = lax.broadcasted_iota(jnp.int32, (HEAD_DIM, wb), 1) == wb - 1
        for kvi, new in ((0, kn[g]), (1, vn[g])):
            for h in range(DEC_HEADS):
                rolled = pltpu.roll(c_refs[g][kvi, h], wb - 1, 1)
                o_refs[g][kvi, h] = jnp.where(last, head(new, h), rolled)


def _merge_ffn_kernel(x_ref, o_ref, hm_ref, ga_ref, gb_ref, wa_ref, wm_ref, wo_ref, g2_ref,
                      wg_ref, wu_ref, wd_ref, *rest, with_decode):
    if with_decode:
        qkv_ref, c0, c1, c2, y_ref, o0, o1, o2, ot_ref = rest
        _decode_attn_step(pl.program_id(0) // 2, qkv_ref, (c0, c1, c2), (o0, o1, o2), ot_ref)
    else:
        (y_ref,) = rest
    yb = jnp.dot(hm_ref[...].astype(BF16), wm_ref[...], preferred_element_type=F32)
    o_att = jnp.concatenate([o_ref[p] for p in range(PAIRS)], axis=1).astype(BF16)
    ya = jnp.dot(o_att, wa_ref[...], preferred_element_type=F32)
    mixed = (jax.nn.sigmoid(ga_ref[...].astype(F32)) * ya
             + jax.nn.sigmoid(gb_ref[...].astype(F32)) * yb)
    x2 = x_ref[...] + jnp.dot(mixed.astype(BF16), wo_ref[...], preferred_element_type=F32)
    h2 = (x2 * lax.rsqrt(jnp.mean(x2 * x2, axis=-1, keepdims=True) + NORM_EPS)
          * g2_ref[...]).astype(BF16)
    acc = x2
    d_ff = wg_ref.shape[1]

    def gate_up(c):
        gt = jnp.dot(h2, wg_ref[:, c:c + FF_CHUNK], preferred_element_type=F32)
        up = jnp.dot(h2, wu_ref[:, c:c + FF_CHUNK], preferred_element_type=F32)
        return gt, up

    chunks = list(range(0, d_ff, FF_CHUNK))
    nxt = gate_up(chunks[0])
    for i, c in enumerate(chunks):
        gt, up = nxt
        if i + 1 < len(chunks):
            nxt = gate_up(chunks[i + 1])
        ff = (gt * jax.nn.sigmoid(gt) * up).astype(BF16)
        acc = acc + jnp.dot(ff, wd_ref[c:c + FF_CHUNK, :], preferred_element_type=F32)
    y_ref[...] = acc


def _merge_ffn(x2d, o_att, hm, ga, gb, wa, wm, wo, g2, wg, wu, wd, seq_len, tm, decode=None):
    m_rows, d_model = x2d.shape
    tiles_per_seq = seq_len // tm
    steps = m_rows // tm
    row = lambda w: pl.BlockSpec((tm, w), lambda i: (i, 0))
    o_spec = pl.BlockSpec((None, PAIRS, tm, LANES),
                          lambda i: (i // tiles_per_seq, 0, i % tiles_per_seq, 0))
    in_specs = [row(d_model), o_spec, row(M_V_W), row(D_MODEL), row(D_MODEL),
                _const_spec(wa.shape), _const_spec(wm.shape), _const_spec(wo.shape),
                _const_spec((1, d_model)), _const_spec(wg.shape), _const_spec(wu.shape),
                _const_spec(wd.shape)]
    args = [x2d, o_att, hm, ga, gb, wa, wm, wo, g2, wg, wu, wd]
    out_specs = [row(d_model)]
    out_shape = [jax.ShapeDtypeStruct((m_rows, d_model), F32)]
    if decode is not None:
        qkv_t, caches_t = decode
        db = caches_t[0].shape[0]
        assert steps == 2 * db, (steps, db)
        in_specs.append(pl.BlockSpec((qkv_t.shape[0], DEC_ROWS, LANES), lambda i: (0, i % 2, 0)))
        args.append(qkv_t)
        for c in caches_t:
            spec = pl.BlockSpec((None, 2, DEC_HEADS, HEAD_DIM, c.shape[-1]),
                                lambda i: (i // 2, 0, i % 2, 0, 0))
            in_specs.append(spec)
            args.append(c)
            out_specs.append(spec)
            out_shape.append(jax.ShapeDtypeStruct(c.shape, F32))
        out_specs.append(pl.BlockSpec((None, None, DEC_ROWS, LANES), lambda i: (i // 2, i % 2, 0, 0)))
        out_shape.append(jax.ShapeDtypeStruct((db, 2, DEC_ROWS, LANES), F32))
    outs = pl.pallas_call(
        functools.partial(_merge_ffn_kernel, with_decode=decode is not None),
        grid=(steps,),
        in_specs=in_specs, out_specs=out_specs, out_shape=out_shape,
        compiler_params=pltpu.CompilerParams(dimension_semantics=("arbitrary",),
                                             vmem_limit_bytes=VMEM_LIMIT),
        name="merge_ffn",
    )(*args)
    if decode is None:
        return outs[0]
    return outs[0], outs[1:4], outs[4]


def _sample_prep_kernel(*refs):
    for i, ref in enumerate(refs[:-1]):
        refs[-1][i] = ref[...].T


def _sample_prep(arrs):
    vm = pl.BlockSpec(memory_space=pltpu.VMEM)
    return pl.pallas_call(
        _sample_prep_kernel, in_specs=[vm] * len(arrs), out_specs=vm,
        out_shape=jax.ShapeDtypeStruct((len(arrs), GROUP_W, LANES), F32),
        name="sample_prep",
    )(*arrs)


def _sample_mlstm_kernel(q_ref, k_ref, v_ref, om_ref, gt_ref, mt_ref, n_ref, c_ref, rexp_ref,
                         hm_ref, co_ref, nt_ref, mo_ref, *, db):
    nh, dqk = M_HEADS, M_DQK
    hi = lax.Precision.HIGHEST
    rexp = rexp_ref[...]
    q_t = q_ref[...].T * (dqk ** -0.5)
    k_t = k_ref[...].T
    n_t = n_ref[...].T
    ig = gt_ref[0:nh, :]
    lf = _log_sigmoid(gt_ref[nh:2 * nh, :])
    m_prev = mt_ref[...]
    m_new = jnp.maximum(lf + m_prev, ig)
    w8 = jnp.exp(ig - m_new)
    dec8 = jnp.exp(lf + m_prev - m_new)
    head_sum = lambda a: lax.dot_general(rexp, a, (((0,), (0,)), ((), ())), precision=hi,
                                         preferred_element_type=F32)
    expand = lambda a: jnp.dot(rexp, a, precision=hi, preferred_element_type=F32)
    qk8 = head_sum(q_t * k_t)
    qn8 = head_sum(q_t * n_t)
    wqk8 = w8 * qk8
    den8 = dec8 * qn8 + wqk8
    inv8 = 1.0 / jnp.maximum(jnp.abs(den8), jnp.exp(-m_new))
    dec_x = expand(dec8)
    w_x = expand(w8)
    nt_ref[...] = dec_x * n_t + w_x * k_t
    mo_ref[...] = m_new
    wk_x = w_x * k_t
    for b in range(db):
        cb = c_ref[b].reshape(nh * dqk, M_DV)
        v_b = v_ref[b]
        v_x = jnp.concatenate([jnp.broadcast_to(v_b[h:h + 1, :], (dqk, M_DV)) for h in range(nh)],
                              axis=0)
        co_ref[b] = (dec_x[:, b:b + 1] * cb + wk_x[:, b:b + 1] * v_x).reshape(nh, dqk, M_DV)
        qc = jnp.sum((q_t[:, b:b + 1] * cb).reshape(nh, dqk, M_DV), axis=1)
        num = dec8[:, b:b + 1] * qc + wqk8[:, b:b + 1] * v_b
        hm_ref[b] = jax.nn.sigmoid(om_ref[b]) * (num * inv8[:, b:b + 1])


def _sample_mlstm(q_t, k_t, v3, om3, gif_t, m_t, n_pad, c_state, rexp, db):
    vm = pl.BlockSpec(memory_space=pltpu.VMEM)
    return pl.pallas_call(
        functools.partial(_sample_mlstm_kernel, db=db),
        in_specs=[vm] * 9,
        out_specs=[vm] * 4,
        out_shape=[jax.ShapeDtypeStruct((db, M_HEADS, M_DV), F32),
                   jax.ShapeDtypeStruct(c_state.shape, F32),
                   jax.ShapeDtypeStruct((M_HEADS * M_DQK, LANES), F32),
                   jax.ShapeDtypeStruct((M_HEADS, LANES), F32)],
        compiler_params=pltpu.CompilerParams(vmem_limit_bytes=VMEM_LIMIT),
        name="sample_mlstm",
    )(q_t, k_t, v3, om3, gif_t, m_t, n_pad, c_state, rexp)


def _rope_tables(pos):
    half = ROT_DIM // 2
    pos = np.asarray(pos, np.float32)
    inv_freq = np.exp(np.float32(-math.log(ROPE_THETA))
                      * np.arange(0, ROT_DIM, 2, dtype=np.float32) / np.float32(ROT_DIM))
    ang = (pos[:, None] * inv_freq[None, :]).astype(np.float32)
    cos, sin = np.cos(ang).astype(np.float32), np.sin(ang).astype(np.float32)
    t = pos.shape[0]
    rest = HEAD_DIM - ROT_DIM
    a = np.concatenate([cos, cos, np.ones((t, rest), np.float32)], axis=1)
    bm = np.concatenate([-sin, np.zeros((t, HEAD_DIM - half), np.float32)], axis=1)
    bp = np.concatenate([np.zeros((t, half), np.float32), sin, np.zeros((t, rest), np.float32)],
                        axis=1)
    return tuple(jnp.asarray(np.concatenate([x, x], axis=1)) for x in (a, bm, bp))


W_PREP_BLK = 512


def _w_prep_kernel(wt_ref, o_ref):
    o_ref[...] = wt_ref[...].T.astype(o_ref.dtype)


def _w_prep(w_in):
    d_model = w_in.shape[0]
    wt = w_in.T
    n_head = _C_GA // W_PREP_BLK
    n_blk = _W_COLS // W_PREP_BLK

    sub = 8

    def src_row(j):
        return (j * (W_PREP_BLK // sub) + jnp.where(j < n_head, 0, _GIF_COLS // sub)) * sub

    return pl.pallas_call(
        _w_prep_kernel,
        grid=(n_blk,),
        in_specs=[pl.BlockSpec((pl.Element(W_PREP_BLK), pl.Element(d_model)),
                               lambda j: (src_row(j), 0))],
        out_specs=pl.BlockSpec((d_model, W_PREP_BLK), lambda j: (0, j)),
        out_shape=jax.ShapeDtypeStruct((d_model, _W_COLS), BF16),
        compiler_params=pltpu.CompilerParams(dimension_semantics=("arbitrary",),
                                             vmem_limit_bytes=VMEM_LIMIT),
        name="w_prep",
    )(wt)


def kernel(x_prompt, x_sample, cache_kv_w128, cache_kv_w512, cache_kv_w2048, state_mlstm_C, state_mlstm_n, state_mlstm_m, norm1_g, w_in, b_if, q_norm_g, k_norm_g, w_att_out, w_m_out, w_o, norm2_g, w_gate, w_up, w_down):
    nb, seq_len, d_model = x_prompt.shape
    db, dec_seq, _ = x_sample.shape
    assert dec_seq == 1 and d_model == D_MODEL and seq_len % ATT_STEP == 0 and db <= LANES
    caches = (cache_kv_w128, cache_kv_w512, cache_kv_w2048)

    w_perm = _w_prep(w_in)
    w_gif = jnp.zeros((d_model, LANES), BF16).at[:, :_GIF_COLS].set(
        w_in[:, _C_GA:_C_GA + _GIF_COLS].astype(BF16))
    g1 = norm1_g.reshape(1, d_model)
    g2 = norm2_g.reshape(1, d_model)
    bif = jnp.concatenate([b_if, jnp.zeros((LANES - b_if.shape[0],), F32)]).reshape(1, LANES)
    qg = jnp.tile(q_norm_g, HEADS).reshape(1, GROUP_W)
    kg = jnp.tile(k_norm_g, HEADS).reshape(1, GROUP_W)
    hid = np.arange(GROUP_W // 2) // HEAD_DIM
    gmat = jnp.asarray(hid[:, None] == hid[None, :], dtype=BF16)

    m_rows = nb * seq_len
    x2d = x_prompt.reshape(m_rows, d_model)
    tabs_p = _rope_tables(np.arange(seq_len))
    qg_p = qg * (HEAD_DIM ** -0.5 * math.log2(math.e))
    outs = _proj(x2d, seq_len, DILATIONS, 256, False, g1, w_perm, w_gif, bif, qg_p, kg, tabs_p, gmat)
    qs, ks, vs = outs[0:3], outs[3:6], outs[6:9]
    hm, ga, gb, st_p, m_p = outs[9:14]

    o_att, (wa, wm, wo, wg, wu, wd) = _attention(
        qs, ks, vs, seq_len, (w_att_out, w_m_out, w_o, w_gate, w_up, w_down))

    x_s = jnp.zeros((LANES, d_model), F32).at[:db].set(x_sample.reshape(db, d_model))
    tabs_s = _rope_tables(np.full((LANES,), PAST_LEN))
    outs_s = _proj(x_s, LANES, (1, 1, 1), LANES, True, g1, w_perm, w_gif, bif, qg, kg, tabs_s, gmat)
    qkv_t = _sample_prep(outs_s[0:9])
    caches_t = [c.transpose(0, 2, 3, 4, 1) for c in caches]
    y_prompt, kv_st, o_att_cols = _merge_ffn(x2d, o_att, hm, ga, gb, wa, wm, wo, g2, wg, wu, wd,
                                             seq_len, 256, decode=(qkv_t, caches_t))
    y_prompt = y_prompt.reshape(nb, seq_len, d_model)
    kv_s = [c.transpose(0, 4, 1, 2, 3) for c in kv_st]

    tails = _kv_tails(ks, vs, seq_len)
    kv_p = [t.reshape(nb, 2, HEADS, HEAD_DIM, t.shape[-1]).transpose(0, 4, 1, 2, 3) for t in tails]
    c_p = st_p[..., :M_DV].reshape(nb, M_HEADS, M_DQK, M_DV)
    n_p = st_p[..., M_DV].reshape(nb, M_HEADS, M_DQK)
    m_pr = m_p[:, :, 0]

    qm_t, km_t, vm_s, om_s, ga_s, gb_s, gif_s = outs_s[9:16]
    gif_ts = gif_s.T

    rexp = jnp.asarray(np.arange(M_HEADS * M_DQK)[:, None] // M_DQK == np.arange(M_HEADS)[None, :],
                       dtype=F32)
    m_t = jnp.zeros((M_HEADS, LANES), F32).at[:, :db].set(state_mlstm_m.T)
    n_pad = jnp.zeros((LANES, M_HEADS * M_DQK), F32).at[:db].set(state_mlstm_n.reshape(db, -1))
    hm_s3, c_s, n_t, m_so = _sample_mlstm(
        qm_t, km_t, vm_s[:db].reshape(db, M_HEADS, M_DV), om_s[:db].reshape(db, M_HEADS, M_DV),
        gif_ts[:2 * M_HEADS], m_t, n_pad, state_mlstm_C, rexp, db)
    n_s = n_t.T[:db].reshape(db, M_HEADS, M_DQK)
    m_s = m_so[:, :db].T

    o_att_s = o_att_cols[:, :, :, 0].reshape(db, PAIRS, LANES)
    o_att_sp = jnp.zeros((PAIRS, LANES, LANES), F32).at[:, :db].set(o_att_s.transpose(1, 0, 2))[None]
    hm_sp = jnp.zeros((LANES, M_V_W), F32).at[:db].set(hm_s3.reshape(db, M_V_W))
    y_s = _merge_ffn(x_s, o_att_sp, hm_sp, ga_s, gb_s, wa, wm, wo, g2, wg, wu, wd, LANES, LANES)
    y_sample = y_s[:db].reshape(db, 1, d_model)

    return (y_prompt, y_sample, kv_p[0], kv_p[1], kv_p[2], c_p, n_p, m_pr,
            kv_s[0], kv_s[1], kv_s[2], c_s, n_s, m_s)
```

```python
import functools
import math

import jax
import jax.numpy as jnp
import numpy as np
from jax import lax
from jax.experimental import pallas as pl
from jax.experimental.pallas import tpu as pltpu

F32 = jnp.float32
BF16 = jnp.bfloat16

HEAD_DIM = 64
HEADS = 8
GROUP_W = HEADS * HEAD_DIM
N_GROUPS = 3
WINDOWS = (128, 512, 2048)
DILATIONS = (1, 4, 16)
SPAN = 128
ROT_DIM = 16
ROPE_THETA = 500000.0
M_HEADS = 8
M_DQK = 64
M_DV = 128
M_QK_W = M_HEADS * M_DQK
M_V_W = M_HEADS * M_DV
D_MODEL = 1024
PAST_LEN = 8192
NORM_EPS = 1e-6
NEG = -1e30

LANES = 128
PAIRS = GROUP_W // LANES
VMEM_LIMIT = 56 * 1024 * 1024

_ATT_W = N_GROUPS * GROUP_W
_C_QM = 3 * _ATT_W
_C_KM = _C_QM + M_QK_W
_C_VM = _C_KM + M_QK_W
_C_OM = _C_VM + M_V_W
_C_GA = _C_OM + M_V_W
_C_GB = _C_GA + D_MODEL
_W_COLS = _C_GB + D_MODEL
_GIF_COLS = 2 * M_HEADS


def _const_spec(shape):
    nd = len(shape)
    return pl.BlockSpec(shape, lambda *_: (0,) * nd, pipeline_mode=pl.Buffered(1))


def _log_sigmoid(x):
    return jnp.minimum(x, 0.0) - jnp.log1p(jnp.exp(-jnp.abs(x)))


MLSTM_CHUNKS = 2


def _proj_kernel(x_ref, g1_ref, w_ref, wgif_ref, bif_ref, qg_ref, kg_ref, ra_ref, rm_ref, rp_ref,
                 gm_ref,
                 q0_ref, q1_ref, q2_ref, k0_ref, k1_ref, k2_ref, v0_ref, v1_ref, v2_ref,
                 *rest, tm, dils, plain, tiles_per_seq):
    if plain:
        qm_ref, km_ref, vm_ref, om_ref, ga_ref, gb_ref, gif_ref, hs_ref = rest
    else:
        hm_ref, ga_ref, gb_ref, st_ref, mo_ref, hs_ref, st_s, m_s = rest
    d_model = x_ref.shape[1]
    x = x_ref[...]
    xn = x * lax.rsqrt(jnp.mean(x * x, axis=-1, keepdims=True) + NORM_EPS) * g1_ref[...]
    h_nat = xn.astype(BF16)
    n_slab = d_model // LANES
    if any(d > 1 for d in dils):
        for c in range(n_slab):
            hs_ref[c] = xn[:, c * LANES:(c + 1) * LANES]

    def permuted_h(d):
        if d == 1:
            return h_nat
        n = tm // d
        rows = [jnp.concatenate([hs_ref[c, pl.ds(r, n, stride=d), :] for c in range(n_slab)], axis=1)
                for r in range(d)]
        return jnp.concatenate(rows, axis=0).astype(BF16)

    def permuted_tab(ref, d):
        if d == 1:
            t = ref[...]
        else:
            n = tm // d
            t = jnp.concatenate([ref[pl.ds(r, n, stride=d), :] for r in range(d)], axis=0)
        return jnp.concatenate([t] * PAIRS, axis=1)

    gmat = gm_ref[...]

    def head_sumsq(z):
        zz = (z * z).astype(BF16)
        half = GROUP_W // 2
        return jnp.concatenate(
            [jnp.dot(zz[:, :half], gmat, preferred_element_type=F32),
             jnp.dot(zz[:, half:], gmat, preferred_element_type=F32)], axis=1)

    def norm_rope(z, ss, gain, ra, rm, rp):
        y = z * lax.rsqrt(ss * (1.0 / HEAD_DIM) + NORM_EPS) * gain
        return (y * ra + pltpu.roll(y, GROUP_W - ROT_DIM // 2, 1) * rm
                + pltpu.roll(y, ROT_DIM // 2, 1) * rp)

    def store_group(ref, y, d):
        if plain:
            ref[...] = y.astype(ref.dtype)
            return
        n = tm // d
        for p in range(PAIRS):
            ref[:, p] = y[:, p * LANES:(p + 1) * LANES].reshape(d, n, LANES).astype(ref.dtype)

    def seg(c0, width):
        return jnp.dot(h_nat, w_ref[:, c0:c0 + width], preferred_element_type=F32)

    if not plain:
        @pl.when(pl.program_id(0) % tiles_per_seq == 0)
        def _():
            st_s[...] = jnp.zeros_like(st_s)
            m_s[...] = jnp.zeros_like(m_s)

        gif = jnp.dot(h_nat, wgif_ref[...], preferred_element_type=F32) + bif_ref[...]
        qm_b, km_b = seg(_C_QM, M_QK_W).astype(BF16), seg(_C_KM, M_QK_W).astype(BF16)
        vm_b, om_b = seg(_C_VM, M_V_W).astype(BF16), seg(_C_OM, M_V_W).astype(BF16)
        gif_t = gif.T[:2 * M_HEADS, :]
        lc = tm // MLSTM_CHUNKS
        m_chunks = []
        for c in range(MLSTM_CHUNKS):
            rows = slice(c * lc, (c + 1) * lc)
            m_chunks.append(functools.partial(
                _mlstm_chunk, qm_b[rows], km_b[rows], vm_b[rows], om_b[rows], gif[rows],
                gif_t[:, rows], st_s, m_s, hm_ref.at[rows], lc))
        m_live, m_started = [], {}
        m_units = [(c, pairs) for c in range(MLSTM_CHUNKS) for pairs in ((0, 1), (2, 3))]

        def m_advance(i):
            if 0 < i <= len(m_units):
                phases, pairs = m_live.pop()
                phases[2](pairs)
            if i < len(m_units):
                c, pairs = m_units[i]
                if c not in m_started:
                    m_started[c] = m_chunks[c]()
                phases = m_started[c]
                phases[0](pairs)
                phases[1](pairs)
                m_live.append((phases, pairs))

        m_advance(0)

    q_refs = (q0_ref, q1_ref, q2_ref)
    k_refs = (k0_ref, k1_ref, k2_ref)
    v_refs = (v0_ref, v1_ref, v2_ref)
    for g in range(N_GROUPS):
        d = dils[g]
        hg = permuted_h(d)
        ra, rm, rp = (permuted_tab(r, d) for r in (ra_ref, rm_ref, rp_ref))
        cq, ck, cv = (t * _ATT_W + g * GROUP_W for t in range(3))
        zq = jnp.dot(hg, w_ref[:, cq:cq + GROUP_W], preferred_element_type=F32)
        zk = jnp.dot(hg, w_ref[:, ck:ck + GROUP_W], preferred_element_type=F32)
        zv = jnp.dot(hg, w_ref[:, cv:cv + GROUP_W], preferred_element_type=F32)
        ssq, ssk = head_sumsq(zq), head_sumsq(zk)
        store_group(v_refs[g], zv, d)
        store_group(q_refs[g], norm_rope(zq, ssq, qg_ref[...], ra, rm, rp), d)
        store_group(k_refs[g], norm_rope(zk, ssk, kg_ref[...], ra, rm, rp), d)

        if not plain:
            m_advance(g + 1)

    if plain:
        segments = ((qm_ref, _C_QM, M_QK_W), (km_ref, _C_KM, M_QK_W), (vm_ref, _C_VM, M_V_W),
                    (om_ref, _C_OM, M_V_W), (ga_ref, _C_GA, D_MODEL), (gb_ref, _C_GB, D_MODEL))
        zg = jnp.dot(h_nat, wgif_ref[...], preferred_element_type=F32)
        gif_ref[...] = zg + bif_ref[...]
    else:
        segments = ((ga_ref, _C_GA, D_MODEL), (gb_ref, _C_GB, D_MODEL))
    for si, (ref, c0, width) in enumerate(segments):
        for cc in range(0, width, GROUP_W):
            ref[:, cc:cc + GROUP_W] = seg(c0 + cc, GROUP_W).astype(ref.dtype)
        if not plain and si == 0:
            m_advance(N_GROUPS + 1)
            st_ref[...] = st_s[...]
            mo_ref[...] = m_s[...]


def _proj(x2d, seq_len, dils, tm, plain, g1, w_perm, w_gif, bif, qg, kg, rope_tabs, gmat):
    m_rows, d_model = x2d.shape
    nb = m_rows // seq_len
    tiles_per_seq = seq_len // tm
    grid = (m_rows // tm,)
    row_spec = lambda w: pl.BlockSpec((tm, w), lambda i: (i, 0))
    tab_spec = pl.BlockSpec((tm, LANES), lambda i: (i % tiles_per_seq, 0))
    sds = jax.ShapeDtypeStruct

    widths = (M_QK_W, M_QK_W, M_V_W, M_V_W, D_MODEL, D_MODEL)
    scratch = [pltpu.VMEM((d_model // LANES, tm, LANES), F32)]
    if plain:
        out_shape = ([sds((m_rows, GROUP_W), F32)] * 9 + [sds((m_rows, w), F32) for w in widths]
                     + [sds((m_rows, LANES), F32)])
        out_specs = ([row_spec(GROUP_W)] * 9 + [row_spec(w) for w in widths] + [row_spec(LANES)])
    else:
        def grp_spec(d):
            return pl.BlockSpec((None, d, PAIRS, tm // d, LANES),
                                lambda i: (i // tiles_per_seq, 0, 0, i % tiles_per_seq, 0))
        grp_shape = lambda d: sds((nb, d, PAIRS, seq_len // d, LANES), BF16)
        state_spec = lambda *dims: pl.BlockSpec((None,) + dims,
                                                lambda i: (i // tiles_per_seq,) + (0,) * len(dims))
        st_dims, m_dims = (M_HEADS // 2, LANES, 2 * LANES), (M_HEADS, LANES)
        row_w = (M_V_W, D_MODEL, D_MODEL)
        out_shape = ([grp_shape(d) for d in dils] * 3 + [sds((m_rows, w), BF16) for w in row_w]
                     + [sds((nb,) + st_dims, F32), sds((nb,) + m_dims, F32)])
        out_specs = ([grp_spec(d) for d in dils] * 3 + [row_spec(w) for w in row_w]
                     + [state_spec(*st_dims), state_spec(*m_dims)])
        scratch += [pltpu.VMEM(st_dims, F32), pltpu.VMEM(m_dims, F32)]
    in_specs = [row_spec(d_model), _const_spec((1, d_model)), _const_spec(w_perm.shape),
                _const_spec(w_gif.shape), _const_spec((1, LANES)), _const_spec((1, GROUP_W)),
                _const_spec((1, GROUP_W)), tab_spec, tab_spec, tab_spec, _const_spec(gmat.shape)]
    return pl.pallas_call(
        functools.partial(_proj_kernel, tm=tm, dils=dils, plain=plain,
                          tiles_per_seq=tiles_per_seq),
        grid=grid, in_specs=in_specs, out_specs=out_specs, out_shape=out_shape,
        scratch_shapes=scratch,
        compiler_params=pltpu.CompilerParams(dimension_semantics=("arbitrary",),
                                             vmem_limit_bytes=VMEM_LIMIT),
        name="proj",
    )(x2d, g1, w_perm, w_gif, bif, qg, kg, *rope_tabs, gmat)


ATT_STEP = SPAN * max(DILATIONS)
FAR_PITCH = SPAN + 8


def _attn_kernel(q0, k0, v0, kh0, vh0, q1, k1, v1, kh1, vh1, q2, k2, v2, kh2, vh2,
                 *rest, n_cast):
    w_f32, o_ref, w_bf16 = rest[:n_cast], rest[n_cast], rest[n_cast + 1:2 * n_cast + 1]
    acc_s, m_s, l_s, far_s = rest[2 * n_cast + 1:]
    j = pl.program_id(1)
    for src, dst in zip(w_f32, w_bf16):
        dst[...] = src[...].astype(dst.dtype)

    def window(cur, halo, r, bi):
        prev = halo[r] if bi == 0 else cur[r, (bi - 1) * SPAN:bi * SPAN, :]
        return jnp.concatenate([prev, cur[r, bi * SPAN:(bi + 1) * SPAN, :]], axis=0)

    qi = lax.broadcasted_iota(jnp.int32, (SPAN, 2 * SPAN), 0)
    ci = lax.broadcasted_iota(jnp.int32, (SPAN, 2 * SPAN), 1)
    band = (ci >= qi) & (ci <= qi + SPAN)
    bias_band = jnp.where(band, 0.0, NEG).astype(F32)
    bias_first = jnp.where(band & (ci >= SPAN), 0.0, NEG).astype(F32)
    lane_q = lax.broadcasted_iota(jnp.int32, (SPAN, LANES), 1)
    lane_kv = lax.broadcasted_iota(jnp.int32, (2 * SPAN, LANES), 1)
    halves_q = (lane_q < HEAD_DIM, lane_q >= HEAD_DIM)
    halves_kv = (lane_kv < HEAD_DIM, lane_kv >= HEAD_DIM)

    d0, d1, d2 = DILATIONS
    units = (
        [(0, bi, pl.ds(bi * SPAN, SPAN)) for bi in range(ATT_STEP // d0 // SPAN)],
        [(r, bi, pl.ds(bi * SPAN * d1 + r, SPAN, stride=d1))
         for r in range(d1) for bi in range(ATT_STEP // d1 // SPAN)],
        [(r, 0, pl.ds(r * FAR_PITCH, SPAN)) for r in range(d2)],
    )
    q_refs = (q0, q1, q2)
    kv_refs = ((k0, kh0, v0, vh0), (k1, kh1, v1, vh1), (k2, kh2, v2, vh2))

    def scores(g):
        k_cur, k_halo = kv_refs[g][:2]
        out = []
        for r, bi, _ in units[g]:
            q2_ = q_refs[g][r, bi * SPAN:(bi + 1) * SPAN, :]
            kk = window(k_cur, k_halo, r, bi)
            bias = jnp.where(j == 0, bias_first, bias_band) if bi == 0 else bias_band
            ss = []
            for hh in range(2):
                qa = jnp.where(halves_q[hh], q2_, jnp.zeros_like(q2_))
                ss.append(lax.dot_general(qa, kk, (((1,), (1,)), ((), ())),
                                          preferred_element_type=F32) + bias)
            out.append(ss)
        return out

    def finish(g, all_ss):
        v_cur, v_halo = kv_refs[g][2:]
        probs = []
        for ss in all_ss:
            mxs = [jnp.max(s, axis=-1, keepdims=True) for s in ss]
            ps = [jnp.exp2(s - mx) for s, mx in zip(ss, mxs)]
            ls = [jnp.sum(p, axis=-1, keepdims=True) for p in ps]
            probs.append(([p.astype(BF16) for p in ps], mxs, ls))
        for (r, bi, sl), (ps, mxs, ls) in zip(units[g], probs):
            vv = window(v_cur, v_halo, r, bi)
            acc = None
            for hh in range(2):
                vh = jnp.where(halves_kv[hh], vv, jnp.zeros_like(vv))
                a = jnp.dot(ps[hh], vh, preferred_element_type=F32)
                acc = a if acc is None else acc + a
            stats = (acc, jnp.where(halves_q[0], mxs[0], mxs[1]),
                     jnp.where(halves_q[0], ls[0], ls[1]))
            if g == N_GROUPS - 1:
                for k, val in enumerate(stats):
                    far_s[k, sl, :] = val
            else:
                for ref, val in zip((acc_s, m_s, l_s), stats):
                    ref[g, sl, :] = val

    for g in range(N_GROUPS):
        finish(g, scores(g))

    def far_rows(k, ci_):
        sub = 8
        pieces = [far_s[k, pl.ds(a * sub * FAR_PITCH + ci_ * (SPAN // d2) + jj, sub,
                                 stride=FAR_PITCH), :]
                  for jj in range(SPAN // d2) for a in range(d2 // sub)]
        return jnp.concatenate(pieces, axis=0)

    def combine(ci_, c):
        sl = pl.ds(pl.multiple_of(ci_ * SPAN, SPAN), SPAN)
        accs = [acc_s[0, sl, :], acc_s[1, sl, :], far_rows(0, ci_)]
        ms = [m_s[0, sl, :], m_s[1, sl, :], far_rows(1, ci_)]
        ls = [l_s[0, sl, :], l_s[1, sl, :], far_rows(2, ci_)]
        mm = jnp.maximum(jnp.maximum(ms[0], ms[1]), ms[2])
        es = [jnp.exp2(m - mm) for m in ms]
        num = es[0] * accs[0] + es[1] * accs[1] + es[2] * accs[2]
        den = es[0] * ls[0] + es[1] * ls[1] + es[2] * ls[2]
        o_ref[sl, :] = (num / den).astype(o_ref.dtype)
        return c
    lax.fori_loop(0, ATT_STEP // SPAN, combine, 0, unroll=4)


def _attention(qs, ks, vs, seq_len, cast_ws):
    nb = qs[0].shape[0]
    steps = seq_len // ATT_STEP
    in_specs, args, scratch = [], [], []
    for g, d in enumerate(DILATIONS):
        rows = ATT_STEP // d
        cur = pl.BlockSpec((None, d, None, rows, LANES), lambda b, j, p: (b, 0, p, j, 0))
        ratio = rows // SPAN
        halo = pl.BlockSpec((None, d, None, SPAN, LANES),
                            lambda b, j, p, ratio=ratio: (b, 0, p, jnp.maximum(j * ratio - 1, 0), 0))
        in_specs += [cur, cur, cur, halo, halo]
        args += [qs[g], ks[g], vs[g], ks[g], vs[g]]
    scratch += [pltpu.VMEM((N_GROUPS - 1, ATT_STEP, LANES), F32)] * 3
    scratch += [pltpu.VMEM((3, max(DILATIONS) * FAR_PITCH, LANES), F32)]
    n_slices = max(nb * steps * PAIRS // 2, 1)
    slice_of = lambda b, j, p: (((b * steps + j) * PAIRS + p) // 2, 0)
    w_specs = []
    for w in cast_ws:
        assert w.shape[0] % (16 * n_slices) == 0, (w.shape, n_slices)
        w_specs.append(pl.BlockSpec((w.shape[0] // n_slices, w.shape[1]), slice_of))
    outs = pl.pallas_call(
        functools.partial(_attn_kernel, n_cast=len(cast_ws)),
        grid=(nb, steps, PAIRS),
        in_specs=in_specs + w_specs,
        out_specs=[pl.BlockSpec((None, None, ATT_STEP, LANES), lambda b, j, p: (b, p, j, 0))]
                  + w_specs,
        out_shape=[jax.ShapeDtypeStruct((nb, PAIRS, seq_len, LANES), BF16)]
                  + [jax.ShapeDtypeStruct(w.shape, BF16) for w in cast_ws],
        scratch_shapes=scratch,
        compiler_params=pltpu.CompilerParams(
            dimension_semantics=("arbitrary", "arbitrary", "arbitrary"),
            vmem_limit_bytes=VMEM_LIMIT),
        name="attn",
    )(*args, *cast_ws)
    return outs[0], outs[1:]


def _kv_tail_kernel(k0, v0, k1, v1, k2, v2, o0, o1, o2, nat_ref):
    for (k_ref, v_ref, o_ref, d) in ((k0, v0, o0, DILATIONS[0]), (k1, v1, o1, DILATIONS[1]),
                                     (k2, v2, o2, DILATIONS[2])):
        for kvi, ref in enumerate((k_ref, v_ref)):
            for p in range(PAIRS):
                rows = slice(p * LANES, (p + 1) * LANES)
                if d == 1:
                    o_ref[kvi, rows, :] = ref[0, p].astype(F32).T
                    continue
                for r in range(d):
                    nat_ref[pl.ds(r, SPAN, stride=d), :] = ref[r, p].astype(F32)
                for c in range(d):
                    o_ref[kvi, rows, c * SPAN:(c + 1) * SPAN] = nat_ref[c * SPAN:(c + 1) * SPAN, :].T


def _kv_tails(ks, vs, seq_len):
    nb = ks[0].shape[0]
    in_specs, args, out_specs, out_shape = [], [], [], []
    for g, d in enumerate(DILATIONS):
        last = seq_len // d // SPAN - 1
        spec = pl.BlockSpec((None, d, PAIRS, SPAN, LANES), lambda b, last=last: (b, 0, 0, last, 0))
        in_specs += [spec, spec]
        args += [ks[g], vs[g]]
        out_specs.append(pl.BlockSpec((None, 2, GROUP_W, SPAN * d), lambda b: (b, 0, 0, 0)))
        out_shape.append(jax.ShapeDtypeStruct((nb, 2, GROUP_W, SPAN * d), F32))
    return pl.pallas_call(
        _kv_tail_kernel,
        grid=(nb,), in_specs=in_specs, out_specs=out_specs, out_shape=out_shape,
        scratch_shapes=[pltpu.VMEM((SPAN * max(DILATIONS), LANES), F32)],
        compiler_params=pltpu.CompilerParams(dimension_semantics=("arbitrary",),
                                             vmem_limit_bytes=VMEM_LIMIT),
        name="kv_tail",
    )(*args)


def _mlstm_chunk(q2d, k2d, v2d, om2d, gif, gt, st_s, m_s, hm_ref, lc):
    lf_c = _log_sigmoid(gif)
    lf_r = _log_sigmoid(gt[M_HEADS:2 * M_HEADS, :])
    row = lax.broadcasted_iota(jnp.int32, (lc, lc), 0)
    col = lax.broadcasted_iota(jnp.int32, (lc, lc), 1)
    causal = col <= row
    tril = causal.astype(F32)
    triu = (row <= col).astype(F32)
    b_c = jnp.dot(tril, lf_c, precision=lax.Precision.HIGHEST, preferred_element_type=F32)
    b_r = jnp.dot(lf_r, triu, precision=lax.Precision.HIGHEST, preferred_element_type=F32)
    lane = lax.broadcasted_iota(jnp.int32, (lc, LANES), 1)
    halves = (lane < M_DQK, lane >= M_DQK)
    sub = lax.broadcasted_iota(jnp.int32, (LANES, 1), 0)
    ones_blk = jnp.ones((lc, LANES), BF16)
    qscale = M_DQK ** -0.5

    states, p1, p2 = {}, {}, {}
    all_pairs = tuple(range(M_HEADS // 2))
    heads_of = lambda pairs: [2 * p + hh for p in pairs for hh in range(2)]

    def phase1(pairs=all_pairs):
        for p in pairs:
            states[p] = st_s[p]
        for h in heads_of(pairs):
            p, hh = divmod(h, 2)
            q2_ = q2d[:, p * LANES:(p + 1) * LANES]
            k2_ = k2d[:, p * LANES:(p + 1) * LANES]
            qa = jnp.where(halves[hh], q2_, jnp.zeros_like(q2_)) * jnp.asarray(qscale, q2_.dtype)
            s = lax.dot_general(qa, k2_, (((1,), (1,)), ((), ())), preferred_element_type=F32)
            qc = jnp.dot(qa, states[p].astype(BF16), preferred_element_type=F32)
            p1[h] = (s, qc)

    def phase2(pairs=all_pairs):
        for h in heads_of(pairs):
            p, hh = divmod(h, 2)
            s, qc = p1[h]
            bcol = b_c[:, M_HEADS + h:M_HEADS + h + 1]
            igcol = gif[:, h:h + 1]
            a_row = gt[h:h + 1, :] - b_r[h:h + 1, :]
            amat = jnp.where(causal, a_row, NEG)
            m_prev = m_s[h:h + 1, 0:1]
            g_t = jnp.maximum(m_prev, jnp.max(amat, axis=-1, keepdims=True))
            wqk = (jnp.exp(amat - g_t) * s).astype(BF16)
            dec = jnp.exp(m_prev - g_t)
            m_t = bcol + g_t
            m_new = m_t[lc - 1:lc, :]
            b_last = bcol[lc - 1:lc, :]
            dstate = jnp.exp(b_last + m_prev - m_new)
            ws = jnp.exp(b_last - bcol + igcol - m_new)
            k2_ = k2d[:, p * LANES:(p + 1) * LANES]
            ka = jnp.where(halves[hh], k2_, jnp.zeros_like(k2_))
            kws = (ka.astype(F32) * ws).astype(BF16)
            m_s[h:h + 1, :] = jnp.broadcast_to(m_new, (1, LANES))
            p2[h] = (wqk, dec * qc, jnp.exp(-m_t), dstate, kws)

    def phase3(pairs=all_pairs):
        upds = {}
        for h in heads_of(pairs):
            wqk, dqc, floor, _, kws = p2[h]
            v1 = jnp.concatenate([v2d[:, h * M_DV:(h + 1) * M_DV], ones_blk], axis=1)
            num_den = dqc + jnp.dot(wqk, v1, preferred_element_type=F32)
            num = num_den[:, :M_DV]
            den = num_den[:, M_DV:]
            hval = num / jnp.maximum(jnp.abs(den), floor)
            gate = jax.nn.sigmoid(om2d[:, h * M_DV:(h + 1) * M_DV].astype(F32))
            hm_ref[:, h * M_DV:(h + 1) * M_DV] = (gate * hval).astype(hm_ref.dtype)
            upds[h] = lax.dot_general(kws, v1, (((0,), (0,)), ((), ())),
                                      preferred_element_type=F32)
        for p in pairs:
            drow = jnp.where(sub < M_DQK, p2[2 * p][3], p2[2 * p + 1][3])
            st_s[p] = drow * states[p] + upds[2 * p] + upds[2 * p + 1]

    return phase1, phase2, phase3


FF_CHUNK = 256


DEC_HEADS = HEADS // 2
DEC_ROWS = DEC_HEADS * HEAD_DIM


def _decode_attn_step(b, qkv_ref, c_refs, o_refs, ot_ref):
    sel = lax.broadcasted_iota(jnp.int32, (DEC_ROWS, LANES), 1) == b
    scale = HEAD_DIM ** -0.5

    def column(i):
        return jnp.sum(jnp.where(sel, qkv_ref[i], 0.0), axis=1, keepdims=True)

    def head(col, h):
        return col[h * HEAD_DIM:(h + 1) * HEAD_DIM, :]

    qc = [column(g) * scale for g in range(N_GROUPS)]
    kn = [column(N_GROUPS + g) for g in range(N_GROUPS)]
    vn = [column(2 * N_GROUPS + g) for g in range(N_GROUPS)]
    cols = []
    for h in range(DEC_HEADS):
        scores, m_h = [], None
        for g, d in enumerate(DILATIONS):
            wb = c_refs[g].shape[-1]
            qh = head(qc[g], h)
            s = jnp.sum(c_refs[g][0, h] * qh, axis=0, keepdims=True)
            pos = lax.broadcasted_iota(jnp.int32, (1, wb), 1)
            s = jnp.where((pos & (d - 1)) == 0, s, NEG)
            s_new = jnp.sum(head(kn[g], h) * qh, axis=0, keepdims=True)
            m_g = jnp.maximum(jnp.max(s, axis=1, keepdims=True), s_new)
            m_h = m_g if m_h is None else jnp.maximum(m_h, m_g)
            scores.append((s, s_new))
        l_h = jnp.zeros((1, 1), F32)
        acc = jnp.zeros((HEAD_DIM, 1), F32)
        for g in range(N_GROUPS):
            s, s_new = scores[g]
            p = jnp.exp(s - m_h)
            p_new = jnp.exp(s_new - m_h)
            l_h = l_h + jnp.sum(p, axis=1, keepdims=True) + p_new
            acc = acc + jnp.sum(c_refs[g][1, h] * p, axis=1, keepdims=True) + p_new * head(vn[g], h)
        cols.append(acc / l_h)
    ocol = jnp.broadcast_to(jnp.concatenate(cols, axis=0), (DEC_ROWS, LANES))
    for c in range(DEC_ROWS // LANES):
        ot_ref[c] = ocol[c * LANES:(c + 1) * LANES, :].T[0:8, :]
    for g in range(N_GROUPS):
        wb = c_refs[g].shape[-1]
        last = lax.broadcasted_iota(jnp.int32, (HEAD_DIM, wb), 1) == wb - 1
        for kvi, new in ((0, kn[g]), (1, vn[g])):
            for h in range(DEC_HEADS):
                rolled = pltpu.roll(c_refs[g][kvi, h], wb - 1, 1)
                o_refs[g][kvi, h] = jnp.where(last, head(new, h), rolled)


def _merge_ffn_kernel(x_ref, o_ref, hm_ref, ga_ref, gb_ref, wa_ref, wm_ref, wo_ref, g2_ref,
                      wg_ref, wu_ref, wd_ref, *rest, with_decode):
    if with_decode:
        qkv_ref, c0, c1, c2, y_ref, o0, o1, o2, ot_ref = rest
        _decode_attn_step(pl.program_id(0) // 2, qkv_ref, (c0, c1, c2), (o0, o1, o2), ot_ref)
    else:
        (y_ref,) = rest
    yb = jnp.dot(hm_ref[...].astype(BF16), wm_ref[...], preferred_element_type=F32)
    o_att = jnp.concatenate([o_ref[p] for p in range(PAIRS)], axis=1).astype(BF16)
    ya = jnp.dot(o_att, wa_ref[...], preferred_element_type=F32)
    mixed = (jax.nn.sigmoid(ga_ref[...].astype(F32)) * ya
             + jax.nn.sigmoid(gb_ref[...].astype(F32)) * yb)
    x2 = x_ref[...] + jnp.dot(mixed.astype(BF16), wo_ref[...], preferred_element_type=F32)
    h2 = (x2 * lax.rsqrt(jnp.mean(x2 * x2, axis=-1, keepdims=True) + NORM_EPS)
          * g2_ref[...]).astype(BF16)
    acc = x2
    d_ff = wg_ref.shape[1]

    def gate_up(c):
        gt = jnp.dot(h2, wg_ref[:, c:c + FF_CHUNK], preferred_element_type=F32)
        up = jnp.dot(h2, wu_ref[:, c:c + FF_CHUNK], preferred_element_type=F32)
        return gt, up

    chunks = list(range(0, d_ff, FF_CHUNK))
    nxt = gate_up(chunks[0])
    for i, c in enumerate(chunks):
        gt, up = nxt
        if i + 1 < len(chunks):
            nxt = gate_up(chunks[i + 1])
        ff = (gt * jax.nn.sigmoid(gt) * up).astype(BF16)
        acc = acc + jnp.dot(ff, wd_ref[c:c + FF_CHUNK, :], preferred_element_type=F32)
    y_ref[...] = acc


def _merge_ffn(x2d, o_att, hm, ga, gb, wa, wm, wo, g2, wg, wu, wd, seq_len, tm, decode=None):
    m_rows, d_model = x2d.shape
    tiles_per_seq = seq_len // tm
    steps = m_rows // tm
    row = lambda w: pl.BlockSpec((tm, w), lambda i: (i, 0))
    o_spec = pl.BlockSpec((None, PAIRS, tm, LANES),
                          lambda i: (i // tiles_per_seq, 0, i % tiles_per_seq, 0))
    in_specs = [row(d_model), o_spec, row(M_V_W), row(D_MODEL), row(D_MODEL),
                _const_spec(wa.shape), _const_spec(wm.shape), _const_spec(wo.shape),
                _const_spec((1, d_model)), _const_spec(wg.shape), _const_spec(wu.shape),
                _const_spec(wd.shape)]
    args = [x2d, o_att, hm, ga, gb, wa, wm, wo, g2, wg, wu, wd]
    out_specs = [row(d_model)]
    out_shape = [jax.ShapeDtypeStruct((m_rows, d_model), F32)]
    if decode is not None:
        qkv_t, caches_t = decode
        db = caches_t[0].shape[0]
        assert steps == 2 * db, (steps, db)
        in_specs.append(pl.BlockSpec((qkv_t.shape[0], DEC_ROWS, LANES), lambda i: (0, i % 2, 0)))
        args.append(qkv_t)
        for c in caches_t:
            spec = pl.BlockSpec((None, 2, DEC_HEADS, HEAD_DIM, c.shape[-1]),
                                lambda i: (i // 2, 0, i % 2, 0, 0))
            in_specs.append(spec)
            args.append(c)
            out_specs.append(spec)
            out_shape.append(jax.ShapeDtypeStruct(c.shape, F32))
        out_specs.append(pl.BlockSpec((None, None, DEC_ROWS // LANES, 8, LANES),
                                      lambda i: (i // 2, i % 2, 0, 0, 0)))
        out_shape.append(jax.ShapeDtypeStruct((db, 2, DEC_ROWS // LANES, 8, LANES), F32))
    outs = pl.pallas_call(
        functools.partial(_merge_ffn_kernel, with_decode=decode is not None),
        grid=(steps,),
        in_specs=in_specs, out_specs=out_specs, out_shape=out_shape,
        compiler_params=pltpu.CompilerParams(dimension_semantics=("arbitrary",),
                                             vmem_limit_bytes=VMEM_LIMIT),
        name="merge_ffn",
    )(*args)
    if decode is None:
        return outs[0]
    return outs[0], outs[1:4], outs[4]


def _sample_prep_kernel(*refs):
    for i, ref in enumerate(refs[:-1]):
        refs[-1][i] = ref[...].T


def _sample_prep(arrs):
    vm = pl.BlockSpec(memory_space=pltpu.VMEM)
    return pl.pallas_call(
        _sample_prep_kernel, in_specs=[vm] * len(arrs), out_specs=vm,
        out_shape=jax.ShapeDtypeStruct((len(arrs), GROUP_W, LANES), F32),
        name="sample_prep",
    )(*arrs)


def _sample_mlstm_kernel(q_ref, k_ref, v_ref, om_ref, gt_ref, mt_ref, n_ref, c_ref, rexp_ref,
                         hm_ref, co_ref, nt_ref, mo_ref, *, db):
    nh, dqk = M_HEADS, M_DQK
    hi = lax.Precision.HIGHEST
    rexp = rexp_ref[...]
    q_t = q_ref[...].T * (dqk ** -0.5)
    k_t = k_ref[...].T
    n_t = n_ref[...].T
    ig = gt_ref[0:nh, :]
    lf = _log_sigmoid(gt_ref[nh:2 * nh, :])
    m_prev = mt_ref[...]
    m_new = jnp.maximum(lf + m_prev, ig)
    w8 = jnp.exp(ig - m_new)
    dec8 = jnp.exp(lf + m_prev - m_new)
    head_sum = lambda a: lax.dot_general(rexp, a, (((0,), (0,)), ((), ())), precision=hi,
                                         preferred_element_type=F32)
    expand = lambda a: jnp.dot(rexp, a, precision=hi, preferred_element_type=F32)
    qk8 = head_sum(q_t * k_t)
    qn8 = head_sum(q_t * n_t)
    wqk8 = w8 * qk8
    den8 = dec8 * qn8 + wqk8
    inv8 = 1.0 / jnp.maximum(jnp.abs(den8), jnp.exp(-m_new))
    dec_x = expand(dec8)
    w_x = expand(w8)
    nt_ref[...] = dec_x * n_t + w_x * k_t
    mo_ref[...] = m_new
    wk_x = w_x * k_t
    for b in range(db):
        cb = c_ref[b].reshape(nh * dqk, M_DV)
        v_b = v_ref[b]
        v_x = jnp.concatenate([jnp.broadcast_to(v_b[h:h + 1, :], (dqk, M_DV)) for h in range(nh)],
                              axis=0)
        co_ref[b] = (dec_x[:, b:b + 1] * cb + wk_x[:, b:b + 1] * v_x).reshape(nh, dqk, M_DV)
        qc = jnp.sum((q_t[:, b:b + 1] * cb).reshape(nh, dqk, M_DV), axis=1)
        num = dec8[:, b:b + 1] * qc + wqk8[:, b:b + 1] * v_b
        hm_ref[b] = jax.nn.sigmoid(om_ref[b]) * (num * inv8[:, b:b + 1])


def _sample_mlstm(q_t, k_t, v3, om3, gif_t, m_t, n_pad, c_state, rexp, db):
    vm = pl.BlockSpec(memory_space=pltpu.VMEM)
    return pl.pallas_call(
        functools.partial(_sample_mlstm_kernel, db=db),
        in_specs=[vm] * 9,
        out_specs=[vm] * 4,
        out_shape=[jax.ShapeDtypeStruct((db, M_HEADS, M_DV), F32),
                   jax.ShapeDtypeStruct(c_state.shape, F32),
                   jax.ShapeDtypeStruct((M_HEADS * M_DQK, LANES), F32),
                   jax.ShapeDtypeStruct((M_HEADS, LANES), F32)],
        compiler_params=pltpu.CompilerParams(vmem_limit_bytes=VMEM_LIMIT),
        name="sample_mlstm",
    )(q_t, k_t, v3, om3, gif_t, m_t, n_pad, c_state, rexp)


def _rope_tables(pos):
    half = ROT_DIM // 2
    pos = np.asarray(pos, np.float32)
    inv_freq = np.exp(np.float32(-math.log(ROPE_THETA))
                      * np.arange(0, ROT_DIM, 2, dtype=np.float32) / np.float32(ROT_DIM))
    ang = (pos[:, None] * inv_freq[None, :]).astype(np.float32)
    cos, sin = np.cos(ang).astype(np.float32), np.sin(ang).astype(np.float32)
    t = pos.shape[0]
    rest = HEAD_DIM - ROT_DIM
    a = np.concatenate([cos, cos, np.ones((t, rest), np.float32)], axis=1)
    bm = np.concatenate([-sin, np.zeros((t, HEAD_DIM - half), np.float32)], axis=1)
    bp = np.concatenate([np.zeros((t, half), np.float32), sin, np.zeros((t, rest), np.float32)],
                        axis=1)
    return tuple(jnp.asarray(np.concatenate([x, x], axis=1)) for x in (a, bm, bp))


W_PREP_BLK = 512


def _w_prep_kernel(wt_ref, o_ref):
    o_ref[...] = wt_ref[...].T.astype(o_ref.dtype)


def _w_prep(w_in):
    d_model = w_in.shape[0]
    wt = w_in.T
    n_head = _C_GA // W_PREP_BLK
    n_blk = _W_COLS // W_PREP_BLK

    sub = 8

    def src_row(j):
        return (j * (W_PREP_BLK // sub) + jnp.where(j < n_head, 0, _GIF_COLS // sub)) * sub

    return pl.pallas_call(
        _w_prep_kernel,
        grid=(n_blk,),
        in_specs=[pl.BlockSpec((pl.Element(W_PREP_BLK), pl.Element(d_model)),
                               lambda j: (src_row(j), 0))],
        out_specs=pl.BlockSpec((d_model, W_PREP_BLK), lambda j: (0, j)),
        out_shape=jax.ShapeDtypeStruct((d_model, _W_COLS), BF16),
        compiler_params=pltpu.CompilerParams(dimension_semantics=("arbitrary",),
                                             vmem_limit_bytes=VMEM_LIMIT),
        name="w_prep",
    )(wt)


def kernel(x_prompt, x_sample, cache_kv_w128, cache_kv_w512, cache_kv_w2048, state_mlstm_C, state_mlstm_n, state_mlstm_m, norm1_g, w_in, b_if, q_norm_g, k_norm_g, w_att_out, w_m_out, w_o, norm2_g, w_gate, w_up, w_down):
    nb, seq_len, d_model = x_prompt.shape
    db, dec_seq, _ = x_sample.shape
    assert dec_seq == 1 and d_model == D_MODEL and seq_len % ATT_STEP == 0 and db <= LANES
    caches = (cache_kv_w128, cache_kv_w512, cache_kv_w2048)

    w_perm = _w_prep(w_in)
    w_gif = jnp.zeros((d_model, LANES), BF16).at[:, :_GIF_COLS].set(
        w_in[:, _C_GA:_C_GA + _GIF_COLS].astype(BF16))
    g1 = norm1_g.reshape(1, d_model)
    g2 = norm2_g.reshape(1, d_model)
    bif = jnp.concatenate([b_if, jnp.zeros((LANES - b_if.shape[0],), F32)]).reshape(1, LANES)
    qg = jnp.tile(q_norm_g, HEADS).reshape(1, GROUP_W)
    kg = jnp.tile(k_norm_g, HEADS).reshape(1, GROUP_W)
    hid = np.arange(GROUP_W // 2) // HEAD_DIM
    gmat = jnp.asarray(hid[:, None] == hid[None, :], dtype=BF16)

    m_rows = nb * seq_len
    x2d = x_prompt.reshape(m_rows, d_model)
    tabs_p = _rope_tables(np.arange(seq_len))
    qg_p = qg * (HEAD_DIM ** -0.5 * math.log2(math.e))
    outs = _proj(x2d, seq_len, DILATIONS, 256, False, g1, w_perm, w_gif, bif, qg_p, kg, tabs_p, gmat)
    qs, ks, vs = outs[0:3], outs[3:6], outs[6:9]
    hm, ga, gb, st_p, m_p = outs[9:14]

    o_att, (wa, wm, wo, wg, wu, wd) = _attention(
        qs, ks, vs, seq_len, (w_att_out, w_m_out, w_o, w_gate, w_up, w_down))

    x_s = jnp.zeros((LANES, d_model), F32).at[:db].set(x_sample.reshape(db, d_model))
    tabs_s = _rope_tables(np.full((LANES,), PAST_LEN))
    outs_s = _proj(x_s, LANES, (1, 1, 1), LANES, True, g1, w_perm, w_gif, bif, qg, kg, tabs_s, gmat)
    qkv_t = _sample_prep(outs_s[0:9])
    caches_t = [c.transpose(0, 2, 3, 4, 1) for c in caches]
    y_prompt, kv_st, o_att_cols = _merge_ffn(x2d, o_att, hm, ga, gb, wa, wm, wo, g2, wg, wu, wd,
                                             seq_len, 256, decode=(qkv_t, caches_t))
    y_prompt = y_prompt.reshape(nb, seq_len, d_model)
    kv_s = [c.transpose(0, 4, 1, 2, 3) for c in kv_st]

    tails = _kv_tails(ks, vs, seq_len)
    kv_p = [t.reshape(nb, 2, HEADS, HEAD_DIM, t.shape[-1]).transpose(0, 4, 1, 2, 3) for t in tails]
    c_p = st_p[..., :M_DV].reshape(nb, M_HEADS, M_DQK, M_DV)
    n_p = st_p[..., M_DV].reshape(nb, M_HEADS, M_DQK)
    m_pr = m_p[:, :, 0]

    qm_t, km_t, vm_s, om_s, ga_s, gb_s, gif_s = outs_s[9:16]
    gif_ts = gif_s.T

    rexp = jnp.asarray(np.arange(M_HEADS * M_DQK)[:, None] // M_DQK == np.arange(M_HEADS)[None, :],
                       dtype=F32)
    m_t = jnp.zeros((M_HEADS, LANES), F32).at[:, :db].set(state_mlstm_m.T)
    n_pad = jnp.zeros((LANES, M_HEADS * M_DQK), F32).at[:db].set(state_mlstm_n.reshape(db, -1))
    hm_s3, c_s, n_t, m_so = _sample_mlstm(
        qm_t, km_t, vm_s[:db].reshape(db, M_HEADS, M_DV), om_s[:db].reshape(db, M_HEADS, M_DV),
        gif_ts[:2 * M_HEADS], m_t, n_pad, state_mlstm_C, rexp, db)
    n_s = n_t.T[:db].reshape(db, M_HEADS, M_DQK)
    m_s = m_so[:, :db].T

    o_att_s = o_att_cols[:, :, :, 0, :].reshape(db, PAIRS, LANES)
    o_att_sp = jnp.zeros((PAIRS, LANES, LANES), F32).at[:, :db].set(o_att_s.transpose(1, 0, 2))[None]
    hm_sp = jnp.zeros((LANES, M_V_W), F32).at[:db].set(hm_s3.reshape(db, M_V_W))
    y_s = _merge_ffn(x_s, o_att_sp, hm_sp, ga_s, gb_s, wa, wm, wo, g2, wg, wu, wd, LANES, LANES)
    y_sample = y_s[:db].reshape(db, 1, d_model)

    return (y_prompt, y_sample, kv_p[0], kv_p[1], kv_p[2], c_p, n_p, m_pr,
            kv_s[0], kv_s[1], kv_s[2], c_s, n_s, m_s)
```

```python
import functools
import math

import jax
import jax.numpy as jnp
import numpy as np
from jax import lax
from jax.experimental import pallas as pl
from jax.experimental.pallas import tpu as pltpu

F32 = jnp.float32
BF16 = jnp.bfloat16

HEAD_DIM = 64
HEADS = 8
GROUP_W = HEADS * HEAD_DIM
N_GROUPS = 3
WINDOWS = (128, 512, 2048)
DILATIONS = (1, 4, 16)
SPAN = 128
ROT_DIM = 16
ROPE_THETA = 500000.0
M_HEADS = 8
M_DQK = 64
M_DV = 128
M_QK_W = M_HEADS * M_DQK
M_V_W = M_HEADS * M_DV
D_MODEL = 1024
PAST_LEN = 8192
NORM_EPS = 1e-6
NEG = -1e30

LANES = 128
PAIRS = GROUP_W // LANES
VMEM_LIMIT = 56 * 1024 * 1024

_ATT_W = N_GROUPS * GROUP_W
_C_QM = 3 * _ATT_W
_C_KM = _C_QM + M_QK_W
_C_VM = _C_KM + M_QK_W
_C_OM = _C_VM + M_V_W
_C_GA = _C_OM + M_V_W
_C_GB = _C_GA + D_MODEL
_W_COLS = _C_GB + D_MODEL
_GIF_COLS = 2 * M_HEADS


def _const_spec(shape):
    nd = len(shape)
    return pl.BlockSpec(shape, lambda *_: (0,) * nd, pipeline_mode=pl.Buffered(1))


def _log_sigmoid(x):
    return jnp.minimum(x, 0.0) - jnp.log1p(jnp.exp(-jnp.abs(x)))


MLSTM_CHUNKS = 2


def _proj_kernel(x_ref, g1_ref, w_ref, wgif_ref, bif_ref, qg_ref, kg_ref, ra_ref, rm_ref, rp_ref,
                 gm_ref,
                 q0_ref, q1_ref, q2_ref, k0_ref, k1_ref, k2_ref, v0_ref, v1_ref, v2_ref,
                 *rest, tm, dils, plain, tiles_per_seq):
    if plain:
        qm_ref, km_ref, vm_ref, om_ref, ga_ref, gb_ref, gif_ref, hs_ref = rest
    else:
        hm_ref, ga_ref, gb_ref, st_ref, mo_ref, hs_ref, st_s, m_s = rest
    d_model = x_ref.shape[1]
    x = x_ref[...]
    xn = x * lax.rsqrt(jnp.mean(x * x, axis=-1, keepdims=True) + NORM_EPS) * g1_ref[...]
    h_nat = xn.astype(BF16)
    n_slab = d_model // LANES
    if any(d > 1 for d in dils):
        for c in range(n_slab):
            hs_ref[c] = xn[:, c * LANES:(c + 1) * LANES]

    def permuted_h(d):
        if d == 1:
            return h_nat
        n = tm // d
        rows = [jnp.concatenate([hs_ref[c, pl.ds(r, n, stride=d), :] for c in range(n_slab)], axis=1)
                for r in range(d)]
        return jnp.concatenate(rows, axis=0).astype(BF16)

    def permuted_tab(ref, d):
        if d == 1:
            t = ref[...]
        else:
            n = tm // d
            t = jnp.concatenate([ref[pl.ds(r, n, stride=d), :] for r in range(d)], axis=0)
        return jnp.concatenate([t] * PAIRS, axis=1)

    gmat = gm_ref[...]

    def head_sumsq(z):
        zz = (z * z).astype(BF16)
        half = GROUP_W // 2
        return jnp.concatenate(
            [jnp.dot(zz[:, :half], gmat, preferred_element_type=F32),
             jnp.dot(zz[:, half:], gmat, preferred_element_type=F32)], axis=1)

    def norm_rope(z, ss, gain, ra, rm, rp):
        y = z * lax.rsqrt(ss * (1.0 / HEAD_DIM) + NORM_EPS) * gain
        return (y * ra + pltpu.roll(y, GROUP_W - ROT_DIM // 2, 1) * rm
                + pltpu.roll(y, ROT_DIM // 2, 1) * rp)

    def store_group(ref, y, d):
        if plain:
            ref[...] = y.astype(ref.dtype)
            return
        n = tm // d
        for p in range(PAIRS):
            ref[:, p] = y[:, p * LANES:(p + 1) * LANES].reshape(d, n, LANES).astype(ref.dtype)

    def seg(c0, width):
        return jnp.dot(h_nat, w_ref[:, c0:c0 + width], preferred_element_type=F32)

    if not plain:
        @pl.when(pl.program_id(0) % tiles_per_seq == 0)
        def _():
            st_s[...] = jnp.zeros_like(st_s)
            m_s[...] = jnp.zeros_like(m_s)

        gif = jnp.dot(h_nat, wgif_ref[...], preferred_element_type=F32) + bif_ref[...]
        qm_b, km_b = seg(_C_QM, M_QK_W).astype(BF16), seg(_C_KM, M_QK_W).astype(BF16)
        vm_b, om_b = seg(_C_VM, M_V_W).astype(BF16), seg(_C_OM, M_V_W).astype(BF16)
        gif_t = gif.T[:2 * M_HEADS, :]
        lc = tm // MLSTM_CHUNKS
        m_chunks = []
        for c in range(MLSTM_CHUNKS):
            rows = slice(c * lc, (c + 1) * lc)
            m_chunks.append(functools.partial(
                _mlstm_chunk, qm_b[rows], km_b[rows], vm_b[rows], om_b[rows], gif[rows],
                gif_t[:, rows], st_s, m_s, hm_ref.at[rows], lc))
        m_live, m_started = [], {}
        m_units = [(c, pairs) for c in range(MLSTM_CHUNKS) for pairs in ((0, 1), (2, 3))]

        def m_advance(i):
            if 0 < i <= len(m_units):
                phases, pairs = m_live.pop()
                phases[2](pairs)
            if i < len(m_units):
                c, pairs = m_units[i]
                if c not in m_started:
                    m_started[c] = m_chunks[c]()
                phases = m_started[c]
                phases[0](pairs)
                phases[1](pairs)
                m_live.append((phases, pairs))

        m_advance(0)

    q_refs = (q0_ref, q1_ref, q2_ref)
    k_refs = (k0_ref, k1_ref, k2_ref)
    v_refs = (v0_ref, v1_ref, v2_ref)
    for g in range(N_GROUPS):
        d = dils[g]
        hg = permuted_h(d)
        ra, rm, rp = (permuted_tab(r, d) for r in (ra_ref, rm_ref, rp_ref))
        cq, ck, cv = (t * _ATT_W + g * GROUP_W for t in range(3))
        zq = jnp.dot(hg, w_ref[:, cq:cq + GROUP_W], preferred_element_type=F32)
        zk = jnp.dot(hg, w_ref[:, ck:ck + GROUP_W], preferred_element_type=F32)
        zv = jnp.dot(hg, w_ref[:, cv:cv + GROUP_W], preferred_element_type=F32)
        ssq, ssk = head_sumsq(zq), head_sumsq(zk)
        store_group(v_refs[g], zv, d)
        store_group(q_refs[g], norm_rope(zq, ssq, qg_ref[...], ra, rm, rp), d)
        store_group(k_refs[g], norm_rope(zk, ssk, kg_ref[...], ra, rm, rp), d)

        if not plain:
            m_advance(g + 1)

    if plain:
        segments = ((qm_ref, _C_QM, M_QK_W), (km_ref, _C_KM, M_QK_W), (vm_ref, _C_VM, M_V_W),
                    (om_ref, _C_OM, M_V_W), (ga_ref, _C_GA, D_MODEL), (gb_ref, _C_GB, D_MODEL))
        zg = jnp.dot(h_nat, wgif_ref[...], preferred_element_type=F32)
        gif_ref[...] = zg + bif_ref[...]
    else:
        segments = ((ga_ref, _C_GA, D_MODEL), (gb_ref, _C_GB, D_MODEL))
    for si, (ref, c0, width) in enumerate(segments):
        for cc in range(0, width, GROUP_W):
            ref[:, cc:cc + GROUP_W] = seg(c0 + cc, GROUP_W).astype(ref.dtype)
        if not plain and si == 0:
            m_advance(N_GROUPS + 1)
            st_ref[...] = st_s[...]
            mo_ref[...] = m_s[...]


def _proj(x2d, seq_len, dils, tm, plain, g1, w_perm, w_gif, bif, qg, kg, rope_tabs, gmat):
    m_rows, d_model = x2d.shape
    nb = m_rows // seq_len
    tiles_per_seq = seq_len // tm
    grid = (m_rows // tm,)
    row_spec = lambda w: pl.BlockSpec((tm, w), lambda i: (i, 0))
    tab_spec = pl.BlockSpec((tm, LANES), lambda i: (i % tiles_per_seq, 0))
    sds = jax.ShapeDtypeStruct

    widths = (M_QK_W, M_QK_W, M_V_W, M_V_W, D_MODEL, D_MODEL)
    scratch = [pltpu.VMEM((d_model // LANES, tm, LANES), F32)]
    if plain:
        out_shape = ([sds((m_rows, GROUP_W), F32)] * 9 + [sds((m_rows, w), F32) for w in widths]
                     + [sds((m_rows, LANES), F32)])
        out_specs = ([row_spec(GROUP_W)] * 9 + [row_spec(w) for w in widths] + [row_spec(LANES)])
    else:
        def grp_spec(d):
            return pl.BlockSpec((None, d, PAIRS, tm // d, LANES),
                                lambda i: (i // tiles_per_seq, 0, 0, i % tiles_per_seq, 0))
        grp_shape = lambda d: sds((nb, d, PAIRS, seq_len // d, LANES), BF16)
        state_spec = lambda *dims: pl.BlockSpec((None,) + dims,
                                                lambda i: (i // tiles_per_seq,) + (0,) * len(dims))
        st_dims, m_dims = (M_HEADS // 2, LANES, 2 * LANES), (M_HEADS, LANES)
        row_w = (M_V_W, D_MODEL, D_MODEL)
        out_shape = ([grp_shape(d) for d in dils] * 3 + [sds((m_rows, w), BF16) for w in row_w]
                     + [sds((nb,) + st_dims, F32), sds((nb,) + m_dims, F32)])
        out_specs = ([grp_spec(d) for d in dils] * 3 + [row_spec(w) for w in row_w]
                     + [state_spec(*st_dims), state_spec(*m_dims)])
        scratch += [pltpu.VMEM(st_dims, F32), pltpu.VMEM(m_dims, F32)]
    in_specs = [row_spec(d_model), _const_spec((1, d_model)), _const_spec(w_perm.shape),
                _const_spec(w_gif.shape), _const_spec((1, LANES)), _const_spec((1, GROUP_W)),
                _const_spec((1, GROUP_W)), tab_spec, tab_spec, tab_spec, _const_spec(gmat.shape)]
    return pl.pallas_call(
        functools.partial(_proj_kernel, tm=tm, dils=dils, plain=plain,
                          tiles_per_seq=tiles_per_seq),
        grid=grid, in_specs=in_specs, out_specs=out_specs, out_shape=out_shape,
        scratch_shapes=scratch,
        compiler_params=pltpu.CompilerParams(dimension_semantics=("arbitrary",),
                                             vmem_limit_bytes=VMEM_LIMIT),
        name="proj",
    )(x2d, g1, w_perm, w_gif, bif, qg, kg, *rope_tabs, gmat)


ATT_STEP = SPAN * max(DILATIONS)
FAR_PITCH = SPAN + 8


def _attn_kernel(q0, k0, v0, kh0, vh0, q1, k1, v1, kh1, vh1, q2, k2, v2, kh2, vh2,
                 *rest, n_cast):
    w_f32, o_ref, w_bf16 = rest[:n_cast], rest[n_cast], rest[n_cast + 1:2 * n_cast + 1]
    acc_s, m_s, l_s, far_s = rest[2 * n_cast + 1:]
    j = pl.program_id(1)
    for src, dst in zip(w_f32, w_bf16):
        dst[...] = src[...].astype(dst.dtype)

    def window(cur, halo, r, bi):
        prev = halo[r] if bi == 0 else cur[r, (bi - 1) * SPAN:bi * SPAN, :]
        return jnp.concatenate([prev, cur[r, bi * SPAN:(bi + 1) * SPAN, :]], axis=0)

    qi = lax.broadcasted_iota(jnp.int32, (SPAN, 2 * SPAN), 0)
    ci = lax.broadcasted_iota(jnp.int32, (SPAN, 2 * SPAN), 1)
    band = (ci >= qi) & (ci <= qi + SPAN)
    bias_band = jnp.where(band, 0.0, NEG).astype(F32)
    bias_first = jnp.where(band & (ci >= SPAN), 0.0, NEG).astype(F32)
    lane_q = lax.broadcasted_iota(jnp.int32, (SPAN, LANES), 1)
    lane_kv = lax.broadcasted_iota(jnp.int32, (2 * SPAN, LANES), 1)
    halves_q = (lane_q < HEAD_DIM, lane_q >= HEAD_DIM)
    halves_kv = (lane_kv < HEAD_DIM, lane_kv >= HEAD_DIM)

    d0, d1, d2 = DILATIONS
    units = (
        [(0, bi, pl.ds(bi * SPAN, SPAN)) for bi in range(ATT_STEP // d0 // SPAN)],
        [(r, bi, pl.ds(bi * SPAN * d1 + r, SPAN, stride=d1))
         for r in range(d1) for bi in range(ATT_STEP // d1 // SPAN)],
        [(r, 0, pl.ds(r * FAR_PITCH, SPAN)) for r in range(d2)],
    )
    q_refs = (q0, q1, q2)
    kv_refs = ((k0, kh0, v0, vh0), (k1, kh1, v1, vh1), (k2, kh2, v2, vh2))

    def scores(g):
        k_cur, k_halo = kv_refs[g][:2]
        out = []
        for r, bi, _ in units[g]:
            q2_ = q_refs[g][r, bi * SPAN:(bi + 1) * SPAN, :]
            kk = window(k_cur, k_halo, r, bi)
            bias = jnp.where(j == 0, bias_first, bias_band) if bi == 0 else bias_band
            ss = []
            for hh in range(2):
                qa = jnp.where(halves_q[hh], q2_, jnp.zeros_like(q2_))
                ss.append(lax.dot_general(qa, kk, (((1,), (1,)), ((), ())),
                                          preferred_element_type=F32) + bias)
            out.append(ss)
        return out

    def finish(g, all_ss):
        v_cur, v_halo = kv_refs[g][2:]
        probs = []
        for ss in all_ss:
            mxs = [jnp.max(s, axis=-1, keepdims=True) for s in ss]
            ps = [jnp.exp2(s - mx) for s, mx in zip(ss, mxs)]
            ls = [jnp.sum(p, axis=-1, keepdims=True) for p in ps]
            probs.append(([p.astype(BF16) for p in ps], mxs, ls))
        for (r, bi, sl), (ps, mxs, ls) in zip(units[g], probs):
            vv = window(v_cur, v_halo, r, bi)
            acc = None
            for hh in range(2):
                vh = jnp.where(halves_kv[hh], vv, jnp.zeros_like(vv))
                a = jnp.dot(ps[hh], vh, preferred_element_type=F32)
                acc = a if acc is None else acc + a
            stats = (acc, jnp.where(halves_q[0], mxs[0], mxs[1]),
                     jnp.where(halves_q[0], ls[0], ls[1]))
            if g == N_GROUPS - 1:
                for k, val in enumerate(stats):
                    far_s[k, sl, :] = val
            else:
                for ref, val in zip((acc_s, m_s, l_s), stats):
                    ref[g, sl, :] = val

    for g in range(N_GROUPS):
        finish(g, scores(g))

    def far_rows(k, ci_):
        sub = 8
        pieces = [far_s[k, pl.ds(a * sub * FAR_PITCH + ci_ * (SPAN // d2) + jj, sub,
                                 stride=FAR_PITCH), :]
                  for jj in range(SPAN // d2) for a in range(d2 // sub)]
        return jnp.concatenate(pieces, axis=0)

    def combine(ci_, c):
        sl = pl.ds(pl.multiple_of(ci_ * SPAN, SPAN), SPAN)
        accs = [acc_s[0, sl, :], acc_s[1, sl, :], far_rows(0, ci_)]
        ms = [m_s[0, sl, :], m_s[1, sl, :], far_rows(1, ci_)]
        ls = [l_s[0, sl, :], l_s[1, sl, :], far_rows(2, ci_)]
        mm = jnp.maximum(jnp.maximum(ms[0], ms[1]), ms[2])
        es = [jnp.exp2(m - mm) for m in ms]
        num = es[0] * accs[0] + es[1] * accs[1] + es[2] * accs[2]
        den = es[0] * ls[0] + es[1] * ls[1] + es[2] * ls[2]
        o_ref[sl, :] = (num / den).astype(o_ref.dtype)
        return c
    lax.fori_loop(0, ATT_STEP // SPAN, combine, 0, unroll=4)


def _attention(qs, ks, vs, seq_len, cast_ws):
    nb = qs[0].shape[0]
    steps = seq_len // ATT_STEP
    in_specs, args, scratch = [], [], []
    for g, d in enumerate(DILATIONS):
        rows = ATT_STEP // d
        cur = pl.BlockSpec((None, d, None, rows, LANES), lambda b, j, p: (b, 0, p, j, 0))
        ratio = rows // SPAN
        halo = pl.BlockSpec((None, d, None, SPAN, LANES),
                            lambda b, j, p, ratio=ratio: (b, 0, p, jnp.maximum(j * ratio - 1, 0), 0))
        in_specs += [cur, cur, cur, halo, halo]
        args += [qs[g], ks[g], vs[g], ks[g], vs[g]]
    scratch += [pltpu.VMEM((N_GROUPS - 1, ATT_STEP, LANES), F32)] * 3
    scratch += [pltpu.VMEM((3, max(DILATIONS) * FAR_PITCH, LANES), F32)]
    n_slices = max(nb * steps * PAIRS // 2, 1)
    slice_of = lambda b, j, p: (((b * steps + j) * PAIRS + p) // 2, 0)
    w_specs = []
    for w in cast_ws:
        assert w.shape[0] % (16 * n_slices) == 0, (w.shape, n_slices)
        w_specs.append(pl.BlockSpec((w.shape[0] // n_slices, w.shape[1]), slice_of))
    outs = pl.pallas_call(
        functools.partial(_attn_kernel, n_cast=len(cast_ws)),
        grid=(nb, steps, PAIRS),
        in_specs=in_specs + w_specs,
        out_specs=[pl.BlockSpec((None, None, ATT_STEP, LANES), lambda b, j, p: (b, p, j, 0))]
                  + w_specs,
        out_shape=[jax.ShapeDtypeStruct((nb, PAIRS, seq_len, LANES), BF16)]
                  + [jax.ShapeDtypeStruct(w.shape, BF16) for w in cast_ws],
        scratch_shapes=scratch,
        compiler_params=pltpu.CompilerParams(
            dimension_semantics=("arbitrary", "arbitrary", "arbitrary"),
            vmem_limit_bytes=VMEM_LIMIT),
        name="attn",
    )(*args, *cast_ws)
    return outs[0], outs[1:]


def _kv_tail_kernel(k0, v0, k1, v1, k2, v2, o0, o1, o2, nat_ref):
    for (k_ref, v_ref, o_ref, d) in ((k0, v0, o0, DILATIONS[0]), (k1, v1, o1, DILATIONS[1]),
                                     (k2, v2, o2, DILATIONS[2])):
        for kvi, ref in enumerate((k_ref, v_ref)):
            for p in range(PAIRS):
                rows = slice(p * LANES, (p + 1) * LANES)
                if d == 1:
                    o_ref[kvi, rows, :] = ref[0, p].astype(F32).T
                    continue
                for r in range(d):
                    nat_ref[pl.ds(r, SPAN, stride=d), :] = ref[r, p].astype(F32)
                for c in range(d):
                    o_ref[kvi, rows, c * SPAN:(c + 1) * SPAN] = nat_ref[c * SPAN:(c + 1) * SPAN, :].T


def _kv_tails(ks, vs, seq_len):
    nb = ks[0].shape[0]
    in_specs, args, out_specs, out_shape = [], [], [], []
    for g, d in enumerate(DILATIONS):
        last = seq_len // d // SPAN - 1
        spec = pl.BlockSpec((None, d, PAIRS, SPAN, LANES), lambda b, last=last: (b, 0, 0, last, 0))
        in_specs += [spec, spec]
        args += [ks[g], vs[g]]
        out_specs.append(pl.BlockSpec((None, 2, GROUP_W, SPAN * d), lambda b: (b, 0, 0, 0)))
        out_shape.append(jax.ShapeDtypeStruct((nb, 2, GROUP_W, SPAN * d), F32))
    return pl.pallas_call(
        _kv_tail_kernel,
        grid=(nb,), in_specs=in_specs, out_specs=out_specs, out_shape=out_shape,
        scratch_shapes=[pltpu.VMEM((SPAN * max(DILATIONS), LANES), F32)],
        compiler_params=pltpu.CompilerParams(dimension_semantics=("arbitrary",),
                                             vmem_limit_bytes=VMEM_LIMIT),
        name="kv_tail",
    )(*args)


def _mlstm_chunk(q2d, k2d, v2d, om2d, gif, gt, st_s, m_s, hm_ref, lc):
    lf_c = _log_sigmoid(gif)
    lf_r = _log_sigmoid(gt[M_HEADS:2 * M_HEADS, :])
    row = lax.broadcasted_iota(jnp.int32, (lc, lc), 0)
    col = lax.broadcasted_iota(jnp.int32, (lc, lc), 1)
    causal = col <= row
    tril = causal.astype(F32)
    triu = (row <= col).astype(F32)
    b_c = jnp.dot(tril, lf_c, precision=lax.Precision.HIGHEST, preferred_element_type=F32)
    b_r = jnp.dot(lf_r, triu, precision=lax.Precision.HIGHEST, preferred_element_type=F32)
    lane = lax.broadcasted_iota(jnp.int32, (lc, LANES), 1)
    halves = (lane < M_DQK, lane >= M_DQK)
    sub = lax.broadcasted_iota(jnp.int32, (LANES, 1), 0)
    ones_blk = jnp.ones((lc, LANES), BF16)
    qscale = M_DQK ** -0.5

    states, p1, p2 = {}, {}, {}
    all_pairs = tuple(range(M_HEADS // 2))
    heads_of = lambda pairs: [2 * p + hh for p in pairs for hh in range(2)]

    def phase1(pairs=all_pairs):
        for p in pairs:
            states[p] = st_s[p]
        for h in heads_of(pairs):
            p, hh = divmod(h, 2)
            q2_ = q2d[:, p * LANES:(p + 1) * LANES]
            k2_ = k2d[:, p * LANES:(p + 1) * LANES]
            qa = jnp.where(halves[hh], q2_, jnp.zeros_like(q2_)) * jnp.asarray(qscale, q2_.dtype)
            s = lax.dot_general(qa, k2_, (((1,), (1,)), ((), ())), preferred_element_type=F32)
            qc = jnp.dot(qa, states[p].astype(BF16), preferred_element_type=F32)
            p1[h] = (s, qc)

    def phase2(pairs=all_pairs):
        for h in heads_of(pairs):
            p, hh = divmod(h, 2)
            s, qc = p1[h]
            bcol = b_c[:, M_HEADS + h:M_HEADS + h + 1]
            igcol = gif[:, h:h + 1]
            a_row = gt[h:h + 1, :] - b_r[h:h + 1, :]
            amat = jnp.where(causal, a_row, NEG)
            m_prev = m_s[h:h + 1, 0:1]
            g_t = jnp.maximum(m_prev, jnp.max(amat, axis=-1, keepdims=True))
            wqk = (jnp.exp(amat - g_t) * s).astype(BF16)
            dec = jnp.exp(m_prev - g_t)
            m_t = bcol + g_t
            m_new = m_t[lc - 1:lc, :]
            b_last = bcol[lc - 1:lc, :]
            dstate = jnp.exp(b_last + m_prev - m_new)
            ws = jnp.exp(b_last - bcol + igcol - m_new)
            k2_ = k2d[:, p * LANES:(p + 1) * LANES]
            ka = jnp.where(halves[hh], k2_, jnp.zeros_like(k2_))
            kws = (ka.astype(F32) * ws).astype(BF16)
            m_s[h:h + 1, :] = jnp.broadcast_to(m_new, (1, LANES))
            p2[h] = (wqk, dec * qc, jnp.exp(-m_t), dstate, kws)

    def phase3(pairs=all_pairs):
        upds = {}
        for h in heads_of(pairs):
            wqk, dqc, floor, _, kws = p2[h]
            v1 = jnp.concatenate([v2d[:, h * M_DV:(h + 1) * M_DV], ones_blk], axis=1)
            num_den = dqc + jnp.dot(wqk, v1, preferred_element_type=F32)
            num = num_den[:, :M_DV]
            den = num_den[:, M_DV:]
            hval = num / jnp.maximum(jnp.abs(den), floor)
            gate = jax.nn.sigmoid(om2d[:, h * M_DV:(h + 1) * M_DV].astype(F32))
            hm_ref[:, h * M_DV:(h + 1) * M_DV] = (gate * hval).astype(hm_ref.dtype)
            upds[h] = lax.dot_general(kws, v1, (((0,), (0,)), ((), ())),
                                      preferred_element_type=F32)
        for p in pairs:
            drow = jnp.where(sub < M_DQK, p2[2 * p][3], p2[2 * p + 1][3])
            st_s[p] = drow * states[p] + upds[2 * p] + upds[2 * p + 1]

    return phase1, phase2, phase3


FF_CHUNK = 256


DEC_HEADS = HEADS // 2
DEC_ROWS = DEC_HEADS * HEAD_DIM


def _decode_attn_step(b, qkv_ref, c_refs, o_refs, ot_ref):
    sel = lax.broadcasted_iota(jnp.int32, (DEC_ROWS, LANES), 1) == b
    scale = HEAD_DIM ** -0.5

    def column(i):
        return jnp.sum(jnp.where(sel, qkv_ref[i], 0.0), axis=1, keepdims=True)

    def head(col, h):
        return col[h * HEAD_DIM:(h + 1) * HEAD_DIM, :]

    qc = [column(g) * scale for g in range(N_GROUPS)]
    kn = [column(N_GROUPS + g) for g in range(N_GROUPS)]
    vn = [column(2 * N_GROUPS + g) for g in range(N_GROUPS)]
    cols = []
    for h in range(DEC_HEADS):
        scores, m_h = [], None
        for g, d in enumerate(DILATIONS):
            wb = c_refs[g].shape[-1]
            qh = head(qc[g], h)
            s = jnp.sum(c_refs[g][0, h] * qh, axis=0, keepdims=True)
            pos = lax.broadcasted_iota(jnp.int32, (1, wb), 1)
            s = jnp.where((pos & (d - 1)) == 0, s, NEG)
            s_new = jnp.sum(head(kn[g], h) * qh, axis=0, keepdims=True)
            m_g = jnp.maximum(jnp.max(s, axis=1, keepdims=True), s_new)
            m_h = m_g if m_h is None else jnp.maximum(m_h, m_g)
            scores.append((s, s_new))
        l_h = jnp.zeros((1, 1), F32)
        acc = jnp.zeros((HEAD_DIM, 1), F32)
        for g in range(N_GROUPS):
            s, s_new = scores[g]
            p = jnp.exp(s - m_h)
            p_new = jnp.exp(s_new - m_h)
            l_h = l_h + jnp.sum(p, axis=1, keepdims=True) + p_new
            acc = acc + jnp.sum(c_refs[g][1, h] * p, axis=1, keepdims=True) + p_new * head(vn[g], h)
        cols.append(acc / l_h)
    ot_ref[...] = jnp.broadcast_to(jnp.concatenate(cols, axis=0), ot_ref.shape)
    for g in range(N_GROUPS):
        wb = c_refs[g].shape[-1]
        last = lax.broadcasted_iota(jnp.int32, (HEAD_DIM, wb), 1) == wb - 1
        for kvi, new in ((0, kn[g]), (1, vn[g])):
            for h in range(DEC_HEADS):
                rolled = pltpu.roll(c_refs[g][kvi, h], wb - 1, 1)
                o_refs[g][kvi, h] = jnp.where(last, head(new, h), rolled)


def _merge_ffn_kernel(x_ref, o_ref, hm_ref, ga_ref, gb_ref, wa_ref, wm_ref, wo_ref, g2_ref,
                      wg_ref, wu_ref, wd_ref, *rest, with_decode):
    if with_decode:
        qkv_ref, c0, c1, c2, y_ref, o0, o1, o2, ot_ref = rest
        _decode_attn_step(pl.program_id(0) // 2, qkv_ref, (c0, c1, c2), (o0, o1, o2), ot_ref)
    else:
        (y_ref,) = rest
    yb = jnp.dot(hm_ref[...].astype(BF16), wm_ref[...], preferred_element_type=F32)
    o_att = jnp.concatenate([o_ref[p] for p in range(PAIRS)], axis=1).astype(BF16)
    ya = jnp.dot(o_att, wa_ref[...], preferred_element_type=F32)
    mixed = (jax.nn.sigmoid(ga_ref[...].astype(F32)) * ya
             + jax.nn.sigmoid(gb_ref[...].astype(F32)) * yb)
    x2 = x_ref[...] + jnp.dot(mixed.astype(BF16), wo_ref[...], preferred_element_type=F32)
    h2 = (x2 * lax.rsqrt(jnp.mean(x2 * x2, axis=-1, keepdims=True) + NORM_EPS)
          * g2_ref[...]).astype(BF16)
    acc = x2
    d_ff = wg_ref.shape[1]

    def gate_up(c):
        gt = jnp.dot(h2, wg_ref[:, c:c + FF_CHUNK], preferred_element_type=F32)
        up = jnp.dot(h2, wu_ref[:, c:c + FF_CHUNK], preferred_element_type=F32)
        return gt, up

    chunks = list(range(0, d_ff, FF_CHUNK))
    nxt = gate_up(chunks[0])
    for i, c in enumerate(chunks):
        gt, up = nxt
        if i + 1 < len(chunks):
            nxt = gate_up(chunks[i + 1])
        ff = (gt * jax.nn.sigmoid(gt) * up).astype(BF16)
        acc = acc + jnp.dot(ff, wd_ref[c:c + FF_CHUNK, :], preferred_element_type=F32)
    y_ref[...] = acc


def _merge_ffn(x2d, o_att, hm, ga, gb, wa, wm, wo, g2, wg, wu, wd, seq_len, tm, decode=None):
    m_rows, d_model = x2d.shape
    tiles_per_seq = seq_len // tm
    steps = m_rows // tm
    row = lambda w: pl.BlockSpec((tm, w), lambda i: (i, 0))
    o_spec = pl.BlockSpec((None, PAIRS, tm, LANES),
                          lambda i: (i // tiles_per_seq, 0, i % tiles_per_seq, 0))
    in_specs = [row(d_model), o_spec, row(M_V_W), row(D_MODEL), row(D_MODEL),
                _const_spec(wa.shape), _const_spec(wm.shape), _const_spec(wo.shape),
                _const_spec((1, d_model)), _const_spec(wg.shape), _const_spec(wu.shape),
                _const_spec(wd.shape)]
    args = [x2d, o_att, hm, ga, gb, wa, wm, wo, g2, wg, wu, wd]
    out_specs = [row(d_model)]
    out_shape = [jax.ShapeDtypeStruct((m_rows, d_model), F32)]
    if decode is not None:
        qkv_t, caches_t = decode
        db = caches_t[0].shape[0]
        assert steps == 2 * db, (steps, db)
        in_specs.append(pl.BlockSpec((qkv_t.shape[0], DEC_ROWS, LANES), lambda i: (0, i % 2, 0)))
        args.append(qkv_t)
        for c in caches_t:
            spec = pl.BlockSpec((None, 2, DEC_HEADS, HEAD_DIM, c.shape[-1]),
                                lambda i: (i // 2, 0, i % 2, 0, 0))
            in_specs.append(spec)
            args.append(c)
            out_specs.append(spec)
            out_shape.append(jax.ShapeDtypeStruct(c.shape, F32))
        out_specs.append(pl.BlockSpec((None, None, DEC_ROWS, LANES), lambda i: (i // 2, i % 2, 0, 0)))
        out_shape.append(jax.ShapeDtypeStruct((db, 2, DEC_ROWS, LANES), F32))
    outs = pl.pallas_call(
        functools.partial(_merge_ffn_kernel, with_decode=decode is not None),
        grid=(steps,),
        in_specs=in_specs, out_specs=out_specs, out_shape=out_shape,
        compiler_params=pltpu.CompilerParams(dimension_semantics=("arbitrary",),
                                             vmem_limit_bytes=VMEM_LIMIT),
        name="merge_ffn",
    )(*args)
    if decode is None:
        return outs[0]
    return outs[0], outs[1:4], outs[4]


def _sample_prep_kernel(*refs):
    for i, ref in enumerate(refs[:-1]):
        refs[-1][i] = ref[...].T


def _sample_prep(arrs):
    vm = pl.BlockSpec(memory_space=pltpu.VMEM)
    return pl.pallas_call(
        _sample_prep_kernel, in_specs=[vm] * len(arrs), out_specs=vm,
        out_shape=jax.ShapeDtypeStruct((len(arrs), GROUP_W, LANES), F32),
        name="sample_prep",
    )(*arrs)


def _sample_mlstm_kernel(q_ref, k_ref, v_ref, om_ref, gt_ref, mt_ref, n_ref, c_ref, rexp_ref,
                         hm_ref, co_ref, nt_ref, mo_ref, *, db):
    nh, dqk = M_HEADS, M_DQK
    hi = lax.Precision.HIGHEST
    rexp = rexp_ref[...]
    q_t = q_ref[...].T * (dqk ** -0.5)
    k_t = k_ref[...].T
    n_t = n_ref[...].T
    ig = gt_ref[0:nh, :]
    lf = _log_sigmoid(gt_ref[nh:2 * nh, :])
    m_prev = mt_ref[...]
    m_new = jnp.maximum(lf + m_prev, ig)
    w8 = jnp.exp(ig - m_new)
    dec8 = jnp.exp(lf + m_prev - m_new)
    head_sum = lambda a: lax.dot_general(rexp, a, (((0,), (0,)), ((), ())), precision=hi,
                                         preferred_element_type=F32)
    expand = lambda a: jnp.dot(rexp, a, precision=hi, preferred_element_type=F32)
    qk8 = head_sum(q_t * k_t)
    qn8 = head_sum(q_t * n_t)
    wqk8 = w8 * qk8
    den8 = dec8 * qn8 + wqk8
    inv8 = 1.0 / jnp.maximum(jnp.abs(den8), jnp.exp(-m_new))
    dec_x = expand(dec8)
    w_x = expand(w8)
    nt_ref[...] = dec_x * n_t + w_x * k_t
    mo_ref[...] = m_new
    wk_x = w_x * k_t
    for b in range(db):
        cb = c_ref[b].reshape(nh * dqk, M_DV)
        v_b = v_ref[b]
        v_x = jnp.concatenate([jnp.broadcast_to(v_b[h:h + 1, :], (dqk, M_DV)) for h in range(nh)],
                              axis=0)
        co_ref[b] = (dec_x[:, b:b + 1] * cb + wk_x[:, b:b + 1] * v_x).reshape(nh, dqk, M_DV)
        qc = jnp.sum((q_t[:, b:b + 1] * cb).reshape(nh, dqk, M_DV), axis=1)
        num = dec8[:, b:b + 1] * qc + wqk8[:, b:b + 1] * v_b
        hm_ref[b] = jax.nn.sigmoid(om_ref[b]) * (num * inv8[:, b:b + 1])


def _sample_mlstm(q_t, k_t, v3, om3, gif_t, m_t, n_pad, c_state, rexp, db):
    vm = pl.BlockSpec(memory_space=pltpu.VMEM)
    return pl.pallas_call(
        functools.partial(_sample_mlstm_kernel, db=db),
        in_specs=[vm] * 9,
        out_specs=[vm] * 4,
        out_shape=[jax.ShapeDtypeStruct((db, M_HEADS, M_DV), F32),
                   jax.ShapeDtypeStruct(c_state.shape, F32),
                   jax.ShapeDtypeStruct((M_HEADS * M_DQK, LANES), F32),
                   jax.ShapeDtypeStruct((M_HEADS, LANES), F32)],
        compiler_params=pltpu.CompilerParams(vmem_limit_bytes=VMEM_LIMIT),
        name="sample_mlstm",
    )(q_t, k_t, v3, om3, gif_t, m_t, n_pad, c_state, rexp)


def _rope_tables(pos):
    half = ROT_DIM // 2
    pos = np.asarray(pos, np.float32)
    inv_freq = np.exp(np.float32(-math.log(ROPE_THETA))
                      * np.arange(0, ROT_DIM, 2, dtype=np.float32) / np.float32(ROT_DIM))
    ang = (pos[:, None] * inv_freq[None, :]).astype(np.float32)
    cos, sin = np.cos(ang).astype(np.float32), np.sin(ang).astype(np.float32)
    t = pos.shape[0]
    rest = HEAD_DIM - ROT_DIM
    a = np.concatenate([cos, cos, np.ones((t, rest), np.float32)], axis=1)
    bm = np.concatenate([-sin, np.zeros((t, HEAD_DIM - half), np.float32)], axis=1)
    bp = np.concatenate([np.zeros((t, half), np.float32), sin, np.zeros((t, rest), np.float32)],
                        axis=1)
    return tuple(jnp.asarray(np.concatenate([x, x], axis=1)) for x in (a, bm, bp))


W_PREP_BLK = 512


W_PREP_SLOTS = 3


def _w_prep_kernel(wt_hbm, o_ref, ring, sems, *, n_head, n_blk):
    j = pl.program_id(0)
    ahead = W_PREP_SLOTS - 1

    def block_copy(k):
        row0 = pl.multiple_of(
            (k * (W_PREP_BLK // 8) + jnp.where(k < n_head, 0, _GIF_COLS // 8)) * 8, 8)
        slot = k % W_PREP_SLOTS
        return pltpu.make_async_copy(wt_hbm.at[pl.ds(row0, W_PREP_BLK)], ring.at[slot],
                                     sems.at[slot])

    @pl.when(j == 0)
    def _():
        for k in range(ahead):
            block_copy(k).start()

    @pl.when(j + ahead < n_blk)
    def _():
        block_copy(j + ahead).start()

    block_copy(j).wait()
    o_ref[...] = ring[j % W_PREP_SLOTS].T.astype(o_ref.dtype)


def _w_prep(w_in):
    d_model = w_in.shape[0]
    wt = w_in.T
    n_head = _C_GA // W_PREP_BLK
    n_blk = _W_COLS // W_PREP_BLK

    return pl.pallas_call(
        functools.partial(_w_prep_kernel, n_head=n_head, n_blk=n_blk),
        grid=(n_blk,),
        in_specs=[pl.BlockSpec(memory_space=pl.ANY)],
        out_specs=pl.BlockSpec((d_model, W_PREP_BLK), lambda j: (0, j)),
        out_shape=jax.ShapeDtypeStruct((d_model, _W_COLS), BF16),
        scratch_shapes=[pltpu.VMEM((W_PREP_SLOTS, W_PREP_BLK, d_model), F32),
                        pltpu.SemaphoreType.DMA((W_PREP_SLOTS,))],
        compiler_params=pltpu.CompilerParams(dimension_semantics=("arbitrary",),
                                             vmem_limit_bytes=VMEM_LIMIT),
        name="w_prep",
    )(wt)


def kernel(x_prompt, x_sample, cache_kv_w128, cache_kv_w512, cache_kv_w2048, state_mlstm_C, state_mlstm_n, state_mlstm_m, norm1_g, w_in, b_if, q_norm_g, k_norm_g, w_att_out, w_m_out, w_o, norm2_g, w_gate, w_up, w_down):
    nb, seq_len, d_model = x_prompt.shape
    db, dec_seq, _ = x_sample.shape
    assert dec_seq == 1 and d_model == D_MODEL and seq_len % ATT_STEP == 0 and db <= LANES
    caches = (cache_kv_w128, cache_kv_w512, cache_kv_w2048)

    w_perm = _w_prep(w_in)
    w_gif = jnp.zeros((d_model, LANES), BF16).at[:, :_GIF_COLS].set(
        w_in[:, _C_GA:_C_GA + _GIF_COLS].astype(BF16))
    g1 = norm1_g.reshape(1, d_model)
    g2 = norm2_g.reshape(1, d_model)
    bif = jnp.concatenate([b_if, jnp.zeros((LANES - b_if.shape[0],), F32)]).reshape(1, LANES)
    qg = jnp.tile(q_norm_g, HEADS).reshape(1, GROUP_W)
    kg = jnp.tile(k_norm_g, HEADS).reshape(1, GROUP_W)
    hid = np.arange(GROUP_W // 2) // HEAD_DIM
    gmat = jnp.asarray(hid[:, None] == hid[None, :], dtype=BF16)

    m_rows = nb * seq_len
    x2d = x_prompt.reshape(m_rows, d_model)
    tabs_p = _rope_tables(np.arange(seq_len))
    qg_p = qg * (HEAD_DIM ** -0.5 * math.log2(math.e))
    outs = _proj(x2d, seq_len, DILATIONS, 256, False, g1, w_perm, w_gif, bif, qg_p, kg, tabs_p, gmat)
    qs, ks, vs = outs[0:3], outs[3:6], outs[6:9]
    hm, ga, gb, st_p, m_p = outs[9:14]

    o_att, (wa, wm, wo, wg, wu, wd) = _attention(
        qs, ks, vs, seq_len, (w_att_out, w_m_out, w_o, w_gate, w_up, w_down))

    x_s = jnp.zeros((LANES, d_model), F32).at[:db].set(x_sample.reshape(db, d_model))
    tabs_s = _rope_tables(np.full((LANES,), PAST_LEN))
    outs_s = _proj(x_s, LANES, (1, 1, 1), LANES, True, g1, w_perm, w_gif, bif, qg, kg, tabs_s, gmat)
    qkv_t = _sample_prep(outs_s[0:9])
    caches_t = [c.transpose(0, 2, 3, 4, 1) for c in caches]
    y_prompt, kv_st, o_att_cols = _merge_ffn(x2d, o_att, hm, ga, gb, wa, wm, wo, g2, wg, wu, wd,
                                             seq_len, 256, decode=(qkv_t, caches_t))
    y_prompt = y_prompt.reshape(nb, seq_len, d_model)
    kv_s = [c.transpose(0, 4, 1, 2, 3) for c in kv_st]

    tails = _kv_tails(ks, vs, seq_len)
    kv_p = [t.reshape(nb, 2, HEADS, HEAD_DIM, t.shape[-1]).transpose(0, 4, 1, 2, 3) for t in tails]
    c_p = st_p[..., :M_DV].reshape(nb, M_HEADS, M_DQK, M_DV)
    n_p = st_p[..., M_DV].reshape(nb, M_HEADS, M_DQK)
    m_pr = m_p[:, :, 0]

    qm_t, km_t, vm_s, om_s, ga_s, gb_s, gif_s = outs_s[9:16]
    gif_ts = gif_s.T

    rexp = jnp.asarray(np.arange(M_HEADS * M_DQK)[:, None] // M_DQK == np.arange(M_HEADS)[None, :],
                       dtype=F32)
    m_t = jnp.zeros((M_HEADS, LANES), F32).at[:, :db].set(state_mlstm_m.T)
    n_pad = jnp.zeros((LANES, M_HEADS * M_DQK), F32).at[:db].set(state_mlstm_n.reshape(db, -1))
    hm_s3, c_s, n_t, m_so = _sample_mlstm(
        qm_t, km_t, vm_s[:db].reshape(db, M_HEADS, M_DV), om_s[:db].reshape(db, M_HEADS, M_DV),
        gif_ts[:2 * M_HEADS], m_t, n_pad, state_mlstm_C, rexp, db)
    n_s = n_t.T[:db].reshape(db, M_HEADS, M_DQK)
    m_s = m_so[:, :db].T

    o_att_s = o_att_cols[:, :, :, 0].reshape(db, PAIRS, LANES)
    o_att_sp = jnp.zeros((PAIRS, LANES, LANES), F32).at[:, :db].set(o_att_s.transpose(1, 0, 2))[None]
    hm_sp = jnp.zeros((LANES, M_V_W), F32).at[:db].set(hm_s3.reshape(db, M_V_W))
    y_s = _merge_ffn(x_s, o_att_sp, hm_sp, ga_s, gb_s, wa, wm, wo, g2, wg, wu, wd, LANES, LANES)
    y_sample = y_s[:db].reshape(db, 1, d_model)

    return (y_prompt, y_sample, kv_p[0], kv_p[1], kv_p[2], c_p, n_p, m_pr,
            kv_s[0], kv_s[1], kv_s[2], c_s, n_s, m_s)
```

```python
import functools
import math

import jax
import jax.numpy as jnp
import numpy as np
from jax import lax
from jax.experimental import pallas as pl
from jax.experimental.pallas import tpu as pltpu

F32 = jnp.float32
BF16 = jnp.bfloat16

HEAD_DIM = 64
HEADS = 8
GROUP_W = HEADS * HEAD_DIM
N_GROUPS = 3
WINDOWS = (128, 512, 2048)
DILATIONS = (1, 4, 16)
SPAN = 128
ROT_DIM = 16
ROPE_THETA = 500000.0
M_HEADS = 8
M_DQK = 64
M_DV = 128
M_QK_W = M_HEADS * M_DQK
M_V_W = M_HEADS * M_DV
D_MODEL = 1024
PAST_LEN = 8192
NORM_EPS = 1e-6
NEG = -1e30

LANES = 128
PAIRS = GROUP_W // LANES
VMEM_LIMIT = 56 * 1024 * 1024

_ATT_W = N_GROUPS * GROUP_W
_C_QM = 3 * _ATT_W
_C_KM = _C_QM + M_QK_W
_C_VM = _C_KM + M_QK_W
_C_OM = _C_VM + M_V_W
_C_GA = _C_OM + M_V_W
_C_GB = _C_GA + D_MODEL
_W_COLS = _C_GB + D_MODEL
_GIF_COLS = 2 * M_HEADS


def _const_spec(shape):
    nd = len(shape)
    return pl.BlockSpec(shape, lambda *_: (0,) * nd, pipeline_mode=pl.Buffered(1))


def _log_sigmoid(x):
    return jnp.minimum(x, 0.0) - jnp.log1p(jnp.exp(-jnp.abs(x)))


MLSTM_CHUNKS = 2


def _proj_kernel(x_ref, g1_ref, w_ref, wgif_ref, bif_ref, qg_ref, kg_ref, ra_ref, rm_ref, rp_ref,
                 gm_ref,
                 q0_ref, q1_ref, q2_ref, k0_ref, k1_ref, k2_ref, v0_ref, v1_ref, v2_ref,
                 *rest, tm, dils, plain, tiles_per_seq):
    if plain:
        qm_ref, km_ref, vm_ref, om_ref, ga_ref, gb_ref, gif_ref, hs_ref = rest
    else:
        hm_ref, ga_ref, gb_ref, st_ref, mo_ref, hs_ref, st_s, m_s = rest
    d_model = x_ref.shape[1]
    x = x_ref[...]
    xn = x * lax.rsqrt(jnp.mean(x * x, axis=-1, keepdims=True) + NORM_EPS) * g1_ref[...]
    h_nat = xn.astype(BF16)
    n_slab = d_model // LANES
    if any(d > 1 for d in dils):
        for c in range(n_slab):
            hs_ref[c] = xn[:, c * LANES:(c + 1) * LANES]

    def permuted_h(d):
        if d == 1:
            return h_nat
        n = tm // d
        rows = [jnp.concatenate([hs_ref[c, pl.ds(r, n, stride=d), :] for c in range(n_slab)], axis=1)
                for r in range(d)]
        return jnp.concatenate(rows, axis=0).astype(BF16)

    def permuted_tab(ref, d):
        if d == 1:
            t = ref[...]
        else:
            n = tm // d
            t = jnp.concatenate([ref[pl.ds(r, n, stride=d), :] for r in range(d)], axis=0)
        return jnp.concatenate([t] * PAIRS, axis=1)

    gmat = gm_ref[...]

    def head_sumsq(z):
        zz = (z * z).astype(BF16)
        half = GROUP_W // 2
        return jnp.concatenate(
            [jnp.dot(zz[:, :half], gmat, preferred_element_type=F32),
             jnp.dot(zz[:, half:], gmat, preferred_element_type=F32)], axis=1)

    def norm_rope(z, ss, gain, ra, rm, rp):
        y = z * lax.rsqrt(ss * (1.0 / HEAD_DIM) + NORM_EPS) * gain
        return (y * ra + pltpu.roll(y, GROUP_W - ROT_DIM // 2, 1) * rm
                + pltpu.roll(y, ROT_DIM // 2, 1) * rp)

    def store_group(ref, y, d):
        if plain:
            ref[...] = y.astype(ref.dtype)
            return
        n = tm // d
        for p in range(PAIRS):
            ref[:, p] = y[:, p * LANES:(p + 1) * LANES].reshape(d, n, LANES).astype(ref.dtype)

    def seg(c0, width):
        return jnp.dot(h_nat, w_ref[:, c0:c0 + width], preferred_element_type=F32)

    if not plain:
        @pl.when(pl.program_id(0) % tiles_per_seq == 0)
        def _():
            st_s[...] = jnp.zeros_like(st_s)
            m_s[...] = jnp.zeros_like(m_s)

        gif = jnp.dot(h_nat, wgif_ref[...], preferred_element_type=F32) + bif_ref[...]
        qm_b, km_b = seg(_C_QM, M_QK_W).astype(BF16), seg(_C_KM, M_QK_W).astype(BF16)
        vm_b, om_b = seg(_C_VM, M_V_W).astype(BF16), seg(_C_OM, M_V_W).astype(BF16)
        gif_t = gif.T[:2 * M_HEADS, :]
        lc = tm // MLSTM_CHUNKS
        m_chunks = []
        for c in range(MLSTM_CHUNKS):
            rows = slice(c * lc, (c + 1) * lc)
            m_chunks.append(functools.partial(
                _mlstm_chunk, qm_b[rows], km_b[rows], vm_b[rows], om_b[rows], gif[rows],
                gif_t[:, rows], st_s, m_s, hm_ref.at[rows], lc))
        m_live, m_started = [], {}
        m_units = [(c, pairs) for c in range(MLSTM_CHUNKS) for pairs in ((0, 1), (2, 3))]

        def m_advance(i):
            if 0 < i <= len(m_units):
                phases, pairs = m_live.pop()
                phases[2](pairs)
            if i < len(m_units):
                c, pairs = m_units[i]
                if c not in m_started:
                    m_started[c] = m_chunks[c]()
                phases = m_started[c]
                phases[0](pairs)
                phases[1](pairs)
                m_live.append((phases, pairs))

        m_advance(0)

    q_refs = (q0_ref, q1_ref, q2_ref)
    k_refs = (k0_ref, k1_ref, k2_ref)
    v_refs = (v0_ref, v1_ref, v2_ref)
    for g in range(N_GROUPS):
        d = dils[g]
        hg = permuted_h(d)
        ra, rm, rp = (permuted_tab(r, d) for r in (ra_ref, rm_ref, rp_ref))
        cq, ck, cv = (t * _ATT_W + g * GROUP_W for t in range(3))
        zq = jnp.dot(hg, w_ref[:, cq:cq + GROUP_W], preferred_element_type=F32)
        zk = jnp.dot(hg, w_ref[:, ck:ck + GROUP_W], preferred_element_type=F32)
        zv = jnp.dot(hg, w_ref[:, cv:cv + GROUP_W], preferred_element_type=F32)
        ssq, ssk = head_sumsq(zq), head_sumsq(zk)
        store_group(v_refs[g], zv, d)
        store_group(q_refs[g], norm_rope(zq, ssq, qg_ref[...], ra, rm, rp), d)
        store_group(k_refs[g], norm_rope(zk, ssk, kg_ref[...], ra, rm, rp), d)

        if not plain:
            m_advance(g + 1)

    if plain:
        segments = ((qm_ref, _C_QM, M_QK_W), (km_ref, _C_KM, M_QK_W), (vm_ref, _C_VM, M_V_W),
                    (om_ref, _C_OM, M_V_W), (ga_ref, _C_GA, D_MODEL), (gb_ref, _C_GB, D_MODEL))
        zg = jnp.dot(h_nat, wgif_ref[...], preferred_element_type=F32)
        gif_ref[...] = zg + bif_ref[...]
    else:
        segments = ((ga_ref, _C_GA, D_MODEL), (gb_ref, _C_GB, D_MODEL))
    for si, (ref, c0, width) in enumerate(segments):
        for cc in range(0, width, GROUP_W):
            ref[:, cc:cc + GROUP_W] = seg(c0 + cc, GROUP_W).astype(ref.dtype)
        if not plain and si == 0:
            m_advance(N_GROUPS + 1)
            st_ref[...] = st_s[...]
            mo_ref[...] = m_s[...]


def _proj(x2d, seq_len, dils, tm, plain, g1, w_perm, w_gif, bif, qg, kg, rope_tabs, gmat):
    m_rows, d_model = x2d.shape
    nb = m_rows // seq_len
    tiles_per_seq = seq_len // tm
    grid = (m_rows // tm,)
    row_spec = lambda w: pl.BlockSpec((tm, w), lambda i: (i, 0))
    tab_spec = pl.BlockSpec((tm, LANES), lambda i: (i % tiles_per_seq, 0))
    sds = jax.ShapeDtypeStruct

    widths = (M_QK_W, M_QK_W, M_V_W, M_V_W, D_MODEL, D_MODEL)
    scratch = [pltpu.VMEM((d_model // LANES, tm, LANES), F32)]
    if plain:
        out_shape = ([sds((m_rows, GROUP_W), F32)] * 9 + [sds((m_rows, w), F32) for w in widths]
                     + [sds((m_rows, LANES), F32)])
        out_specs = ([row_spec(GROUP_W)] * 9 + [row_spec(w) for w in widths] + [row_spec(LANES)])
    else:
        def grp_spec(d):
            return pl.BlockSpec((None, d, PAIRS, tm // d, LANES),
                                lambda i: (i // tiles_per_seq, 0, 0, i % tiles_per_seq, 0))
        grp_shape = lambda d: sds((nb, d, PAIRS, seq_len // d, LANES), BF16)
        state_spec = lambda *dims: pl.BlockSpec((None,) + dims,
                                                lambda i: (i // tiles_per_seq,) + (0,) * len(dims))
        st_dims, m_dims = (M_HEADS // 2, LANES, 2 * LANES), (M_HEADS, LANES)
        row_w = (M_V_W, D_MODEL, D_MODEL)
        out_shape = ([grp_shape(d) for d in dils] * 3 + [sds((m_rows, w), BF16) for w in row_w]
                     + [sds((nb,) + st_dims, F32), sds((nb,) + m_dims, F32)])
        out_specs = ([grp_spec(d) for d in dils] * 3 + [row_spec(w) for w in row_w]
                     + [state_spec(*st_dims), state_spec(*m_dims)])
        scratch += [pltpu.VMEM(st_dims, F32), pltpu.VMEM(m_dims, F32)]
    in_specs = [row_spec(d_model), _const_spec((1, d_model)), _const_spec(w_perm.shape),
                _const_spec(w_gif.shape), _const_spec((1, LANES)), _const_spec((1, GROUP_W)),
                _const_spec((1, GROUP_W)), tab_spec, tab_spec, tab_spec, _const_spec(gmat.shape)]
    return pl.pallas_call(
        functools.partial(_proj_kernel, tm=tm, dils=dils, plain=plain,
                          tiles_per_seq=tiles_per_seq),
        grid=grid, in_specs=in_specs, out_specs=out_specs, out_shape=out_shape,
        scratch_shapes=scratch,
        compiler_params=pltpu.CompilerParams(dimension_semantics=("arbitrary",),
                                             vmem_limit_bytes=VMEM_LIMIT),
        name="proj",
    )(x2d, g1, w_perm, w_gif, bif, qg, kg, *rope_tabs, gmat)


ATT_STEP = SPAN * max(DILATIONS)
FAR_PITCH = SPAN + 8


def _attn_kernel(q0, k0, v0, kh0, vh0, q1, k1, v1, kh1, vh1, q2, k2, v2, kh2, vh2,
                 *rest, n_cast):
    w_f32, o_ref, w_bf16 = rest[:n_cast], rest[n_cast], rest[n_cast + 1:2 * n_cast + 1]
    acc_s, m_s, l_s, far_s = rest[2 * n_cast + 1:]
    j = pl.program_id(1)
    for src, dst in zip(w_f32, w_bf16):
        dst[...] = src[...].astype(dst.dtype)

    def window(cur, halo, r, bi):
        prev = halo[r] if bi == 0 else cur[r, (bi - 1) * SPAN:bi * SPAN, :]
        return jnp.concatenate([prev, cur[r, bi * SPAN:(bi + 1) * SPAN, :]], axis=0)

    qi = lax.broadcasted_iota(jnp.int32, (SPAN, 2 * SPAN), 0)
    ci = lax.broadcasted_iota(jnp.int32, (SPAN, 2 * SPAN), 1)
    band = (ci >= qi) & (ci <= qi + SPAN)
    bias_band = jnp.where(band, 0.0, NEG).astype(F32)
    bias_first = jnp.where(band & (ci >= SPAN), 0.0, NEG).astype(F32)
    lane_q = lax.broadcasted_iota(jnp.int32, (SPAN, LANES), 1)
    lane_kv = lax.broadcasted_iota(jnp.int32, (2 * SPAN, LANES), 1)
    halves_q = (lane_q < HEAD_DIM, lane_q >= HEAD_DIM)
    halves_kv = (lane_kv < HEAD_DIM, lane_kv >= HEAD_DIM)

    d0, d1, d2 = DILATIONS
    units = (
        [(0, bi, pl.ds(bi * SPAN, SPAN)) for bi in range(ATT_STEP // d0 // SPAN)],
        [(r, bi, pl.ds(bi * SPAN * d1 + r, SPAN, stride=d1))
         for r in range(d1) for bi in range(ATT_STEP // d1 // SPAN)],
        [(r, 0, pl.ds(r * FAR_PITCH, SPAN)) for r in range(d2)],
    )
    q_refs = (q0, q1, q2)
    kv_refs = ((k0, kh0, v0, vh0), (k1, kh1, v1, vh1), (k2, kh2, v2, vh2))

    def scores(g):
        k_cur, k_halo = kv_refs[g][:2]
        out = []
        for r, bi, _ in units[g]:
            q2_ = q_refs[g][r, bi * SPAN:(bi + 1) * SPAN, :]
            kk = window(k_cur, k_halo, r, bi)
            bias = jnp.where(j == 0, bias_first, bias_band) if bi == 0 else bias_band
            ss = []
            for hh in range(2):
                qa = jnp.where(halves_q[hh], q2_, jnp.zeros_like(q2_))
                ss.append(lax.dot_general(qa, kk, (((1,), (1,)), ((), ())),
                                          preferred_element_type=F32) + bias)
            out.append(ss)
        return out

    def finish(g, all_ss):
        v_cur, v_halo = kv_refs[g][2:]
        probs = []
        for ss in all_ss:
            mxs = [jnp.max(s, axis=-1, keepdims=True) for s in ss]
            ps = [jnp.exp2(s - mx) for s, mx in zip(ss, mxs)]
            ls = [jnp.sum(p, axis=-1, keepdims=True) for p in ps]
            probs.append(([p.astype(BF16) for p in ps], mxs, ls))
        for (r, bi, sl), (ps, mxs, ls) in zip(units[g], probs):
            vv = window(v_cur, v_halo, r, bi)
            acc = None
            for hh in range(2):
                vh = jnp.where(halves_kv[hh], vv, jnp.zeros_like(vv))
                a = jnp.dot(ps[hh], vh, preferred_element_type=F32)
                acc = a if acc is None else acc + a
            stats = (acc, jnp.where(halves_q[0], mxs[0], mxs[1]),
                     jnp.where(halves_q[0], ls[0], ls[1]))
            if g == N_GROUPS - 1:
                for k, val in enumerate(stats):
                    far_s[k, sl, :] = val
            else:
                for ref, val in zip((acc_s, m_s, l_s), stats):
                    ref[g, sl, :] = val

    for g in range(N_GROUPS):
        finish(g, scores(g))

    def far_rows(k, ci_):
        sub = 8
        pieces = [far_s[k, pl.ds(a * sub * FAR_PITCH + ci_ * (SPAN // d2) + jj, sub,
                                 stride=FAR_PITCH), :]
                  for jj in range(SPAN // d2) for a in range(d2 // sub)]
        return jnp.concatenate(pieces, axis=0)

    def combine(ci_, c):
        sl = pl.ds(pl.multiple_of(ci_ * SPAN, SPAN), SPAN)
        accs = [acc_s[0, sl, :], acc_s[1, sl, :], far_rows(0, ci_)]
        ms = [m_s[0, sl, :], m_s[1, sl, :], far_rows(1, ci_)]
        ls = [l_s[0, sl, :], l_s[1, sl, :], far_rows(2, ci_)]
        mm = jnp.maximum(jnp.maximum(ms[0], ms[1]), ms[2])
        es = [jnp.exp2(m - mm) for m in ms]
        num = es[0] * accs[0] + es[1] * accs[1] + es[2] * accs[2]
        den = es[0] * ls[0] + es[1] * ls[1] + es[2] * ls[2]
        o_ref[sl, :] = (num / den).astype(o_ref.dtype)
        return c
    lax.fori_loop(0, ATT_STEP // SPAN, combine, 0, unroll=4)


def _attention(qs, ks, vs, seq_len, cast_ws):
    nb = qs[0].shape[0]
    steps = seq_len // ATT_STEP
    in_specs, args, scratch = [], [], []
    for g, d in enumerate(DILATIONS):
        rows = ATT_STEP // d
        cur = pl.BlockSpec((None, d, None, rows, LANES), lambda b, j, p: (b, 0, p, j, 0))
        ratio = rows // SPAN
        halo = pl.BlockSpec((None, d, None, SPAN, LANES),
                            lambda b, j, p, ratio=ratio: (b, 0, p, jnp.maximum(j * ratio - 1, 0), 0))
        in_specs += [cur, cur, cur, halo, halo]
        args += [qs[g], ks[g], vs[g], ks[g], vs[g]]
    scratch += [pltpu.VMEM((N_GROUPS - 1, ATT_STEP, LANES), F32)] * 3
    scratch += [pltpu.VMEM((3, max(DILATIONS) * FAR_PITCH, LANES), F32)]
    n_slices = max(nb * steps * PAIRS // 2, 1)
    slice_of = lambda b, j, p: (((b * steps + j) * PAIRS + p) // 2, 0)
    w_specs = []
    for w in cast_ws:
        assert w.shape[0] % (16 * n_slices) == 0, (w.shape, n_slices)
        w_specs.append(pl.BlockSpec((w.shape[0] // n_slices, w.shape[1]), slice_of))
    outs = pl.pallas_call(
        functools.partial(_attn_kernel, n_cast=len(cast_ws)),
        grid=(nb, steps, PAIRS),
        in_specs=in_specs + w_specs,
        out_specs=[pl.BlockSpec((None, None, ATT_STEP, LANES), lambda b, j, p: (b, p, j, 0))]
                  + w_specs,
        out_shape=[jax.ShapeDtypeStruct((nb, PAIRS, seq_len, LANES), BF16)]
                  + [jax.ShapeDtypeStruct(w.shape, BF16) for w in cast_ws],
        scratch_shapes=scratch,
        compiler_params=pltpu.CompilerParams(
            dimension_semantics=("arbitrary", "arbitrary", "arbitrary"),
            vmem_limit_bytes=VMEM_LIMIT),
        name="attn",
    )(*args, *cast_ws)
    return outs[0], outs[1:]


def _kv_tail_kernel(k0, v0, k1, v1, k2, v2, o0, o1, o2, nat_ref):
    for (k_ref, v_ref, o_ref, d) in ((k0, v0, o0, DILATIONS[0]), (k1, v1, o1, DILATIONS[1]),
                                     (k2, v2, o2, DILATIONS[2])):
        for kvi, ref in enumerate((k_ref, v_ref)):
            for p in range(PAIRS):
                rows = slice(p * LANES, (p + 1) * LANES)
                if d == 1:
                    o_ref[kvi, rows, :] = ref[0, p].astype(F32).T
                    continue
                for r in range(d):
                    nat_ref[pl.ds(r, SPAN, stride=d), :] = ref[r, p].astype(F32)
                for c in range(d):
                    o_ref[kvi, rows, c * SPAN:(c + 1) * SPAN] = nat_ref[c * SPAN:(c + 1) * SPAN, :].T


def _kv_tails(ks, vs, seq_len):
    nb = ks[0].shape[0]
    in_specs, args, out_specs, out_shape = [], [], [], []
    for g, d in enumerate(DILATIONS):
        last = seq_len // d // SPAN - 1
        spec = pl.BlockSpec((None, d, PAIRS, SPAN, LANES), lambda b, last=last: (b, 0, 0, last, 0))
        in_specs += [spec, spec]
        args += [ks[g], vs[g]]
        out_specs.append(pl.BlockSpec((None, 2, GROUP_W, SPAN * d), lambda b: (b, 0, 0, 0)))
        out_shape.append(jax.ShapeDtypeStruct((nb, 2, GROUP_W, SPAN * d), F32))
    return pl.pallas_call(
        _kv_tail_kernel,
        grid=(nb,), in_specs=in_specs, out_specs=out_specs, out_shape=out_shape,
        scratch_shapes=[pltpu.VMEM((SPAN * max(DILATIONS), LANES), F32)],
        compiler_params=pltpu.CompilerParams(dimension_semantics=("arbitrary",),
                                             vmem_limit_bytes=VMEM_LIMIT),
        name="kv_tail",
    )(*args)


def _mlstm_chunk(q2d, k2d, v2d, om2d, gif, gt, st_s, m_s, hm_ref, lc):
    lf_c = _log_sigmoid(gif)
    lf_r = _log_sigmoid(gt[M_HEADS:2 * M_HEADS, :])
    row = lax.broadcasted_iota(jnp.int32, (lc, lc), 0)
    col = lax.broadcasted_iota(jnp.int32, (lc, lc), 1)
    causal = col <= row
    tril = causal.astype(F32)
    triu = (row <= col).astype(F32)
    b_c = jnp.dot(tril, lf_c, precision=lax.Precision.HIGHEST, preferred_element_type=F32)
    b_r = jnp.dot(lf_r, triu, precision=lax.Precision.HIGHEST, preferred_element_type=F32)
    lane = lax.broadcasted_iota(jnp.int32, (lc, LANES), 1)
    halves = (lane < M_DQK, lane >= M_DQK)
    sub = lax.broadcasted_iota(jnp.int32, (LANES, 1), 0)
    ones_blk = jnp.ones((lc, LANES), BF16)
    qscale = M_DQK ** -0.5

    states, p1, p2 = {}, {}, {}
    all_pairs = tuple(range(M_HEADS // 2))
    heads_of = lambda pairs: [2 * p + hh for p in pairs for hh in range(2)]

    def phase1(pairs=all_pairs):
        for p in pairs:
            states[p] = st_s[p]
        for h in heads_of(pairs):
            p, hh = divmod(h, 2)
            q2_ = q2d[:, p * LANES:(p + 1) * LANES]
            k2_ = k2d[:, p * LANES:(p + 1) * LANES]
            qa = jnp.where(halves[hh], q2_, jnp.zeros_like(q2_)) * jnp.asarray(qscale, q2_.dtype)
            s = lax.dot_general(qa, k2_, (((1,), (1,)), ((), ())), preferred_element_type=F32)
            qc = jnp.dot(qa, states[p].astype(BF16), preferred_element_type=F32)
            p1[h] = (s, qc)

    def phase2(pairs=all_pairs):
        for h in heads_of(pairs):
            p, hh = divmod(h, 2)
            s, qc = p1[h]
            bcol = b_c[:, M_HEADS + h:M_HEADS + h + 1]
            igcol = gif[:, h:h + 1]
            a_row = gt[h:h + 1, :] - b_r[h:h + 1, :]
            amat = jnp.where(causal, a_row, NEG)
            m_prev = m_s[h:h + 1, 0:1]
            g_t = jnp.maximum(m_prev, jnp.max(amat, axis=-1, keepdims=True))
            wqk = (jnp.exp(amat - g_t) * s).astype(BF16)
            dec = jnp.exp(m_prev - g_t)
            m_t = bcol + g_t
            m_new = m_t[lc - 1:lc, :]
            b_last = bcol[lc - 1:lc, :]
            dstate = jnp.exp(b_last + m_prev - m_new)
            ws = jnp.exp(b_last - bcol + igcol - m_new)
            k2_ = k2d[:, p * LANES:(p + 1) * LANES]
            ka = jnp.where(halves[hh], k2_, jnp.zeros_like(k2_))
            kws = (ka.astype(F32) * ws).astype(BF16)
            m_s[h:h + 1, :] = jnp.broadcast_to(m_new, (1, LANES))
            p2[h] = (wqk, dec * qc, jnp.exp(-m_t), dstate, kws)

    def phase3(pairs=all_pairs):
        upds = {}
        for h in heads_of(pairs):
            wqk, dqc, floor, _, kws = p2[h]
            v1 = jnp.concatenate([v2d[:, h * M_DV:(h + 1) * M_DV], ones_blk], axis=1)
            num_den = dqc + jnp.dot(wqk, v1, preferred_element_type=F32)
            num = num_den[:, :M_DV]
            den = num_den[:, M_DV:]
            hval = num / jnp.maximum(jnp.abs(den), floor)
            gate = jax.nn.sigmoid(om2d[:, h * M_DV:(h + 1) * M_DV].astype(F32))
            hm_ref[:, h * M_DV:(h + 1) * M_DV] = (gate * hval).astype(hm_ref.dtype)
            upds[h] = lax.dot_general(kws, v1, (((0,), (0,)), ((), ())),
                                      preferred_element_type=F32)
        for p in pairs:
            drow = jnp.where(sub < M_DQK, p2[2 * p][3], p2[2 * p + 1][3])
            st_s[p] = drow * states[p] + upds[2 * p] + upds[2 * p + 1]

    return phase1, phase2, phase3


FF_CHUNK = 256


DEC_HEADS = HEADS // 2
DEC_ROWS = DEC_HEADS * HEAD_DIM


def _decode_attn_step(b, qkv_ref, c_refs, o_refs, ot_ref):
    sel = lax.broadcasted_iota(jnp.int32, (DEC_ROWS, LANES), 1) == b
    scale = HEAD_DIM ** -0.5

    def column(i):
        return jnp.sum(jnp.where(sel, qkv_ref[i], 0.0), axis=1, keepdims=True)

    def head(col, h):
        return col[h * HEAD_DIM:(h + 1) * HEAD_DIM, :]

    qc = [column(g) * scale for g in range(N_GROUPS)]
    kn = [column(N_GROUPS + g) for g in range(N_GROUPS)]
    vn = [column(2 * N_GROUPS + g) for g in range(N_GROUPS)]
    cols = []
    for h in range(DEC_HEADS):
        scores, m_h = [], None
        for g, d in enumerate(DILATIONS):
            wb = c_refs[g].shape[-1]
            qh = head(qc[g], h)
            s = jnp.sum(c_refs[g][0, h] * qh, axis=0, keepdims=True)
            pos = lax.broadcasted_iota(jnp.int32, (1, wb), 1)
            s = jnp.where((pos & (d - 1)) == 0, s, NEG)
            s_new = jnp.sum(head(kn[g], h) * qh, axis=0, keepdims=True)
            m_g = jnp.maximum(jnp.max(s, axis=1, keepdims=True), s_new)
            m_h = m_g if m_h is None else jnp.maximum(m_h, m_g)
            scores.append((s, s_new))
        l_h = jnp.zeros((1, 1), F32)
        acc = jnp.zeros((HEAD_DIM, 1), F32)
        for g in range(N_GROUPS):
            s, s_new = scores[g]
            p = jnp.exp(s - m_h)
            p_new = jnp.exp(s_new - m_h)
            l_h = l_h + jnp.sum(p, axis=1, keepdims=True) + p_new
            acc = acc + jnp.sum(c_refs[g][1, h] * p, axis=1, keepdims=True) + p_new * head(vn[g], h)
        cols.append(acc / l_h)
    ot_ref[...] = jnp.broadcast_to(jnp.concatenate(cols, axis=0), ot_ref.shape)
    for g in range(N_GROUPS):
        wb = c_refs[g].shape[-1]
        last = lax.broadcasted_iota(jnp.int32, (HEAD_DIM, wb), 1) == wb - 1
        for kvi, new in ((0, kn[g]), (1, vn[g])):
            for h in range(DEC_HEADS):
                rolled = pltpu.roll(c_refs[g][kvi, h], wb - 1, 1)
                o_refs[g][kvi, h] = jnp.where(last, head(new, h), rolled)


def _merge_ffn_kernel(x_ref, o_ref, hm_ref, ga_ref, gb_ref, wa_ref, wm_ref, wo_ref, g2_ref,
                      wg_ref, wu_ref, wd_ref, *rest, with_decode):
    if with_decode:
        qkv_ref, c0, c1, c2, y_ref, o0, o1, o2, ot_ref = rest
        _decode_attn_step(pl.program_id(0) // 2, qkv_ref, (c0, c1, c2), (o0, o1, o2), ot_ref)
    else:
        (y_ref,) = rest
    yb = jnp.dot(hm_ref[...].astype(BF16), wm_ref[...], preferred_element_type=F32)
    o_att = jnp.concatenate([o_ref[p] for p in range(PAIRS)], axis=1).astype(BF16)
    ya = jnp.dot(o_att, wa_ref[...], preferred_element_type=F32)
    mixed = (jax.nn.sigmoid(ga_ref[...].astype(F32)) * ya
             + jax.nn.sigmoid(gb_ref[...].astype(F32)) * yb)
    x2 = x_ref[...] + jnp.dot(mixed.astype(BF16), wo_ref[...], preferred_element_type=F32)
    h2 = (x2 * lax.rsqrt(jnp.mean(x2 * x2, axis=-1, keepdims=True) + NORM_EPS)
          * g2_ref[...]).astype(BF16)
    acc = x2
    d_ff = wg_ref.shape[1]

    def gate_up(c):
        gt = jnp.dot(h2, wg_ref[:, c:c + FF_CHUNK], preferred_element_type=F32)
        up = jnp.dot(h2, wu_ref[:, c:c + FF_CHUNK], preferred_element_type=F32)
        return gt, up

    chunks = list(range(0, d_ff, FF_CHUNK))
    nxt = gate_up(chunks[0])
    for i, c in enumerate(chunks):
        gt, up = nxt
        if i + 1 < len(chunks):
            nxt = gate_up(chunks[i + 1])
        ff = (gt * jax.nn.sigmoid(gt) * up).astype(BF16)
        acc = acc + jnp.dot(ff, wd_ref[c:c + FF_CHUNK, :], preferred_element_type=F32)
    y_ref[...] = acc


def _merge_ffn(x2d, o_att, hm, ga, gb, wa, wm, wo, g2, wg, wu, wd, seq_len, tm, decode=None):
    m_rows, d_model = x2d.shape
    tiles_per_seq = seq_len // tm
    steps = m_rows // tm
    row = lambda w: pl.BlockSpec((tm, w), lambda i: (i, 0))
    o_spec = pl.BlockSpec((None, PAIRS, tm, LANES),
                          lambda i: (i // tiles_per_seq, 0, i % tiles_per_seq, 0))
    in_specs = [row(d_model), o_spec, row(M_V_W), row(D_MODEL), row(D_MODEL),
                _const_spec(wa.shape), _const_spec(wm.shape), _const_spec(wo.shape),
                _const_spec((1, d_model)), _const_spec(wg.shape), _const_spec(wu.shape),
                _const_spec(wd.shape)]
    args = [x2d, o_att, hm, ga, gb, wa, wm, wo, g2, wg, wu, wd]
    out_specs = [row(d_model)]
    out_shape = [jax.ShapeDtypeStruct((m_rows, d_model), F32)]
    if decode is not None:
        qkv_t, caches_t = decode
        db = caches_t[0].shape[0]
        assert steps == 2 * db, (steps, db)
        in_specs.append(pl.BlockSpec((qkv_t.shape[0], DEC_ROWS, LANES), lambda i: (0, i % 2, 0)))
        args.append(qkv_t)
        for c in caches_t:
            spec = pl.BlockSpec((None, 2, DEC_HEADS, HEAD_DIM, c.shape[-1]),
                                lambda i: (i // 2, 0, i % 2, 0, 0))
            in_specs.append(spec)
            args.append(c)
            out_specs.append(spec)
            out_shape.append(jax.ShapeDtypeStruct(c.shape, F32))
        out_specs.append(pl.BlockSpec((None, None, DEC_ROWS, LANES), lambda i: (i // 2, i % 2, 0, 0)))
        out_shape.append(jax.ShapeDtypeStruct((db, 2, DEC_ROWS, LANES), F32))
    outs = pl.pallas_call(
        functools.partial(_merge_ffn_kernel, with_decode=decode is not None),
        grid=(steps,),
        in_specs=in_specs, out_specs=out_specs, out_shape=out_shape,
        compiler_params=pltpu.CompilerParams(dimension_semantics=("arbitrary",),
                                             vmem_limit_bytes=VMEM_LIMIT),
        name="merge_ffn",
    )(*args)
    if decode is None:
        return outs[0]
    return outs[0], outs[1:4], outs[4]


def _sample_prep_kernel(*refs):
    for i, ref in enumerate(refs[:-1]):
        refs[-1][i] = ref[...].T


def _sample_prep(arrs):
    vm = pl.BlockSpec(memory_space=pltpu.VMEM)
    return pl.pallas_call(
        _sample_prep_kernel, in_specs=[vm] * len(arrs), out_specs=vm,
        out_shape=jax.ShapeDtypeStruct((len(arrs), GROUP_W, LANES), F32),
        name="sample_prep",
    )(*arrs)


def _sample_mlstm_kernel(q_ref, k_ref, v_ref, om_ref, gt_ref, mt_ref, n_ref, c_ref, rexp_ref,
                         hm_ref, co_ref, nt_ref, mo_ref, *, db):
    nh, dqk = M_HEADS, M_DQK
    hi = lax.Precision.HIGHEST
    rexp = rexp_ref[...]
    q_t = q_ref[...].T * (dqk ** -0.5)
    k_t = k_ref[...].T
    n_t = n_ref[...].T
    ig = gt_ref[0:nh, :]
    lf = _log_sigmoid(gt_ref[nh:2 * nh, :])
    m_prev = mt_ref[...]
    m_new = jnp.maximum(lf + m_prev, ig)
    w8 = jnp.exp(ig - m_new)
    dec8 = jnp.exp(lf + m_prev - m_new)
    head_sum = lambda a: lax.dot_general(rexp, a, (((0,), (0,)), ((), ())), precision=hi,
                                         preferred_element_type=F32)
    expand = lambda a: jnp.dot(rexp, a, precision=hi, preferred_element_type=F32)
    qk8 = head_sum(q_t * k_t)
    qn8 = head_sum(q_t * n_t)
    wqk8 = w8 * qk8
    den8 = dec8 * qn8 + wqk8
    inv8 = 1.0 / jnp.maximum(jnp.abs(den8), jnp.exp(-m_new))
    dec_x = expand(dec8)
    w_x = expand(w8)
    nt_ref[...] = dec_x * n_t + w_x * k_t
    mo_ref[...] = m_new
    wk_x = w_x * k_t
    for b in range(db):
        cb = c_ref[b].reshape(nh * dqk, M_DV)
        v_b = v_ref[b]
        v_x = jnp.concatenate([jnp.broadcast_to(v_b[h:h + 1, :], (dqk, M_DV)) for h in range(nh)],
                              axis=0)
        co_ref[b] = (dec_x[:, b:b + 1] * cb + wk_x[:, b:b + 1] * v_x).reshape(nh, dqk, M_DV)
        qc = jnp.sum((q_t[:, b:b + 1] * cb).reshape(nh, dqk, M_DV), axis=1)
        num = dec8[:, b:b + 1] * qc + wqk8[:, b:b + 1] * v_b
        hm_ref[b] = jax.nn.sigmoid(om_ref[b]) * (num * inv8[:, b:b + 1])


def _sample_mlstm(q_t, k_t, v3, om3, gif_t, m_t, n_pad, c_state, rexp, db):
    vm = pl.BlockSpec(memory_space=pltpu.VMEM)
    return pl.pallas_call(
        functools.partial(_sample_mlstm_kernel, db=db),
        in_specs=[vm] * 9,
        out_specs=[vm] * 4,
        out_shape=[jax.ShapeDtypeStruct((db, M_HEADS, M_DV), F32),
                   jax.ShapeDtypeStruct(c_state.shape, F32),
                   jax.ShapeDtypeStruct((M_HEADS * M_DQK, LANES), F32),
                   jax.ShapeDtypeStruct((M_HEADS, LANES), F32)],
        compiler_params=pltpu.CompilerParams(vmem_limit_bytes=VMEM_LIMIT),
        name="sample_mlstm",
    )(q_t, k_t, v3, om3, gif_t, m_t, n_pad, c_state, rexp)


def _rope_tables(pos):
    half = ROT_DIM // 2
    pos = np.asarray(pos, np.float32)
    inv_freq = np.exp(np.float32(-math.log(ROPE_THETA))
                      * np.arange(0, ROT_DIM, 2, dtype=np.float32) / np.float32(ROT_DIM))
    ang = (pos[:, None] * inv_freq[None, :]).astype(np.float32)
    cos, sin = np.cos(ang).astype(np.float32), np.sin(ang).astype(np.float32)
    t = pos.shape[0]
    rest = HEAD_DIM - ROT_DIM
    a = np.concatenate([cos, cos, np.ones((t, rest), np.float32)], axis=1)
    bm = np.concatenate([-sin, np.zeros((t, HEAD_DIM - half), np.float32)], axis=1)
    bp = np.concatenate([np.zeros((t, half), np.float32), sin, np.zeros((t, rest), np.float32)],
                        axis=1)
    return tuple(jnp.asarray(np.concatenate([x, x], axis=1)) for x in (a, bm, bp))


W_PREP_BLK = 512


W_PREP_SLOTS = 4


def _w_prep_kernel(wt_hbm, o_ref, ring, sems, *, n_head, n_blk):
    j = pl.program_id(0)
    ahead = W_PREP_SLOTS - 1

    def block_copy(k):
        row0 = pl.multiple_of(
            (k * (W_PREP_BLK // 8) + jnp.where(k < n_head, 0, _GIF_COLS // 8)) * 8, 8)
        slot = k % W_PREP_SLOTS
        return pltpu.make_async_copy(wt_hbm.at[pl.ds(row0, W_PREP_BLK)], ring.at[slot],
                                     sems.at[slot])

    @pl.when(j == 0)
    def _():
        for k in range(ahead):
            block_copy(k).start()

    @pl.when(j + ahead < n_blk)
    def _():
        block_copy(j + ahead).start()

    block_copy(j).wait()
    o_ref[...] = ring[j % W_PREP_SLOTS].T.astype(o_ref.dtype)


def _w_prep(w_in):
    d_model = w_in.shape[0]
    wt = w_in.T
    n_head = _C_GA // W_PREP_BLK
    n_blk = _W_COLS // W_PREP_BLK

    return pl.pallas_call(
        functools.partial(_w_prep_kernel, n_head=n_head, n_blk=n_blk),
        grid=(n_blk,),
        in_specs=[pl.BlockSpec(memory_space=pl.ANY)],
        out_specs=pl.BlockSpec((d_model, W_PREP_BLK), lambda j: (0, j)),
        out_shape=jax.ShapeDtypeStruct((d_model, _W_COLS), BF16),
        scratch_shapes=[pltpu.VMEM((W_PREP_SLOTS, W_PREP_BLK, d_model), F32),
                        pltpu.SemaphoreType.DMA((W_PREP_SLOTS,))],
        compiler_params=pltpu.CompilerParams(dimension_semantics=("arbitrary",),
                                             vmem_limit_bytes=VMEM_LIMIT),
        name="w_prep",
    )(wt)


def kernel(x_prompt, x_sample, cache_kv_w128, cache_kv_w512, cache_kv_w2048, state_mlstm_C, state_mlstm_n, state_mlstm_m, norm1_g, w_in, b_if, q_norm_g, k_norm_g, w_att_out, w_m_out, w_o, norm2_g, w_gate, w_up, w_down):
    nb, seq_len, d_model = x_prompt.shape
    db, dec_seq, _ = x_sample.shape
    assert dec_seq == 1 and d_model == D_MODEL and seq_len % ATT_STEP == 0 and db <= LANES
    caches = (cache_kv_w128, cache_kv_w512, cache_kv_w2048)

    w_perm = _w_prep(w_in)
    w_gif = jnp.zeros((d_model, LANES), BF16).at[:, :_GIF_COLS].set(
        w_in[:, _C_GA:_C_GA + _GIF_COLS].astype(BF16))
    g1 = norm1_g.reshape(1, d_model)
    g2 = norm2_g.reshape(1, d_model)
    bif = jnp.concatenate([b_if, jnp.zeros((LANES - b_if.shape[0],), F32)]).reshape(1, LANES)
    qg = jnp.tile(q_norm_g, HEADS).reshape(1, GROUP_W)
    kg = jnp.tile(k_norm_g, HEADS).reshape(1, GROUP_W)
    hid = np.arange(GROUP_W // 2) // HEAD_DIM
    gmat = jnp.asarray(hid[:, None] == hid[None, :], dtype=BF16)

    m_rows = nb * seq_len
    x2d = x_prompt.reshape(m_rows, d_model)
    tabs_p = _rope_tables(np.arange(seq_len))
    qg_p = qg * (HEAD_DIM ** -0.5 * math.log2(math.e))
    outs = _proj(x2d, seq_len, DILATIONS, 256, False, g1, w_perm, w_gif, bif, qg_p, kg, tabs_p, gmat)
    qs, ks, vs = outs[0:3], outs[3:6], outs[6:9]
    hm, ga, gb, st_p, m_p = outs[9:14]

    o_att, (wa, wm, wo, wg, wu, wd) = _attention(
        qs, ks, vs, seq_len, (w_att_out, w_m_out, w_o, w_gate, w_up, w_down))

    x_s = jnp.zeros((LANES, d_model), F32).at[:db].set(x_sample.reshape(db, d_model))
    tabs_s = _rope_tables(np.full((LANES,), PAST_LEN))
    outs_s = _proj(x_s, LANES, (1, 1, 1), LANES, True, g1, w_perm, w_gif, bif, qg, kg, tabs_s, gmat)
    qkv_t = _sample_prep(outs_s[0:9])
    caches_t = [c.transpose(0, 2, 3, 4, 1) for c in caches]
    y_prompt, kv_st, o_att_cols = _merge_ffn(x2d, o_att, hm, ga, gb, wa, wm, wo, g2, wg, wu, wd,
                                             seq_len, 256, decode=(qkv_t, caches_t))
    y_prompt = y_prompt.reshape(nb, seq_len, d_model)
    kv_s = [c.transpose(0, 4, 1, 2, 3) for c in kv_st]

    tails = _kv_tails(ks, vs, seq_len)
    kv_p = [t.reshape(nb, 2, HEADS, HEAD_DIM, t.shape[-1]).transpose(0, 4, 1, 2, 3) for t in tails]
    c_p = st_p[..., :M_DV].reshape(nb, M_HEADS, M_DQK, M_DV)
    n_p = st_p[..., M_DV].reshape(nb, M_HEADS, M_DQK)
    m_pr = m_p[:, :, 0]

    qm_t, km_t, vm_s, om_s, ga_s, gb_s, gif_s = outs_s[9:16]
    gif_ts = gif_s.T

    rexp = jnp.asarray(np.arange(M_HEADS * M_DQK)[:, None] // M_DQK == np.arange(M_HEADS)[None, :],
                       dtype=F32)
    m_t = jnp.zeros((M_HEADS, LANES), F32).at[:, :db].set(state_mlstm_m.T)
    n_pad = jnp.zeros((LANES, M_HEADS * M_DQK), F32).at[:db].set(state_mlstm_n.reshape(db, -1))
    hm_s3, c_s, n_t, m_so = _sample_mlstm(
        qm_t, km_t, vm_s[:db].reshape(db, M_HEADS, M_DV), om_s[:db].reshape(db, M_HEADS, M_DV),
        gif_ts[:2 * M_HEADS], m_t, n_pad, state_mlstm_C, rexp, db)
    n_s = n_t.T[:db].reshape(db, M_HEADS, M_DQK)
    m_s = m_so[:, :db].T

    o_att_s = o_att_cols[:, :, :, 0].reshape(db, PAIRS, LANES)
    o_att_sp = jnp.zeros((PAIRS, LANES, LANES), F32).at[:, :db].set(o_att_s.transpose(1, 0, 2))[None]
    hm_sp = jnp.zeros((LANES, M_V_W), F32).at[:db].set(hm_s3.reshape(db, M_V_W))
    y_s = _merge_ffn(x_s, o_att_sp, hm_sp, ga_s, gb_s, wa, wm, wo, g2, wg, wu, wd, LANES, LANES)
    y_sample = y_s[:db].reshape(db, 1, d_model)

    return (y_prompt, y_sample, kv_p[0], kv_p[1], kv_p[2], c_p, n_p, m_pr,
            kv_s[0], kv_s[1], kv_s[2], c_s, n_s, m_s)
```
